```python
import jax, jax.numpy as jnp
from jax import lax
import numpy as np

D_MODEL = 1024
BATCH = 8
SEQ = 2048
DEPTH = 2

GRID_W = 64
CTX_LEN = 256
HEAD_DIM = 64
N_Q_HEADS = 8
N_KV_HEADS = 2
Q_PER_KV = N_Q_HEADS // N_KV_HEADS
ATTN_W = N_Q_HEADS * HEAD_DIM
KV_W = N_KV_HEADS * HEAD_DIM
WINDOW = 128
ATTN_BLOCK = 128
ROPE_BASE = 10000.0
POOL_WINDOWS = (2, 4, 8, 16)
POOL_GROUPS = len(POOL_WINDOWS)
POOL_W = 512
POOL_GW = POOL_W // POOL_GROUPS
RNN_W = 512
RNN_BLOCKS = 8
RNN_BW = RNN_W // RNN_BLOCKS
CONV_W = 4
LRU_C = 8.0
N_DIRS = 2
N_BRANCH = 3
D_FF = -(-8 * D_MODEL // (3 * 256)) * 256
EPS = 1e-6
NEG_INF = -1e30

Q0 = 0
K0 = Q0 + ATTN_W
V0 = K0 + KV_W
RX0 = V0 + KV_W
RY0 = RX0 + RNN_W
PU0 = RY0 + RNN_W
GL0 = PU0 + POOL_W
IN_W = GL0 + N_BRANCH * D_MODEL

kernel_name = "hybrid_gated_parallel_diffusion_block"


def rms_norm(x, g):
    xf = x.astype(jnp.float32)
    y = xf * lax.rsqrt(jnp.mean(xf * xf, axis=-1, keepdims=True) + EPS)
    return (y * g.astype(jnp.float32)).astype(x.dtype)


def modulate(h, shift, scale):
    return h * (1 + scale) + shift


def axial_rope_tables(seq_len):
    rows = seq_len // GRID_W
    row = jnp.repeat(jnp.arange(rows), GRID_W).astype(jnp.float32)
    col = jnp.tile(jnp.arange(GRID_W), rows).astype(jnp.float32)
    half = HEAD_DIM // 2
    quarter = half // 2
    inv = ROPE_BASE ** (-(jnp.arange(quarter, dtype=jnp.float32) * 2.0 / half))
    ang_r = row[:, None] * inv[None, :]
    ang_c = col[:, None] * inv[None, :]
    return jnp.cos(ang_r), jnp.sin(ang_r), jnp.cos(ang_c), jnp.sin(ang_c)


def apply_axial_rope(x, tables):
    cos_r, sin_r, cos_c, sin_c = tables
    xf = x.astype(jnp.float32)
    half = HEAD_DIM // 2
    quarter = half // 2

    def rot(u, cos, sin):
        cos = cos[None, :, None, :]
        sin = sin[None, :, None, :]
        u1, u2 = u[..., :quarter], u[..., quarter:]
        return jnp.concatenate([u1 * cos - u2 * sin, u2 * cos + u1 * sin], axis=-1)

    out = jnp.concatenate([rot(xf[..., :half], cos_r, sin_r), rot(xf[..., half:], cos_c, sin_c)], axis=-1)
    return out.astype(x.dtype)


def windowed_attention(q_lat, k_lat, v_lat, q_ctx, k_ctx, v_ctx, sink):
    B, S = q_lat.shape[0], q_lat.shape[1]
    C = k_ctx.shape[1]
    nb = S // ATTN_BLOCK
    scale = HEAD_DIM ** -0.5
    sink_hg = sink.astype(jnp.float32).reshape(N_KV_HEADS, Q_PER_KV)

    q = (q_lat * scale).reshape(B, nb, ATTN_BLOCK, N_KV_HEADS, Q_PER_KV, HEAD_DIM)
    pad = ((0, 0), (ATTN_BLOCK, ATTN_BLOCK), (0, 0), (0, 0))
    kp = jnp.pad(k_lat, pad).reshape(B, nb + 2, ATTN_BLOCK, N_KV_HEADS, HEAD_DIM)
    vp = jnp.pad(v_lat, pad).reshape(B, nb + 2, ATTN_BLOCK, N_KV_HEADS, HEAD_DIM)
    k_win = jnp.concatenate([kp[:, :nb], kp[:, 1:nb + 1], kp[:, 2:]], axis=2)
    v_win = jnp.concatenate([vp[:, :nb], vp[:, 1:nb + 1], vp[:, 2:]], axis=2)
    nk = 3 * ATTN_BLOCK

    blk = jnp.arange(nb)[:, None, None] * ATTN_BLOCK
    qpos = blk + jnp.arange(ATTN_BLOCK)[None, :, None]
    kpos = blk - ATTN_BLOCK + jnp.arange(nk)[None, None, :]
    valid = (kpos >= 0) & (kpos < S) & (jnp.abs(qpos - kpos) <= WINDOW)

    s_loc = jnp.einsum('bnqhgd,bnkhd->bnhgqk', q, k_win).astype(jnp.float32)
    s_loc = jnp.where(valid[None, :, None, None], s_loc, NEG_INF)
    s_ctx = jnp.einsum('bnqhgd,bkhd->bnhgqk', q, k_ctx).astype(jnp.float32)
    s_sink = jnp.broadcast_to(sink_hg[None, None, :, :, None, None],
                              (B, nb, N_KV_HEADS, Q_PER_KV, ATTN_BLOCK, 1))
    p = jax.nn.softmax(jnp.concatenate([s_loc, s_ctx, s_sink], axis=-1), axis=-1).astype(v_lat.dtype)
    o = (jnp.einsum('bnhgqk,bnkhd->bnqhgd', p[..., :nk], v_win)
         + jnp.einsum('bnhgqk,bkhd->bnqhgd', p[..., nk:nk + C], v_ctx))
    o_lat = o.reshape(B, S, ATTN_W)

    o_ctx = None
    if q_ctx is not None:
        qc = (q_ctx * scale).reshape(B, C, N_KV_HEADS, Q_PER_KV, HEAD_DIM)
        sc = jnp.einsum('bqhgd,bkhd->bhgqk', qc, k_ctx).astype(jnp.float32)
        sc_sink = jnp.broadcast_to(sink_hg[None, :, :, None, None], (B, N_KV_HEADS, Q_PER_KV, C, 1))
        pc = jax.nn.softmax(jnp.concatenate([sc, sc_sink], axis=-1), axis=-1).astype(v_ctx.dtype)
        o_ctx = jnp.einsum('bhgqk,bkhd->bqhgd', pc[..., :C], v_ctx).reshape(B, C, ATTN_W)
    return o_lat, o_ctx


def multiscale_pool(u, w_mix, ch_scale):
    B, L = u.shape[0], u.shape[1]
    t = jnp.arange(L)
    outs = []
    for g, w in enumerate(POOL_WINDOWS):
        ug = u[..., g * POOL_GW:(g + 1) * POOL_GW].astype(jnp.float32)
        cs = jnp.pad(jnp.cumsum(ug, axis=1), ((0, 0), (1, 0), (0, 0)))
        lo = jnp.clip(t - (w - 1) // 2, 0, L)
        hi = jnp.clip(t + w // 2 + 1, 0, L)
        mean = (cs[:, hi] - cs[:, lo]) / (hi - lo).astype(jnp.float32)[None, :, None]
        outs.append(mean - ug)
    d = jnp.concatenate(outs, axis=-1).astype(u.dtype).reshape(B, L, POOL_GROUPS, POOL_GW)
    y = jnp.einsum('blgc,gcd->blgd', d, w_mix).reshape(B, L, POOL_W)
    return y * ch_scale


def dwconv_centred(x, w, b):
    L = x.shape[1]
    xp = jnp.pad(x, ((0, 0), ((CONV_W - 1) // 2, CONV_W // 2), (0, 0)))
    y = b
    for k in range(CONV_W):
        y = y + xp[:, k:k + L] * w[k]
    return y


def lru_coeffs(x, w_a, b_a, w_x, b_x, lam):
    B, L = x.shape[0], x.shape[1]
    xb = x.reshape(B, L, RNN_BLOCKS, RNN_BW)
    r = jax.nn.sigmoid(jnp.einsum('blhi,hij->blhj', xb, w_a).reshape(B, L, RNN_W).astype(jnp.float32)
                       + b_a.astype(jnp.float32))
    i = jax.nn.sigmoid(jnp.einsum('blhi,hij->blhj', xb, w_x).reshape(B, L, RNN_W).astype(jnp.float32)
                       + b_x.astype(jnp.float32))
    log_a = LRU_C * r * jax.nn.log_sigmoid(lam.astype(jnp.float32))
    a = jnp.exp(log_a)
    mult = jnp.sqrt(-jnp.expm1(2.0 * log_a))
    return a, mult * i * x.astype(jnp.float32)


def linear_scan(a, b, h0):
    def comb(l, r):
        return (l[0] * r[0], r[0] * l[1] + r[1])
    a_cum, b_cum = lax.associative_scan(comb, (a, b), axis=1)
    return a_cum * h0[:, None, :] + b_cum


def bidir_rglru_branch(x_lat, y_lat, x_ctx, y_ctx, conv_w, conv_b, w_a, b_a, w_x, b_x, lam):
    xl = dwconv_centred(x_lat, conv_w, conv_b)
    xc = dwconv_centred(x_ctx, conv_w, conv_b)
    h0 = jnp.zeros((x_lat.shape[0], RNN_W), jnp.float32)
    h_lat = []
    h_ctx = []
    for d in range(N_DIRS):
        al, bl = lru_coeffs(xl, w_a[d], b_a[d], w_x[d], b_x[d], lam[d])
        ac, bc = lru_coeffs(xc, w_a[d], b_a[d], w_x[d], b_x[d], lam[d])
        if d == 1:
            al, bl, ac, bc = (jnp.flip(t, axis=1) for t in (al, bl, ac, bc))
        hc = linear_scan(ac, bc, h0)
        hl = linear_scan(al, bl, hc[:, -1])
        if d == 1:
            hl, hc = jnp.flip(hl, axis=1), jnp.flip(hc, axis=1)
        h_lat.append(hl)
        h_ctx.append(hc)
    out_l = ((h_lat[0] + h_lat[1]) * jax.nn.gelu(y_lat.astype(jnp.float32))).astype(x_lat.dtype)
    out_c = None
    if y_ctx is not None:
        out_c = ((h_ctx[0] + h_ctx[1]) * jax.nn.gelu(y_ctx.astype(jnp.float32))).astype(x_ctx.dtype)
    return out_l, out_c


def merge_branches(attn, pool, rnn, gate_logits, w_attn_o, w_pool_o, w_rnn_o, w_out):
    lead = gate_logits.shape[:-1]
    g = jax.nn.sigmoid(gate_logits.astype(jnp.float32)).astype(attn.dtype).reshape(*lead, N_BRANCH, D_MODEL)
    merged = (g[..., 0, :] * (attn @ w_attn_o) + g[..., 1, :] * (pool @ w_pool_o)
              + g[..., 2, :] * (rnn @ w_rnn_o))
    return merged @ w_out


def ffn_sublayer(x, shift, scale, gate, g_pre, g_post, w_gu, w_down):
    h = modulate(rms_norm(x, g_pre), shift, scale)
    gu = h @ w_gu
    f = (jax.nn.silu(gu[..., :D_FF]) * gu[..., D_FF:]) @ w_down
    return x + gate * rms_norm(f, g_post)


def setup_inputs(seed: int = 0) -> dict:
    key = jax.random.key(seed)
    ks = jax.random.split(key, 32)
    f32 = jnp.float32

    def nrm(k, shape, scale):
        return jax.random.normal(k, shape, f32) * scale

    L = DEPTH
    u = jax.random.uniform(ks[22], (L, N_DIRS, RNN_W), f32, 0.9, 0.999)
    return {
        "x": nrm(ks[0], (BATCH, SEQ, D_MODEL), 1.0),
        "c": nrm(ks[1], (BATCH, D_MODEL), 1.0),
        "ctx": nrm(ks[2], (BATCH, CTX_LEN, D_MODEL), 1.0),
        "c_ctx": nrm(ks[3], (D_MODEL,), 1.0),
        "w_ada": nrm(ks[4], (L, D_MODEL, 6 * D_MODEL), 0.5 * D_MODEL ** -0.5),
        "b_ada": nrm(ks[5], (L, 6 * D_MODEL), 0.02),
        "g_pre_mix": 1.0 + nrm(ks[6], (L, D_MODEL), 0.05),
        "g_post_mix": 1.0 + nrm(ks[7], (L, D_MODEL), 0.05),
        "g_pre_ffn": 1.0 + nrm(ks[8], (L, D_MODEL), 0.05),
        "g_post_ffn": 1.0 + nrm(ks[9], (L, D_MODEL), 0.05),
        "w_in": nrm(ks[10], (L, D_MODEL, IN_W), D_MODEL ** -0.5),
        "attn_sink": nrm(ks[11], (L, N_Q_HEADS), 0.5),
        "w_attn_o": nrm(ks[12], (L, ATTN_W, D_MODEL), ATTN_W ** -0.5),
        "pool_mix": nrm(ks[13], (L, POOL_GROUPS, POOL_GW, POOL_GW), POOL_GW ** -0.5),
        "pool_scale": 1.0 + nrm(ks[14], (L, POOL_W), 0.1),
        "w_pool_o": nrm(ks[15], (L, POOL_W, D_MODEL), POOL_W ** -0.5),
        "conv_w": nrm(ks[16], (L, CONV_W, RNN_W), CONV_W ** -0.5),
        "conv_b": nrm(ks[17], (L, RNN_W), 0.02),
        "lru_w_a": nrm(ks[18], (L, N_DIRS, RNN_BLOCKS, RNN_BW, RNN_BW), RNN_BW ** -0.5),
        "lru_b_a": nrm(ks[19], (L, N_DIRS, RNN_W), 0.02),
        "lru_w_x": nrm(ks[20], (L, N_DIRS, RNN_BLOCKS, RNN_BW, RNN_BW), RNN_BW ** -0.5),
        "lru_b_x": nrm(ks[21], (L, N_DIRS, RNN_W), 0.02),
        "lru_lambda": jnp.log(u) - jnp.log1p(-u),
        "w_rnn_o": nrm(ks[23], (L, RNN_W, D_MODEL), RNN_W ** -0.5),
        "w_out": nrm(ks[24], (L, D_MODEL, D_MODEL), D_MODEL ** -0.5),
        "w_gu": nrm(ks[25], (L, D_MODEL, 2 * D_FF), D_MODEL ** -0.5),
        "w_down": nrm(ks[26], (L, D_FF, D_MODEL), D_FF ** -0.5),
    }


def reference(x, c, ctx, c_ctx, w_ada, b_ada, g_pre_mix, g_post_mix, g_pre_ffn, g_post_ffn,
              w_in, attn_sink, w_attn_o, pool_mix, pool_scale, w_pool_o, conv_w, conv_b,
              lru_w_a, lru_b_a, lru_w_x, lru_b_x, lru_lambda, w_rnn_o, w_out, w_gu, w_down):
    B, S, D = x.shape
    C = ctx.shape[1]
    rope = axial_rope_tables(S)
    silu_c = jax.nn.silu(c)
    silu_cc = jax.nn.silu(c_ctx)[None]
    for l in range(DEPTH):
        need_ctx = l < DEPTH - 1
        mod_x = (silu_c @ w_ada[l] + b_ada[l]).reshape(B, 1, 6, D)
        mod_c = (silu_cc @ w_ada[l] + b_ada[l]).reshape(1, 1, 6, D)
        wl = w_in[l]

        h = modulate(rms_norm(x, g_pre_mix[l]), mod_x[:, :, 0], mod_x[:, :, 1])
        hc = modulate(rms_norm(ctx, g_pre_mix[l]), mod_c[:, :, 0], mod_c[:, :, 1])
        p = h @ wl
        q = apply_axial_rope(p[..., Q0:K0].reshape(B, S, N_Q_HEADS, HEAD_DIM), rope)
        k = apply_axial_rope(p[..., K0:V0].reshape(B, S, N_KV_HEADS, HEAD_DIM), rope)
        v = p[..., V0:RX0].reshape(B, S, N_KV_HEADS, HEAD_DIM)
        rx, ry, pu, gl = p[..., RX0:RY0], p[..., RY0:PU0], p[..., PU0:GL0], p[..., GL0:]

        pc_kvx = hc @ wl[:, K0:RY0]
        kc = pc_kvx[..., :KV_W].reshape(B, C, N_KV_HEADS, HEAD_DIM)
        vc = pc_kvx[..., KV_W:2 * KV_W].reshape(B, C, N_KV_HEADS, HEAD_DIM)
        rxc = pc_kvx[..., 2 * KV_W:]
        if need_ctx:
            qc = (hc @ wl[:, Q0:K0]).reshape(B, C, N_Q_HEADS, HEAD_DIM)
            pc_rest = hc @ wl[:, RY0:]
            ryc, puc, glc = pc_rest[..., :RNN_W], pc_rest[..., RNN_W:RNN_W + POOL_W], pc_rest[..., RNN_W + POOL_W:]
        else:
            qc, ryc = None, None

        attn_l, attn_c = windowed_attention(q, k, v, qc, kc, vc, attn_sink[l])
        rnn_l, rnn_c = bidir_rglru_branch(rx, ry, rxc, ryc, conv_w[l], conv_b[l], lru_w_a[l], lru_b_a[l],
                                          lru_w_x[l], lru_b_x[l], lru_lambda[l])
        pool_l = multiscale_pool(pu, pool_mix[l], pool_scale[l])
        mix_l = merge_branches(attn_l, pool_l, rnn_l, gl, w_attn_o[l], w_pool_o[l], w_rnn_o[l], w_out[l])
        x = x + mod_x[:, :, 2] * rms_norm(mix_l, g_post_mix[l])

        x = ffn_sublayer(x, mod_x[:, :, 3], mod_x[:, :, 4], mod_x[:, :, 5], g_pre_ffn[l], g_post_ffn[l], w_gu[l], w_down[l])

        if need_ctx:
            pool_c = multiscale_pool(puc, pool_mix[l], pool_scale[l])
            mix_c = merge_branches(attn_c, pool_c, rnn_c, glc, w_attn_o[l], w_pool_o[l], w_rnn_o[l], w_out[l])
            ctx = ctx + mod_c[:, :, 2] * rms_norm(mix_c, g_post_mix[l])
            ctx = ffn_sublayer(ctx, mod_c[:, :, 3], mod_c[:, :, 4], mod_c[:, :, 5], g_pre_ffn[l], g_post_ffn[l], w_gu[l], w_down[l])
    return x
```

```python
import functools
import math

import jax
import jax.numpy as jnp
from jax import lax
from jax.experimental import pallas as pl
from jax.experimental.pallas import tpu as pltpu

F32 = jnp.float32
BF16 = jnp.bfloat16

GRID_W = 64
HEAD_DIM = 64
N_Q_HEADS = 8
N_KV_HEADS = 2
ATTN_W = N_Q_HEADS * HEAD_DIM
KV_W = N_KV_HEADS * HEAD_DIM
ATTN_BLOCK = 128
ROPE_BASE = 10000.0
POOL_WINDOWS = (2, 4, 8, 16)
POOL_W = 512
POOL_GW = POOL_W // len(POOL_WINDOWS)
RNN_W = 512
RNN_BLOCKS = 8
CONV_W = 4
LRU_C = 8.0
N_BRANCH = 3
EPS = 1e-6
NEG_INF = -1e30

LANES = 128
SUBLANES = 8
VMEM_LIMIT_BYTES = 56 * 1024 * 1024

SEG = 36
CHUNK = SUBLANES * SEG


def _params(n_axes):
    return pltpu.CompilerParams(
        dimension_semantics=("arbitrary",) * n_axes, vmem_limit_bytes=VMEM_LIMIT_BYTES)


def _resident(shape):
    nd = len(shape)
    return pl.BlockSpec(shape, lambda *_: (0,) * nd, pipeline_mode=pl.Buffered(1))


def _rms(xf):
    return xf * lax.rsqrt(jnp.mean(xf * xf, axis=-1, keepdims=True) + EPS)


def _mod_body(c_ref, w_ref, b_ref, o_ref):
    c = c_ref[...]
    s = (c * jax.nn.sigmoid(c)).astype(BF16)
    o_ref[0] = jnp.dot(s, w_ref[0].astype(BF16), preferred_element_type=F32) + b_ref[0]


def _adaln_mod(cvec, w_ada, b_ada):
    n_layers, d, n6 = w_ada.shape
    rows = cvec.shape[0]
    tn = n6 // 4
    return pl.pallas_call(
        _mod_body,
        grid=(n_layers, n6 // tn),
        in_specs=[
            pl.BlockSpec((rows, d), lambda l, j: (0, 0)),
            pl.BlockSpec((1, d, tn), lambda l, j: (l, 0, j)),
            pl.BlockSpec((1, 1, tn), lambda l, j: (l, 0, j)),
        ],
        out_specs=pl.BlockSpec((1, rows, tn), lambda l, j: (l, 0, j)),
        out_shape=jax.ShapeDtypeStruct((n_layers, rows, n6), F32),
        compiler_params=_params(2),
        name="adaln_mod",
    )(cvec, w_ada, b_ada.reshape(n_layers, 1, n6))


def _rope_tables(seq_len):
    pos = jnp.arange(seq_len)
    row = (pos // GRID_W).astype(F32)
    col = (pos % GRID_W).astype(F32)
    half = HEAD_DIM // 2
    quarter = half // 2
    inv = ROPE_BASE ** (-(jnp.arange(quarter, dtype=F32) * 2.0 / half))
    j = jnp.arange(LANES) % HEAD_DIM
    is_col = j >= half
    second = (j % half) >= quarter
    freq = inv[j % quarter]
    ang = jnp.where(is_col[None, :], col[:, None], row[:, None]) * freq[None, :]
    cos, sin = jnp.cos(ang), jnp.sin(ang)
    sin_prev = jnp.where(second[None, :], sin, 0.0)
    sin_next = jnp.where(second[None, :], 0.0, -sin)
    return cos, sin_prev, sin_next


def _norm_modulate_store(x_ref, mod_ref, g_ref, h_ref, shift_row, scale_row):
    y = _rms(x_ref[...]) * g_ref[...]
    h = y * (1.0 + mod_ref[scale_row:scale_row + 1, :]) + mod_ref[shift_row:shift_row + 1, :]
    h_ref[...] = h.astype(BF16)


def _inproj_full_body(*refs, rope):
    if rope:
        (x_ref, mod_ref, g_ref, w_ref, cos_ref, sp_ref, sn_ref,
         q_ref, kd_ref, vd_ref, rx_ref, ry_ref, pu_ref, gt_ref, h_ref) = refs
    else:
        (x_ref, mod_ref, g_ref, w_ref,
         q_ref, kd_ref, vd_ref, rx_ref, ry_ref, pu_ref, gt_ref, h_ref) = refs
    _norm_modulate_store(x_ref, mod_ref, g_ref, h_ref, 0, 1)

    def proj(c0, c1):
        return jnp.dot(h_ref[...], w_ref[:, c0:c1], preferred_element_type=F32)

    quarter = HEAD_DIM // 4
    n_q = ATTN_W // LANES
    for c in range(n_q + 2):
        pc = proj(c * LANES, (c + 1) * LANES)
        if rope:
            pc = (pc * cos_ref[...]
                  + pltpu.roll(pc, quarter, 1) * sp_ref[...]
                  + pltpu.roll(pc, LANES - quarter, 1) * sn_ref[...])
        if c < n_q:
            q_ref[:, c * LANES:(c + 1) * LANES] = (pc * (HEAD_DIM ** -0.5)).astype(BF16)
        else:
            kd_ref[:, (c - n_q) * LANES:(c - n_q + 1) * LANES] = pc.astype(BF16)
    o = ATTN_W + 2 * KV_W
    vd_ref[...] = proj(o, o + 2 * KV_W).astype(BF16)
    o += 2 * KV_W
    rx_ref[...] = proj(o, o + RNN_W)
    o += RNN_W
    ry_ref[...] = proj(o, o + RNN_W).astype(BF16)
    o += RNN_W
    pu_ref[...] = proj(o, o + POOL_W)
    o += POOL_W
    d = x_ref.shape[1]
    for c in range(N_BRANCH):
        gl = proj(o + c * d, o + (c + 1) * d)
        gt_ref[:, c * d:(c + 1) * d] = jax.nn.sigmoid(gl).astype(BF16)


def _inproj_kv_body(x_ref, mod_ref, g_ref, w_ref, kd_ref, vd_ref, rx_ref, h_ref):
    _norm_modulate_store(x_ref, mod_ref, g_ref, h_ref, 0, 1)
    kd_ref[...] = jnp.dot(h_ref[...], w_ref[:, 0:2 * KV_W], preferred_element_type=F32).astype(BF16)
    vd_ref[...] = jnp.dot(h_ref[...], w_ref[:, 2 * KV_W:4 * KV_W], preferred_element_type=F32).astype(BF16)
    rx_ref[...] = jnp.dot(h_ref[...], w_ref[:, 4 * KV_W:], preferred_element_type=F32)


def _inproj(x2d, mod, mod_row_of_tile, g, w, *, tm, tiles_per_seq, rope_tables, kv_only):
    n_tok, d = x2d.shape
    grid = (n_tok // tm,)
    row = lambda i: (i, 0)
    in_specs = [
        pl.BlockSpec((tm, d), row),
        pl.BlockSpec((None,) + mod.shape[1:], lambda i: (mod_row_of_tile(i), 0, 0)),
        _resident(g.shape),
        _resident(w.shape),
    ]
    args = [x2d, mod, g, w]
    if rope_tables is not None:
        in_specs += [pl.BlockSpec((tm, LANES), lambda i: (i % tiles_per_seq, 0))] * 3
        args += list(rope_tables)

    def out(width, dtype):
        return pl.BlockSpec((tm, width), row), jax.ShapeDtypeStruct((n_tok, width), dtype)

    if kv_only:
        outs = [out(2 * KV_W, BF16), out(2 * KV_W, BF16), out(RNN_W, F32)]
        body = _inproj_kv_body
    else:
        outs = [out(ATTN_W, BF16), out(2 * KV_W, BF16), out(2 * KV_W, BF16), out(RNN_W, F32),
                out(RNN_W, BF16), out(POOL_W, F32), out(N_BRANCH * d, BF16)]
        body = functools.partial(_inproj_full_body, rope=rope_tables is not None)
    return pl.pallas_call(
        body,
        grid=grid,
        in_specs=in_specs,
        out_specs=[o[0] for o in outs],
        out_shape=[o[1] for o in outs],
        scratch_shapes=[pltpu.VMEM((tm, d), BF16)],
        compiler_params=_params(1),
        name="inproj_kv" if kv_only else "inproj",
    )(*args)


def _attn_body(*refs, n_blocks, local):
    if local:
        (sink_ref, q_ref, kp_ref, km_ref, kn_ref, vp_ref, vm_ref, vn_ref, kc_ref, vc_ref, o_ref) = refs
        k_all = jnp.concatenate([kp_ref[...], km_ref[...], kn_ref[...], kc_ref[...]], axis=0)
        v_all = jnp.concatenate([vp_ref[...], vm_ref[...], vn_ref[...], vc_ref[...]], axis=0)
        n = pl.program_id(1)
        blk = ATTN_BLOCK
        ii = lax.broadcasted_iota(jnp.int32, (blk, blk), 0)
        jj = lax.broadcasted_iota(jnp.int32, (blk, blk), 1)
        bias_prev = jnp.where((jj >= ii) & (n > 0), 0.0, NEG_INF).astype(F32)
        bias_next = jnp.where((jj <= ii) & (n < n_blocks - 1), 0.0, NEG_INF).astype(F32)
        bias = jnp.concatenate(
            [bias_prev, jnp.zeros((blk, blk), F32), bias_next,
             jnp.zeros((blk, kc_ref.shape[0]), F32)], axis=1)
    else:
        (sink_ref, q_ref, kc_ref, vc_ref, o_ref) = refs
        k_all, v_all, bias = kc_ref[...], vc_ref[...], None

    tq = q_ref.shape[0]
    nk = k_all.shape[0]
    q_low = lax.broadcasted_iota(jnp.int32, (tq, LANES), 1) < HEAD_DIM
    k_low = lax.broadcasted_iota(jnp.int32, (nk, LANES), 1) < HEAD_DIM
    zero = jnp.zeros((), BF16)
    for h in range(N_KV_HEADS):
        k_h = k_all[:, h * LANES:(h + 1) * LANES]
        v_h = v_all[:, h * LANES:(h + 1) * LANES]
        v_halves = (jnp.where(k_low, v_h, zero), jnp.where(k_low, zero, v_h))
        chunks_per_kv = (N_Q_HEADS // N_KV_HEADS) * HEAD_DIM // LANES
        for c2 in range(chunks_per_kv):
            c = h * chunks_per_kv + c2
            qc = q_ref[:, c * LANES:(c + 1) * LANES]
            acc = None
            for e in range(2):
                qe = jnp.where(q_low, qc, zero) if e == 0 else jnp.where(q_low, zero, qc)
                s = lax.dot_general(qe, k_h, (((1,), (1,)), ((), ())), preferred_element_type=F32)
                if bias is not None:
                    s = s + bias
                sk = sink_ref[2 * c + e]
                m = jnp.maximum(jnp.max(s, axis=-1, keepdims=True), sk)
                p = jnp.exp(s - m)
                denom = jnp.sum(p, axis=-1, keepdims=True) + jnp.exp(sk - m)
                o = jnp.dot(p.astype(BF16), v_halves[e], preferred_element_type=F32) * (1.0 / denom)
                acc = o if acc is None else acc + o
            o_ref[:, c * LANES:(c + 1) * LANES] = acc.astype(BF16)


def _attention(sink, q, kd, vd, kdc, vdc):
    b, s, _ = q.shape
    c = kdc.shape[1]
    nb = s // ATTN_BLOCK
    blk = ATTN_BLOCK
    kw = 2 * KV_W
    prev = lambda bi, n: (bi, jnp.maximum(n - 1, 0), 0)
    mid = lambda bi, n: (bi, n, 0)
    nxt = lambda bi, n: (bi, jnp.minimum(n + 1, nb - 1), 0)
    ctx = lambda bi, n: (bi, 0, 0)
    kv = lambda f: pl.BlockSpec((None, blk, kw), f)
    return pl.pallas_call(
        functools.partial(_attn_body, n_blocks=nb, local=True),
        grid=(b, nb),
        in_specs=[
            pl.BlockSpec(memory_space=pltpu.SMEM),
            pl.BlockSpec((None, blk, ATTN_W), mid),
            kv(prev), kv(mid), kv(nxt), kv(prev), kv(mid), kv(nxt),
            pl.BlockSpec((None, c, kw), ctx), pl.BlockSpec((None, c, kw), ctx),
        ],
        out_specs=pl.BlockSpec((None, blk, ATTN_W), mid),
        out_shape=jax.ShapeDtypeStruct((b, s, ATTN_W), BF16),
        compiler_params=_params(2),
        name="attn_local",
    )(sink, q, kd, kd, kd, vd, vd, vd, kdc, vdc)


def _attention_ctx(sink, qc, kdc, vdc):
    b, c, _ = qc.shape
    kw = 2 * KV_W
    idx = lambda bi: (bi, 0, 0)
    return pl.pallas_call(
        functools.partial(_attn_body, n_blocks=1, local=False),
        grid=(b,),
        in_specs=[
            pl.BlockSpec(memory_space=pltpu.SMEM),
            pl.BlockSpec((None, c, ATTN_W), idx),
            pl.BlockSpec((None, c, kw), idx), pl.BlockSpec((None, c, kw), idx),
        ],
        out_specs=pl.BlockSpec((None, c, ATTN_W), idx),
        out_shape=jax.ShapeDtypeStruct((b, c, ATTN_W), BF16),
        compiler_params=_params(1),
        name="attn_ctx",
    )(sink, qc, kdc, vdc)


def _pool_body(u_ref, w_ref, sc_ref, o_ref, pad_ref, *, tile):
    length = u_ref.shape[0]
    halo = SUBLANES
    pad_ref[0:halo, :] = jnp.zeros((halo, POOL_W), F32)
    pad_ref[halo + length:2 * halo + length, :] = jnp.zeros((halo, POOL_W), F32)
    pad_ref[halo:halo + length, :] = u_ref[...]
    rows = tile + 2 * halo
    for t0 in range(0, length, tile):
        t = t0 + lax.broadcasted_iota(jnp.int32, (tile, POOL_GW), 0)
        for gi, w in enumerate(POOL_WINDOWS):
            lanes = slice(gi * POOL_GW, (gi + 1) * POOL_GW)
            p = pad_ref[t0:t0 + rows, lanes]
            acc = p
            step = 1
            while step < w:
                acc = acc + pltpu.roll(acc, rows - step, 0)
                step *= 2
            back = (w - 1) // 2
            win = (pltpu.roll(acc, back, 0) if back else acc)[halo:halo + tile]
            lo = jnp.maximum(t - back, 0)
            hi = jnp.minimum(t + w // 2 + 1, length)
            mean = win / (hi - lo).astype(F32)
            dlt = (mean - p[halo:halo + tile]).astype(BF16)
            y = jnp.dot(dlt, w_ref[gi], preferred_element_type=F32) * sc_ref[:, lanes]
            o_ref[t0:t0 + tile, lanes] = y.astype(BF16)


def _pool(pu, w_mix, ch_scale):
    b, length, _ = pu.shape
    tile = min(length, 256)
    idx = lambda bi: (bi, 0, 0)
    return pl.pallas_call(
        functools.partial(_pool_body, tile=tile),
        grid=(b,),
        in_specs=[pl.BlockSpec((None, length, POOL_W), idx), _resident(w_mix.shape),
                  _resident(ch_scale.shape)],
        out_specs=pl.BlockSpec((None, length, POOL_W), idx),
        out_shape=jax.ShapeDtypeStruct((b, length, POOL_W), BF16),
        scratch_shapes=[pltpu.VMEM((length + 2 * SUBLANES, POOL_W), F32)],
        compiler_params=_params(1),
        name="pool",
    )(pu, w_mix, ch_scale)


def _lru_body(*refs, n_ctx, n_lat, tile, with_ctx_out):
    if with_ctx_out:
        (rxc_ref, rxl_ref, ryc_ref, ryl_ref, cw_ref, cb_ref, wg_ref, ba_ref, bx_ref, lam_ref,
         ol_ref, oc_ref, pad_ref, xl_ref, hf_ref, hb_ref, xp_ref, a_ref, b_ref) = refs
    else:
        (rxc_ref, rxl_ref, ryl_ref, cw_ref, cb_ref, wg_ref, ba_ref, bx_ref, lam_ref,
         ol_ref, pad_ref, xl_ref, hf_ref, hb_ref, xp_ref, a_ref, b_ref) = refs
    halo = SUBLANES
    n_tot = n_ctx + n_lat
    n_slab = RNN_W // LANES

    zeros = jnp.zeros((halo, RNN_W), F32)
    c0, l0 = halo, 2 * halo + n_ctx
    pad_ref[0:halo, :] = zeros
    pad_ref[c0 + n_ctx:l0, :] = zeros
    pad_ref[l0 + n_lat:l0 + n_lat + halo, :] = zeros
    pad_ref[c0:c0 + n_ctx, :] = rxc_ref[...]
    pad_ref[l0:l0 + n_lat, :] = rxl_ref[...]

    def conv_tile(src0, dst_rows):
        rows = tile + 2 * halo
        p = pad_ref[src0 - halo:src0 + tile + halo, :]
        y = (cb_ref[...] + p[halo:halo + tile] * cw_ref[1:2, :]
             + pltpu.roll(p, 1, 0)[halo:halo + tile] * cw_ref[0:1, :]
             + pltpu.roll(p, rows - 1, 0)[halo:halo + tile] * cw_ref[2:3, :]
             + pltpu.roll(p, rows - 2, 0)[halo:halo + tile] * cw_ref[3:4, :])
        for s in range(n_slab):
            for d0 in dst_rows:
                xl_ref[s, d0:d0 + tile, :] = y[:, s * LANES:(s + 1) * LANES]

    for t0 in range(0, n_ctx, tile):
        conv_tile(c0 + t0, (t0, n_tot + t0))
    for t0 in range(0, n_lat, tile):
        conv_tile(l0 + t0, (n_ctx + t0,))

    lam = lam_ref[...]
    log_sig = jnp.minimum(lam, 0.0) - jnp.log1p(jnp.exp(-jnp.abs(lam)))
    decay = LRU_C * log_sig
    half = RNN_W // 2

    def run_chunk(d, base, carry):
        res_ref = hf_ref if d == 0 else hb_ref
        win0 = 0 if d == 0 else n_ctx
        for v in range(SEG):
            for s in range(n_slab):
                xp_ref[d, v * SUBLANES:(v + 1) * SUBLANES, s * LANES:(s + 1) * LANES] = (
                    xl_ref[s, pl.ds(win0 + base + v, SUBLANES, stride=SEG), :])
        for j in range(2):
            ch = slice(j * half, (j + 1) * half)
            xj = xp_ref[d, :, ch]
            gates = jnp.dot(xj.astype(BF16), wg_ref[d, j], preferred_element_type=F32)
            r_gate = jax.nn.sigmoid(gates[:, :half] + ba_ref[d:d + 1, ch])
            i_gate = jax.nn.sigmoid(gates[:, half:] + bx_ref[d:d + 1, ch])
            a = jnp.exp(r_gate * decay[d:d + 1, ch])
            mult = jnp.sqrt(1.0 - a * a)
            a_ref[d, :, ch] = a
            b_ref[d, :, ch] = mult * i_gate * xj
        order = range(SEG) if d == 0 else range(SEG - 1, -1, -1)
        h = jnp.zeros((SUBLANES, RNN_W), F32)
        acum = jnp.ones((SUBLANES, RNN_W), F32)
        for v in order:
            rows = slice(v * SUBLANES, (v + 1) * SUBLANES)
            av = a_ref[d, rows, :]
            h = av * h + b_ref[d, rows, :]
            acum = av * acum
            b_ref[d, rows, :] = h
            a_ref[d, rows, :] = acum
        seg_in = [None] * SUBLANES
        state = carry
        seg_order = range(SUBLANES) if d == 0 else range(SUBLANES - 1, -1, -1)
        for i in seg_order:
            seg_in[i] = state
            state = h[i:i + 1, :] + acum[i:i + 1, :] * state
        seg_state = jnp.concatenate(seg_in, axis=0)
        for v in range(SEG):
            rows = slice(v * SUBLANES, (v + 1) * SUBLANES)
            hv = b_ref[d, rows, :] + a_ref[d, rows, :] * seg_state
            for s in range(n_slab):
                res_ref[s, pl.ds(base + v, SUBLANES, stride=SEG), :] = hv[:, s * LANES:(s + 1) * LANES]
        return state

    n_chunks = n_tot // CHUNK

    def step(c, carries):
        cf, cb = carries
        cf = run_chunk(0, c * CHUNK, cf)
        cb = run_chunk(1, (n_chunks - 1 - c) * CHUNK, cb)
        return cf, cb

    zero_state = jnp.zeros((1, RNN_W), F32)
    lax.fori_loop(0, n_chunks, step, (zero_state, zero_state))

    def emit(out_ref, ry_ref, f0, b0, length):
        for t0 in range(0, length, tile):
            hsum = jnp.concatenate(
                [hf_ref[s, f0 + t0:f0 + t0 + tile, :] + hb_ref[s, b0 + t0:b0 + t0 + tile, :]
                 for s in range(n_slab)], axis=1)
            gate = jax.nn.gelu(ry_ref[t0:t0 + tile, :].astype(F32))
            out_ref[t0:t0 + tile, :] = (hsum * gate).astype(BF16)

    emit(ol_ref, ryl_ref, n_ctx, 0, n_lat)
    if with_ctx_out:
        emit(oc_ref, ryc_ref, 0, n_lat, n_ctx)


def _lru(rxc, rxl, ryc, ryl, conv_w, conv_b, wg, b_a, b_x, lam):
    b, n_ctx, _ = rxc.shape
    n_lat = rxl.shape[1]
    n_tot = n_ctx + n_lat
    tile = math.gcd(n_ctx, n_lat, 256)
    assert n_tot % CHUNK == 0 and tile % SUBLANES == 0
    with_ctx_out = ryc is not None
    idx = lambda bi: (bi, 0, 0)
    seq = lambda n, dt=None: pl.BlockSpec((None, n, RNN_W), idx)
    n_slab = RNN_W // LANES
    in_specs = [seq(n_ctx), seq(n_lat)] + ([seq(n_ctx)] if with_ctx_out else []) + [seq(n_lat)]
    args = [rxc, rxl] + ([ryc] if with_ctx_out else []) + [ryl]
    for wgt in (conv_w, conv_b, wg, b_a, b_x, lam):
        in_specs.append(_resident(wgt.shape))
        args.append(wgt)
    out_specs = [seq(n_lat)] + ([seq(n_ctx)] if with_ctx_out else [])
    out_shape = [jax.ShapeDtypeStruct((b, n_lat, RNN_W), BF16)]
    if with_ctx_out:
        out_shape.append(jax.ShapeDtypeStruct((b, n_ctx, RNN_W), BF16))
    res = pl.pallas_call(
        functools.partial(_lru_body, n_ctx=n_ctx, n_lat=n_lat, tile=tile, with_ctx_out=with_ctx_out),
        grid=(b,),
        in_specs=in_specs,
        out_specs=out_specs,
        out_shape=out_shape,
        scratch_shapes=[
            pltpu.VMEM((n_tot + 3 * SUBLANES, RNN_W), F32),
            pltpu.VMEM((n_slab, n_tot + n_ctx, LANES), F32),
            pltpu.VMEM((n_slab, n_tot, LANES), F32),
            pltpu.VMEM((n_slab, n_tot, LANES), F32),
            pltpu.VMEM((2, CHUNK, RNN_W), F32),
            pltpu.VMEM((2, CHUNK, RNN_W), F32),
            pltpu.VMEM((2, CHUNK, RNN_W), F32),
        ],
        compiler_params=_params(1),
        name="rglru",
    )(*args)
    return (res[0], res[1]) if with_ctx_out else (res[0], None)


def _merge_body(x_ref, mod_ref, at_ref, po_ref, rn_ref, gt_ref, wa_ref, wp_ref, wr_ref, wo_ref,
                g_ref, o_ref, m_ref):
    d = x_ref.shape[1]
    merged = None
    for k, (br_ref, w_ref) in enumerate(((at_ref, wa_ref), (po_ref, wp_ref), (rn_ref, wr_ref))):
        t = gt_ref[:, k * d:(k + 1) * d].astype(F32) * jnp.dot(
            br_ref[...], w_ref[...], preferred_element_type=F32)
        merged = t if merged is None else merged + t
    m_ref[...] = merged.astype(BF16)
    mix = jnp.dot(m_ref[...], wo_ref[...], preferred_element_type=F32)
    o_ref[...] = x_ref[...] + mod_ref[2:3, :] * (_rms(mix) * g_ref[...])


def _merge(x2d, mod, mod_row_of_tile, attn, pool, rnn, gates, wa, wp, wr, wo, g_post, *, tm):
    n_tok, d = x2d.shape
    row = lambda i: (i, 0)
    tok = lambda arr: pl.BlockSpec((tm, arr.shape[1]), row)
    return pl.pallas_call(
        _merge_body,
        grid=(n_tok // tm,),
        in_specs=[tok(x2d),
                  pl.BlockSpec((None,) + mod.shape[1:], lambda i: (mod_row_of_tile(i), 0, 0)),
                  tok(attn), tok(pool), tok(rnn), tok(gates),
                  _resident(wa.shape), _resident(wp.shape), _resident(wr.shape), _resident(wo.shape),
                  _resident(g_post.shape)],
        out_specs=pl.BlockSpec((tm, d), row),
        out_shape=jax.ShapeDtypeStruct((n_tok, d), F32),
        scratch_shapes=[pltpu.VMEM((tm, d), BF16)],
        compiler_params=_params(1),
        name="merge",
    )(x2d, mod, attn, pool, rnn, gates, wa, wp, wr, wo, g_post)


def _ffn_body(x_ref, mod_ref, gpre_ref, wgu_ref, wd_ref, gpost_ref, o_ref, h_ref, *, d_ff, chunk):
    _norm_modulate_store(x_ref, mod_ref, gpre_ref, h_ref, 3, 4)
    acc = None
    for c0 in range(0, d_ff, chunk):
        c1 = min(c0 + chunk, d_ff)
        gate = jnp.dot(h_ref[...], wgu_ref[:, c0:c1], preferred_element_type=F32)
        up = jnp.dot(h_ref[...], wgu_ref[:, d_ff + c0:d_ff + c1], preferred_element_type=F32)
        act = ((gate * jax.nn.sigmoid(gate)) * up).astype(BF16)
        part = jnp.dot(act, wd_ref[c0:c1, :], preferred_element_type=F32)
        acc = part if acc is None else acc + part
    o_ref[...] = x_ref[...] + mod_ref[5:6, :] * (_rms(acc) * gpost_ref[...])


def _ffn(x2d, mod, mod_row_of_tile, g_pre, w_gu, w_down, g_post, *, tm):
    n_tok, d = x2d.shape
    d_ff = w_down.shape[0]
    row = lambda i: (i, 0)
    return pl.pallas_call(
        functools.partial(_ffn_body, d_ff=d_ff, chunk=1024),
        grid=(n_tok // tm,),
        in_specs=[pl.BlockSpec((tm, d), row),
                  pl.BlockSpec((None,) + mod.shape[1:], lambda i: (mod_row_of_tile(i), 0, 0)),
                  _resident(g_pre.shape), _resident(w_gu.shape), _resident(w_down.shape),
                  _resident(g_post.shape)],
        out_specs=pl.BlockSpec((tm, d), row),
        out_shape=jax.ShapeDtypeStruct((n_tok, d), F32),
        scratch_shapes=[pltpu.VMEM((tm, d), BF16)],
        compiler_params=_params(1),
        name="ffn",
    )(x2d, mod, g_pre, w_gu, w_down, g_post)


def _dup_heads(w):
    d = w.shape[0]
    return jnp.repeat(w.reshape(d, N_KV_HEADS, 1, HEAD_DIM), 2, axis=2).reshape(d, 2 * KV_W)


def _pack_gate_weights(w_a, w_x):
    def block_diag(w):
        eye = jnp.eye(RNN_BLOCKS, dtype=w.dtype)
        return jnp.einsum('dhij,hg->dhigj', w, eye).reshape(w.shape[0], RNN_W, RNN_W)
    bd_a, bd_x = block_diag(w_a), block_diag(w_x)
    half = RNN_W // 2
    halves = []
    for j in range(2):
        sl = slice(j * half, (j + 1) * half)
        halves.append(jnp.concatenate([bd_a[:, sl, sl], bd_x[:, sl, sl]], axis=2))
    return jnp.stack(halves, axis=1).astype(BF16)


def kernel(x, c, ctx, c_ctx, w_ada, b_ada, g_pre_mix, g_post_mix, g_pre_ffn, g_post_ffn, w_in, attn_sink, w_attn_o, pool_mix, pool_scale, w_pool_o, conv_w, conv_b, lru_w_a, lru_b_a, lru_w_x, lru_b_x, lru_lambda, w_rnn_o, w_out, w_gu, w_down):
    bsz, seq, d = x.shape
    n_ctx = ctx.shape[1]
    depth = w_ada.shape[0]
    tm = min(512, seq)
    tm_ctx = min(tm, n_ctx)

    q0, k0, v0 = 0, ATTN_W, ATTN_W + KV_W
    rx0 = v0 + KV_W
    ry0 = rx0 + RNN_W

    mod_rows = -(-(bsz + 1) // SUBLANES) * SUBLANES
    cvec = jnp.zeros((mod_rows, d), F32).at[:bsz].set(c).at[bsz].set(c_ctx)
    mod_all = _adaln_mod(cvec, w_ada, b_ada).reshape(depth, mod_rows, 6, d)

    rope = _rope_tables(seq)
    lat_row = lambda i: i // (seq // tm)
    ctx_row = lambda i: bsz

    x2 = x.reshape(bsz * seq, d)
    c2 = ctx.reshape(bsz * n_ctx, d)
    row2 = lambda a: a.reshape(1, -1)
    for l in range(depth):
        need_ctx = l < depth - 1
        mod = mod_all[l]
        wl = w_in[l]
        w_kvx = jnp.concatenate([_dup_heads(wl[:, k0:v0]), _dup_heads(wl[:, v0:rx0]), wl[:, rx0:ry0]],
                                axis=1).astype(BF16)
        w_full = jnp.concatenate([wl[:, q0:k0].astype(BF16), w_kvx, wl[:, ry0:].astype(BF16)], axis=1)
        g_pre = row2(g_pre_mix[l])

        q, kd, vd, rx, ry, pu, gt = _inproj(
            x2, mod, lat_row, g_pre, w_full, tm=tm, tiles_per_seq=seq // tm, rope_tables=rope,
            kv_only=False)
        if need_ctx:
            qc, kdc, vdc, rxc, ryc, puc, gtc = _inproj(
                c2, mod, ctx_row, g_pre, w_full, tm=tm_ctx, tiles_per_seq=1, rope_tables=None,
                kv_only=False)
        else:
            kdc, vdc, rxc = _inproj(
                c2, mod, ctx_row, g_pre, w_kvx, tm=tm_ctx, tiles_per_seq=1, rope_tables=None,
                kv_only=True)
            ryc = None

        r3 = lambda a, n: a.reshape(bsz, n, a.shape[-1])
        sink = attn_sink[l]
        kdc3, vdc3 = r3(kdc, n_ctx), r3(vdc, n_ctx)
        attn_l = _attention(sink, r3(q, seq), r3(kd, seq), r3(vd, seq), kdc3, vdc3)

        w_mix = pool_mix[l].astype(BF16)
        p_scale = row2(pool_scale[l])
        pool_l = _pool(r3(pu, seq), w_mix, p_scale)

        wg = _pack_gate_weights(lru_w_a[l], lru_w_x[l])
        rnn_l, rnn_c = _lru(r3(rxc, n_ctx), r3(rx, seq), None if ryc is None else r3(ryc, n_ctx),
                            r3(ry, seq), conv_w[l], row2(conv_b[l]), wg, lru_b_a[l], lru_b_x[l],
                            lru_lambda[l])

        wa, wp = w_attn_o[l].astype(BF16), w_pool_o[l].astype(BF16)
        wr, wo = w_rnn_o[l].astype(BF16), w_out[l].astype(BF16)
        wgu, wdn = w_gu[l].astype(BF16), w_down[l].astype(BF16)
        g_post = row2(g_post_mix[l])
        gf_pre, gf_post = row2(g_pre_ffn[l]), row2(g_post_ffn[l])

        flat = lambda a: a.reshape(-1, a.shape[-1])
        x2 = _merge(x2, mod, lat_row, flat(attn_l), flat(pool_l), flat(rnn_l), gt, wa, wp, wr, wo,
                    g_post, tm=tm)
        x2 = _ffn(x2, mod, lat_row, gf_pre, wgu, wdn, gf_post, tm=tm)

        if need_ctx:
            attn_c = _attention_ctx(sink, r3(qc, n_ctx), kdc3, vdc3)
            pool_c = _pool(r3(puc, n_ctx), w_mix, p_scale)
            c2 = _merge(c2, mod, ctx_row, flat(attn_c), flat(pool_c), flat(rnn_c), gtc, wa, wp, wr,
                        wo, g_post, tm=tm_ctx)
            c2 = _ffn(c2, mod, ctx_row, gf_pre, wgu, wdn, gf_post, tm=tm_ctx)
    return x2.reshape(bsz, seq, d)
```

```python
import functools
import math

import jax
import jax.numpy as jnp
from jax import lax
from jax.experimental import pallas as pl
from jax.experimental.pallas import tpu as pltpu

F32 = jnp.float32
BF16 = jnp.bfloat16

GRID_W = 64
HEAD_DIM = 64
N_Q_HEADS = 8
N_KV_HEADS = 2
ATTN_W = N_Q_HEADS * HEAD_DIM
KV_W = N_KV_HEADS * HEAD_DIM
ATTN_BLOCK = 128
ROPE_BASE = 10000.0
POOL_WINDOWS = (2, 4, 8, 16)
POOL_W = 512
POOL_GW = POOL_W // len(POOL_WINDOWS)
RNN_W = 512
RNN_BLOCKS = 8
CONV_W = 4
LRU_C = 8.0
N_BRANCH = 3
EPS = 1e-6
NEG_INF = -1e30
LOG2E = 1.4426950408889634

LANES = 128
SUBLANES = 8
VMEM_LIMIT_BYTES = 56 * 1024 * 1024

SEG = 36
CHUNK = SUBLANES * SEG


def _params(n_axes):
    return pltpu.CompilerParams(
        dimension_semantics=("arbitrary",) * n_axes, vmem_limit_bytes=VMEM_LIMIT_BYTES)


def _resident(shape):
    nd = len(shape)
    return pl.BlockSpec(shape, lambda *_: (0,) * nd, pipeline_mode=pl.Buffered(1))


def _rms(xf):
    return xf * lax.rsqrt(jnp.mean(xf * xf, axis=-1, keepdims=True) + EPS)


def _mod_body(c_ref, w_ref, b_ref, o_ref):
    c = c_ref[...]
    s = (c * jax.nn.sigmoid(c)).astype(BF16)
    o_ref[0] = jnp.dot(s, w_ref[0].astype(BF16), preferred_element_type=F32) + b_ref[0]


def _adaln_mod(cvec, w_ada, b_ada):
    n_layers, d, n6 = w_ada.shape
    rows = cvec.shape[0]
    tn = n6 // 4
    return pl.pallas_call(
        _mod_body,
        grid=(n_layers, n6 // tn),
        in_specs=[
            pl.BlockSpec((rows, d), lambda l, j: (0, 0)),
            pl.BlockSpec((1, d, tn), lambda l, j: (l, 0, j)),
            pl.BlockSpec((1, 1, tn), lambda l, j: (l, 0, j)),
        ],
        out_specs=pl.BlockSpec((1, rows, tn), lambda l, j: (l, 0, j)),
        out_shape=jax.ShapeDtypeStruct((n_layers, rows, n6), F32),
        compiler_params=_params(2),
        name="adaln_mod",
    )(cvec, w_ada, b_ada.reshape(n_layers, 1, n6))


def _rope_tables(seq_len):
    pos = jnp.arange(seq_len)
    row = (pos // GRID_W).astype(F32)
    col = (pos % GRID_W).astype(F32)
    half = HEAD_DIM // 2
    quarter = half // 2
    inv = ROPE_BASE ** (-(jnp.arange(quarter, dtype=F32) * 2.0 / half))
    j = jnp.arange(LANES) % HEAD_DIM
    is_col = j >= half
    second = (j % half) >= quarter
    freq = inv[j % quarter]
    ang = jnp.where(is_col[None, :], col[:, None], row[:, None]) * freq[None, :]
    cos, sin = jnp.cos(ang), jnp.sin(ang)
    sin_prev = jnp.where(second[None, :], sin, 0.0)
    sin_next = jnp.where(second[None, :], 0.0, -sin)
    return cos, sin_prev, sin_next


def _norm_modulate_store(x_ref, mod_ref, g_ref, h_ref, shift_row, scale_row):
    y = _rms(x_ref[...]) * g_ref[...]
    h = y * (1.0 + mod_ref[scale_row:scale_row + 1, :]) + mod_ref[shift_row:shift_row + 1, :]
    h_ref[...] = h.astype(BF16)


def _inproj_full_body(*refs, rope):
    if rope:
        (x_ref, mod_ref, g_ref, w_ref, cos_ref, sp_ref, sn_ref,
         q_ref, kd_ref, vd_ref, rx_ref, ry_ref, pu_ref, gt_ref, h_ref) = refs
    else:
        (x_ref, mod_ref, g_ref, w_ref,
         q_ref, kd_ref, vd_ref, rx_ref, ry_ref, pu_ref, gt_ref, h_ref) = refs
    _norm_modulate_store(x_ref, mod_ref, g_ref, h_ref, 0, 1)

    def proj(c0, c1):
        return jnp.dot(h_ref[...], w_ref[:, c0:c1], preferred_element_type=F32)

    quarter = HEAD_DIM // 4
    n_q = ATTN_W // LANES
    for c in range(n_q + 2):
        pc = proj(c * LANES, (c + 1) * LANES)
        if rope:
            pc = (pc * cos_ref[...]
                  + pltpu.roll(pc, quarter, 1) * sp_ref[...]
                  + pltpu.roll(pc, LANES - quarter, 1) * sn_ref[...])
        if c < n_q:
            q_ref[:, c * LANES:(c + 1) * LANES] = (pc * (HEAD_DIM ** -0.5 * LOG2E)).astype(BF16)
        else:
            kd_ref[:, (c - n_q) * LANES:(c - n_q + 1) * LANES] = pc.astype(BF16)
    o = ATTN_W + 2 * KV_W
    vd_ref[...] = proj(o, o + 2 * KV_W).astype(BF16)
    o += 2 * KV_W
    rx_ref[...] = proj(o, o + RNN_W)
    o += RNN_W
    ry_ref[...] = proj(o, o + RNN_W).astype(BF16)
    o += RNN_W
    pu_ref[...] = proj(o, o + POOL_W)
    o += POOL_W
    d = x_ref.shape[1]
    for c in range(N_BRANCH):
        gl = proj(o + c * d, o + (c + 1) * d)
        gt_ref[:, c * d:(c + 1) * d] = jax.nn.sigmoid(gl).astype(BF16)


def _inproj_kv_body(x_ref, mod_ref, g_ref, w_ref, kd_ref, vd_ref, rx_ref, h_ref):
    _norm_modulate_store(x_ref, mod_ref, g_ref, h_ref, 0, 1)
    kd_ref[...] = jnp.dot(h_ref[...], w_ref[:, 0:2 * KV_W], preferred_element_type=F32).astype(BF16)
    vd_ref[...] = jnp.dot(h_ref[...], w_ref[:, 2 * KV_W:4 * KV_W], preferred_element_type=F32).astype(BF16)
    rx_ref[...] = jnp.dot(h_ref[...], w_ref[:, 4 * KV_W:], preferred_element_type=F32)


def _inproj(x2d, mod, mod_row_of_tile, g, w, *, tm, tiles_per_seq, rope_tables, kv_only):
    n_tok, d = x2d.shape
    grid = (n_tok // tm,)
    row = lambda i: (i, 0)
    in_specs = [
        pl.BlockSpec((tm, d), row),
        pl.BlockSpec((None,) + mod.shape[1:], lambda i: (mod_row_of_tile(i), 0, 0)),
        _resident(g.shape),
        _resident(w.shape),
    ]
    args = [x2d, mod, g, w]
    if rope_tables is not None:
        in_specs += [pl.BlockSpec((tm, LANES), lambda i: (i % tiles_per_seq, 0))] * 3
        args += list(rope_tables)

    def out(width, dtype):
        return pl.BlockSpec((tm, width), row), jax.ShapeDtypeStruct((n_tok, width), dtype)

    if kv_only:
        outs = [out(2 * KV_W, BF16), out(2 * KV_W, BF16), out(RNN_W, F32)]
        body = _inproj_kv_body
    else:
        outs = [out(ATTN_W, BF16), out(2 * KV_W, BF16), out(2 * KV_W, BF16), out(RNN_W, F32),
                out(RNN_W, BF16), out(POOL_W, F32), out(N_BRANCH * d, BF16)]
        body = functools.partial(_inproj_full_body, rope=rope_tables is not None)
    return pl.pallas_call(
        body,
        grid=grid,
        in_specs=in_specs,
        out_specs=[o[0] for o in outs],
        out_shape=[o[1] for o in outs],
        scratch_shapes=[pltpu.VMEM((tm, d), BF16)],
        compiler_params=_params(1),
        name="inproj_kv" if kv_only else "inproj",
    )(*args)


def _attend(sink_ref, q_blk, k_rows, v_rows, biases):
    tq = q_blk.shape[0]
    heads_per_kv = N_Q_HEADS // N_KV_HEADS
    low = lax.broadcasted_iota(jnp.int32, (tq, LANES), 1) < HEAD_DIM
    zero = jnp.zeros((), BF16)
    outs = []
    for h in range(N_KV_HEADS):
        k_h = k_rows[:, h * LANES:(h + 1) * LANES]
        v_h = v_rows[:, h * LANES:(h + 1) * LANES]
        stacked = []
        for c2 in range(heads_per_kv // 2):
            qc = q_blk[:, (2 * h + c2) * LANES:(2 * h + c2 + 1) * LANES]
            stacked += [jnp.where(low, qc, zero), jnp.where(low, zero, qc)]
        lhs = jnp.concatenate(stacked, axis=0)
        s = lax.dot_general(lhs, k_h, (((1,), (1,)), ((), ())), preferred_element_type=F32)
        blocks = []
        for j, bias in enumerate(biases):
            blk = s[:, j * LANES:(j + 1) * LANES]
            if bias is not None:
                blk = blk + jnp.concatenate([bias] * heads_per_kv, axis=0)
            blocks.append(blk)
        sink_col = jnp.concatenate(
            [jnp.full((tq, 1), sink_ref[h * heads_per_kv + g] * LOG2E, F32) for g in range(heads_per_kv)],
            axis=0)
        m = jnp.maximum(jnp.max(functools.reduce(jnp.maximum, blocks), axis=-1, keepdims=True), sink_col)
        probs = [jnp.exp2(blk - m) for blk in blocks]
        denom = jnp.sum(functools.reduce(jnp.add, probs), axis=-1, keepdims=True) + jnp.exp2(sink_col - m)
        p = jnp.concatenate(probs, axis=1).astype(BF16)
        o = jnp.dot(p, v_h, preferred_element_type=F32) * (1.0 / denom)
        for c2 in range(heads_per_kv // 2):
            outs.append(jnp.where(low, o[(2 * c2) * tq:(2 * c2 + 1) * tq],
                                  o[(2 * c2 + 1) * tq:(2 * c2 + 2) * tq]).astype(BF16))
    return jnp.concatenate(outs, axis=1)


def _attn_local_body(sink_ref, q_ref, kd_ref, vd_ref, kc_ref, vc_ref, o_ref, *, n_blocks):
    blk = ATTN_BLOCK
    ii = lax.broadcasted_iota(jnp.int32, (blk, blk), 0)
    jj = lax.broadcasted_iota(jnp.int32, (blk, blk), 1)
    n_ctx_blocks = kc_ref.shape[0] // LANES

    def body(n, carry):
        rows = lambda i: pl.ds(pl.multiple_of(i * blk, blk), blk)
        prv, nxt = jnp.maximum(n - 1, 0), jnp.minimum(n + 1, n_blocks - 1)
        bias_prev = jnp.where((jj >= ii) & (n > 0), 0.0, NEG_INF).astype(F32)
        bias_next = jnp.where((jj <= ii) & (n < n_blocks - 1), 0.0, NEG_INF).astype(F32)
        k_rows = jnp.concatenate([kd_ref[rows(prv), :], kd_ref[rows(n), :], kd_ref[rows(nxt), :],
                                  kc_ref[...]], axis=0)
        v_rows = jnp.concatenate([vd_ref[rows(prv), :], vd_ref[rows(n), :], vd_ref[rows(nxt), :],
                                  vc_ref[...]], axis=0)
        o_ref[rows(n), :] = _attend(sink_ref, q_ref[rows(n), :], k_rows, v_rows,
                                    [bias_prev, None, bias_next] + [None] * n_ctx_blocks)
        return carry

    lax.fori_loop(0, n_blocks, body, 0)


def _attn_ctx_body(sink_ref, q_ref, kc_ref, vc_ref, o_ref):
    o_ref[...] = _attend(sink_ref, q_ref[...], kc_ref[...], vc_ref[...],
                         [None] * (kc_ref.shape[0] // LANES))


def _attention(sink, q, kd, vd, kdc, vdc):
    b, s, _ = q.shape
    c = kdc.shape[1]
    kw = 2 * KV_W
    idx = lambda bi: (bi, 0, 0)
    return pl.pallas_call(
        functools.partial(_attn_local_body, n_blocks=s // ATTN_BLOCK),
        grid=(b,),
        in_specs=[
            pl.BlockSpec(memory_space=pltpu.SMEM),
            pl.BlockSpec((None, s, ATTN_W), idx),
            pl.BlockSpec((None, s, kw), idx), pl.BlockSpec((None, s, kw), idx),
            pl.BlockSpec((None, c, kw), idx), pl.BlockSpec((None, c, kw), idx),
        ],
        out_specs=pl.BlockSpec((None, s, ATTN_W), idx),
        out_shape=jax.ShapeDtypeStruct((b, s, ATTN_W), BF16),
        compiler_params=_params(1),
        name="attn_local",
    )(sink, q, kd, vd, kdc, vdc)


def _attention_ctx(sink, qc, kdc, vdc):
    b, c, _ = qc.shape
    kw = 2 * KV_W
    idx = lambda bi: (bi, 0, 0)
    return pl.pallas_call(
        _attn_ctx_body,
        grid=(b,),
        in_specs=[
            pl.BlockSpec(memory_space=pltpu.SMEM),
            pl.BlockSpec((None, c, ATTN_W), idx),
            pl.BlockSpec((None, c, kw), idx), pl.BlockSpec((None, c, kw), idx),
        ],
        out_specs=pl.BlockSpec((None, c, ATTN_W), idx),
        out_shape=jax.ShapeDtypeStruct((b, c, ATTN_W), BF16),
        compiler_params=_params(1),
        name="attn_ctx",
    )(sink, qc, kdc, vdc)


def _pool_body(u_ref, w_ref, sc_ref, o_ref, pad_ref, *, tile):
    length = u_ref.shape[0]
    halo = SUBLANES
    pad_ref[0:halo, :] = jnp.zeros((halo, POOL_W), F32)
    pad_ref[halo + length:2 * halo + length, :] = jnp.zeros((halo, POOL_W), F32)
    pad_ref[halo:halo + length, :] = u_ref[...]
    rows = tile + 2 * halo
    for t0 in range(0, length, tile):
        t = t0 + lax.broadcasted_iota(jnp.int32, (tile, POOL_GW), 0)
        for gi, w in enumerate(POOL_WINDOWS):
            lanes = slice(gi * POOL_GW, (gi + 1) * POOL_GW)
            p = pad_ref[t0:t0 + rows, lanes]
            acc = p
            step = 1
            while step < w:
                acc = acc + pltpu.roll(acc, rows - step, 0)
                step *= 2
            back = (w - 1) // 2
            win = (pltpu.roll(acc, back, 0) if back else acc)[halo:halo + tile]
            lo = jnp.maximum(t - back, 0)
            hi = jnp.minimum(t + w // 2 + 1, length)
            mean = win / (hi - lo).astype(F32)
            dlt = (mean - p[halo:halo + tile]).astype(BF16)
            y = jnp.dot(dlt, w_ref[gi], preferred_element_type=F32) * sc_ref[:, lanes]
            o_ref[t0:t0 + tile, lanes] = y.astype(BF16)


def _pool(pu, w_mix, ch_scale):
    b, length, _ = pu.shape
    tile = min(length, 256)
    idx = lambda bi: (bi, 0, 0)
    return pl.pallas_call(
        functools.partial(_pool_body, tile=tile),
        grid=(b,),
        in_specs=[pl.BlockSpec((None, length, POOL_W), idx), _resident(w_mix.shape),
                  _resident(ch_scale.shape)],
        out_specs=pl.BlockSpec((None, length, POOL_W), idx),
        out_shape=jax.ShapeDtypeStruct((b, length, POOL_W), BF16),
        scratch_shapes=[pltpu.VMEM((length + 2 * SUBLANES, POOL_W), F32)],
        compiler_params=_params(1),
        name="pool",
    )(pu, w_mix, ch_scale)


def _lru_body(*refs, n_ctx, n_lat, tile, with_ctx_out):
    if with_ctx_out:
        (rxc_ref, rxl_ref, ryc_ref, ryl_ref, cw_ref, cb_ref, wg_ref, ba_ref, bx_ref, lam_ref,
         ol_ref, oc_ref, pad_ref, xl_ref, hf_ref, hb_ref, xp_ref, a_ref, b_ref) = refs
    else:
        (rxc_ref, rxl_ref, ryl_ref, cw_ref, cb_ref, wg_ref, ba_ref, bx_ref, lam_ref,
         ol_ref, pad_ref, xl_ref, hf_ref, hb_ref, xp_ref, a_ref, b_ref) = refs
    halo = SUBLANES
    n_tot = n_ctx + n_lat
    n_slab = RNN_W // LANES

    zeros = jnp.zeros((halo, RNN_W), F32)
    c0, l0 = halo, 2 * halo + n_ctx
    pad_ref[0:halo, :] = zeros
    pad_ref[c0 + n_ctx:l0, :] = zeros
    pad_ref[l0 + n_lat:l0 + n_lat + halo, :] = zeros
    pad_ref[c0:c0 + n_ctx, :] = rxc_ref[...]
    pad_ref[l0:l0 + n_lat, :] = rxl_ref[...]

    def conv_tile(src0, dst_rows):
        rows = tile + 2 * halo
        p = pad_ref[src0 - halo:src0 + tile + halo, :]
        y = (cb_ref[...] + p[halo:halo + tile] * cw_ref[1:2, :]
             + pltpu.roll(p, 1, 0)[halo:halo + tile] * cw_ref[0:1, :]
             + pltpu.roll(p, rows - 1, 0)[halo:halo + tile] * cw_ref[2:3, :]
             + pltpu.roll(p, rows - 2, 0)[halo:halo + tile] * cw_ref[3:4, :])
        for s in range(n_slab):
            for d0 in dst_rows:
                xl_ref[s, d0:d0 + tile, :] = y[:, s * LANES:(s + 1) * LANES]

    for t0 in range(0, n_ctx, tile):
        conv_tile(c0 + t0, (t0, n_tot + t0))
    for t0 in range(0, n_lat, tile):
        conv_tile(l0 + t0, (n_ctx + t0,))

    lam = lam_ref[...]
    log_sig = jnp.minimum(lam, 0.0) - jnp.log1p(jnp.exp(-jnp.abs(lam)))
    decay = LRU_C * log_sig
    half = RNN_W // 2

    def run_chunk(d, base, carry):
        res_ref = hf_ref if d == 0 else hb_ref
        win0 = 0 if d == 0 else n_ctx
        for v in range(SEG):
            for s in range(n_slab):
                xp_ref[d, v * SUBLANES:(v + 1) * SUBLANES, s * LANES:(s + 1) * LANES] = (
                    xl_ref[s, pl.ds(win0 + base + v, SUBLANES, stride=SEG), :])
        for j in range(2):
            ch = slice(j * half, (j + 1) * half)
            xj = xp_ref[d, :, ch]
            gates = jnp.dot(xj.astype(BF16), wg_ref[d, j], preferred_element_type=F32)
            r_gate = jax.nn.sigmoid(gates[:, :half] + ba_ref[d:d + 1, ch])
            i_gate = jax.nn.sigmoid(gates[:, half:] + bx_ref[d:d + 1, ch])
            a = jnp.exp(r_gate * decay[d:d + 1, ch])
            mult = jnp.sqrt(1.0 - a * a)
            a_ref[d, :, ch] = a
            b_ref[d, :, ch] = mult * i_gate * xj
        order = range(SEG) if d == 0 else range(SEG - 1, -1, -1)
        h = jnp.zeros((SUBLANES, RNN_W), F32)
        acum = jnp.ones((SUBLANES, RNN_W), F32)
        for v in order:
            rows = slice(v * SUBLANES, (v + 1) * SUBLANES)
            av = a_ref[d, rows, :]
            h = av * h + b_ref[d, rows, :]
            acum = av * acum
            b_ref[d, rows, :] = h
            a_ref[d, rows, :] = acum
        seg_in = [None] * SUBLANES
        state = carry
        seg_order = range(SUBLANES) if d == 0 else range(SUBLANES - 1, -1, -1)
        for i in seg_order:
            seg_in[i] = state
            state = h[i:i + 1, :] + acum[i:i + 1, :] * state
        seg_state = jnp.concatenate(seg_in, axis=0)
        for v in range(SEG):
            rows = slice(v * SUBLANES, (v + 1) * SUBLANES)
            hv = b_ref[d, rows, :] + a_ref[d, rows, :] * seg_state
            for s in range(n_slab):
                res_ref[s, pl.ds(base + v, SUBLANES, stride=SEG), :] = hv[:, s * LANES:(s + 1) * LANES]
        return state

    n_chunks = n_tot // CHUNK

    def step(c, carries):
        cf, cb = carries
        cf = run_chunk(0, c * CHUNK, cf)
        cb = run_chunk(1, (n_chunks - 1 - c) * CHUNK, cb)
        return cf, cb

    zero_state = jnp.zeros((1, RNN_W), F32)
    lax.fori_loop(0, n_chunks, step, (zero_state, zero_state))

    def emit(out_ref, ry_ref, f0, b0, length):
        for t0 in range(0, length, tile):
            hsum = jnp.concatenate(
                [hf_ref[s, f0 + t0:f0 + t0 + tile, :] + hb_ref[s, b0 + t0:b0 + t0 + tile, :]
                 for s in range(n_slab)], axis=1)
            gate = jax.nn.gelu(ry_ref[t0:t0 + tile, :].astype(F32))
            out_ref[t0:t0 + tile, :] = (hsum * gate).astype(BF16)

    emit(ol_ref, ryl_ref, n_ctx, 0, n_lat)
    if with_ctx_out:
        emit(oc_ref, ryc_ref, 0, n_lat, n_ctx)


def _lru(rxc, rxl, ryc, ryl, conv_w, conv_b, wg, b_a, b_x, lam):
    b, n_ctx, _ = rxc.shape
    n_lat = rxl.shape[1]
    n_tot = n_ctx + n_lat
    tile = math.gcd(n_ctx, n_lat, 256)
    assert n_tot % CHUNK == 0 and tile % SUBLANES == 0
    with_ctx_out = ryc is not None
    idx = lambda bi: (bi, 0, 0)
    seq = lambda n, dt=None: pl.BlockSpec((None, n, RNN_W), idx)
    n_slab = RNN_W // LANES
    in_specs = [seq(n_ctx), seq(n_lat)] + ([seq(n_ctx)] if with_ctx_out else []) + [seq(n_lat)]
    args = [rxc, rxl] + ([ryc] if with_ctx_out else []) + [ryl]
    for wgt in (conv_w, conv_b, wg, b_a, b_x, lam):
        in_specs.append(_resident(wgt.shape))
        args.append(wgt)
    out_specs = [seq(n_lat)] + ([seq(n_ctx)] if with_ctx_out else [])
    out_shape = [jax.ShapeDtypeStruct((b, n_lat, RNN_W), BF16)]
    if with_ctx_out:
        out_shape.append(jax.ShapeDtypeStruct((b, n_ctx, RNN_W), BF16))
    res = pl.pallas_call(
        functools.partial(_lru_body, n_ctx=n_ctx, n_lat=n_lat, tile=tile, with_ctx_out=with_ctx_out),
        grid=(b,),
        in_specs=in_specs,
        out_specs=out_specs,
        out_shape=out_shape,
        scratch_shapes=[
            pltpu.VMEM((n_tot + 3 * SUBLANES, RNN_W), F32),
            pltpu.VMEM((n_slab, n_tot + n_ctx, LANES), F32),
            pltpu.VMEM((n_slab, n_tot, LANES), F32),
            pltpu.VMEM((n_slab, n_tot, LANES), F32),
            pltpu.VMEM((2, CHUNK, RNN_W), F32),
            pltpu.VMEM((2, CHUNK, RNN_W), F32),
            pltpu.VMEM((2, CHUNK, RNN_W), F32),
        ],
        compiler_params=_params(1),
        name="rglru",
    )(*args)
    return (res[0], res[1]) if with_ctx_out else (res[0], None)


def _merge_body(x_ref, mod_ref, at_ref, po_ref, rn_ref, gt_ref, wa_ref, wp_ref, wr_ref, wo_ref,
                g_ref, o_ref, m_ref):
    d = x_ref.shape[1]
    merged = None
    for k, (br_ref, w_ref) in enumerate(((at_ref, wa_ref), (po_ref, wp_ref), (rn_ref, wr_ref))):
        t = gt_ref[:, k * d:(k + 1) * d].astype(F32) * jnp.dot(
            br_ref[...], w_ref[...], preferred_element_type=F32)
        merged = t if merged is None else merged + t
    m_ref[...] = merged.astype(BF16)
    mix = jnp.dot(m_ref[...], wo_ref[...], preferred_element_type=F32)
    o_ref[...] = x_ref[...] + mod_ref[2:3, :] * (_rms(mix) * g_ref[...])


def _merge(x2d, mod, mod_row_of_tile, attn, pool, rnn, gates, wa, wp, wr, wo, g_post, *, tm):
    n_tok, d = x2d.shape
    row = lambda i: (i, 0)
    tok = lambda arr: pl.BlockSpec((tm, arr.shape[1]), row)
    return pl.pallas_call(
        _merge_body,
        grid=(n_tok // tm,),
        in_specs=[tok(x2d),
                  pl.BlockSpec((None,) + mod.shape[1:], lambda i: (mod_row_of_tile(i), 0, 0)),
                  tok(attn), tok(pool), tok(rnn), tok(gates),
                  _resident(wa.shape), _resident(wp.shape), _resident(wr.shape), _resident(wo.shape),
                  _resident(g_post.shape)],
        out_specs=pl.BlockSpec((tm, d), row),
        out_shape=jax.ShapeDtypeStruct((n_tok, d), F32),
        scratch_shapes=[pltpu.VMEM((tm, d), BF16)],
        compiler_params=_params(1),
        name="merge",
    )(x2d, mod, attn, pool, rnn, gates, wa, wp, wr, wo, g_post)


def _ffn_body(x_ref, mod_ref, gpre_ref, wgu_ref, wd_ref, gpost_ref, o_ref, h_ref, *, d_ff, chunk):
    _norm_modulate_store(x_ref, mod_ref, gpre_ref, h_ref, 3, 4)
    acc = None
    for c0 in range(0, d_ff, chunk):
        c1 = min(c0 + chunk, d_ff)
        gate = jnp.dot(h_ref[...], wgu_ref[:, c0:c1], preferred_element_type=F32)
        up = jnp.dot(h_ref[...], wgu_ref[:, d_ff + c0:d_ff + c1], preferred_element_type=F32)
        act = ((gate * jax.nn.sigmoid(gate)) * up).astype(BF16)
        part = jnp.dot(act, wd_ref[c0:c1, :], preferred_element_type=F32)
        acc = part if acc is None else acc + part
    o_ref[...] = x_ref[...] + mod_ref[5:6, :] * (_rms(acc) * gpost_ref[...])


def _ffn(x2d, mod, mod_row_of_tile, g_pre, w_gu, w_down, g_post, *, tm):
    n_tok, d = x2d.shape
    d_ff = w_down.shape[0]
    row = lambda i: (i, 0)
    return pl.pallas_call(
        functools.partial(_ffn_body, d_ff=d_ff, chunk=1024),
        grid=(n_tok // tm,),
        in_specs=[pl.BlockSpec((tm, d), row),
                  pl.BlockSpec((None,) + mod.shape[1:], lambda i: (mod_row_of_tile(i), 0, 0)),
                  _resident(g_pre.shape), _resident(w_gu.shape), _resident(w_down.shape),
                  _resident(g_post.shape)],
        out_specs=pl.BlockSpec((tm, d), row),
        out_shape=jax.ShapeDtypeStruct((n_tok, d), F32),
        scratch_shapes=[pltpu.VMEM((tm, d), BF16)],
        compiler_params=_params(1),
        name="ffn",
    )(x2d, mod, g_pre, w_gu, w_down, g_post)


def _dup_heads(w):
    d = w.shape[0]
    return jnp.repeat(w.reshape(d, N_KV_HEADS, 1, HEAD_DIM), 2, axis=2).reshape(d, 2 * KV_W)


def _pack_gate_weights(w_a, w_x):
    def block_diag(w):
        eye = jnp.eye(RNN_BLOCKS, dtype=w.dtype)
        return jnp.einsum('dhij,hg->dhigj', w, eye).reshape(w.shape[0], RNN_W, RNN_W)
    bd_a, bd_x = block_diag(w_a), block_diag(w_x)
    half = RNN_W // 2
    halves = []
    for j in range(2):
        sl = slice(j * half, (j + 1) * half)
        halves.append(jnp.concatenate([bd_a[:, sl, sl], bd_x[:, sl, sl]], axis=2))
    return jnp.stack(halves, axis=1).astype(BF16)


def kernel(x, c, ctx, c_ctx, w_ada, b_ada, g_pre_mix, g_post_mix, g_pre_ffn, g_post_ffn, w_in, attn_sink, w_attn_o, pool_mix, pool_scale, w_pool_o, conv_w, conv_b, lru_w_a, lru_b_a, lru_w_x, lru_b_x, lru_lambda, w_rnn_o, w_out, w_gu, w_down):
    bsz, seq, d = x.shape
    n_ctx = ctx.shape[1]
    depth = w_ada.shape[0]
    tm = min(512, seq)
    tm_ctx = min(tm, n_ctx)

    q0, k0, v0 = 0, ATTN_W, ATTN_W + KV_W
    rx0 = v0 + KV_W
    ry0 = rx0 + RNN_W

    mod_rows = -(-(bsz + 1) // SUBLANES) * SUBLANES
    cvec = jnp.zeros((mod_rows, d), F32).at[:bsz].set(c).at[bsz].set(c_ctx)
    mod_all = _adaln_mod(cvec, w_ada, b_ada).reshape(depth, mod_rows, 6, d)

    rope = _rope_tables(seq)
    lat_row = lambda i: i // (seq // tm)
    ctx_row = lambda i: bsz

    x2 = x.reshape(bsz * seq, d)
    c2 = ctx.reshape(bsz * n_ctx, d)
    row2 = lambda a: a.reshape(1, -1)
    for l in range(depth):
        need_ctx = l < depth - 1
        mod = mod_all[l]
        wl = w_in[l]
        w_kvx = jnp.concatenate([_dup_heads(wl[:, k0:v0]), _dup_heads(wl[:, v0:rx0]), wl[:, rx0:ry0]],
                                axis=1).astype(BF16)
        w_full = jnp.concatenate([wl[:, q0:k0].astype(BF16), w_kvx, wl[:, ry0:].astype(BF16)], axis=1)
        g_pre = row2(g_pre_mix[l])

        q, kd, vd, rx, ry, pu, gt = _inproj(
            x2, mod, lat_row, g_pre, w_full, tm=tm, tiles_per_seq=seq // tm, rope_tables=rope,
            kv_only=False)
        if need_ctx:
            qc, kdc, vdc, rxc, ryc, puc, gtc = _inproj(
                c2, mod, ctx_row, g_pre, w_full, tm=tm_ctx, tiles_per_seq=1, rope_tables=None,
                kv_only=False)
        else:
            kdc, vdc, rxc = _inproj(
                c2, mod, ctx_row, g_pre, w_kvx, tm=tm_ctx, tiles_per_seq=1, rope_tables=None,
                kv_only=True)
            ryc = None

        r3 = lambda a, n: a.reshape(bsz, n, a.shape[-1])
        sink = attn_sink[l]
        kdc3, vdc3 = r3(kdc, n_ctx), r3(vdc, n_ctx)
        attn_l = _attention(sink, r3(q, seq), r3(kd, seq), r3(vd, seq), kdc3, vdc3)

        w_mix = pool_mix[l].astype(BF16)
        p_scale = row2(pool_scale[l])
        pool_l = _pool(r3(pu, seq), w_mix, p_scale)

        wg = _pack_gate_weights(lru_w_a[l], lru_w_x[l])
        rnn_l, rnn_c = _lru(r3(rxc, n_ctx), r3(rx, seq), None if ryc is None else r3(ryc, n_ctx),
                            r3(ry, seq), conv_w[l], row2(conv_b[l]), wg, lru_b_a[l], lru_b_x[l],
                            lru_lambda[l])

        wa, wp = w_attn_o[l].astype(BF16), w_pool_o[l].astype(BF16)
        wr, wo = w_rnn_o[l].astype(BF16), w_out[l].astype(BF16)
        wgu, wdn = w_gu[l].astype(BF16), w_down[l].astype(BF16)
        g_post = row2(g_post_mix[l])
        gf_pre, gf_post = row2(g_pre_ffn[l]), row2(g_post_ffn[l])

        flat = lambda a: a.reshape(-1, a.shape[-1])
        x2 = _merge(x2, mod, lat_row, flat(attn_l), flat(pool_l), flat(rnn_l), gt, wa, wp, wr, wo,
                    g_post, tm=tm)
        x2 = _ffn(x2, mod, lat_row, gf_pre, wgu, wdn, gf_post, tm=tm)

        if need_ctx:
            attn_c = _attention_ctx(sink, r3(qc, n_ctx), kdc3, vdc3)
            pool_c = _pool(r3(puc, n_ctx), w_mix, p_scale)
            c2 = _merge(c2, mod, ctx_row, flat(attn_c), flat(pool_c), flat(rnn_c), gtc, wa, wp, wr,
                        wo, g_post, tm=tm_ctx)
            c2 = _ffn(c2, mod, ctx_row, gf_pre, wgu, wdn, gf_post, tm=tm_ctx)
    return x2.reshape(bsz, seq, d)
```

```python
import functools
import math

import jax
import jax.numpy as jnp
from jax import lax
from jax.experimental import pallas as pl
from jax.experimental.pallas import tpu as pltpu

F32 = jnp.float32
BF16 = jnp.bfloat16

GRID_W = 64
HEAD_DIM = 64
N_Q_HEADS = 8
N_KV_HEADS = 2
ATTN_W = N_Q_HEADS * HEAD_DIM
KV_W = N_KV_HEADS * HEAD_DIM
ATTN_BLOCK = 128
ROPE_BASE = 10000.0
POOL_WINDOWS = (2, 4, 8, 16)
POOL_W = 512
POOL_GW = POOL_W // len(POOL_WINDOWS)
RNN_W = 512
RNN_BLOCKS = 8
CONV_W = 4
LRU_C = 8.0
N_BRANCH = 3
EPS = 1e-6
NEG_INF = -1e30
LOG2E = 1.4426950408889634

LANES = 128
SUBLANES = 8
VMEM_LIMIT_BYTES = 56 * 1024 * 1024

SEG = 36
CHUNK = SUBLANES * SEG


def _params(n_axes):
    return pltpu.CompilerParams(
        dimension_semantics=("arbitrary",) * n_axes, vmem_limit_bytes=VMEM_LIMIT_BYTES)


def _resident(shape):
    nd = len(shape)
    return pl.BlockSpec(shape, lambda *_: (0,) * nd, pipeline_mode=pl.Buffered(1))


def _rms(xf):
    return xf * lax.rsqrt(jnp.mean(xf * xf, axis=-1, keepdims=True) + EPS)


def _mod_body(c_ref, w_ref, b_ref, o_ref):
    c = c_ref[...]
    s = (c * jax.nn.sigmoid(c)).astype(BF16)
    o_ref[0] = jnp.dot(s, w_ref[0].astype(BF16), preferred_element_type=F32) + b_ref[0]


def _adaln_mod(cvec, w_ada, b_ada):
    n_layers, d, n6 = w_ada.shape
    rows = cvec.shape[0]
    tn = n6 // 4
    return pl.pallas_call(
        _mod_body,
        grid=(n_layers, n6 // tn),
        in_specs=[
            pl.BlockSpec((rows, d), lambda l, j: (0, 0)),
            pl.BlockSpec((1, d, tn), lambda l, j: (l, 0, j)),
            pl.BlockSpec((1, 1, tn), lambda l, j: (l, 0, j)),
        ],
        out_specs=pl.BlockSpec((1, rows, tn), lambda l, j: (l, 0, j)),
        out_shape=jax.ShapeDtypeStruct((n_layers, rows, n6), F32),
        compiler_params=_params(2),
        name="adaln_mod",
    )(cvec, w_ada, b_ada.reshape(n_layers, 1, n6))


def _rope_tables(seq_len):
    pos = jnp.arange(seq_len)
    row = (pos // GRID_W).astype(F32)
    col = (pos % GRID_W).astype(F32)
    half = HEAD_DIM // 2
    quarter = half // 2
    inv = ROPE_BASE ** (-(jnp.arange(quarter, dtype=F32) * 2.0 / half))
    j = jnp.arange(LANES) % HEAD_DIM
    is_col = j >= half
    second = (j % half) >= quarter
    freq = inv[j % quarter]
    ang = jnp.where(is_col[None, :], col[:, None], row[:, None]) * freq[None, :]
    cos, sin = jnp.cos(ang), jnp.sin(ang)
    sin_prev = jnp.where(second[None, :], sin, 0.0)
    sin_next = jnp.where(second[None, :], 0.0, -sin)
    return cos, sin_prev, sin_next


def _row_halves(n_rows):
    if n_rows % (2 * SUBLANES * 2) or n_rows < 256:
        return [slice(0, n_rows)]
    return [slice(0, n_rows // 2), slice(n_rows // 2, n_rows)]


def _norm_modulate_store(x_ref, mod_ref, g_ref, h_ref, shift_row, scale_row, rows):
    y = _rms(x_ref[rows, :]) * g_ref[...]
    h = y * (1.0 + mod_ref[scale_row:scale_row + 1, :]) + mod_ref[shift_row:shift_row + 1, :]
    h_ref[rows, :] = h.astype(BF16)


def _inproj_full_body(*refs, rope):
    if rope:
        (x_ref, mod_ref, g_ref, w_ref, cos_ref, sp_ref, sn_ref,
         q_ref, kd_ref, vd_ref, rx_ref, ry_ref, pu_ref, gt_ref, h_ref) = refs
    else:
        (x_ref, mod_ref, g_ref, w_ref,
         q_ref, kd_ref, vd_ref, rx_ref, ry_ref, pu_ref, gt_ref, h_ref) = refs
    quarter = HEAD_DIM // 4
    d = x_ref.shape[1]
    for rows in _row_halves(x_ref.shape[0]):
        _norm_modulate_store(x_ref, mod_ref, g_ref, h_ref, 0, 1, rows)

        def proj(c0, c1):
            return jnp.dot(h_ref[rows, :], w_ref[:, c0:c1], preferred_element_type=F32)

        for c in range((ATTN_W + 2 * KV_W) // (2 * LANES)):
            pair = proj(c * 2 * LANES, (c + 1) * 2 * LANES)
            for half in range(2):
                pc = pair[:, half * LANES:(half + 1) * LANES]
                if rope:
                    pc = (pc * cos_ref[rows, :]
                          + pltpu.roll(pc, quarter, 1) * sp_ref[rows, :]
                          + pltpu.roll(pc, LANES - quarter, 1) * sn_ref[rows, :])
                col = (2 * c + half) * LANES
                if col < ATTN_W:
                    q_ref[rows, col:col + LANES] = (pc * (HEAD_DIM ** -0.5 * LOG2E)).astype(BF16)
                else:
                    kd_ref[rows, col - ATTN_W:col - ATTN_W + LANES] = pc.astype(BF16)
        o = ATTN_W + 2 * KV_W
        vd_ref[rows, :] = proj(o, o + 2 * KV_W).astype(BF16)
        o += 2 * KV_W
        rx_ref[rows, :] = proj(o, o + RNN_W)
        o += RNN_W
        ry_ref[rows, :] = jax.nn.gelu(proj(o, o + RNN_W)).astype(BF16)
        o += RNN_W
        pu_ref[rows, :] = proj(o, o + POOL_W)
        o += POOL_W
        for c in range(N_BRANCH):
            gl = proj(o + c * d, o + (c + 1) * d)
            gt_ref[rows, c * d:(c + 1) * d] = jax.nn.sigmoid(gl).astype(BF16)


def _inproj_kv_body(x_ref, mod_ref, g_ref, w_ref, kd_ref, vd_ref, rx_ref, h_ref):
    _norm_modulate_store(x_ref, mod_ref, g_ref, h_ref, 0, 1, slice(0, x_ref.shape[0]))
    kd_ref[...] = jnp.dot(h_ref[...], w_ref[:, 0:2 * KV_W], preferred_element_type=F32).astype(BF16)
    vd_ref[...] = jnp.dot(h_ref[...], w_ref[:, 2 * KV_W:4 * KV_W], preferred_element_type=F32).astype(BF16)
    rx_ref[...] = jnp.dot(h_ref[...], w_ref[:, 4 * KV_W:], preferred_element_type=F32)


def _inproj(x2d, mod, mod_row_of_tile, g, w, *, tm, tiles_per_seq, rope_tables, kv_only):
    n_tok, d = x2d.shape
    grid = (n_tok // tm,)
    row = lambda i: (i, 0)
    in_specs = [
        pl.BlockSpec((tm, d), row),
        pl.BlockSpec((None,) + mod.shape[1:], lambda i: (mod_row_of_tile(i), 0, 0)),
        _resident(g.shape),
        _resident(w.shape),
    ]
    args = [x2d, mod, g, w]
    if rope_tables is not None:
        in_specs += [pl.BlockSpec((tm, LANES), lambda i: (i % tiles_per_seq, 0))] * 3
        args += list(rope_tables)

    def out(width, dtype):
        return pl.BlockSpec((tm, width), row), jax.ShapeDtypeStruct((n_tok, width), dtype)

    if kv_only:
        outs = [out(2 * KV_W, BF16), out(2 * KV_W, BF16), out(RNN_W, F32)]
        body = _inproj_kv_body
    else:
        outs = [out(ATTN_W, BF16), out(2 * KV_W, BF16), out(2 * KV_W, BF16), out(RNN_W, F32),
                out(RNN_W, BF16), out(POOL_W, F32), out(N_BRANCH * d, BF16)]
        body = functools.partial(_inproj_full_body, rope=rope_tables is not None)
    return pl.pallas_call(
        body,
        grid=grid,
        in_specs=in_specs,
        out_specs=[o[0] for o in outs],
        out_shape=[o[1] for o in outs],
        scratch_shapes=[pltpu.VMEM((tm, d), BF16)],
        compiler_params=_params(1),
        name="inproj_kv" if kv_only else "inproj",
    )(*args)


def _attend(sink_ref, q_blk, k_rows, v_rows, biases):
    tq = q_blk.shape[0]
    heads_per_kv = N_Q_HEADS // N_KV_HEADS
    low = lax.broadcasted_iota(jnp.int32, (tq, LANES), 1) < HEAD_DIM
    zero = jnp.zeros((), BF16)
    outs = []
    for h in range(N_KV_HEADS):
        k_h = k_rows[:, h * LANES:(h + 1) * LANES]
        v_h = v_rows[:, h * LANES:(h + 1) * LANES]
        stacked = []
        for c2 in range(heads_per_kv // 2):
            qc = q_blk[:, (2 * h + c2) * LANES:(2 * h + c2 + 1) * LANES]
            stacked += [jnp.where(low, qc, zero), jnp.where(low, zero, qc)]
        lhs = jnp.concatenate(stacked, axis=0)
        s = lax.dot_general(lhs, k_h, (((1,), (1,)), ((), ())), preferred_element_type=F32)
        blocks = []
        for j, bias in enumerate(biases):
            blk = s[:, j * LANES:(j + 1) * LANES]
            if bias is not None:
                blk = blk + jnp.concatenate([bias] * heads_per_kv, axis=0)
            blocks.append(blk)
        sink_col = jnp.concatenate(
            [jnp.full((tq, 1), sink_ref[h * heads_per_kv + g] * LOG2E, F32) for g in range(heads_per_kv)],
            axis=0)
        m = jnp.maximum(jnp.max(functools.reduce(jnp.maximum, blocks), axis=-1, keepdims=True), sink_col)
        probs = [jnp.exp2(blk - m) for blk in blocks]
        denom = jnp.sum(functools.reduce(jnp.add, probs), axis=-1, keepdims=True) + jnp.exp2(sink_col - m)
        p = jnp.concatenate(probs, axis=1).astype(BF16)
        o = jnp.dot(p, v_h, preferred_element_type=F32) * (1.0 / denom)
        for c2 in range(heads_per_kv // 2):
            outs.append(jnp.where(low, o[(2 * c2) * tq:(2 * c2 + 1) * tq],
                                  o[(2 * c2 + 1) * tq:(2 * c2 + 2) * tq]).astype(BF16))
    return jnp.concatenate(outs, axis=1)


def _attn_local_body(sink_ref, q_ref, kd_ref, vd_ref, kc_ref, vc_ref, o_ref, *, n_blocks):
    blk = ATTN_BLOCK
    ii = lax.broadcasted_iota(jnp.int32, (blk, blk), 0)
    jj = lax.broadcasted_iota(jnp.int32, (blk, blk), 1)
    n_ctx_blocks = kc_ref.shape[0] // LANES

    def body(n, carry):
        rows = lambda i: pl.ds(pl.multiple_of(i * blk, blk), blk)
        prv, nxt = jnp.maximum(n - 1, 0), jnp.minimum(n + 1, n_blocks - 1)
        bias_prev = jnp.where((jj >= ii) & (n > 0), 0.0, NEG_INF).astype(F32)
        bias_next = jnp.where((jj <= ii) & (n < n_blocks - 1), 0.0, NEG_INF).astype(F32)
        k_rows = jnp.concatenate([kd_ref[rows(prv), :], kd_ref[rows(n), :], kd_ref[rows(nxt), :],
                                  kc_ref[...]], axis=0)
        v_rows = jnp.concatenate([vd_ref[rows(prv), :], vd_ref[rows(n), :], vd_ref[rows(nxt), :],
                                  vc_ref[...]], axis=0)
        o_ref[rows(n), :] = _attend(sink_ref, q_ref[rows(n), :], k_rows, v_rows,
                                    [bias_prev, None, bias_next] + [None] * n_ctx_blocks)
        return carry

    lax.fori_loop(0, n_blocks, body, 0)


def _attn_ctx_body(sink_ref, q_ref, kc_ref, vc_ref, o_ref):
    o_ref[...] = _attend(sink_ref, q_ref[...], kc_ref[...], vc_ref[...],
                         [None] * (kc_ref.shape[0] // LANES))


def _attention(sink, q, kd, vd, kdc, vdc):
    b, s, _ = q.shape
    c = kdc.shape[1]
    kw = 2 * KV_W
    idx = lambda bi: (bi, 0, 0)
    return pl.pallas_call(
        functools.partial(_attn_local_body, n_blocks=s // ATTN_BLOCK),
        grid=(b,),
        in_specs=[
            pl.BlockSpec(memory_space=pltpu.SMEM),
            pl.BlockSpec((None, s, ATTN_W), idx),
            pl.BlockSpec((None, s, kw), idx), pl.BlockSpec((None, s, kw), idx),
            pl.BlockSpec((None, c, kw), idx), pl.BlockSpec((None, c, kw), idx),
        ],
        out_specs=pl.BlockSpec((None, s, ATTN_W), idx),
        out_shape=jax.ShapeDtypeStruct((b, s, ATTN_W), BF16),
        compiler_params=_params(1),
        name="attn_local",
    )(sink, q, kd, vd, kdc, vdc)


def _attention_ctx(sink, qc, kdc, vdc):
    b, c, _ = qc.shape
    kw = 2 * KV_W
    idx = lambda bi: (bi, 0, 0)
    return pl.pallas_call(
        _attn_ctx_body,
        grid=(b,),
        in_specs=[
            pl.BlockSpec(memory_space=pltpu.SMEM),
            pl.BlockSpec((None, c, ATTN_W), idx),
            pl.BlockSpec((None, c, kw), idx), pl.BlockSpec((None, c, kw), idx),
        ],
        out_specs=pl.BlockSpec((None, c, ATTN_W), idx),
        out_shape=jax.ShapeDtypeStruct((b, c, ATTN_W), BF16),
        compiler_params=_params(1),
        name="attn_ctx",
    )(sink, qc, kdc, vdc)


def _pool_body(u_ref, w_ref, sc_ref, o_ref, pad_ref, *, tile):
    length = u_ref.shape[0]
    halo = SUBLANES
    pad_ref[0:halo, :] = jnp.zeros((halo, POOL_W), F32)
    pad_ref[halo + length:2 * halo + length, :] = jnp.zeros((halo, POOL_W), F32)
    pad_ref[halo:halo + length, :] = u_ref[...]
    rows = tile + 2 * halo
    for t0 in range(0, length, tile):
        t = t0 + lax.broadcasted_iota(jnp.int32, (tile, POOL_GW), 0)
        for gi, w in enumerate(POOL_WINDOWS):
            lanes = slice(gi * POOL_GW, (gi + 1) * POOL_GW)
            p = pad_ref[t0:t0 + rows, lanes]
            acc = p
            step = 1
            while step < w:
                acc = acc + pltpu.roll(acc, rows - step, 0)
                step *= 2
            back = (w - 1) // 2
            win = (pltpu.roll(acc, back, 0) if back else acc)[halo:halo + tile]
            lo = jnp.maximum(t - back, 0)
            hi = jnp.minimum(t + w // 2 + 1, length)
            mean = win / (hi - lo).astype(F32)
            dlt = (mean - p[halo:halo + tile]).astype(BF16)
            y = jnp.dot(dlt, w_ref[gi], preferred_element_type=F32) * sc_ref[:, lanes]
            o_ref[t0:t0 + tile, lanes] = y.astype(BF16)


def _pool(pu, w_mix, ch_scale):
    b, length, _ = pu.shape
    tile = min(length, 256)
    idx = lambda bi: (bi, 0, 0)
    return pl.pallas_call(
        functools.partial(_pool_body, tile=tile),
        grid=(b,),
        in_specs=[pl.BlockSpec((None, length, POOL_W), idx), _resident(w_mix.shape),
                  _resident(ch_scale.shape)],
        out_specs=pl.BlockSpec((None, length, POOL_W), idx),
        out_shape=jax.ShapeDtypeStruct((b, length, POOL_W), BF16),
        scratch_shapes=[pltpu.VMEM((length + 2 * SUBLANES, POOL_W), F32)],
        compiler_params=_params(1),
        name="pool",
    )(pu, w_mix, ch_scale)


def _lru_body(*refs, n_ctx, n_lat, tile, with_ctx_out):
    if with_ctx_out:
        (rxc_ref, rxl_ref, ryc_ref, ryl_ref, cw_ref, cb_ref, wg_ref, ba_ref, bx_ref, lam_ref,
         ol_ref, oc_ref, pad_ref, xl_ref, hf_ref, hb_ref, xp_ref, a_ref, b_ref) = refs
    else:
        (rxc_ref, rxl_ref, ryl_ref, cw_ref, cb_ref, wg_ref, ba_ref, bx_ref, lam_ref,
         ol_ref, pad_ref, xl_ref, hf_ref, hb_ref, xp_ref, a_ref, b_ref) = refs
    halo = SUBLANES
    n_tot = n_ctx + n_lat
    n_slab = RNN_W // LANES

    zeros = jnp.zeros((halo, RNN_W), F32)
    c0, l0 = halo, 2 * halo + n_ctx
    pad_ref[0:halo, :] = zeros
    pad_ref[c0 + n_ctx:l0, :] = zeros
    pad_ref[l0 + n_lat:l0 + n_lat + halo, :] = zeros
    pad_ref[c0:c0 + n_ctx, :] = rxc_ref[...]
    pad_ref[l0:l0 + n_lat, :] = rxl_ref[...]

    def conv_tile(src0, dst_rows):
        rows = tile + 2 * halo
        p = pad_ref[src0 - halo:src0 + tile + halo, :]
        y = (cb_ref[...] + p[halo:halo + tile] * cw_ref[1:2, :]
             + pltpu.roll(p, 1, 0)[halo:halo + tile] * cw_ref[0:1, :]
             + pltpu.roll(p, rows - 1, 0)[halo:halo + tile] * cw_ref[2:3, :]
             + pltpu.roll(p, rows - 2, 0)[halo:halo + tile] * cw_ref[3:4, :])
        for s in range(n_slab):
            for d0 in dst_rows:
                xl_ref[s, d0:d0 + tile, :] = y[:, s * LANES:(s + 1) * LANES]

    for t0 in range(0, n_ctx, tile):
        conv_tile(c0 + t0, (t0, n_tot + t0))
    for t0 in range(0, n_lat, tile):
        conv_tile(l0 + t0, (n_ctx + t0,))

    lam = lam_ref[...]
    log_sig = jnp.minimum(lam, 0.0) - jnp.log1p(jnp.exp(-jnp.abs(lam)))
    decay = LRU_C * log_sig
    half = RNN_W // 2

    def run_chunk(d, base, carry):
        res_ref = hf_ref if d == 0 else hb_ref
        win0 = 0 if d == 0 else n_ctx
        for v in range(SEG):
            for s in range(n_slab):
                xp_ref[d, v * SUBLANES:(v + 1) * SUBLANES, s * LANES:(s + 1) * LANES] = (
                    xl_ref[s, pl.ds(win0 + base + v, SUBLANES, stride=SEG), :])
        for j in range(2):
            ch = slice(j * half, (j + 1) * half)
            xj = xp_ref[d, :, ch]
            gates = jnp.dot(xj.astype(BF16), wg_ref[d, j], preferred_element_type=F32)
            r_gate = jax.nn.sigmoid(gates[:, :half] + ba_ref[d:d + 1, ch])
            i_gate = jax.nn.sigmoid(gates[:, half:] + bx_ref[d:d + 1, ch])
            a = jnp.exp(r_gate * decay[d:d + 1, ch])
            mult = jnp.sqrt(1.0 - a * a)
            a_ref[d, :, ch] = a
            b_ref[d, :, ch] = mult * i_gate * xj
        order = range(SEG) if d == 0 else range(SEG - 1, -1, -1)
        h = jnp.zeros((SUBLANES, RNN_W), F32)
        acum = jnp.ones((SUBLANES, RNN_W), F32)
        for v in order:
            rows = slice(v * SUBLANES, (v + 1) * SUBLANES)
            av = a_ref[d, rows, :]
            h = av * h + b_ref[d, rows, :]
            acum = av * acum
            b_ref[d, rows, :] = h
            a_ref[d, rows, :] = acum
        seg_in = [None] * SUBLANES
        state = carry
        seg_order = range(SUBLANES) if d == 0 else range(SUBLANES - 1, -1, -1)
        for i in seg_order:
            seg_in[i] = state
            state = h[i:i + 1, :] + acum[i:i + 1, :] * state
        seg_state = jnp.concatenate(seg_in, axis=0)
        for v in range(SEG):
            rows = slice(v * SUBLANES, (v + 1) * SUBLANES)
            hv = b_ref[d, rows, :] + a_ref[d, rows, :] * seg_state
            for s in range(n_slab):
                res_ref[s, pl.ds(base + v, SUBLANES, stride=SEG), :] = hv[:, s * LANES:(s + 1) * LANES]
        return state

    n_chunks = n_tot // CHUNK

    def step(c, carries):
        cf, cb = carries
        cf = run_chunk(0, c * CHUNK, cf)
        cb = run_chunk(1, (n_chunks - 1 - c) * CHUNK, cb)
        return cf, cb

    zero_state = jnp.zeros((1, RNN_W), F32)
    lax.fori_loop(0, n_chunks, step, (zero_state, zero_state))

    def emit(out_ref, ry_ref, f0, b0, length):
        for t0 in range(0, length, tile):
            hsum = jnp.concatenate(
                [hf_ref[s, f0 + t0:f0 + t0 + tile, :] + hb_ref[s, b0 + t0:b0 + t0 + tile, :]
                 for s in range(n_slab)], axis=1)
            gate = ry_ref[t0:t0 + tile, :].astype(F32)
            out_ref[t0:t0 + tile, :] = (hsum * gate).astype(BF16)

    emit(ol_ref, ryl_ref, n_ctx, 0, n_lat)
    if with_ctx_out:
        emit(oc_ref, ryc_ref, 0, n_lat, n_ctx)


def _lru(rxc, rxl, ryc, ryl, conv_w, conv_b, wg, b_a, b_x, lam):
    b, n_ctx, _ = rxc.shape
    n_lat = rxl.shape[1]
    n_tot = n_ctx + n_lat
    tile = math.gcd(n_ctx, n_lat, 256)
    assert n_tot % CHUNK == 0 and tile % SUBLANES == 0
    with_ctx_out = ryc is not None
    idx = lambda bi: (bi, 0, 0)
    seq = lambda n, dt=None: pl.BlockSpec((None, n, RNN_W), idx)
    n_slab = RNN_W // LANES
    in_specs = [seq(n_ctx), seq(n_lat)] + ([seq(n_ctx)] if with_ctx_out else []) + [seq(n_lat)]
    args = [rxc, rxl] + ([ryc] if with_ctx_out else []) + [ryl]
    for wgt in (conv_w, conv_b, wg, b_a, b_x, lam):
        in_specs.append(_resident(wgt.shape))
        args.append(wgt)
    out_specs = [seq(n_lat)] + ([seq(n_ctx)] if with_ctx_out else [])
    out_shape = [jax.ShapeDtypeStruct((b, n_lat, RNN_W), BF16)]
    if with_ctx_out:
        out_shape.append(jax.ShapeDtypeStruct((b, n_ctx, RNN_W), BF16))
    res = pl.pallas_call(
        functools.partial(_lru_body, n_ctx=n_ctx, n_lat=n_lat, tile=tile, with_ctx_out=with_ctx_out),
        grid=(b,),
        in_specs=in_specs,
        out_specs=out_specs,
        out_shape=out_shape,
        scratch_shapes=[
            pltpu.VMEM((n_tot + 3 * SUBLANES, RNN_W), F32),
            pltpu.VMEM((n_slab, n_tot + n_ctx, LANES), F32),
            pltpu.VMEM((n_slab, n_tot, LANES), F32),
            pltpu.VMEM((n_slab, n_tot, LANES), F32),
            pltpu.VMEM((2, CHUNK, RNN_W), F32),
            pltpu.VMEM((2, CHUNK, RNN_W), F32),
            pltpu.VMEM((2, CHUNK, RNN_W), F32),
        ],
        compiler_params=_params(1),
        name="rglru",
    )(*args)
    return (res[0], res[1]) if with_ctx_out else (res[0], None)


def _merge_body(x_ref, mod_ref, at_ref, po_ref, rn_ref, gt_ref, wa_ref, wp_ref, wr_ref, wo_ref,
                g_ref, o_ref, m_ref):
    d = x_ref.shape[1]
    for rows in _row_halves(x_ref.shape[0]):
        merged = None
        for k, (br_ref, w_ref) in enumerate(((at_ref, wa_ref), (po_ref, wp_ref), (rn_ref, wr_ref))):
            t = gt_ref[rows, k * d:(k + 1) * d].astype(F32) * jnp.dot(
                br_ref[rows, :], w_ref[...], preferred_element_type=F32)
            merged = t if merged is None else merged + t
        m_ref[rows, :] = merged.astype(BF16)
        mix = jnp.dot(m_ref[rows, :], wo_ref[...], preferred_element_type=F32)
        o_ref[rows, :] = x_ref[rows, :] + mod_ref[2:3, :] * (_rms(mix) * g_ref[...])


def _merge(x2d, mod, mod_row_of_tile, attn, pool, rnn, gates, wa, wp, wr, wo, g_post, *, tm):
    n_tok, d = x2d.shape
    row = lambda i: (i, 0)
    tok = lambda arr: pl.BlockSpec((tm, arr.shape[1]), row)
    return pl.pallas_call(
        _merge_body,
        grid=(n_tok // tm,),
        in_specs=[tok(x2d),
                  pl.BlockSpec((None,) + mod.shape[1:], lambda i: (mod_row_of_tile(i), 0, 0)),
                  tok(attn), tok(pool), tok(rnn), tok(gates),
                  _resident(wa.shape), _resident(wp.shape), _resident(wr.shape), _resident(wo.shape),
                  _resident(g_post.shape)],
        out_specs=pl.BlockSpec((tm, d), row),
        out_shape=jax.ShapeDtypeStruct((n_tok, d), F32),
        scratch_shapes=[pltpu.VMEM((tm, d), BF16)],
        compiler_params=_params(1),
        name="merge",
    )(x2d, mod, attn, pool, rnn, gates, wa, wp, wr, wo, g_post)


def _ffn_body(x_ref, mod_ref, gpre_ref, wgu_ref, wd_ref, gpost_ref, o_ref, h_ref, *, d_ff, chunk):
    for rows in _row_halves(x_ref.shape[0]):
        _norm_modulate_store(x_ref, mod_ref, gpre_ref, h_ref, 3, 4, rows)
        acc = None
        for c0 in range(0, d_ff, chunk):
            c1 = min(c0 + chunk, d_ff)
            gate = jnp.dot(h_ref[rows, :], wgu_ref[:, c0:c1], preferred_element_type=F32)
            up = jnp.dot(h_ref[rows, :], wgu_ref[:, d_ff + c0:d_ff + c1], preferred_element_type=F32)
            act = ((gate * jax.nn.sigmoid(gate)) * up).astype(BF16)
            part = jnp.dot(act, wd_ref[c0:c1, :], preferred_element_type=F32)
            acc = part if acc is None else acc + part
        o_ref[rows, :] = x_ref[rows, :] + mod_ref[5:6, :] * (_rms(acc) * gpost_ref[...])


def _ffn(x2d, mod, mod_row_of_tile, g_pre, w_gu, w_down, g_post, *, tm):
    n_tok, d = x2d.shape
    d_ff = w_down.shape[0]
    row = lambda i: (i, 0)
    return pl.pallas_call(
        functools.partial(_ffn_body, d_ff=d_ff, chunk=1024),
        grid=(n_tok // tm,),
        in_specs=[pl.BlockSpec((tm, d), row),
                  pl.BlockSpec((None,) + mod.shape[1:], lambda i: (mod_row_of_tile(i), 0, 0)),
                  _resident(g_pre.shape), _resident(w_gu.shape), _resident(w_down.shape),
                  _resident(g_post.shape)],
        out_specs=pl.BlockSpec((tm, d), row),
        out_shape=jax.ShapeDtypeStruct((n_tok, d), F32),
        scratch_shapes=[pltpu.VMEM((tm, d), BF16)],
        compiler_params=_params(1),
        name="ffn",
    )(x2d, mod, g_pre, w_gu, w_down, g_post)


def _dup_heads(w):
    d = w.shape[0]
    return jnp.repeat(w.reshape(d, N_KV_HEADS, 1, HEAD_DIM), 2, axis=2).reshape(d, 2 * KV_W)


def _pack_gate_weights(w_a, w_x):
    def block_diag(w):
        eye = jnp.eye(RNN_BLOCKS, dtype=w.dtype)
        return jnp.einsum('dhij,hg->dhigj', w, eye).reshape(w.shape[0], RNN_W, RNN_W)
    bd_a, bd_x = block_diag(w_a), block_diag(w_x)
    half = RNN_W // 2
    halves = []
    for j in range(2):
        sl = slice(j * half, (j + 1) * half)
        halves.append(jnp.concatenate([bd_a[:, sl, sl], bd_x[:, sl, sl]], axis=2))
    return jnp.stack(halves, axis=1).astype(BF16)


def kernel(x, c, ctx, c_ctx, w_ada, b_ada, g_pre_mix, g_post_mix, g_pre_ffn, g_post_ffn, w_in, attn_sink, w_attn_o, pool_mix, pool_scale, w_pool_o, conv_w, conv_b, lru_w_a, lru_b_a, lru_w_x, lru_b_x, lru_lambda, w_rnn_o, w_out, w_gu, w_down):
    bsz, seq, d = x.shape
    n_ctx = ctx.shape[1]
    depth = w_ada.shape[0]
    tm = min(512, seq)
    tm_ctx = min(tm, n_ctx)

    q0, k0, v0 = 0, ATTN_W, ATTN_W + KV_W
    rx0 = v0 + KV_W
    ry0 = rx0 + RNN_W

    mod_rows = -(-(bsz + 1) // SUBLANES) * SUBLANES
    cvec = jnp.zeros((mod_rows, d), F32).at[:bsz].set(c).at[bsz].set(c_ctx)
    mod_all = _adaln_mod(cvec, w_ada, b_ada).reshape(depth, mod_rows, 6, d)

    rope = _rope_tables(seq)
    lat_row = lambda i: i // (seq // tm)
    ctx_row = lambda i: bsz

    x2 = x.reshape(bsz * seq, d)
    c2 = ctx.reshape(bsz * n_ctx, d)
    row2 = lambda a: a.reshape(1, -1)
    for l in range(depth):
        need_ctx = l < depth - 1
        mod = mod_all[l]
        wl = w_in[l]
        w_kvx = jnp.concatenate([_dup_heads(wl[:, k0:v0]), _dup_heads(wl[:, v0:rx0]), wl[:, rx0:ry0]],
                                axis=1).astype(BF16)
        w_full = jnp.concatenate([wl[:, q0:k0].astype(BF16), w_kvx, wl[:, ry0:].astype(BF16)], axis=1)
        g_pre = row2(g_pre_mix[l])

        q, kd, vd, rx, ry, pu, gt = _inproj(
            x2, mod, lat_row, g_pre, w_full, tm=tm, tiles_per_seq=seq // tm, rope_tables=rope,
            kv_only=False)
        if need_ctx:
            qc, kdc, vdc, rxc, ryc, puc, gtc = _inproj(
                c2, mod, ctx_row, g_pre, w_full, tm=tm_ctx, tiles_per_seq=1, rope_tables=None,
                kv_only=False)
        else:
            kdc, vdc, rxc = _inproj(
                c2, mod, ctx_row, g_pre, w_kvx, tm=tm_ctx, tiles_per_seq=1, rope_tables=None,
                kv_only=True)
            ryc = None

        r3 = lambda a, n: a.reshape(bsz, n, a.shape[-1])
        sink = attn_sink[l]
        kdc3, vdc3 = r3(kdc, n_ctx), r3(vdc, n_ctx)
        attn_l = _attention(sink, r3(q, seq), r3(kd, seq), r3(vd, seq), kdc3, vdc3)

        w_mix = pool_mix[l].astype(BF16)
        p_scale = row2(pool_scale[l])
        pool_l = _pool(r3(pu, seq), w_mix, p_scale)

        wg = _pack_gate_weights(lru_w_a[l], lru_w_x[l])
        rnn_l, rnn_c = _lru(r3(rxc, n_ctx), r3(rx, seq), None if ryc is None else r3(ryc, n_ctx),
                            r3(ry, seq), conv_w[l], row2(conv_b[l]), wg, lru_b_a[l], lru_b_x[l],
                            lru_lambda[l])

        wa, wp = w_attn_o[l].astype(BF16), w_pool_o[l].astype(BF16)
        wr, wo = w_rnn_o[l].astype(BF16), w_out[l].astype(BF16)
        wgu, wdn = w_gu[l].astype(BF16), w_down[l].astype(BF16)
        g_post = row2(g_post_mix[l])
        gf_pre, gf_post = row2(g_pre_ffn[l]), row2(g_post_ffn[l])

        flat = lambda a: a.reshape(-1, a.shape[-1])
        x2 = _merge(x2, mod, lat_row, flat(attn_l), flat(pool_l), flat(rnn_l), gt, wa, wp, wr, wo,
                    g_post, tm=tm)
        x2 = _ffn(x2, mod, lat_row, gf_pre, wgu, wdn, gf_post, tm=tm)

        if need_ctx:
            attn_c = _attention_ctx(sink, r3(qc, n_ctx), kdc3, vdc3)
            pool_c = _pool(r3(puc, n_ctx), w_mix, p_scale)
            c2 = _merge(c2, mod, ctx_row, flat(attn_c), flat(pool_c), flat(rnn_c), gtc, wa, wp, wr,
                        wo, g_post, tm=tm_ctx)
            c2 = _ffn(c2, mod, ctx_row, gf_pre, wgu, wdn, gf_post, tm=tm_ctx)
    return x2.reshape(bsz, seq, d)
```

```python
import functools
import math

import jax
import jax.numpy as jnp
import numpy as np
from jax import lax
from jax.experimental import pallas as pl
from jax.experimental.pallas import tpu as pltpu

F32 = jnp.float32
BF16 = jnp.bfloat16

GRID_W = 64
HEAD_DIM = 64
N_Q_HEADS = 8
N_KV_HEADS = 2
ATTN_W = N_Q_HEADS * HEAD_DIM
KV_W = N_KV_HEADS * HEAD_DIM
ATTN_BLOCK = 128
ROPE_BASE = 10000.0
POOL_WINDOWS = (2, 4, 8, 16)
POOL_W = 512
POOL_GW = POOL_W // len(POOL_WINDOWS)
RNN_W = 512
RNN_BLOCKS = 8
CONV_W = 4
LRU_C = 8.0
N_BRANCH = 3
EPS = 1e-6
NEG_INF = -1e30
LOG2E = 1.4426950408889634

LANES = 128
SUBLANES = 8
VMEM_LIMIT_BYTES = 56 * 1024 * 1024

SEG = 36
CHUNK = SUBLANES * SEG


def _params(n_axes):
    return pltpu.CompilerParams(
        dimension_semantics=("arbitrary",) * n_axes, vmem_limit_bytes=VMEM_LIMIT_BYTES)


def _layer_block(arr, layer, block=None):
    block = tuple(arr.shape[1:]) if block is None else block
    index = (layer,) + (0,) * len(block)
    return pl.BlockSpec((None,) + block, lambda *_: index, pipeline_mode=pl.Buffered(1))


def _mod_block(mod_all, layer, mod_row_of_tile):
    return pl.BlockSpec((None, None) + mod_all.shape[2:], lambda i: (layer, mod_row_of_tile(i), 0, 0))


def _rms(xf):
    return xf * lax.rsqrt(jnp.mean(xf * xf, axis=-1, keepdims=True) + EPS)


def _mod_body(c_ref, w_ref, b_ref, o_ref):
    c = c_ref[...]
    s = (c * jax.nn.sigmoid(c)).astype(BF16)
    o_ref[0] = jnp.dot(s, w_ref[0].astype(BF16), preferred_element_type=F32) + b_ref[0]


def _adaln_mod(cvec, w_ada, b_ada):
    n_layers, d, n6 = w_ada.shape
    rows = cvec.shape[0]
    tn = n6 // 4
    return pl.pallas_call(
        _mod_body,
        grid=(n_layers, n6 // tn),
        in_specs=[
            pl.BlockSpec((rows, d), lambda l, j: (0, 0)),
            pl.BlockSpec((1, d, tn), lambda l, j: (l, 0, j)),
            pl.BlockSpec((1, 1, tn), lambda l, j: (l, 0, j)),
        ],
        out_specs=pl.BlockSpec((1, rows, tn), lambda l, j: (l, 0, j)),
        out_shape=jax.ShapeDtypeStruct((n_layers, rows, n6), F32),
        compiler_params=_params(2),
        name="adaln_mod",
    )(cvec, w_ada, b_ada.reshape(n_layers, 1, n6))


def _rope_tables(seq_len):
    pos = np.arange(seq_len)
    row = (pos // GRID_W).astype(np.float32)
    col = (pos % GRID_W).astype(np.float32)
    half = HEAD_DIM // 2
    quarter = half // 2
    inv = (np.float32(ROPE_BASE) ** (-(np.arange(quarter, dtype=np.float32) * np.float32(2.0 / half)))
           ).astype(np.float32)
    j = np.arange(LANES) % HEAD_DIM
    is_col = j >= half
    second = (j % half) >= quarter
    freq = inv[j % quarter]
    ang = (np.where(is_col[None, :], col[:, None], row[:, None]) * freq[None, :]).astype(np.float32)
    cos, sin = np.cos(ang), np.sin(ang)
    sin_prev = np.where(second[None, :], sin, 0.0)
    sin_next = np.where(second[None, :], 0.0, -sin)
    return tuple(jnp.asarray(t, F32) for t in (cos, sin_prev, sin_next))


def _row_halves(n_rows):
    if n_rows % (2 * SUBLANES * 2) or n_rows < 256:
        return [slice(0, n_rows)]
    return [slice(0, n_rows // 2), slice(n_rows // 2, n_rows)]


def _store_dup_heads(dst_ref, rows, kv):
    low = lax.broadcasted_iota(jnp.int32, kv.shape, 1) < HEAD_DIM
    swapped = pltpu.roll(kv, HEAD_DIM, 1)
    dst_ref[rows, 0:LANES] = jnp.where(low, kv, swapped).astype(BF16)
    dst_ref[rows, LANES:2 * LANES] = jnp.where(low, swapped, kv).astype(BF16)


def _norm_modulate_store(x_ref, mod_ref, g_ref, h_ref, shift_row, scale_row, rows):
    y = _rms(x_ref[rows, :]) * g_ref[...]
    h = y * (1.0 + mod_ref[scale_row:scale_row + 1, :]) + mod_ref[shift_row:shift_row + 1, :]
    h_ref[rows, :] = h.astype(BF16)


def _inproj_full_body(*refs, rope):
    if rope:
        (x_ref, mod_ref, g_ref, w_ref, cos_ref, sp_ref, sn_ref,
         q_ref, kd_ref, vd_ref, rx_ref, ry_ref, pu_ref, gt_ref, h_ref) = refs
    else:
        (x_ref, mod_ref, g_ref, w_ref,
         q_ref, kd_ref, vd_ref, rx_ref, ry_ref, pu_ref, gt_ref, h_ref) = refs
    quarter = HEAD_DIM // 4
    d = x_ref.shape[1]
    for rows in _row_halves(x_ref.shape[0]):
        _norm_modulate_store(x_ref, mod_ref, g_ref, h_ref, 0, 1, rows)

        def proj(c0, c1):
            return jnp.dot(h_ref[rows, :], w_ref[:, c0:c1], preferred_element_type=F32)

        def rotary(pc):
            if not rope:
                return pc
            return (pc * cos_ref[rows, :]
                    + pltpu.roll(pc, quarter, 1) * sp_ref[rows, :]
                    + pltpu.roll(pc, LANES - quarter, 1) * sn_ref[rows, :])

        for c in range(ATTN_W // (2 * LANES)):
            pair = proj(c * 2 * LANES, (c + 1) * 2 * LANES)
            for half in range(2):
                col = (2 * c + half) * LANES
                pc = rotary(pair[:, half * LANES:(half + 1) * LANES])
                q_ref[rows, col:col + LANES] = (pc * (HEAD_DIM ** -0.5 * LOG2E)).astype(BF16)
        o = ATTN_W
        kv = proj(o, o + 2 * KV_W)
        _store_dup_heads(kd_ref, rows, rotary(kv[:, :KV_W]))
        _store_dup_heads(vd_ref, rows, kv[:, KV_W:])
        o += 2 * KV_W
        rx_ref[rows, :] = proj(o, o + RNN_W)
        o += RNN_W
        ry_ref[rows, :] = jax.nn.gelu(proj(o, o + RNN_W)).astype(BF16)
        o += RNN_W
        pu_ref[rows, :] = proj(o, o + POOL_W)
        o += POOL_W
        for c in range(N_BRANCH):
            gl = proj(o + c * d, o + (c + 1) * d)
            gt_ref[rows, c * d:(c + 1) * d] = jax.nn.sigmoid(gl).astype(BF16)


def _inproj_kv_body(x_ref, mod_ref, g_ref, w_ref, kd_ref, vd_ref, rx_ref, h_ref):
    rows = slice(0, x_ref.shape[0])
    _norm_modulate_store(x_ref, mod_ref, g_ref, h_ref, 0, 1, rows)
    kv = jnp.dot(h_ref[...], w_ref[:, ATTN_W:ATTN_W + 2 * KV_W], preferred_element_type=F32)
    _store_dup_heads(kd_ref, rows, kv[:, :KV_W])
    _store_dup_heads(vd_ref, rows, kv[:, KV_W:])
    rx_ref[...] = jnp.dot(h_ref[...], w_ref[:, ATTN_W + 2 * KV_W:ATTN_W + 2 * KV_W + RNN_W],
                          preferred_element_type=F32)


def _inproj(x2d, mod_all, layer, mod_row_of_tile, g, w, *, tm, tiles_per_seq, rope_tables, kv_only):
    n_tok, d = x2d.shape
    grid = (n_tok // tm,)
    row = lambda i: (i, 0)
    w_cols = ATTN_W + 2 * KV_W + RNN_W if kv_only else w.shape[2]
    in_specs = [
        pl.BlockSpec((tm, d), row),
        _mod_block(mod_all, layer, mod_row_of_tile),
        _layer_block(g, layer),
        _layer_block(w, layer, (d, w_cols)),
    ]
    args = [x2d, mod_all, g, w]
    if rope_tables is not None:
        in_specs += [pl.BlockSpec((tm, LANES), lambda i: (i % tiles_per_seq, 0))] * 3
        args += list(rope_tables)

    def out(width, dtype):
        return pl.BlockSpec((tm, width), row), jax.ShapeDtypeStruct((n_tok, width), dtype)

    if kv_only:
        outs = [out(2 * KV_W, BF16), out(2 * KV_W, BF16), out(RNN_W, F32)]
        body = _inproj_kv_body
    else:
        outs = [out(ATTN_W, BF16), out(2 * KV_W, BF16), out(2 * KV_W, BF16), out(RNN_W, F32),
                out(RNN_W, BF16), out(POOL_W, F32), out(N_BRANCH * d, BF16)]
        body = functools.partial(_inproj_full_body, rope=rope_tables is not None)
    return pl.pallas_call(
        body,
        grid=grid,
        in_specs=in_specs,
        out_specs=[o[0] for o in outs],
        out_shape=[o[1] for o in outs],
        scratch_shapes=[pltpu.VMEM((tm, d), BF16)],
        compiler_params=_params(1),
        name="inproj_kv" if kv_only else "inproj",
    )(*args)


def _attend(sink_ref, layer, q_blk, k_rows, v_rows, biases):
    tq = q_blk.shape[0]
    heads_per_kv = N_Q_HEADS // N_KV_HEADS
    low = lax.broadcasted_iota(jnp.int32, (tq, LANES), 1) < HEAD_DIM
    zero = jnp.zeros((), BF16)
    outs = []
    for h in range(N_KV_HEADS):
        k_h = k_rows[:, h * LANES:(h + 1) * LANES]
        v_h = v_rows[:, h * LANES:(h + 1) * LANES]
        stacked = []
        for c2 in range(heads_per_kv // 2):
            qc = q_blk[:, (2 * h + c2) * LANES:(2 * h + c2 + 1) * LANES]
            stacked += [jnp.where(low, qc, zero), jnp.where(low, zero, qc)]
        lhs = jnp.concatenate(stacked, axis=0)
        s = lax.dot_general(lhs, k_h, (((1,), (1,)), ((), ())), preferred_element_type=F32)
        blocks = []
        for j, bias in enumerate(biases):
            blk = s[:, j * LANES:(j + 1) * LANES]
            if bias is not None:
                blk = blk + jnp.concatenate([bias] * heads_per_kv, axis=0)
            blocks.append(blk)
        sink_col = jnp.concatenate(
            [jnp.full((tq, 1), sink_ref[layer, h * heads_per_kv + g] * LOG2E, F32)
             for g in range(heads_per_kv)],
            axis=0)
        m = jnp.maximum(jnp.max(functools.reduce(jnp.maximum, blocks), axis=-1, keepdims=True), sink_col)
        probs = [jnp.exp2(blk - m) for blk in blocks]
        denom = jnp.sum(functools.reduce(jnp.add, probs), axis=-1, keepdims=True) + jnp.exp2(sink_col - m)
        p = jnp.concatenate(probs, axis=1).astype(BF16)
        o = jnp.dot(p, v_h, preferred_element_type=F32) * (1.0 / denom)
        for c2 in range(heads_per_kv // 2):
            outs.append(jnp.where(low, o[(2 * c2) * tq:(2 * c2 + 1) * tq],
                                  o[(2 * c2 + 1) * tq:(2 * c2 + 2) * tq]).astype(BF16))
    return jnp.concatenate(outs, axis=1)


def _attn_local_body(sink_ref, q_ref, kd_ref, vd_ref, kc_ref, vc_ref, o_ref, *, layer, n_blocks):
    blk = ATTN_BLOCK
    ii = lax.broadcasted_iota(jnp.int32, (blk, blk), 0)
    jj = lax.broadcasted_iota(jnp.int32, (blk, blk), 1)
    n_ctx_blocks = kc_ref.shape[0] // LANES

    def body(n, carry):
        rows = lambda i: pl.ds(pl.multiple_of(i * blk, blk), blk)
        prv, nxt = jnp.maximum(n - 1, 0), jnp.minimum(n + 1, n_blocks - 1)
        bias_prev = jnp.where((jj >= ii) & (n > 0), 0.0, NEG_INF).astype(F32)
        bias_next = jnp.where((jj <= ii) & (n < n_blocks - 1), 0.0, NEG_INF).astype(F32)
        k_rows = jnp.concatenate([kd_ref[rows(prv), :], kd_ref[rows(n), :], kd_ref[rows(nxt), :],
                                  kc_ref[...]], axis=0)
        v_rows = jnp.concatenate([vd_ref[rows(prv), :], vd_ref[rows(n), :], vd_ref[rows(nxt), :],
                                  vc_ref[...]], axis=0)
        o_ref[rows(n), :] = _attend(sink_ref, layer, q_ref[rows(n), :], k_rows, v_rows,
                                    [bias_prev, None, bias_next] + [None] * n_ctx_blocks)
        return carry

    lax.fori_loop(0, n_blocks, body, 0)


def _attn_ctx_body(sink_ref, q_ref, kc_ref, vc_ref, o_ref, *, layer):
    o_ref[...] = _attend(sink_ref, layer, q_ref[...], kc_ref[...], vc_ref[...],
                         [None] * (kc_ref.shape[0] // LANES))


def _attention(sink, layer, q, kd, vd, kdc, vdc):
    b, s, _ = q.shape
    c = kdc.shape[1]
    kw = 2 * KV_W
    idx = lambda bi: (bi, 0, 0)
    return pl.pallas_call(
        functools.partial(_attn_local_body, layer=layer, n_blocks=s // ATTN_BLOCK),
        grid=(b,),
        in_specs=[
            pl.BlockSpec(memory_space=pltpu.SMEM),
            pl.BlockSpec((None, s, ATTN_W), idx),
            pl.BlockSpec((None, s, kw), idx), pl.BlockSpec((None, s, kw), idx),
            pl.BlockSpec((None, c, kw), idx), pl.BlockSpec((None, c, kw), idx),
        ],
        out_specs=pl.BlockSpec((None, s, ATTN_W), idx),
        out_shape=jax.ShapeDtypeStruct((b, s, ATTN_W), BF16),
        compiler_params=_params(1),
        name="attn_local",
    )(sink, q, kd, vd, kdc, vdc)


def _attention_ctx(sink, layer, qc, kdc, vdc):
    b, c, _ = qc.shape
    kw = 2 * KV_W
    idx = lambda bi: (bi, 0, 0)
    return pl.pallas_call(
        functools.partial(_attn_ctx_body, layer=layer),
        grid=(b,),
        in_specs=[
            pl.BlockSpec(memory_space=pltpu.SMEM),
            pl.BlockSpec((None, c, ATTN_W), idx),
            pl.BlockSpec((None, c, kw), idx), pl.BlockSpec((None, c, kw), idx),
        ],
        out_specs=pl.BlockSpec((None, c, ATTN_W), idx),
        out_shape=jax.ShapeDtypeStruct((b, c, ATTN_W), BF16),
        compiler_params=_params(1),
        name="attn_ctx",
    )(sink, qc, kdc, vdc)


def _pool_body(u_ref, w_ref, sc_ref, o_ref, pad_ref, *, tile):
    length = u_ref.shape[0]
    halo = SUBLANES
    pad_ref[0:halo, :] = jnp.zeros((halo, POOL_W), F32)
    pad_ref[halo + length:2 * halo + length, :] = jnp.zeros((halo, POOL_W), F32)
    pad_ref[halo:halo + length, :] = u_ref[...]
    rows = tile + 2 * halo
    for t0 in range(0, length, tile):
        t = t0 + lax.broadcasted_iota(jnp.int32, (tile, POOL_GW), 0)
        for gi, w in enumerate(POOL_WINDOWS):
            lanes = slice(gi * POOL_GW, (gi + 1) * POOL_GW)
            p = pad_ref[t0:t0 + rows, lanes]
            acc = p
            step = 1
            while step < w:
                acc = acc + pltpu.roll(acc, rows - step, 0)
                step *= 2
            back = (w - 1) // 2
            win = (pltpu.roll(acc, back, 0) if back else acc)[halo:halo + tile]
            lo = jnp.maximum(t - back, 0)
            hi = jnp.minimum(t + w // 2 + 1, length)
            mean = win / (hi - lo).astype(F32)
            dlt = (mean - p[halo:halo + tile]).astype(BF16)
            y = jnp.dot(dlt, w_ref[gi], preferred_element_type=F32) * sc_ref[:, lanes]
            o_ref[t0:t0 + tile, lanes] = y.astype(BF16)


def _pool(pu, layer, w_mix, ch_scale):
    b, length, _ = pu.shape
    tile = min(length, 256)
    idx = lambda bi: (bi, 0, 0)
    return pl.pallas_call(
        functools.partial(_pool_body, tile=tile),
        grid=(b,),
        in_specs=[pl.BlockSpec((None, length, POOL_W), idx), _layer_block(w_mix, layer),
                  _layer_block(ch_scale, layer)],
        out_specs=pl.BlockSpec((None, length, POOL_W), idx),
        out_shape=jax.ShapeDtypeStruct((b, length, POOL_W), BF16),
        scratch_shapes=[pltpu.VMEM((length + 2 * SUBLANES, POOL_W), F32)],
        compiler_params=_params(1),
        name="pool",
    )(pu, w_mix, ch_scale)


def _lru_body(*refs, n_ctx, n_lat, tile, with_ctx_out):
    if with_ctx_out:
        (rxc_ref, rxl_ref, ryc_ref, ryl_ref, cw_ref, cb_ref, wg_ref, ba_ref, bx_ref, lam_ref,
         ol_ref, oc_ref, pad_ref, xl_ref, hf_ref, hb_ref, xp_ref, a_ref, b_ref) = refs
    else:
        (rxc_ref, rxl_ref, ryl_ref, cw_ref, cb_ref, wg_ref, ba_ref, bx_ref, lam_ref,
         ol_ref, pad_ref, xl_ref, hf_ref, hb_ref, xp_ref, a_ref, b_ref) = refs
    halo = SUBLANES
    n_tot = n_ctx + n_lat
    n_slab = RNN_W // LANES

    zeros = jnp.zeros((halo, RNN_W), F32)
    c0, l0 = halo, 2 * halo + n_ctx
    pad_ref[0:halo, :] = zeros
    pad_ref[c0 + n_ctx:l0, :] = zeros
    pad_ref[l0 + n_lat:l0 + n_lat + halo, :] = zeros
    pad_ref[c0:c0 + n_ctx, :] = rxc_ref[...]
    pad_ref[l0:l0 + n_lat, :] = rxl_ref[...]

    def conv_tile(src0, dst_rows):
        rows = tile + 2 * halo
        p = pad_ref[src0 - halo:src0 + tile + halo, :]
        y = (cb_ref[...] + p[halo:halo + tile] * cw_ref[1:2, :]
             + pltpu.roll(p, 1, 0)[halo:halo + tile] * cw_ref[0:1, :]
             + pltpu.roll(p, rows - 1, 0)[halo:halo + tile] * cw_ref[2:3, :]
             + pltpu.roll(p, rows - 2, 0)[halo:halo + tile] * cw_ref[3:4, :])
        for s in range(n_slab):
            for d0 in dst_rows:
                xl_ref[s, d0:d0 + tile, :] = y[:, s * LANES:(s + 1) * LANES]

    for t0 in range(0, n_ctx, tile):
        conv_tile(c0 + t0, (t0, n_tot + t0))
    for t0 in range(0, n_lat, tile):
        conv_tile(l0 + t0, (n_ctx + t0,))

    lam = lam_ref[...]
    log_sig = jnp.minimum(lam, 0.0) - jnp.log1p(jnp.exp(-jnp.abs(lam)))
    decay = LRU_C * log_sig
    half = RNN_W // 2

    def run_chunk(d, base, carry):
        res_ref = hf_ref if d == 0 else hb_ref
        win0 = 0 if d == 0 else n_ctx
        for v in range(SEG):
            for s in range(n_slab):
                xp_ref[d, v * SUBLANES:(v + 1) * SUBLANES, s * LANES:(s + 1) * LANES] = (
                    xl_ref[s, pl.ds(win0 + base + v, SUBLANES, stride=SEG), :])
        for j in range(2):
            ch = slice(j * half, (j + 1) * half)
            xj = xp_ref[d, :, ch]
            gates = jnp.dot(xj.astype(BF16), wg_ref[d, j], preferred_element_type=F32)
            r_gate = jax.nn.sigmoid(gates[:, :half] + ba_ref[d:d + 1, ch])
            i_gate = jax.nn.sigmoid(gates[:, half:] + bx_ref[d:d + 1, ch])
            a = jnp.exp(r_gate * decay[d:d + 1, ch])
            mult = jnp.sqrt(1.0 - a * a)
            a_ref[d, :, ch] = a
            b_ref[d, :, ch] = mult * i_gate * xj
        order = range(SEG) if d == 0 else range(SEG - 1, -1, -1)
        h = jnp.zeros((SUBLANES, RNN_W), F32)
        acum = jnp.ones((SUBLANES, RNN_W), F32)
        for v in order:
            rows = slice(v * SUBLANES, (v + 1) * SUBLANES)
            av = a_ref[d, rows, :]
            h = av * h + b_ref[d, rows, :]
            acum = av * acum
            b_ref[d, rows, :] = h
            a_ref[d, rows, :] = acum
        seg_in = [None] * SUBLANES
        state = carry
        seg_order = range(SUBLANES) if d == 0 else range(SUBLANES - 1, -1, -1)
        for i in seg_order:
            seg_in[i] = state
            state = h[i:i + 1, :] + acum[i:i + 1, :] * state
        seg_state = jnp.concatenate(seg_in, axis=0)
        for v in range(SEG):
            rows = slice(v * SUBLANES, (v + 1) * SUBLANES)
            hv = b_ref[d, rows, :] + a_ref[d, rows, :] * seg_state
            for s in range(n_slab):
                res_ref[s, pl.ds(base + v, SUBLANES, stride=SEG), :] = hv[:, s * LANES:(s + 1) * LANES]
        return state

    n_chunks = n_tot // CHUNK

    def step(c, carries):
        cf, cb = carries
        cf = run_chunk(0, c * CHUNK, cf)
        cb = run_chunk(1, (n_chunks - 1 - c) * CHUNK, cb)
        return cf, cb

    zero_state = jnp.zeros((1, RNN_W), F32)
    lax.fori_loop(0, n_chunks, step, (zero_state, zero_state))

    def emit(out_ref, ry_ref, f0, b0, length):
        for t0 in range(0, length, tile):
            hsum = jnp.concatenate(
                [hf_ref[s, f0 + t0:f0 + t0 + tile, :] + hb_ref[s, b0 + t0:b0 + t0 + tile, :]
                 for s in range(n_slab)], axis=1)
            gate = ry_ref[t0:t0 + tile, :].astype(F32)
            out_ref[t0:t0 + tile, :] = (hsum * gate).astype(BF16)

    emit(ol_ref, ryl_ref, n_ctx, 0, n_lat)
    if with_ctx_out:
        emit(oc_ref, ryc_ref, 0, n_lat, n_ctx)


def _lru(layer, rxc, rxl, ryc, ryl, conv_w, conv_b, wg, b_a, b_x, lam):
    b, n_ctx, _ = rxc.shape
    n_lat = rxl.shape[1]
    n_tot = n_ctx + n_lat
    tile = math.gcd(n_ctx, n_lat, 256)
    assert n_tot % CHUNK == 0 and tile % SUBLANES == 0
    with_ctx_out = ryc is not None
    idx = lambda bi: (bi, 0, 0)
    seq = lambda n, dt=None: pl.BlockSpec((None, n, RNN_W), idx)
    n_slab = RNN_W // LANES
    in_specs = [seq(n_ctx), seq(n_lat)] + ([seq(n_ctx)] if with_ctx_out else []) + [seq(n_lat)]
    args = [rxc, rxl] + ([ryc] if with_ctx_out else []) + [ryl]
    for wgt in (conv_w, conv_b, wg, b_a, b_x, lam):
        in_specs.append(_layer_block(wgt, layer))
        args.append(wgt)
    out_specs = [seq(n_lat)] + ([seq(n_ctx)] if with_ctx_out else [])
    out_shape = [jax.ShapeDtypeStruct((b, n_lat, RNN_W), BF16)]
    if with_ctx_out:
        out_shape.append(jax.ShapeDtypeStruct((b, n_ctx, RNN_W), BF16))
    res = pl.pallas_call(
        functools.partial(_lru_body, n_ctx=n_ctx, n_lat=n_lat, tile=tile, with_ctx_out=with_ctx_out),
        grid=(b,),
        in_specs=in_specs,
        out_specs=out_specs,
        out_shape=out_shape,
        scratch_shapes=[
            pltpu.VMEM((n_tot + 3 * SUBLANES, RNN_W), F32),
            pltpu.VMEM((n_slab, n_tot + n_ctx, LANES), F32),
            pltpu.VMEM((n_slab, n_tot, LANES), F32),
            pltpu.VMEM((n_slab, n_tot, LANES), F32),
            pltpu.VMEM((2, CHUNK, RNN_W), F32),
            pltpu.VMEM((2, CHUNK, RNN_W), F32),
            pltpu.VMEM((2, CHUNK, RNN_W), F32),
        ],
        compiler_params=_params(1),
        name="rglru",
    )(*args)
    return (res[0], res[1]) if with_ctx_out else (res[0], None)


def _merge_body(x_ref, mod_ref, at_ref, po_ref, rn_ref, gt_ref, wa_ref, wp_ref, wr_ref, wo_ref,
                g_ref, o_ref, m_ref):
    d = x_ref.shape[1]
    for rows in _row_halves(x_ref.shape[0]):
        merged = None
        for k, (br_ref, w_ref) in enumerate(((at_ref, wa_ref), (po_ref, wp_ref), (rn_ref, wr_ref))):
            t = gt_ref[rows, k * d:(k + 1) * d].astype(F32) * jnp.dot(
                br_ref[rows, :], w_ref[...], preferred_element_type=F32)
            merged = t if merged is None else merged + t
        m_ref[rows, :] = merged.astype(BF16)
        mix = jnp.dot(m_ref[rows, :], wo_ref[...], preferred_element_type=F32)
        o_ref[rows, :] = x_ref[rows, :] + mod_ref[2:3, :] * (_rms(mix) * g_ref[...])


def _merge(x2d, mod_all, layer, mod_row_of_tile, attn, pool, rnn, gates, wa, wp, wr, wo, g_post, *, tm):
    n_tok, d = x2d.shape
    row = lambda i: (i, 0)
    tok = lambda arr: pl.BlockSpec((tm, arr.shape[1]), row)
    return pl.pallas_call(
        _merge_body,
        grid=(n_tok // tm,),
        in_specs=[tok(x2d),
                  _mod_block(mod_all, layer, mod_row_of_tile),
                  tok(attn), tok(pool), tok(rnn), tok(gates),
                  _layer_block(wa, layer), _layer_block(wp, layer), _layer_block(wr, layer),
                  _layer_block(wo, layer), _layer_block(g_post, layer)],
        out_specs=pl.BlockSpec((tm, d), row),
        out_shape=jax.ShapeDtypeStruct((n_tok, d), F32),
        scratch_shapes=[pltpu.VMEM((tm, d), BF16)],
        compiler_params=_params(1),
        name="merge",
    )(x2d, mod_all, attn, pool, rnn, gates, wa, wp, wr, wo, g_post)


def _ffn_body(x_ref, mod_ref, gpre_ref, wgu_ref, wd_ref, gpost_ref, o_ref, h_ref, *, d_ff, chunk):
    for rows in _row_halves(x_ref.shape[0]):
        _norm_modulate_store(x_ref, mod_ref, gpre_ref, h_ref, 3, 4, rows)
        acc = None
        for c0 in range(0, d_ff, chunk):
            c1 = min(c0 + chunk, d_ff)
            gate = jnp.dot(h_ref[rows, :], wgu_ref[:, c0:c1], preferred_element_type=F32)
            up = jnp.dot(h_ref[rows, :], wgu_ref[:, d_ff + c0:d_ff + c1], preferred_element_type=F32)
            act = ((gate * jax.nn.sigmoid(gate)) * up).astype(BF16)
            part = jnp.dot(act, wd_ref[c0:c1, :], preferred_element_type=F32)
            acc = part if acc is None else acc + part
        o_ref[rows, :] = x_ref[rows, :] + mod_ref[5:6, :] * (_rms(acc) * gpost_ref[...])


def _ffn(x2d, mod_all, layer, mod_row_of_tile, g_pre, w_gu, w_down, g_post, *, tm):
    n_tok, d = x2d.shape
    d_ff = w_down.shape[1]
    row = lambda i: (i, 0)
    return pl.pallas_call(
        functools.partial(_ffn_body, d_ff=d_ff, chunk=1024),
        grid=(n_tok // tm,),
        in_specs=[pl.BlockSpec((tm, d), row),
                  _mod_block(mod_all, layer, mod_row_of_tile),
                  _layer_block(g_pre, layer), _layer_block(w_gu, layer), _layer_block(w_down, layer),
                  _layer_block(g_post, layer)],
        out_specs=pl.BlockSpec((tm, d), row),
        out_shape=jax.ShapeDtypeStruct((n_tok, d), F32),
        scratch_shapes=[pltpu.VMEM((tm, d), BF16)],
        compiler_params=_params(1),
        name="ffn",
    )(x2d, mod_all, g_pre, w_gu, w_down, g_post)


def _pack_gate_weights(w_a, w_x):
    def block_diag(w):
        eye = jnp.eye(RNN_BLOCKS, dtype=w.dtype)
        return jnp.einsum('ldhij,hg->ldhigj', w, eye).reshape(w.shape[:2] + (RNN_W, RNN_W))
    bd_a, bd_x = block_diag(w_a), block_diag(w_x)
    half = RNN_W // 2
    halves = []
    for j in range(2):
        sl = slice(j * half, (j + 1) * half)
        halves.append(jnp.concatenate([bd_a[:, :, sl, sl], bd_x[:, :, sl, sl]], axis=3))
    return jnp.stack(halves, axis=2).astype(BF16)


def kernel(x, c, ctx, c_ctx, w_ada, b_ada, g_pre_mix, g_post_mix, g_pre_ffn, g_post_ffn, w_in, attn_sink, w_attn_o, pool_mix, pool_scale, w_pool_o, conv_w, conv_b, lru_w_a, lru_b_a, lru_w_x, lru_b_x, lru_lambda, w_rnn_o, w_out, w_gu, w_down):
    bsz, seq, d = x.shape
    n_ctx = ctx.shape[1]
    depth = w_ada.shape[0]
    tm = min(512, seq)
    tm_ctx = min(tm, n_ctx)

    mod_rows = -(-(bsz + 1) // SUBLANES) * SUBLANES
    cvec = jnp.zeros((mod_rows, d), F32).at[:bsz].set(c).at[bsz].set(c_ctx)
    mod_all = _adaln_mod(cvec, w_ada, b_ada).reshape(depth, mod_rows, 6, d)

    rope = _rope_tables(seq)
    lat_row = lambda i: i // (seq // tm)
    ctx_row = lambda i: bsz

    rowvec = lambda a: a.reshape(depth, 1, -1)
    w_in_b, w_gu_b, w_dn_b = w_in.astype(BF16), w_gu.astype(BF16), w_down.astype(BF16)
    wa, wp, wr, wo = (w.astype(BF16) for w in (w_attn_o, w_pool_o, w_rnn_o, w_out))
    w_mix = pool_mix.astype(BF16)
    wg = _pack_gate_weights(lru_w_a, lru_w_x)
    g_pre, g_post = rowvec(g_pre_mix), rowvec(g_post_mix)
    gf_pre, gf_post = rowvec(g_pre_ffn), rowvec(g_post_ffn)
    p_scale, cv_b = rowvec(pool_scale), rowvec(conv_b)

    x2 = x.reshape(bsz * seq, d)
    c2 = ctx.reshape(bsz * n_ctx, d)
    r3 = lambda a, n: a.reshape(bsz, n, a.shape[-1])
    flat = lambda a: a.reshape(-1, a.shape[-1])
    for l in range(depth):
        need_ctx = l < depth - 1
        q, kd, vd, rx, ry, pu, gt = _inproj(
            x2, mod_all, l, lat_row, g_pre, w_in_b, tm=tm, tiles_per_seq=seq // tm, rope_tables=rope,
            kv_only=False)
        if need_ctx:
            qc, kdc, vdc, rxc, ryc, puc, gtc = _inproj(
                c2, mod_all, l, ctx_row, g_pre, w_in_b, tm=tm_ctx, tiles_per_seq=1, rope_tables=None,
                kv_only=False)
        else:
            kdc, vdc, rxc = _inproj(
                c2, mod_all, l, ctx_row, g_pre, w_in_b, tm=tm_ctx, tiles_per_seq=1, rope_tables=None,
                kv_only=True)
            ryc = None

        kdc3, vdc3 = r3(kdc, n_ctx), r3(vdc, n_ctx)
        attn_l = _attention(attn_sink, l, r3(q, seq), r3(kd, seq), r3(vd, seq), kdc3, vdc3)
        pool_l = _pool(r3(pu, seq), l, w_mix, p_scale)
        rnn_l, rnn_c = _lru(l, r3(rxc, n_ctx), r3(rx, seq), None if ryc is None else r3(ryc, n_ctx),
                            r3(ry, seq), conv_w, cv_b, wg, lru_b_a, lru_b_x, lru_lambda)

        x2 = _merge(x2, mod_all, l, lat_row, flat(attn_l), flat(pool_l), flat(rnn_l), gt, wa, wp, wr, wo,
                    g_post, tm=tm)
        x2 = _ffn(x2, mod_all, l, lat_row, gf_pre, w_gu_b, w_dn_b, gf_post, tm=tm)

        if need_ctx:
            attn_c = _attention_ctx(attn_sink, l, r3(qc, n_ctx), kdc3, vdc3)
            pool_c = _pool(r3(puc, n_ctx), l, w_mix, p_scale)
            c2 = _merge(c2, mod_all, l, ctx_row, flat(attn_c), flat(pool_c), flat(rnn_c), gtc, wa, wp, wr,
                        wo, g_post, tm=tm_ctx)
            c2 = _ffn(c2, mod_all, l, ctx_row, gf_pre, w_gu_b, w_dn_b, gf_post, tm=tm_ctx)
    return x2.reshape(bsz, seq, d)
```

```python
import functools
import math

import jax
import jax.numpy as jnp
import numpy as np
from jax import lax
from jax.experimental import pallas as pl
from jax.experimental.pallas import tpu as pltpu

F32 = jnp.float32
BF16 = jnp.bfloat16

GRID_W = 64
HEAD_DIM = 64
N_Q_HEADS = 8
N_KV_HEADS = 2
ATTN_W = N_Q_HEADS * HEAD_DIM
KV_W = N_KV_HEADS * HEAD_DIM
ATTN_BLOCK = 128
ROPE_BASE = 10000.0
POOL_WINDOWS = (2, 4, 8, 16)
POOL_W = 512
POOL_GW = POOL_W // len(POOL_WINDOWS)
RNN_W = 512
RNN_BLOCKS = 8
CONV_W = 4
LRU_C = 8.0
N_BRANCH = 3
EPS = 1e-6
NEG_INF = -1e30
LOG2E = 1.4426950408889634

LANES = 128
SUBLANES = 8
VMEM_LIMIT_BYTES = 56 * 1024 * 1024

SEG = 36
CHUNK = SUBLANES * SEG


def _params(n_axes):
    return pltpu.CompilerParams(
        dimension_semantics=("arbitrary",) * n_axes, vmem_limit_bytes=VMEM_LIMIT_BYTES)


def _layer_block(arr, layer, block=None):
    block = tuple(arr.shape[1:]) if block is None else block
    index = (layer,) + (0,) * len(block)
    return pl.BlockSpec((None,) + block, lambda *_: index, pipeline_mode=pl.Buffered(1))


def _mod_block(mod_all, layer, mod_row_of_tile):
    return pl.BlockSpec((None, None) + mod_all.shape[2:], lambda i: (layer, mod_row_of_tile(i), 0, 0))


def _rms(xf):
    return xf * lax.rsqrt(jnp.mean(xf * xf, axis=-1, keepdims=True) + EPS)


def _mod_body(c_ref, w_ref, b_ref, o_ref):
    c = c_ref[...]
    s = (c * jax.nn.sigmoid(c)).astype(BF16)
    o_ref[0] = jnp.dot(s, w_ref[0].astype(BF16), preferred_element_type=F32) + b_ref[0]


def _adaln_mod(cvec, w_ada, b_ada):
    n_layers, d, n6 = w_ada.shape
    rows = cvec.shape[0]
    tn = n6 // 4
    return pl.pallas_call(
        _mod_body,
        grid=(n_layers, n6 // tn),
        in_specs=[
            pl.BlockSpec((rows, d), lambda l, j: (0, 0)),
            pl.BlockSpec((1, d, tn), lambda l, j: (l, 0, j)),
            pl.BlockSpec((1, 1, tn), lambda l, j: (l, 0, j)),
        ],
        out_specs=pl.BlockSpec((1, rows, tn), lambda l, j: (l, 0, j)),
        out_shape=jax.ShapeDtypeStruct((n_layers, rows, n6), F32),
        compiler_params=_params(2),
        name="adaln_mod",
    )(cvec, w_ada, b_ada.reshape(n_layers, 1, n6))


def _rope_tables(seq_len):
    pos = np.arange(seq_len)
    row = (pos // GRID_W).astype(np.float32)
    col = (pos % GRID_W).astype(np.float32)
    half = HEAD_DIM // 2
    quarter = half // 2
    inv = (np.float32(ROPE_BASE) ** (-(np.arange(quarter, dtype=np.float32) * np.float32(2.0 / half)))
           ).astype(np.float32)
    j = np.arange(LANES) % HEAD_DIM
    is_col = j >= half
    second = (j % half) >= quarter
    freq = inv[j % quarter]
    ang = (np.where(is_col[None, :], col[:, None], row[:, None]) * freq[None, :]).astype(np.float32)
    cos, sin = np.cos(ang), np.sin(ang)
    sin_prev = np.where(second[None, :], sin, 0.0)
    sin_next = np.where(second[None, :], 0.0, -sin)
    return tuple(jnp.asarray(t, F32) for t in (cos, sin_prev, sin_next))


def _row_halves(n_rows):
    if n_rows % (2 * SUBLANES * 2) or n_rows < 256:
        return [slice(0, n_rows)]
    return [slice(0, n_rows // 2), slice(n_rows // 2, n_rows)]


def _store_dup_heads(dst_ref, rows, kv):
    low = lax.broadcasted_iota(jnp.int32, kv.shape, 1) < HEAD_DIM
    swapped = pltpu.roll(kv, HEAD_DIM, 1)
    dst_ref[rows, 0:LANES] = jnp.where(low, kv, swapped).astype(BF16)
    dst_ref[rows, LANES:2 * LANES] = jnp.where(low, swapped, kv).astype(BF16)


def _norm_modulate_store(x_ref, mod_ref, g_ref, h_ref, shift_row, scale_row, rows):
    y = _rms(x_ref[rows, :]) * g_ref[...]
    h = y * (1.0 + mod_ref[scale_row:scale_row + 1, :]) + mod_ref[shift_row:shift_row + 1, :]
    h_ref[rows, :] = h.astype(BF16)


def _inproj_full_body(*refs, rope):
    if rope:
        (x_ref, mod_ref, g_ref, w_ref, cos_ref, sp_ref, sn_ref,
         q_ref, kd_ref, vd_ref, rx_ref, ry_ref, pu_ref, gt_ref, h_ref) = refs
    else:
        (x_ref, mod_ref, g_ref, w_ref,
         q_ref, kd_ref, vd_ref, rx_ref, ry_ref, pu_ref, gt_ref, h_ref) = refs
    quarter = HEAD_DIM // 4
    d = x_ref.shape[1]
    for rows in _row_halves(x_ref.shape[0]):
        _norm_modulate_store(x_ref, mod_ref, g_ref, h_ref, 0, 1, rows)

        def proj(c0, c1):
            return jnp.dot(h_ref[rows, :], w_ref[:, c0:c1], preferred_element_type=F32)

        def rotary(pc):
            if not rope:
                return pc
            return (pc * cos_ref[rows, :]
                    + pltpu.roll(pc, quarter, 1) * sp_ref[rows, :]
                    + pltpu.roll(pc, LANES - quarter, 1) * sn_ref[rows, :])

        for c in range(ATTN_W // (2 * LANES)):
            pair = proj(c * 2 * LANES, (c + 1) * 2 * LANES)
            for half in range(2):
                col = (2 * c + half) * LANES
                pc = rotary(pair[:, half * LANES:(half + 1) * LANES])
                q_ref[rows, col:col + LANES] = (pc * (HEAD_DIM ** -0.5 * LOG2E)).astype(BF16)
        o = ATTN_W
        kv = proj(o, o + 2 * KV_W)
        _store_dup_heads(kd_ref, rows, rotary(kv[:, :KV_W]))
        _store_dup_heads(vd_ref, rows, kv[:, KV_W:])
        o += 2 * KV_W
        rx_ref[rows, :] = proj(o, o + RNN_W)
        o += RNN_W
        ry_ref[rows, :] = jax.nn.gelu(proj(o, o + RNN_W)).astype(BF16)
        o += RNN_W
        pu_ref[rows, :] = proj(o, o + POOL_W)
        o += POOL_W
        for c in range(N_BRANCH):
            gl = proj(o + c * d, o + (c + 1) * d)
            gt_ref[rows, c * d:(c + 1) * d] = jax.nn.sigmoid(gl).astype(BF16)


def _inproj_kv_body(x_ref, mod_ref, g_ref, w_ref, kd_ref, vd_ref, rx_ref, h_ref):
    rows = slice(0, x_ref.shape[0])
    _norm_modulate_store(x_ref, mod_ref, g_ref, h_ref, 0, 1, rows)
    kv = jnp.dot(h_ref[...], w_ref[:, ATTN_W:ATTN_W + 2 * KV_W], preferred_element_type=F32)
    _store_dup_heads(kd_ref, rows, kv[:, :KV_W])
    _store_dup_heads(vd_ref, rows, kv[:, KV_W:])
    rx_ref[...] = jnp.dot(h_ref[...], w_ref[:, ATTN_W + 2 * KV_W:ATTN_W + 2 * KV_W + RNN_W],
                          preferred_element_type=F32)


def _inproj(x2d, mod_all, layer, mod_row_of_tile, g, w, *, tm, tiles_per_seq, rope_tables, kv_only):
    n_tok, d = x2d.shape
    grid = (n_tok // tm,)
    row = lambda i: (i, 0)
    w_cols = ATTN_W + 2 * KV_W + RNN_W if kv_only else w.shape[2]
    in_specs = [
        pl.BlockSpec((tm, d), row),
        _mod_block(mod_all, layer, mod_row_of_tile),
        _layer_block(g, layer),
        _layer_block(w, layer, (d, w_cols)),
    ]
    args = [x2d, mod_all, g, w]
    if rope_tables is not None:
        in_specs += [pl.BlockSpec((tm, LANES), lambda i: (i % tiles_per_seq, 0))] * 3
        args += list(rope_tables)

    def out(width, dtype):
        return pl.BlockSpec((tm, width), row), jax.ShapeDtypeStruct((n_tok, width), dtype)

    if kv_only:
        outs = [out(2 * KV_W, BF16), out(2 * KV_W, BF16), out(RNN_W, F32)]
        body = _inproj_kv_body
    else:
        outs = [out(ATTN_W, BF16), out(2 * KV_W, BF16), out(2 * KV_W, BF16), out(RNN_W, F32),
                out(RNN_W, BF16), out(POOL_W, F32), out(N_BRANCH * d, BF16)]
        body = functools.partial(_inproj_full_body, rope=rope_tables is not None)
    return pl.pallas_call(
        body,
        grid=grid,
        in_specs=in_specs,
        out_specs=[o[0] for o in outs],
        out_shape=[o[1] for o in outs],
        scratch_shapes=[pltpu.VMEM((tm, d), BF16)],
        compiler_params=_params(1),
        name="inproj_kv" if kv_only else "inproj",
    )(*args)


def _attend(sink_ref, layer, q_blk, k_rows, v_rows, biases):
    tq = q_blk.shape[0]
    heads_per_kv = N_Q_HEADS // N_KV_HEADS
    low = lax.broadcasted_iota(jnp.int32, (tq, LANES), 1) < HEAD_DIM
    zero = jnp.zeros((), BF16)
    outs = []
    for h in range(N_KV_HEADS):
        k_h = k_rows[:, h * LANES:(h + 1) * LANES]
        v_h = v_rows[:, h * LANES:(h + 1) * LANES]
        stacked = []
        for c2 in range(heads_per_kv // 2):
            qc = q_blk[:, (2 * h + c2) * LANES:(2 * h + c2 + 1) * LANES]
            stacked += [jnp.where(low, qc, zero), jnp.where(low, zero, qc)]
        lhs = jnp.concatenate(stacked, axis=0)
        s = lax.dot_general(lhs, k_h, (((1,), (1,)), ((), ())), preferred_element_type=F32)
        blocks = []
        for j, bias in enumerate(biases):
            blk = s[:, j * LANES:(j + 1) * LANES]
            if bias is not None:
                blk = blk + jnp.concatenate([bias] * heads_per_kv, axis=0)
            blocks.append(blk)
        sink_col = jnp.concatenate(
            [jnp.full((tq, 1), sink_ref[layer, h * heads_per_kv + g] * LOG2E, F32)
             for g in range(heads_per_kv)],
            axis=0)
        m = jnp.maximum(jnp.max(functools.reduce(jnp.maximum, blocks), axis=-1, keepdims=True), sink_col)
        probs = [jnp.exp2(blk - m) for blk in blocks]
        denom = jnp.sum(functools.reduce(jnp.add, probs), axis=-1, keepdims=True) + jnp.exp2(sink_col - m)
        p = jnp.concatenate(probs, axis=1).astype(BF16)
        o = jnp.dot(p, v_h, preferred_element_type=F32) * (1.0 / denom)
        for c2 in range(heads_per_kv // 2):
            outs.append(jnp.where(low, o[(2 * c2) * tq:(2 * c2 + 1) * tq],
                                  o[(2 * c2 + 1) * tq:(2 * c2 + 2) * tq]).astype(BF16))
    return jnp.concatenate(outs, axis=1)


def _attn_local_body(sink_ref, q_ref, kd_ref, vd_ref, kc_ref, vc_ref, o_ref, *, layer, n_blocks):
    blk = ATTN_BLOCK
    ii = lax.broadcasted_iota(jnp.int32, (blk, blk), 0)
    jj = lax.broadcasted_iota(jnp.int32, (blk, blk), 1)
    n_ctx_blocks = kc_ref.shape[0] // LANES

    def body(n, carry):
        rows = lambda i: pl.ds(pl.multiple_of(i * blk, blk), blk)
        prv, nxt = jnp.maximum(n - 1, 0), jnp.minimum(n + 1, n_blocks - 1)
        bias_prev = jnp.where((jj >= ii) & (n > 0), 0.0, NEG_INF).astype(F32)
        bias_next = jnp.where((jj <= ii) & (n < n_blocks - 1), 0.0, NEG_INF).astype(F32)
        k_rows = jnp.concatenate([kd_ref[rows(prv), :], kd_ref[rows(n), :], kd_ref[rows(nxt), :],
                                  kc_ref[...]], axis=0)
        v_rows = jnp.concatenate([vd_ref[rows(prv), :], vd_ref[rows(n), :], vd_ref[rows(nxt), :],
                                  vc_ref[...]], axis=0)
        o_ref[rows(n), :] = _attend(sink_ref, layer, q_ref[rows(n), :], k_rows, v_rows,
                                    [bias_prev, None, bias_next] + [None] * n_ctx_blocks)
        return carry

    lax.fori_loop(0, n_blocks, body, 0)


def _attn_ctx_body(sink_ref, q_ref, kc_ref, vc_ref, o_ref, *, layer):
    o_ref[...] = _attend(sink_ref, layer, q_ref[...], kc_ref[...], vc_ref[...],
                         [None] * (kc_ref.shape[0] // LANES))


def _attention(sink, layer, q, kd, vd, kdc, vdc):
    b, s, _ = q.shape
    c = kdc.shape[1]
    kw = 2 * KV_W
    idx = lambda bi: (bi, 0, 0)
    return pl.pallas_call(
        functools.partial(_attn_local_body, layer=layer, n_blocks=s // ATTN_BLOCK),
        grid=(b,),
        in_specs=[
            pl.BlockSpec(memory_space=pltpu.SMEM),
            pl.BlockSpec((None, s, ATTN_W), idx),
            pl.BlockSpec((None, s, kw), idx), pl.BlockSpec((None, s, kw), idx),
            pl.BlockSpec((None, c, kw), idx), pl.BlockSpec((None, c, kw), idx),
        ],
        out_specs=pl.BlockSpec((None, s, ATTN_W), idx),
        out_shape=jax.ShapeDtypeStruct((b, s, ATTN_W), BF16),
        compiler_params=_params(1),
        name="attn_local",
    )(sink, q, kd, vd, kdc, vdc)


def _attention_ctx(sink, layer, qc, kdc, vdc):
    b, c, _ = qc.shape
    kw = 2 * KV_W
    idx = lambda bi: (bi, 0, 0)
    return pl.pallas_call(
        functools.partial(_attn_ctx_body, layer=layer),
        grid=(b,),
        in_specs=[
            pl.BlockSpec(memory_space=pltpu.SMEM),
            pl.BlockSpec((None, c, ATTN_W), idx),
            pl.BlockSpec((None, c, kw), idx), pl.BlockSpec((None, c, kw), idx),
        ],
        out_specs=pl.BlockSpec((None, c, ATTN_W), idx),
        out_shape=jax.ShapeDtypeStruct((b, c, ATTN_W), BF16),
        compiler_params=_params(1),
        name="attn_ctx",
    )(sink, qc, kdc, vdc)


def _pool_body(u_ref, w_ref, sc_ref, o_ref, pad_ref, *, tile):
    length = u_ref.shape[0]
    halo = SUBLANES
    pad_ref[0:halo, :] = jnp.zeros((halo, POOL_W), F32)
    pad_ref[halo + length:2 * halo + length, :] = jnp.zeros((halo, POOL_W), F32)
    pad_ref[halo:halo + length, :] = u_ref[...]
    rows = tile + 2 * halo
    for t0 in range(0, length, tile):
        t = t0 + lax.broadcasted_iota(jnp.int32, (tile, POOL_GW), 0)
        for gi, w in enumerate(POOL_WINDOWS):
            lanes = slice(gi * POOL_GW, (gi + 1) * POOL_GW)
            p = pad_ref[t0:t0 + rows, lanes]
            acc = p
            step = 1
            while step < w:
                acc = acc + pltpu.roll(acc, rows - step, 0)
                step *= 2
            back = (w - 1) // 2
            win = (pltpu.roll(acc, back, 0) if back else acc)[halo:halo + tile]
            lo = jnp.maximum(t - back, 0)
            hi = jnp.minimum(t + w // 2 + 1, length)
            mean = win / (hi - lo).astype(F32)
            dlt = (mean - p[halo:halo + tile]).astype(BF16)
            y = jnp.dot(dlt, w_ref[gi], preferred_element_type=F32) * sc_ref[:, lanes]
            o_ref[t0:t0 + tile, lanes] = y.astype(BF16)


def _pool(pu, layer, w_mix, ch_scale):
    b, length, _ = pu.shape
    tile = min(length, 256)
    idx = lambda bi: (bi, 0, 0)
    return pl.pallas_call(
        functools.partial(_pool_body, tile=tile),
        grid=(b,),
        in_specs=[pl.BlockSpec((None, length, POOL_W), idx), _layer_block(w_mix, layer),
                  _layer_block(ch_scale, layer)],
        out_specs=pl.BlockSpec((None, length, POOL_W), idx),
        out_shape=jax.ShapeDtypeStruct((b, length, POOL_W), BF16),
        scratch_shapes=[pltpu.VMEM((length + 2 * SUBLANES, POOL_W), F32)],
        compiler_params=_params(1),
        name="pool",
    )(pu, w_mix, ch_scale)


def _lru_body(*refs, n_ctx, n_lat, tile, with_ctx_out):
    if with_ctx_out:
        (rxc_ref, rxl_ref, ryc_ref, ryl_ref, cw_ref, cb_ref, wg_ref, ba_ref, bx_ref, lam_ref,
         ol_ref, oc_ref, pad_ref, xl_ref, hf_ref, hb_ref, xp_ref, a_ref, b_ref) = refs
    else:
        (rxc_ref, rxl_ref, ryl_ref, cw_ref, cb_ref, wg_ref, ba_ref, bx_ref, lam_ref,
         ol_ref, pad_ref, xl_ref, hf_ref, hb_ref, xp_ref, a_ref, b_ref) = refs
    halo = SUBLANES
    n_tot = n_ctx + n_lat
    n_slab = RNN_W // LANES

    zeros = jnp.zeros((halo, RNN_W), F32)
    c0, l0 = halo, 2 * halo + n_ctx
    pad_ref[0:halo, :] = zeros
    pad_ref[c0 + n_ctx:l0, :] = zeros
    pad_ref[l0 + n_lat:l0 + n_lat + halo, :] = zeros
    pad_ref[c0:c0 + n_ctx, :] = rxc_ref[...]
    pad_ref[l0:l0 + n_lat, :] = rxl_ref[...]

    def conv_tile(src0, dst_rows):
        rows = tile + 2 * halo
        p = pad_ref[src0 - halo:src0 + tile + halo, :]
        y = (cb_ref[...] + p[halo:halo + tile] * cw_ref[1:2, :]
             + pltpu.roll(p, 1, 0)[halo:halo + tile] * cw_ref[0:1, :]
             + pltpu.roll(p, rows - 1, 0)[halo:halo + tile] * cw_ref[2:3, :]
             + pltpu.roll(p, rows - 2, 0)[halo:halo + tile] * cw_ref[3:4, :])
        for s in range(n_slab):
            for d0 in dst_rows:
                xl_ref[s, d0:d0 + tile, :] = y[:, s * LANES:(s + 1) * LANES]

    for t0 in range(0, n_ctx, tile):
        conv_tile(c0 + t0, (t0, n_tot + t0))
    for t0 in range(0, n_lat, tile):
        conv_tile(l0 + t0, (n_ctx + t0,))

    lam = lam_ref[...]
    log_sig = jnp.minimum(lam, 0.0) - jnp.log1p(jnp.exp(-jnp.abs(lam)))
    decay = LRU_C * log_sig
    half = RNN_W // 2

    def run_chunk(d, base, carry):
        res_ref = hf_ref if d == 0 else hb_ref
        win0 = 0 if d == 0 else n_ctx
        for v in range(SEG):
            for s in range(n_slab):
                xp_ref[d, v * SUBLANES:(v + 1) * SUBLANES, s * LANES:(s + 1) * LANES] = (
                    xl_ref[s, pl.ds(win0 + base + v, SUBLANES, stride=SEG), :])
        for j in range(2):
            ch = slice(j * half, (j + 1) * half)
            xj = xp_ref[d, :, ch]
            gates = jnp.dot(xj.astype(BF16), wg_ref[d, j], preferred_element_type=F32)
            r_gate = jax.nn.sigmoid(gates[:, :half] + ba_ref[d:d + 1, ch])
            i_gate = jax.nn.sigmoid(gates[:, half:] + bx_ref[d:d + 1, ch])
            a = jnp.exp(r_gate * decay[d:d + 1, ch])
            mult = jnp.sqrt(1.0 - a * a)
            a_ref[d, :, ch] = a
            b_ref[d, :, ch] = mult * i_gate * xj
        order = range(SEG) if d == 0 else range(SEG - 1, -1, -1)
        h = jnp.zeros((SUBLANES, RNN_W), F32)
        acum = jnp.ones((SUBLANES, RNN_W), F32)
        for v in order:
            rows = slice(v * SUBLANES, (v + 1) * SUBLANES)
            av = a_ref[d, rows, :]
            h = av * h + b_ref[d, rows, :]
            acum = av * acum
            b_ref[d, rows, :] = h
            a_ref[d, rows, :] = acum
        seg_in = [None] * SUBLANES
        state = carry
        seg_order = range(SUBLANES) if d == 0 else range(SUBLANES - 1, -1, -1)
        for i in seg_order:
            seg_in[i] = state
            state = h[i:i + 1, :] + acum[i:i + 1, :] * state
        seg_state = jnp.concatenate(seg_in, axis=0)
        for v in range(SEG):
            rows = slice(v * SUBLANES, (v + 1) * SUBLANES)
            hv = b_ref[d, rows, :] + a_ref[d, rows, :] * seg_state
            for s in range(n_slab):
                res_ref[s, pl.ds(base + v, SUBLANES, stride=SEG), :] = hv[:, s * LANES:(s + 1) * LANES]
        return state

    n_chunks = n_tot // CHUNK

    def step(c, carries):
        cf, cb = carries
        cf = run_chunk(0, c * CHUNK, cf)
        cb = run_chunk(1, (n_chunks - 1 - c) * CHUNK, cb)
        return cf, cb

    zero_state = jnp.zeros((1, RNN_W), F32)
    lax.fori_loop(0, n_chunks, step, (zero_state, zero_state))

    def emit(out_ref, ry_ref, f0, b0, length):
        for t0 in range(0, length, tile):
            hsum = jnp.concatenate(
                [hf_ref[s, f0 + t0:f0 + t0 + tile, :] + hb_ref[s, b0 + t0:b0 + t0 + tile, :]
                 for s in range(n_slab)], axis=1)
            gate = ry_ref[t0:t0 + tile, :].astype(F32)
            out_ref[t0:t0 + tile, :] = (hsum * gate).astype(BF16)

    emit(ol_ref, ryl_ref, n_ctx, 0, n_lat)
    if with_ctx_out:
        emit(oc_ref, ryc_ref, 0, n_lat, n_ctx)


def _lru(layer, rxc, rxl, ryc, ryl, conv_w, conv_b, wg, b_a, b_x, lam):
    b, n_ctx, _ = rxc.shape
    n_lat = rxl.shape[1]
    n_tot = n_ctx + n_lat
    tile = math.gcd(n_ctx, n_lat, 256)
    assert n_tot % CHUNK == 0 and tile % SUBLANES == 0
    with_ctx_out = ryc is not None
    idx = lambda bi: (bi, 0, 0)
    seq = lambda n, dt=None: pl.BlockSpec((None, n, RNN_W), idx)
    n_slab = RNN_W // LANES
    in_specs = [seq(n_ctx), seq(n_lat)] + ([seq(n_ctx)] if with_ctx_out else []) + [seq(n_lat)]
    args = [rxc, rxl] + ([ryc] if with_ctx_out else []) + [ryl]
    for wgt in (conv_w, conv_b, wg, b_a, b_x, lam):
        in_specs.append(_layer_block(wgt, layer))
        args.append(wgt)
    out_specs = [seq(n_lat)] + ([seq(n_ctx)] if with_ctx_out else [])
    out_shape = [jax.ShapeDtypeStruct((b, n_lat, RNN_W), BF16)]
    if with_ctx_out:
        out_shape.append(jax.ShapeDtypeStruct((b, n_ctx, RNN_W), BF16))
    res = pl.pallas_call(
        functools.partial(_lru_body, n_ctx=n_ctx, n_lat=n_lat, tile=tile, with_ctx_out=with_ctx_out),
        grid=(b,),
        in_specs=in_specs,
        out_specs=out_specs,
        out_shape=out_shape,
        scratch_shapes=[
            pltpu.VMEM((n_tot + 3 * SUBLANES, RNN_W), F32),
            pltpu.VMEM((n_slab, n_tot + n_ctx, LANES), F32),
            pltpu.VMEM((n_slab, n_tot, LANES), F32),
            pltpu.VMEM((n_slab, n_tot, LANES), F32),
            pltpu.VMEM((2, CHUNK, RNN_W), F32),
            pltpu.VMEM((2, CHUNK, RNN_W), F32),
            pltpu.VMEM((2, CHUNK, RNN_W), F32),
        ],
        compiler_params=_params(1),
        name="rglru",
    )(*args)
    return (res[0], res[1]) if with_ctx_out else (res[0], None)


def _mix_ffn_body(x_ref, mod_ref, at_ref, po_ref, rn_ref, gt_ref, wa_ref, wp_ref, wr_ref, wo_ref,
                  gmix_ref, gpre_ref, wgu_ref, wd_ref, gpost_ref, o_ref, m_ref, h_ref, *, d_ff, chunk):
    d = x_ref.shape[1]
    for rows in _row_halves(x_ref.shape[0]):
        merged = None
        for k, (br_ref, w_ref) in enumerate(((at_ref, wa_ref), (po_ref, wp_ref), (rn_ref, wr_ref))):
            t = gt_ref[rows, k * d:(k + 1) * d].astype(F32) * jnp.dot(
                br_ref[rows, :], w_ref[...], preferred_element_type=F32)
            merged = t if merged is None else merged + t
        m_ref[rows, :] = merged.astype(BF16)
        mix = jnp.dot(m_ref[rows, :], wo_ref[...], preferred_element_type=F32)
        o_ref[rows, :] = x_ref[rows, :] + mod_ref[2:3, :] * (_rms(mix) * gmix_ref[...])

        _norm_modulate_store(o_ref, mod_ref, gpre_ref, h_ref, 3, 4, rows)
        acc = None
        for c0 in range(0, d_ff, chunk):
            c1 = min(c0 + chunk, d_ff)
            gate = jnp.dot(h_ref[rows, :], wgu_ref[:, c0:c1], preferred_element_type=F32)
            up = jnp.dot(h_ref[rows, :], wgu_ref[:, d_ff + c0:d_ff + c1], preferred_element_type=F32)
            act = ((gate * jax.nn.sigmoid(gate)) * up).astype(BF16)
            part = jnp.dot(act, wd_ref[c0:c1, :], preferred_element_type=F32)
            acc = part if acc is None else acc + part
        o_ref[rows, :] = o_ref[rows, :] + mod_ref[5:6, :] * (_rms(acc) * gpost_ref[...])


def _mix_ffn(x2d, mod_all, layer, mod_row_of_tile, attn, pool, rnn, gates, wa, wp, wr, wo, g_mix,
             g_pre, w_gu, w_down, g_post, *, tm):
    n_tok, d = x2d.shape
    d_ff = w_down.shape[1]
    row = lambda i: (i, 0)
    tok = lambda arr: pl.BlockSpec((tm, arr.shape[1]), row)
    params = (wa, wp, wr, wo, g_mix, g_pre, w_gu, w_down, g_post)
    return pl.pallas_call(
        functools.partial(_mix_ffn_body, d_ff=d_ff, chunk=1024),
        grid=(n_tok // tm,),
        in_specs=[tok(x2d), _mod_block(mod_all, layer, mod_row_of_tile),
                  tok(attn), tok(pool), tok(rnn), tok(gates)]
                 + [_layer_block(p, layer) for p in params],
        out_specs=pl.BlockSpec((tm, d), row),
        out_shape=jax.ShapeDtypeStruct((n_tok, d), F32),
        scratch_shapes=[pltpu.VMEM((tm, d), BF16), pltpu.VMEM((tm, d), BF16)],
        compiler_params=_params(1),
        name="mix_ffn",
    )(x2d, mod_all, attn, pool, rnn, gates, *params)


def _pack_gate_weights(w_a, w_x):
    def block_diag(w):
        eye = jnp.eye(RNN_BLOCKS, dtype=w.dtype)
        return jnp.einsum('ldhij,hg->ldhigj', w, eye).reshape(w.shape[:2] + (RNN_W, RNN_W))
    bd_a, bd_x = block_diag(w_a), block_diag(w_x)
    half = RNN_W // 2
    halves = []
    for j in range(2):
        sl = slice(j * half, (j + 1) * half)
        halves.append(jnp.concatenate([bd_a[:, :, sl, sl], bd_x[:, :, sl, sl]], axis=3))
    return jnp.stack(halves, axis=2).astype(BF16)


def kernel(x, c, ctx, c_ctx, w_ada, b_ada, g_pre_mix, g_post_mix, g_pre_ffn, g_post_ffn, w_in, attn_sink, w_attn_o, pool_mix, pool_scale, w_pool_o, conv_w, conv_b, lru_w_a, lru_b_a, lru_w_x, lru_b_x, lru_lambda, w_rnn_o, w_out, w_gu, w_down):
    bsz, seq, d = x.shape
    n_ctx = ctx.shape[1]
    depth = w_ada.shape[0]
    tm = min(512, seq)
    tm_ctx = min(tm, n_ctx)

    mod_rows = -(-(bsz + 1) // SUBLANES) * SUBLANES
    cvec = jnp.zeros((mod_rows, d), F32).at[:bsz].set(c).at[bsz].set(c_ctx)
    mod_all = _adaln_mod(cvec, w_ada, b_ada).reshape(depth, mod_rows, 6, d)

    rope = _rope_tables(seq)
    lat_row = lambda i: i // (seq // tm)
    ctx_row = lambda i: bsz

    rowvec = lambda a: a.reshape(depth, 1, -1)
    w_in_b, w_gu_b, w_dn_b = w_in.astype(BF16), w_gu.astype(BF16), w_down.astype(BF16)
    wa, wp, wr, wo = (w.astype(BF16) for w in (w_attn_o, w_pool_o, w_rnn_o, w_out))
    w_mix = pool_mix.astype(BF16)
    wg = _pack_gate_weights(lru_w_a, lru_w_x)
    g_pre, g_post = rowvec(g_pre_mix), rowvec(g_post_mix)
    gf_pre, gf_post = rowvec(g_pre_ffn), rowvec(g_post_ffn)
    p_scale, cv_b = rowvec(pool_scale), rowvec(conv_b)

    x2 = x.reshape(bsz * seq, d)
    c2 = ctx.reshape(bsz * n_ctx, d)
    r3 = lambda a, n: a.reshape(bsz, n, a.shape[-1])
    flat = lambda a: a.reshape(-1, a.shape[-1])
    for l in range(depth):
        need_ctx = l < depth - 1
        q, kd, vd, rx, ry, pu, gt = _inproj(
            x2, mod_all, l, lat_row, g_pre, w_in_b, tm=tm, tiles_per_seq=seq // tm, rope_tables=rope,
            kv_only=False)
        if need_ctx:
            qc, kdc, vdc, rxc, ryc, puc, gtc = _inproj(
                c2, mod_all, l, ctx_row, g_pre, w_in_b, tm=tm_ctx, tiles_per_seq=1, rope_tables=None,
                kv_only=False)
        else:
            kdc, vdc, rxc = _inproj(
                c2, mod_all, l, ctx_row, g_pre, w_in_b, tm=tm_ctx, tiles_per_seq=1, rope_tables=None,
                kv_only=True)
            ryc = None

        kdc3, vdc3 = r3(kdc, n_ctx), r3(vdc, n_ctx)
        attn_l = _attention(attn_sink, l, r3(q, seq), r3(kd, seq), r3(vd, seq), kdc3, vdc3)
        pool_l = _pool(r3(pu, seq), l, w_mix, p_scale)
        rnn_l, rnn_c = _lru(l, r3(rxc, n_ctx), r3(rx, seq), None if ryc is None else r3(ryc, n_ctx),
                            r3(ry, seq), conv_w, cv_b, wg, lru_b_a, lru_b_x, lru_lambda)

        x2 = _mix_ffn(x2, mod_all, l, lat_row, flat(attn_l), flat(pool_l), flat(rnn_l), gt, wa, wp, wr, wo,
                      g_post, gf_pre, w_gu_b, w_dn_b, gf_post, tm=tm)

        if need_ctx:
            attn_c = _attention_ctx(attn_sink, l, r3(qc, n_ctx), kdc3, vdc3)
            pool_c = _pool(r3(puc, n_ctx), l, w_mix, p_scale)
            c2 = _mix_ffn(c2, mod_all, l, ctx_row, flat(attn_c), flat(pool_c), flat(rnn_c), gtc, wa, wp, wr,
                          wo, g_post, gf_pre, w_gu_b, w_dn_b, gf_post, tm=tm_ctx)
    return x2.reshape(bsz, seq, d)
```

```python
import functools
import math

import jax
import jax.numpy as jnp
import numpy as np
from jax import lax
from jax.experimental import pallas as pl
from jax.experimental.pallas import tpu as pltpu

F32 = jnp.float32
BF16 = jnp.bfloat16

GRID_W = 64
HEAD_DIM = 64
N_Q_HEADS = 8
N_KV_HEADS = 2
ATTN_W = N_Q_HEADS * HEAD_DIM
KV_W = N_KV_HEADS * HEAD_DIM
ATTN_BLOCK = 128
ROPE_BASE = 10000.0
POOL_WINDOWS = (2, 4, 8, 16)
POOL_W = 512
POOL_GW = POOL_W // len(POOL_WINDOWS)
RNN_W = 512
RNN_BLOCKS = 8
CONV_W = 4
LRU_C = 8.0
N_BRANCH = 3
EPS = 1e-6
NEG_INF = -1e30
LOG2E = 1.4426950408889634

LANES = 128
SUBLANES = 8
VMEM_LIMIT_BYTES = 56 * 1024 * 1024

SEG = 36
CHUNK = SUBLANES * SEG


def _params(n_axes):
    return pltpu.CompilerParams(
        dimension_semantics=("arbitrary",) * n_axes, vmem_limit_bytes=VMEM_LIMIT_BYTES)


def _layer_block(arr, layer, block=None):
    block = tuple(arr.shape[1:]) if block is None else block
    index = (layer,) + (0,) * len(block)
    return pl.BlockSpec((None,) + block, lambda *_: index, pipeline_mode=pl.Buffered(1))


def _mod_block(mod_all, layer, mod_row_of_tile):
    return pl.BlockSpec((None, None) + mod_all.shape[2:], lambda i: (layer, mod_row_of_tile(i), 0, 0))


def _rms(xf):
    return xf * lax.rsqrt(jnp.mean(xf * xf, axis=-1, keepdims=True) + EPS)


def _mod_body(c_ref, w_ref, b_ref, o_ref):
    c = c_ref[...]
    s = (c * jax.nn.sigmoid(c)).astype(BF16)
    o_ref[0] = jnp.dot(s, w_ref[0].astype(BF16), preferred_element_type=F32) + b_ref[0]


def _adaln_mod(cvec, w_ada, b_ada):
    n_layers, d, n6 = w_ada.shape
    rows = cvec.shape[0]
    tn = n6 // 4
    return pl.pallas_call(
        _mod_body,
        grid=(n_layers, n6 // tn),
        in_specs=[
            pl.BlockSpec((rows, d), lambda l, j: (0, 0)),
            pl.BlockSpec((1, d, tn), lambda l, j: (l, 0, j)),
            pl.BlockSpec((1, 1, tn), lambda l, j: (l, 0, j)),
        ],
        out_specs=pl.BlockSpec((1, rows, tn), lambda l, j: (l, 0, j)),
        out_shape=jax.ShapeDtypeStruct((n_layers, rows, n6), F32),
        compiler_params=_params(2),
        name="adaln_mod",
    )(cvec, w_ada, b_ada.reshape(n_layers, 1, n6))


def _rope_tables(seq_len):
    pos = np.arange(seq_len)
    row = (pos // GRID_W).astype(np.float32)
    col = (pos % GRID_W).astype(np.float32)
    half = HEAD_DIM // 2
    quarter = half // 2
    inv = (np.float32(ROPE_BASE) ** (-(np.arange(quarter, dtype=np.float32) * np.float32(2.0 / half)))
           ).astype(np.float32)
    j = np.arange(LANES) % HEAD_DIM
    is_col = j >= half
    second = (j % half) >= quarter
    freq = inv[j % quarter]
    ang = (np.where(is_col[None, :], col[:, None], row[:, None]) * freq[None, :]).astype(np.float32)
    cos, sin = np.cos(ang), np.sin(ang)
    sin_prev = np.where(second[None, :], sin, 0.0)
    sin_next = np.where(second[None, :], 0.0, -sin)
    return tuple(jnp.asarray(t, F32) for t in (cos, sin_prev, sin_next))


def _row_halves(n_rows):
    if n_rows % (2 * SUBLANES * 2) or n_rows < 256:
        return [slice(0, n_rows)]
    return [slice(0, n_rows // 2), slice(n_rows // 2, n_rows)]


def _store_dup_heads(dst_ref, rows, kv):
    low = lax.broadcasted_iota(jnp.int32, kv.shape, 1) < HEAD_DIM
    swapped = pltpu.roll(kv, HEAD_DIM, 1)
    dst_ref[rows, 0:LANES] = jnp.where(low, kv, swapped).astype(BF16)
    dst_ref[rows, LANES:2 * LANES] = jnp.where(low, swapped, kv).astype(BF16)


def _norm_modulate_store(x_ref, mod_ref, g_ref, h_ref, shift_row, scale_row, rows):
    y = _rms(x_ref[rows, :]) * g_ref[...]
    h = y * (1.0 + mod_ref[scale_row:scale_row + 1, :]) + mod_ref[shift_row:shift_row + 1, :]
    h_ref[rows, :] = h.astype(BF16)


def _inproj_full_body(*refs, rope):
    if rope:
        (x_ref, mod_ref, g_ref, w_ref, cos_ref, sp_ref, sn_ref,
         q_ref, kd_ref, vd_ref, rx_ref, ry_ref, pu_ref, gt_ref, h_ref) = refs
    else:
        (x_ref, mod_ref, g_ref, w_ref,
         q_ref, kd_ref, vd_ref, rx_ref, ry_ref, pu_ref, gt_ref, h_ref) = refs
    quarter = HEAD_DIM // 4
    d = x_ref.shape[1]
    for rows in _row_halves(x_ref.shape[0]):
        _norm_modulate_store(x_ref, mod_ref, g_ref, h_ref, 0, 1, rows)

        def proj(c0, c1):
            return jnp.dot(h_ref[rows, :], w_ref[:, c0:c1], preferred_element_type=F32)

        def rotary(pc):
            if not rope:
                return pc
            return (pc * cos_ref[rows, :]
                    + pltpu.roll(pc, quarter, 1) * sp_ref[rows, :]
                    + pltpu.roll(pc, LANES - quarter, 1) * sn_ref[rows, :])

        for c in range(ATTN_W // (2 * LANES)):
            pair = proj(c * 2 * LANES, (c + 1) * 2 * LANES)
            for half in range(2):
                col = (2 * c + half) * LANES
                pc = rotary(pair[:, half * LANES:(half + 1) * LANES])
                q_ref[rows, col:col + LANES] = (pc * (HEAD_DIM ** -0.5 * LOG2E)).astype(BF16)
        o = ATTN_W
        kv = proj(o, o + 2 * KV_W)
        _store_dup_heads(kd_ref, rows, rotary(kv[:, :KV_W]))
        _store_dup_heads(vd_ref, rows, kv[:, KV_W:])
        o += 2 * KV_W
        rx_ref[rows, :] = proj(o, o + RNN_W)
        o += RNN_W
        ry_ref[rows, :] = jax.nn.gelu(proj(o, o + RNN_W)).astype(BF16)
        o += RNN_W
        pu_ref[rows, :] = proj(o, o + POOL_W)
        o += POOL_W
        for c in range(N_BRANCH):
            gl = proj(o + c * d, o + (c + 1) * d)
            gt_ref[rows, c * d:(c + 1) * d] = jax.nn.sigmoid(gl).astype(BF16)


def _inproj_kv_body(x_ref, mod_ref, g_ref, w_ref, kd_ref, vd_ref, rx_ref, h_ref):
    rows = slice(0, x_ref.shape[0])
    _norm_modulate_store(x_ref, mod_ref, g_ref, h_ref, 0, 1, rows)
    kv = jnp.dot(h_ref[...], w_ref[:, ATTN_W:ATTN_W + 2 * KV_W], preferred_element_type=F32)
    _store_dup_heads(kd_ref, rows, kv[:, :KV_W])
    _store_dup_heads(vd_ref, rows, kv[:, KV_W:])
    rx_ref[...] = jnp.dot(h_ref[...], w_ref[:, ATTN_W + 2 * KV_W:ATTN_W + 2 * KV_W + RNN_W],
                          preferred_element_type=F32)


def _inproj(x2d, mod_all, layer, mod_row_of_tile, g, w, *, tm, tiles_per_seq, rope_tables, kv_only):
    n_tok, d = x2d.shape
    grid = (n_tok // tm,)
    row = lambda i: (i, 0)
    w_cols = ATTN_W + 2 * KV_W + RNN_W if kv_only else w.shape[2]
    in_specs = [
        pl.BlockSpec((tm, d), row),
        _mod_block(mod_all, layer, mod_row_of_tile),
        _layer_block(g, layer),
        _layer_block(w, layer, (d, w_cols)),
    ]
    args = [x2d, mod_all, g, w]
    if rope_tables is not None:
        in_specs += [pl.BlockSpec((tm, LANES), lambda i: (i % tiles_per_seq, 0))] * 3
        args += list(rope_tables)

    def out(width, dtype):
        return pl.BlockSpec((tm, width), row), jax.ShapeDtypeStruct((n_tok, width), dtype)

    if kv_only:
        outs = [out(2 * KV_W, BF16), out(2 * KV_W, BF16), out(RNN_W, F32)]
        body = _inproj_kv_body
    else:
        outs = [out(ATTN_W, BF16), out(2 * KV_W, BF16), out(2 * KV_W, BF16), out(RNN_W, F32),
                out(RNN_W, BF16), out(POOL_W, F32), out(N_BRANCH * d, BF16)]
        body = functools.partial(_inproj_full_body, rope=rope_tables is not None)
    return pl.pallas_call(
        body,
        grid=grid,
        in_specs=in_specs,
        out_specs=[o[0] for o in outs],
        out_shape=[o[1] for o in outs],
        scratch_shapes=[pltpu.VMEM((tm, d), BF16)],
        compiler_params=_params(1),
        name="inproj_kv" if kv_only else "inproj",
    )(*args)


def _attend(sink_ref, layer, q_blk, k_rows, v_rows, biases):
    tq = q_blk.shape[0]
    heads_per_kv = N_Q_HEADS // N_KV_HEADS
    low = lax.broadcasted_iota(jnp.int32, (tq, LANES), 1) < HEAD_DIM
    zero = jnp.zeros((), BF16)
    outs = []
    for h in range(N_KV_HEADS):
        k_h = k_rows[:, h * LANES:(h + 1) * LANES]
        v_h = v_rows[:, h * LANES:(h + 1) * LANES]
        stacked = []
        for c2 in range(heads_per_kv // 2):
            qc = q_blk[:, (2 * h + c2) * LANES:(2 * h + c2 + 1) * LANES]
            stacked += [jnp.where(low, qc, zero), jnp.where(low, zero, qc)]
        lhs = jnp.concatenate(stacked, axis=0)
        s = lax.dot_general(lhs, k_h, (((1,), (1,)), ((), ())), preferred_element_type=F32)
        blocks = []
        for j, bias in enumerate(biases):
            blk = s[:, j * LANES:(j + 1) * LANES]
            if bias is not None:
                blk = blk + jnp.concatenate([bias] * heads_per_kv, axis=0)
            blocks.append(blk)
        sink_col = jnp.concatenate(
            [jnp.full((tq, 1), sink_ref[layer, h * heads_per_kv + g] * LOG2E, F32)
             for g in range(heads_per_kv)],
            axis=0)
        m = jnp.maximum(jnp.max(functools.reduce(jnp.maximum, blocks), axis=-1, keepdims=True), sink_col)
        probs = [jnp.exp2(blk - m) for blk in blocks]
        denom = jnp.sum(functools.reduce(jnp.add, probs), axis=-1, keepdims=True) + jnp.exp2(sink_col - m)
        p = jnp.concatenate(probs, axis=1).astype(BF16)
        o = jnp.dot(p, v_h, preferred_element_type=F32) * (1.0 / denom)
        for c2 in range(heads_per_kv // 2):
            outs.append(jnp.where(low, o[(2 * c2) * tq:(2 * c2 + 1) * tq],
                                  o[(2 * c2 + 1) * tq:(2 * c2 + 2) * tq]).astype(BF16))
    return jnp.concatenate(outs, axis=1)


def _attn_local_body(sink_ref, q_ref, kd_ref, vd_ref, kc_ref, vc_ref, o_ref, *, layer, n_blocks):
    blk = ATTN_BLOCK
    ii = lax.broadcasted_iota(jnp.int32, (blk, blk), 0)
    jj = lax.broadcasted_iota(jnp.int32, (blk, blk), 1)
    n_ctx_blocks = kc_ref.shape[0] // LANES

    def body(n, carry):
        rows = lambda i: pl.ds(pl.multiple_of(i * blk, blk), blk)
        prv, nxt = jnp.maximum(n - 1, 0), jnp.minimum(n + 1, n_blocks - 1)
        bias_prev = jnp.where((jj >= ii) & (n > 0), 0.0, NEG_INF).astype(F32)
        bias_next = jnp.where((jj <= ii) & (n < n_blocks - 1), 0.0, NEG_INF).astype(F32)
        k_rows = jnp.concatenate([kd_ref[rows(prv), :], kd_ref[rows(n), :], kd_ref[rows(nxt), :],
                                  kc_ref[...]], axis=0)
        v_rows = jnp.concatenate([vd_ref[rows(prv), :], vd_ref[rows(n), :], vd_ref[rows(nxt), :],
                                  vc_ref[...]], axis=0)
        o_ref[rows(n), :] = _attend(sink_ref, layer, q_ref[rows(n), :], k_rows, v_rows,
                                    [bias_prev, None, bias_next] + [None] * n_ctx_blocks)
        return carry

    lax.fori_loop(0, n_blocks, body, 0)


def _attn_ctx_body(sink_ref, q_ref, kc_ref, vc_ref, o_ref, *, layer):
    o_ref[...] = _attend(sink_ref, layer, q_ref[...], kc_ref[...], vc_ref[...],
                         [None] * (kc_ref.shape[0] // LANES))


def _attention(sink, layer, q, kd, vd, kdc, vdc):
    b, s, _ = q.shape
    c = kdc.shape[1]
    kw = 2 * KV_W
    idx = lambda bi: (bi, 0, 0)
    return pl.pallas_call(
        functools.partial(_attn_local_body, layer=layer, n_blocks=s // ATTN_BLOCK),
        grid=(b,),
        in_specs=[
            pl.BlockSpec(memory_space=pltpu.SMEM),
            pl.BlockSpec((None, s, ATTN_W), idx),
            pl.BlockSpec((None, s, kw), idx), pl.BlockSpec((None, s, kw), idx),
            pl.BlockSpec((None, c, kw), idx), pl.BlockSpec((None, c, kw), idx),
        ],
        out_specs=pl.BlockSpec((None, s, ATTN_W), idx),
        out_shape=jax.ShapeDtypeStruct((b, s, ATTN_W), BF16),
        compiler_params=_params(1),
        name="attn_local",
    )(sink, q, kd, vd, kdc, vdc)


def _attention_ctx(sink, layer, qc, kdc, vdc):
    b, c, _ = qc.shape
    kw = 2 * KV_W
    idx = lambda bi: (bi, 0, 0)
    return pl.pallas_call(
        functools.partial(_attn_ctx_body, layer=layer),
        grid=(b,),
        in_specs=[
            pl.BlockSpec(memory_space=pltpu.SMEM),
            pl.BlockSpec((None, c, ATTN_W), idx),
            pl.BlockSpec((None, c, kw), idx), pl.BlockSpec((None, c, kw), idx),
        ],
        out_specs=pl.BlockSpec((None, c, ATTN_W), idx),
        out_shape=jax.ShapeDtypeStruct((b, c, ATTN_W), BF16),
        compiler_params=_params(1),
        name="attn_ctx",
    )(sink, qc, kdc, vdc)


def _pool_body(u_ref, w_ref, sc_ref, o_ref, pad_ref, *, tile):
    length = u_ref.shape[0]
    halo = SUBLANES
    pad_ref[0:halo, :] = jnp.zeros((halo, POOL_W), F32)
    pad_ref[halo + length:2 * halo + length, :] = jnp.zeros((halo, POOL_W), F32)
    pad_ref[halo:halo + length, :] = u_ref[...]
    rows = tile + 2 * halo
    for t0 in range(0, length, tile):
        t = t0 + lax.broadcasted_iota(jnp.int32, (tile, POOL_GW), 0)
        for gi, w in enumerate(POOL_WINDOWS):
            lanes = slice(gi * POOL_GW, (gi + 1) * POOL_GW)
            p = pad_ref[t0:t0 + rows, lanes]
            acc = p
            step = 1
            while step < w:
                acc = acc + pltpu.roll(acc, rows - step, 0)
                step *= 2
            back = (w - 1) // 2
            win = (pltpu.roll(acc, back, 0) if back else acc)[halo:halo + tile]
            lo = jnp.maximum(t - back, 0)
            hi = jnp.minimum(t + w // 2 + 1, length)
            mean = win / (hi - lo).astype(F32)
            dlt = (mean - p[halo:halo + tile]).astype(BF16)
            y = jnp.dot(dlt, w_ref[gi], preferred_element_type=F32) * sc_ref[:, lanes]
            o_ref[t0:t0 + tile, lanes] = y.astype(BF16)


def _pool(pu, layer, w_mix, ch_scale):
    b, length, _ = pu.shape
    tile = min(length, 256)
    idx = lambda bi: (bi, 0, 0)
    return pl.pallas_call(
        functools.partial(_pool_body, tile=tile),
        grid=(b,),
        in_specs=[pl.BlockSpec((None, length, POOL_W), idx), _layer_block(w_mix, layer),
                  _layer_block(ch_scale, layer)],
        out_specs=pl.BlockSpec((None, length, POOL_W), idx),
        out_shape=jax.ShapeDtypeStruct((b, length, POOL_W), BF16),
        scratch_shapes=[pltpu.VMEM((length + 2 * SUBLANES, POOL_W), F32)],
        compiler_params=_params(1),
        name="pool",
    )(pu, w_mix, ch_scale)


def _lru_body(*refs, n_ctx, n_lat, tile, with_ctx_out):
    if with_ctx_out:
        (rxc_ref, rxl_ref, ryc_ref, ryl_ref, cw_ref, cb_ref, wg_ref, ba_ref, bx_ref, lam_ref,
         ol_ref, oc_ref, pad_ref, xl_ref, hf_ref, hb_ref, xp_ref, a_ref, b_ref) = refs
    else:
        (rxc_ref, rxl_ref, ryl_ref, cw_ref, cb_ref, wg_ref, ba_ref, bx_ref, lam_ref,
         ol_ref, pad_ref, xl_ref, hf_ref, hb_ref, xp_ref, a_ref, b_ref) = refs
    halo = SUBLANES
    n_tot = n_ctx + n_lat
    n_slab = RNN_W // LANES

    zeros = jnp.zeros((halo, RNN_W), F32)
    c0, l0 = halo, 2 * halo + n_ctx
    pad_ref[0:halo, :] = zeros
    pad_ref[c0 + n_ctx:l0, :] = zeros
    pad_ref[l0 + n_lat:l0 + n_lat + halo, :] = zeros
    pad_ref[c0:c0 + n_ctx, :] = rxc_ref[...]
    pad_ref[l0:l0 + n_lat, :] = rxl_ref[...]

    def conv_tile(src0, dst_rows):
        rows = tile + 2 * halo
        p = pad_ref[src0 - halo:src0 + tile + halo, :]
        y = (cb_ref[...] + p[halo:halo + tile] * cw_ref[1:2, :]
             + pltpu.roll(p, 1, 0)[halo:halo + tile] * cw_ref[0:1, :]
             + pltpu.roll(p, rows - 1, 0)[halo:halo + tile] * cw_ref[2:3, :]
             + pltpu.roll(p, rows - 2, 0)[halo:halo + tile] * cw_ref[3:4, :])
        for s in range(n_slab):
            for d0 in dst_rows:
                xl_ref[s, d0:d0 + tile, :] = y[:, s * LANES:(s + 1) * LANES]

    for t0 in range(0, n_ctx, tile):
        conv_tile(c0 + t0, (t0, n_tot + t0))
    for t0 in range(0, n_lat, tile):
        conv_tile(l0 + t0, (n_ctx + t0,))

    lam = lam_ref[...]
    log_sig = jnp.minimum(lam, 0.0) - jnp.log1p(jnp.exp(-jnp.abs(lam)))
    decay = LRU_C * log_sig
    half = RNN_W // 2

    def run_chunk(d, base, carry):
        res_ref = hf_ref if d == 0 else hb_ref
        win0 = 0 if d == 0 else n_ctx
        for v in range(SEG):
            for s in range(n_slab):
                xp_ref[d, v * SUBLANES:(v + 1) * SUBLANES, s * LANES:(s + 1) * LANES] = (
                    xl_ref[s, pl.ds(win0 + base + v, SUBLANES, stride=SEG), :])
        for j in range(2):
            ch = slice(j * half, (j + 1) * half)
            xj = xp_ref[d, :, ch]
            gates = jnp.dot(xj.astype(BF16), wg_ref[d, j], preferred_element_type=F32)
            r_gate = jax.nn.sigmoid(gates[:, :half] + ba_ref[d:d + 1, ch])
            i_gate = jax.nn.sigmoid(gates[:, half:] + bx_ref[d:d + 1, ch])
            a = jnp.exp(r_gate * decay[d:d + 1, ch])
            mult = jnp.sqrt(1.0 - a * a)
            a_ref[d, :, ch] = a
            b_ref[d, :, ch] = mult * i_gate * xj
        order = range(SEG) if d == 0 else range(SEG - 1, -1, -1)
        h = jnp.zeros((SUBLANES, RNN_W), F32)
        acum = jnp.ones((SUBLANES, RNN_W), F32)
        for v in order:
            rows = slice(v * SUBLANES, (v + 1) * SUBLANES)
            av = a_ref[d, rows, :]
            h = av * h + b_ref[d, rows, :]
            acum = av * acum
            b_ref[d, rows, :] = h
            a_ref[d, rows, :] = acum
        seg_in = [None] * SUBLANES
        state = carry
        seg_order = range(SUBLANES) if d == 0 else range(SUBLANES - 1, -1, -1)
        for i in seg_order:
            seg_in[i] = state
            state = h[i:i + 1, :] + acum[i:i + 1, :] * state
        seg_state = jnp.concatenate(seg_in, axis=0)
        for v in range(SEG):
            rows = slice(v * SUBLANES, (v + 1) * SUBLANES)
            hv = b_ref[d, rows, :] + a_ref[d, rows, :] * seg_state
            for s in range(n_slab):
                res_ref[s, pl.ds(base + v, SUBLANES, stride=SEG), :] = hv[:, s * LANES:(s + 1) * LANES]
        return state

    n_chunks = n_tot // CHUNK

    def step(c, carries):
        cf, cb = carries
        cf = run_chunk(0, c * CHUNK, cf)
        cb = run_chunk(1, (n_chunks - 1 - c) * CHUNK, cb)
        return cf, cb

    zero_state = jnp.zeros((1, RNN_W), F32)
    lax.fori_loop(0, n_chunks, step, (zero_state, zero_state))

    def emit(out_ref, ry_ref, f0, b0, length):
        for t0 in range(0, length, tile):
            hsum = jnp.concatenate(
                [hf_ref[s, f0 + t0:f0 + t0 + tile, :] + hb_ref[s, b0 + t0:b0 + t0 + tile, :]
                 for s in range(n_slab)], axis=1)
            gate = ry_ref[t0:t0 + tile, :].astype(F32)
            out_ref[t0:t0 + tile, :] = (hsum * gate).astype(BF16)

    emit(ol_ref, ryl_ref, n_ctx, 0, n_lat)
    if with_ctx_out:
        emit(oc_ref, ryc_ref, 0, n_lat, n_ctx)


def _lru(layer, rxc, rxl, ryc, ryl, conv_w, conv_b, wg, b_a, b_x, lam):
    b, n_ctx, _ = rxc.shape
    n_lat = rxl.shape[1]
    n_tot = n_ctx + n_lat
    tile = math.gcd(n_ctx, n_lat, 256)
    assert n_tot % CHUNK == 0 and tile % SUBLANES == 0
    with_ctx_out = ryc is not None
    idx = lambda bi: (bi, 0, 0)
    seq = lambda n, dt=None: pl.BlockSpec((None, n, RNN_W), idx)
    n_slab = RNN_W // LANES
    in_specs = [seq(n_ctx), seq(n_lat)] + ([seq(n_ctx)] if with_ctx_out else []) + [seq(n_lat)]
    args = [rxc, rxl] + ([ryc] if with_ctx_out else []) + [ryl]
    for wgt in (conv_w, conv_b, wg, b_a, b_x, lam):
        in_specs.append(_layer_block(wgt, layer))
        args.append(wgt)
    out_specs = [seq(n_lat)] + ([seq(n_ctx)] if with_ctx_out else [])
    out_shape = [jax.ShapeDtypeStruct((b, n_lat, RNN_W), BF16)]
    if with_ctx_out:
        out_shape.append(jax.ShapeDtypeStruct((b, n_ctx, RNN_W), BF16))
    res = pl.pallas_call(
        functools.partial(_lru_body, n_ctx=n_ctx, n_lat=n_lat, tile=tile, with_ctx_out=with_ctx_out),
        grid=(b,),
        in_specs=in_specs,
        out_specs=out_specs,
        out_shape=out_shape,
        scratch_shapes=[
            pltpu.VMEM((n_tot + 3 * SUBLANES, RNN_W), F32),
            pltpu.VMEM((n_slab, n_tot + n_ctx, LANES), F32),
            pltpu.VMEM((n_slab, n_tot, LANES), F32),
            pltpu.VMEM((n_slab, n_tot, LANES), F32),
            pltpu.VMEM((2, CHUNK, RNN_W), F32),
            pltpu.VMEM((2, CHUNK, RNN_W), F32),
            pltpu.VMEM((2, CHUNK, RNN_W), F32),
        ],
        compiler_params=_params(1),
        name="rglru",
    )(*args)
    return (res[0], res[1]) if with_ctx_out else (res[0], None)


def _mix_ffn_body(x_ref, mod_ref, at_ref, po_ref, rn_ref, gt_ref, wa_ref, wp_ref, wr_ref, wo_ref,
                  gmix_ref, gpre_ref, wgu_ref, wd_ref, gpost_ref, o_ref, m_ref, h_ref, *, d_ff, chunk):
    d = x_ref.shape[1]
    for rows in _row_halves(x_ref.shape[0]):
        merged = None
        for k, (br_ref, w_ref) in enumerate(((at_ref, wa_ref), (po_ref, wp_ref), (rn_ref, wr_ref))):
            t = gt_ref[rows, k * d:(k + 1) * d].astype(F32) * jnp.dot(
                br_ref[rows, :], w_ref[...], preferred_element_type=F32)
            merged = t if merged is None else merged + t
        m_ref[rows, :] = merged.astype(BF16)
        mix = jnp.dot(m_ref[rows, :], wo_ref[...], preferred_element_type=F32)
        o_ref[rows, :] = x_ref[rows, :] + mod_ref[2:3, :] * (_rms(mix) * gmix_ref[...])
        _norm_modulate_store(o_ref, mod_ref, gpre_ref, h_ref, 3, 4, rows)

    halves = _row_halves(x_ref.shape[0])
    accs = [None] * len(halves)
    for c0 in range(0, d_ff, chunk):
        c1 = min(c0 + chunk, d_ff)
        for i, rows in enumerate(halves):
            gate = jnp.dot(h_ref[rows, :], wgu_ref[:, c0:c1], preferred_element_type=F32)
            up = jnp.dot(h_ref[rows, :], wgu_ref[:, d_ff + c0:d_ff + c1], preferred_element_type=F32)
            act = ((gate * jax.nn.sigmoid(gate)) * up).astype(BF16)
            part = jnp.dot(act, wd_ref[c0:c1, :], preferred_element_type=F32)
            accs[i] = part if accs[i] is None else accs[i] + part
    for acc, rows in zip(accs, halves):
        o_ref[rows, :] = o_ref[rows, :] + mod_ref[5:6, :] * (_rms(acc) * gpost_ref[...])


def _mix_ffn(x2d, mod_all, layer, mod_row_of_tile, attn, pool, rnn, gates, wa, wp, wr, wo, g_mix,
             g_pre, w_gu, w_down, g_post, *, tm):
    n_tok, d = x2d.shape
    d_ff = w_down.shape[1]
    row = lambda i: (i, 0)
    tok = lambda arr: pl.BlockSpec((tm, arr.shape[1]), row)
    params = (wa, wp, wr, wo, g_mix, g_pre, w_gu, w_down, g_post)
    return pl.pallas_call(
        functools.partial(_mix_ffn_body, d_ff=d_ff, chunk=1024),
        grid=(n_tok // tm,),
        in_specs=[tok(x2d), _mod_block(mod_all, layer, mod_row_of_tile),
                  tok(attn), tok(pool), tok(rnn), tok(gates)]
                 + [_layer_block(p, layer) for p in params],
        out_specs=pl.BlockSpec((tm, d), row),
        out_shape=jax.ShapeDtypeStruct((n_tok, d), F32),
        scratch_shapes=[pltpu.VMEM((tm, d), BF16), pltpu.VMEM((tm, d), BF16)],
        compiler_params=_params(1),
        name="mix_ffn",
    )(x2d, mod_all, attn, pool, rnn, gates, *params)


def _pack_gate_weights(w_a, w_x):
    def block_diag(w):
        eye = jnp.eye(RNN_BLOCKS, dtype=w.dtype)
        return jnp.einsum('ldhij,hg->ldhigj', w, eye).reshape(w.shape[:2] + (RNN_W, RNN_W))
    bd_a, bd_x = block_diag(w_a), block_diag(w_x)
    half = RNN_W // 2
    halves = []
    for j in range(2):
        sl = slice(j * half, (j + 1) * half)
        halves.append(jnp.concatenate([bd_a[:, :, sl, sl], bd_x[:, :, sl, sl]], axis=3))
    return jnp.stack(halves, axis=2).astype(BF16)


def kernel(x, c, ctx, c_ctx, w_ada, b_ada, g_pre_mix, g_post_mix, g_pre_ffn, g_post_ffn, w_in, attn_sink, w_attn_o, pool_mix, pool_scale, w_pool_o, conv_w, conv_b, lru_w_a, lru_b_a, lru_w_x, lru_b_x, lru_lambda, w_rnn_o, w_out, w_gu, w_down):
    bsz, seq, d = x.shape
    n_ctx = ctx.shape[1]
    depth = w_ada.shape[0]
    tm = min(512, seq)
    tm_ctx = min(tm, n_ctx)

    mod_rows = -(-(bsz + 1) // SUBLANES) * SUBLANES
    cvec = jnp.zeros((mod_rows, d), F32).at[:bsz].set(c).at[bsz].set(c_ctx)
    mod_all = _adaln_mod(cvec, w_ada, b_ada).reshape(depth, mod_rows, 6, d)

    rope = _rope_tables(seq)
    lat_row = lambda i: i // (seq // tm)
    ctx_row = lambda i: bsz

    rowvec = lambda a: a.reshape(depth, 1, -1)
    w_in_b, w_gu_b, w_dn_b = w_in.astype(BF16), w_gu.astype(BF16), w_down.astype(BF16)
    wa, wp, wr, wo = (w.astype(BF16) for w in (w_attn_o, w_pool_o, w_rnn_o, w_out))
    w_mix = pool_mix.astype(BF16)
    wg = _pack_gate_weights(lru_w_a, lru_w_x)
    g_pre, g_post = rowvec(g_pre_mix), rowvec(g_post_mix)
    gf_pre, gf_post = rowvec(g_pre_ffn), rowvec(g_post_ffn)
    p_scale, cv_b = rowvec(pool_scale), rowvec(conv_b)

    x2 = x.reshape(bsz * seq, d)
    c2 = ctx.reshape(bsz * n_ctx, d)
    r3 = lambda a, n: a.reshape(bsz, n, a.shape[-1])
    flat = lambda a: a.reshape(-1, a.shape[-1])
    for l in range(depth):
        need_ctx = l < depth - 1
        q, kd, vd, rx, ry, pu, gt = _inproj(
            x2, mod_all, l, lat_row, g_pre, w_in_b, tm=tm, tiles_per_seq=seq // tm, rope_tables=rope,
            kv_only=False)
        if need_ctx:
            qc, kdc, vdc, rxc, ryc, puc, gtc = _inproj(
                c2, mod_all, l, ctx_row, g_pre, w_in_b, tm=tm_ctx, tiles_per_seq=1, rope_tables=None,
                kv_only=False)
        else:
            kdc, vdc, rxc = _inproj(
                c2, mod_all, l, ctx_row, g_pre, w_in_b, tm=tm_ctx, tiles_per_seq=1, rope_tables=None,
                kv_only=True)
            ryc = None

        kdc3, vdc3 = r3(kdc, n_ctx), r3(vdc, n_ctx)
        attn_l = _attention(attn_sink, l, r3(q, seq), r3(kd, seq), r3(vd, seq), kdc3, vdc3)
        pool_l = _pool(r3(pu, seq), l, w_mix, p_scale)
        rnn_l, rnn_c = _lru(l, r3(rxc, n_ctx), r3(rx, seq), None if ryc is None else r3(ryc, n_ctx),
                            r3(ry, seq), conv_w, cv_b, wg, lru_b_a, lru_b_x, lru_lambda)

        x2 = _mix_ffn(x2, mod_all, l, lat_row, flat(attn_l), flat(pool_l), flat(rnn_l), gt, wa, wp, wr, wo,
                      g_post, gf_pre, w_gu_b, w_dn_b, gf_post, tm=tm)

        if need_ctx:
            attn_c = _attention_ctx(attn_sink, l, r3(qc, n_ctx), kdc3, vdc3)
            pool_c = _pool(r3(puc, n_ctx), l, w_mix, p_scale)
            c2 = _mix_ffn(c2, mod_all, l, ctx_row, flat(attn_c), flat(pool_c), flat(rnn_c), gtc, wa, wp, wr,
                          wo, g_post, gf_pre, w_gu_b, w_dn_b, gf_post, tm=tm_ctx)
    return x2.reshape(bsz, seq, d)
```

```python
import functools
import math

import jax
import jax.numpy as jnp
import numpy as np
from jax import lax
from jax.experimental import pallas as pl
from jax.experimental.pallas import tpu as pltpu

F32 = jnp.float32
BF16 = jnp.bfloat16

GRID_W = 64
HEAD_DIM = 64
N_Q_HEADS = 8
N_KV_HEADS = 2
ATTN_W = N_Q_HEADS * HEAD_DIM
KV_W = N_KV_HEADS * HEAD_DIM
ATTN_BLOCK = 128
ROPE_BASE = 10000.0
POOL_WINDOWS = (2, 4, 8, 16)
POOL_W = 512
POOL_GW = POOL_W // len(POOL_WINDOWS)
RNN_W = 512
RNN_BLOCKS = 8
CONV_W = 4
LRU_C = 8.0
N_BRANCH = 3
EPS = 1e-6
NEG_INF = -1e30
LOG2E = 1.4426950408889634

LANES = 128
SUBLANES = 8
VMEM_LIMIT_BYTES = 56 * 1024 * 1024

SEG = 36
CHUNK = SUBLANES * SEG


def _params(n_axes):
    return pltpu.CompilerParams(
        dimension_semantics=("arbitrary",) * n_axes, vmem_limit_bytes=VMEM_LIMIT_BYTES)


def _layer_block(arr, layer, block=None):
    block = tuple(arr.shape[1:]) if block is None else block
    index = (layer,) + (0,) * len(block)
    return pl.BlockSpec((None,) + block, lambda *_: index, pipeline_mode=pl.Buffered(1))


def _mod_block(mod_all, layer, mod_row_of_tile):
    return pl.BlockSpec((None, None) + mod_all.shape[2:], lambda i: (layer, mod_row_of_tile(i), 0, 0))


def _rms(xf):
    return xf * lax.rsqrt(jnp.mean(xf * xf, axis=-1, keepdims=True) + EPS)


def _mod_body(c_ref, w_ref, b_ref, o_ref):
    c = c_ref[...]
    s = (c * jax.nn.sigmoid(c)).astype(BF16)
    o_ref[0] = jnp.dot(s, w_ref[0].astype(BF16), preferred_element_type=F32) + b_ref[0]


def _adaln_mod(cvec, w_ada, b_ada):
    n_layers, d, n6 = w_ada.shape
    rows = cvec.shape[0]
    tn = n6 // 4
    return pl.pallas_call(
        _mod_body,
        grid=(n_layers, n6 // tn),
        in_specs=[
            pl.BlockSpec((rows, d), lambda l, j: (0, 0)),
            pl.BlockSpec((1, d, tn), lambda l, j: (l, 0, j)),
            pl.BlockSpec((1, 1, tn), lambda l, j: (l, 0, j)),
        ],
        out_specs=pl.BlockSpec((1, rows, tn), lambda l, j: (l, 0, j)),
        out_shape=jax.ShapeDtypeStruct((n_layers, rows, n6), F32),
        compiler_params=_params(2),
        name="adaln_mod",
    )(cvec, w_ada, b_ada.reshape(n_layers, 1, n6))


def _rope_tables(seq_len):
    pos = np.arange(seq_len)
    row = (pos // GRID_W).astype(np.float32)
    col = (pos % GRID_W).astype(np.float32)
    half = HEAD_DIM // 2
    quarter = half // 2
    inv = (np.float32(ROPE_BASE) ** (-(np.arange(quarter, dtype=np.float32) * np.float32(2.0 / half)))
           ).astype(np.float32)
    j = np.arange(LANES) % HEAD_DIM
    is_col = j >= half
    second = (j % half) >= quarter
    freq = inv[j % quarter]
    ang = (np.where(is_col[None, :], col[:, None], row[:, None]) * freq[None, :]).astype(np.float32)
    cos, sin = np.cos(ang), np.sin(ang)
    sin_prev = np.where(second[None, :], sin, 0.0)
    sin_next = np.where(second[None, :], 0.0, -sin)
    return tuple(jnp.asarray(t, F32) for t in (cos, sin_prev, sin_next))


def _row_halves(n_rows):
    if n_rows % (2 * SUBLANES * 2) or n_rows < 256:
        return [slice(0, n_rows)]
    return [slice(0, n_rows // 2), slice(n_rows // 2, n_rows)]


def _store_dup_heads(dst_ref, rows, kv):
    low = lax.broadcasted_iota(jnp.int32, kv.shape, 1) < HEAD_DIM
    swapped = pltpu.roll(kv, HEAD_DIM, 1)
    dst_ref[rows, 0:LANES] = jnp.where(low, kv, swapped).astype(BF16)
    dst_ref[rows, LANES:2 * LANES] = jnp.where(low, swapped, kv).astype(BF16)


def _norm_modulate_store(x_ref, mod_ref, g_ref, h_ref, shift_row, scale_row, rows):
    y = _rms(x_ref[rows, :]) * g_ref[...]
    h = y * (1.0 + mod_ref[scale_row:scale_row + 1, :]) + mod_ref[shift_row:shift_row + 1, :]
    h_ref[rows, :] = h.astype(BF16)


def _inproj_full_body(*refs, rope):
    if rope:
        (x_ref, mod_ref, g_ref, w_ref, cos_ref, sp_ref, sn_ref,
         q_ref, kd_ref, vd_ref, rx_ref, ry_ref, pu_ref, gt_ref, h_ref) = refs
    else:
        (x_ref, mod_ref, g_ref, w_ref,
         q_ref, kd_ref, vd_ref, rx_ref, ry_ref, pu_ref, gt_ref, h_ref) = refs
    quarter = HEAD_DIM // 4
    d = x_ref.shape[1]
    for rows in _row_halves(x_ref.shape[0]):
        _norm_modulate_store(x_ref, mod_ref, g_ref, h_ref, 0, 1, rows)

        def proj(c0, c1):
            return jnp.dot(h_ref[rows, :], w_ref[:, c0:c1], preferred_element_type=F32)

        def rotary(pc):
            if not rope:
                return pc
            return (pc * cos_ref[rows, :]
                    + pltpu.roll(pc, quarter, 1) * sp_ref[rows, :]
                    + pltpu.roll(pc, LANES - quarter, 1) * sn_ref[rows, :])

        for c in range(ATTN_W // (2 * LANES)):
            pair = proj(c * 2 * LANES, (c + 1) * 2 * LANES)
            for half in range(2):
                col = (2 * c + half) * LANES
                pc = rotary(pair[:, half * LANES:(half + 1) * LANES])
                q_ref[rows, col:col + LANES] = (pc * (HEAD_DIM ** -0.5 * LOG2E)).astype(BF16)
        o = ATTN_W
        kv = proj(o, o + 2 * KV_W)
        _store_dup_heads(kd_ref, rows, rotary(kv[:, :KV_W]))
        _store_dup_heads(vd_ref, rows, kv[:, KV_W:])
        o += 2 * KV_W
        rx_ref[rows, :] = proj(o, o + RNN_W)
        o += RNN_W
        ry_ref[rows, :] = jax.nn.gelu(proj(o, o + RNN_W)).astype(BF16)
        o += RNN_W
        pu_ref[rows, :] = proj(o, o + POOL_W)
        o += POOL_W
        for c in range(N_BRANCH):
            gl = proj(o + c * d, o + (c + 1) * d)
            gt_ref[rows, c * d:(c + 1) * d] = jax.nn.sigmoid(gl).astype(BF16)


def _inproj_kv_body(x_ref, mod_ref, g_ref, w_ref, kd_ref, vd_ref, rx_ref, h_ref):
    rows = slice(0, x_ref.shape[0])
    _norm_modulate_store(x_ref, mod_ref, g_ref, h_ref, 0, 1, rows)
    kv = jnp.dot(h_ref[...], w_ref[:, ATTN_W:ATTN_W + 2 * KV_W], preferred_element_type=F32)
    _store_dup_heads(kd_ref, rows, kv[:, :KV_W])
    _store_dup_heads(vd_ref, rows, kv[:, KV_W:])
    rx_ref[...] = jnp.dot(h_ref[...], w_ref[:, ATTN_W + 2 * KV_W:ATTN_W + 2 * KV_W + RNN_W],
                          preferred_element_type=F32)


def _inproj(x2d, mod_all, layer, mod_row_of_tile, g, w, *, tm, tiles_per_seq, rope_tables, kv_only):
    n_tok, d = x2d.shape
    grid = (n_tok // tm,)
    row = lambda i: (i, 0)
    w_cols = ATTN_W + 2 * KV_W + RNN_W if kv_only else w.shape[2]
    in_specs = [
        pl.BlockSpec((tm, d), row),
        _mod_block(mod_all, layer, mod_row_of_tile),
        _layer_block(g, layer),
        _layer_block(w, layer, (d, w_cols)),
    ]
    args = [x2d, mod_all, g, w]
    if rope_tables is not None:
        in_specs += [pl.BlockSpec((tm, LANES), lambda i: (i % tiles_per_seq, 0))] * 3
        args += list(rope_tables)

    def out(width, dtype):
        return pl.BlockSpec((tm, width), row), jax.ShapeDtypeStruct((n_tok, width), dtype)

    if kv_only:
        outs = [out(2 * KV_W, BF16), out(2 * KV_W, BF16), out(RNN_W, F32)]
        body = _inproj_kv_body
    else:
        outs = [out(ATTN_W, BF16), out(2 * KV_W, BF16), out(2 * KV_W, BF16), out(RNN_W, F32),
                out(RNN_W, BF16), out(POOL_W, F32), out(N_BRANCH * d, BF16)]
        body = functools.partial(_inproj_full_body, rope=rope_tables is not None)
    return pl.pallas_call(
        body,
        grid=grid,
        in_specs=in_specs,
        out_specs=[o[0] for o in outs],
        out_shape=[o[1] for o in outs],
        scratch_shapes=[pltpu.VMEM((tm, d), BF16)],
        compiler_params=_params(1),
        name="inproj_kv" if kv_only else "inproj",
    )(*args)


def _attend(sink_ref, layer, items):
    heads_per_kv = N_Q_HEADS // N_KV_HEADS
    zero = jnp.zeros((), BF16)

    def scores(item, h):
        q_blk, k_rows, _, biases = item
        low = lax.broadcasted_iota(jnp.int32, (q_blk.shape[0], LANES), 1) < HEAD_DIM
        k_h = k_rows[:, h * LANES:(h + 1) * LANES]
        stacked = []
        for c2 in range(heads_per_kv // 2):
            qc = q_blk[:, (2 * h + c2) * LANES:(2 * h + c2 + 1) * LANES]
            stacked += [jnp.where(low, qc, zero), jnp.where(low, zero, qc)]
        lhs = jnp.concatenate(stacked, axis=0)
        s = lax.dot_general(lhs, k_h, (((1,), (1,)), ((), ())), preferred_element_type=F32)
        blocks = []
        for j, bias in enumerate(biases):
            blk = s[:, j * LANES:(j + 1) * LANES]
            if bias is not None:
                blk = blk + jnp.concatenate([bias] * heads_per_kv, axis=0)
            blocks.append(blk)
        return blocks

    def softmax(h, blocks):
        tq = blocks[0].shape[0] // heads_per_kv
        sink_col = jnp.concatenate(
            [jnp.full((tq, 1), sink_ref[layer, h * heads_per_kv + g] * LOG2E, F32)
             for g in range(heads_per_kv)],
            axis=0)
        m = jnp.maximum(jnp.max(functools.reduce(jnp.maximum, blocks), axis=-1, keepdims=True), sink_col)
        probs = [jnp.exp2(blk - m) for blk in blocks]
        denom = jnp.sum(functools.reduce(jnp.add, probs), axis=-1, keepdims=True) + jnp.exp2(sink_col - m)
        return jnp.concatenate(probs, axis=1).astype(BF16), denom

    def weighted(item, h, p, denom):
        tq = item[0].shape[0]
        low = lax.broadcasted_iota(jnp.int32, (tq, LANES), 1) < HEAD_DIM
        v_h = item[2][:, h * LANES:(h + 1) * LANES]
        o = jnp.dot(p, v_h, preferred_element_type=F32) * (1.0 / denom)
        return [jnp.where(low, o[(2 * c2) * tq:(2 * c2 + 1) * tq],
                          o[(2 * c2 + 1) * tq:(2 * c2 + 2) * tq]).astype(BF16)
                for c2 in range(heads_per_kv // 2)]

    units = [(item, h) for item in items for h in range(N_KV_HEADS)]
    all_scores = [scores(item, h) for item, h in units]
    all_probs = [softmax(h, blocks) for (_, h), blocks in zip(units, all_scores)]
    all_outs = [weighted(item, h, *pd) for (item, h), pd in zip(units, all_probs)]
    return [jnp.concatenate([o for outs in all_outs[i * N_KV_HEADS:(i + 1) * N_KV_HEADS] for o in outs],
                            axis=1) for i in range(len(items))]


def _attn_local_body(sink_ref, q_ref, kd_ref, vd_ref, kc_ref, vc_ref, o_ref, *, layer, n_blocks, group):
    blk = ATTN_BLOCK
    ii = lax.broadcasted_iota(jnp.int32, (blk, blk), 0)
    jj = lax.broadcasted_iota(jnp.int32, (blk, blk), 1)
    n_ctx_blocks = kc_ref.shape[0] // LANES
    rows = lambda i: pl.ds(pl.multiple_of(i * blk, blk), blk)

    def item(n):
        prv, nxt = jnp.maximum(n - 1, 0), jnp.minimum(n + 1, n_blocks - 1)
        bias_prev = jnp.where((jj >= ii) & (n > 0), 0.0, NEG_INF).astype(F32)
        bias_next = jnp.where((jj <= ii) & (n < n_blocks - 1), 0.0, NEG_INF).astype(F32)
        k_rows = jnp.concatenate([kd_ref[rows(prv), :], kd_ref[rows(n), :], kd_ref[rows(nxt), :],
                                  kc_ref[...]], axis=0)
        v_rows = jnp.concatenate([vd_ref[rows(prv), :], vd_ref[rows(n), :], vd_ref[rows(nxt), :],
                                  vc_ref[...]], axis=0)
        return (q_ref[rows(n), :], k_rows, v_rows, [bias_prev, None, bias_next] + [None] * n_ctx_blocks)

    def body(i, carry):
        blocks = [i * group + g for g in range(group)]
        for n, out in zip(blocks, _attend(sink_ref, layer, [item(n) for n in blocks])):
            o_ref[rows(n), :] = out
        return carry

    lax.fori_loop(0, n_blocks // group, body, 0)


def _attn_ctx_body(sink_ref, q_ref, kc_ref, vc_ref, o_ref, *, layer):
    item = (q_ref[...], kc_ref[...], vc_ref[...], [None] * (kc_ref.shape[0] // LANES))
    o_ref[...] = _attend(sink_ref, layer, [item])[0]


def _attention(sink, layer, q, kd, vd, kdc, vdc):
    b, s, _ = q.shape
    c = kdc.shape[1]
    kw = 2 * KV_W
    n_blocks = s // ATTN_BLOCK
    group = math.gcd(n_blocks, 4)
    idx = lambda bi: (bi, 0, 0)
    return pl.pallas_call(
        functools.partial(_attn_local_body, layer=layer, n_blocks=n_blocks, group=group),
        grid=(b,),
        in_specs=[
            pl.BlockSpec(memory_space=pltpu.SMEM),
            pl.BlockSpec((None, s, ATTN_W), idx),
            pl.BlockSpec((None, s, kw), idx), pl.BlockSpec((None, s, kw), idx),
            pl.BlockSpec((None, c, kw), idx), pl.BlockSpec((None, c, kw), idx),
        ],
        out_specs=pl.BlockSpec((None, s, ATTN_W), idx),
        out_shape=jax.ShapeDtypeStruct((b, s, ATTN_W), BF16),
        compiler_params=_params(1),
        name="attn_local",
    )(sink, q, kd, vd, kdc, vdc)


def _attention_ctx(sink, layer, qc, kdc, vdc):
    b, c, _ = qc.shape
    kw = 2 * KV_W
    idx = lambda bi: (bi, 0, 0)
    return pl.pallas_call(
        functools.partial(_attn_ctx_body, layer=layer),
        grid=(b,),
        in_specs=[
            pl.BlockSpec(memory_space=pltpu.SMEM),
            pl.BlockSpec((None, c, ATTN_W), idx),
            pl.BlockSpec((None, c, kw), idx), pl.BlockSpec((None, c, kw), idx),
        ],
        out_specs=pl.BlockSpec((None, c, ATTN_W), idx),
        out_shape=jax.ShapeDtypeStruct((b, c, ATTN_W), BF16),
        compiler_params=_params(1),
        name="attn_ctx",
    )(sink, qc, kdc, vdc)


def _pool_body(u_ref, w_ref, sc_ref, o_ref, pad_ref, *, tile):
    length = u_ref.shape[0]
    halo = SUBLANES
    pad_ref[0:halo, :] = jnp.zeros((halo, POOL_W), F32)
    pad_ref[halo + length:2 * halo + length, :] = jnp.zeros((halo, POOL_W), F32)
    pad_ref[halo:halo + length, :] = u_ref[...]
    rows = tile + 2 * halo
    for t0 in range(0, length, tile):
        t = t0 + lax.broadcasted_iota(jnp.int32, (tile, POOL_GW), 0)
        for gi, w in enumerate(POOL_WINDOWS):
            lanes = slice(gi * POOL_GW, (gi + 1) * POOL_GW)
            p = pad_ref[t0:t0 + rows, lanes]
            acc = p
            step = 1
            while step < w:
                acc = acc + pltpu.roll(acc, rows - step, 0)
                step *= 2
            back = (w - 1) // 2
            win = (pltpu.roll(acc, back, 0) if back else acc)[halo:halo + tile]
            lo = jnp.maximum(t - back, 0)
            hi = jnp.minimum(t + w // 2 + 1, length)
            mean = win / (hi - lo).astype(F32)
            dlt = (mean - p[halo:halo + tile]).astype(BF16)
            y = jnp.dot(dlt, w_ref[gi], preferred_element_type=F32) * sc_ref[:, lanes]
            o_ref[t0:t0 + tile, lanes] = y.astype(BF16)


def _pool(pu, layer, w_mix, ch_scale):
    b, length, _ = pu.shape
    tile = min(length, 256)
    idx = lambda bi: (bi, 0, 0)
    return pl.pallas_call(
        functools.partial(_pool_body, tile=tile),
        grid=(b,),
        in_specs=[pl.BlockSpec((None, length, POOL_W), idx), _layer_block(w_mix, layer),
                  _layer_block(ch_scale, layer)],
        out_specs=pl.BlockSpec((None, length, POOL_W), idx),
        out_shape=jax.ShapeDtypeStruct((b, length, POOL_W), BF16),
        scratch_shapes=[pltpu.VMEM((length + 2 * SUBLANES, POOL_W), F32)],
        compiler_params=_params(1),
        name="pool",
    )(pu, w_mix, ch_scale)


def _lru_body(*refs, n_ctx, n_lat, tile, with_ctx_out):
    if with_ctx_out:
        (rxc_ref, rxl_ref, ryc_ref, ryl_ref, cw_ref, cb_ref, wg_ref, ba_ref, bx_ref, lam_ref,
         ol_ref, oc_ref, pad_ref, xl_ref, hf_ref, hb_ref, xp_ref, a_ref, b_ref) = refs
    else:
        (rxc_ref, rxl_ref, ryl_ref, cw_ref, cb_ref, wg_ref, ba_ref, bx_ref, lam_ref,
         ol_ref, pad_ref, xl_ref, hf_ref, hb_ref, xp_ref, a_ref, b_ref) = refs
    halo = SUBLANES
    n_tot = n_ctx + n_lat
    n_slab = RNN_W // LANES

    zeros = jnp.zeros((halo, RNN_W), F32)
    c0, l0 = halo, 2 * halo + n_ctx
    pad_ref[0:halo, :] = zeros
    pad_ref[c0 + n_ctx:l0, :] = zeros
    pad_ref[l0 + n_lat:l0 + n_lat + halo, :] = zeros
    pad_ref[c0:c0 + n_ctx, :] = rxc_ref[...]
    pad_ref[l0:l0 + n_lat, :] = rxl_ref[...]

    def conv_tile(src0, dst_rows):
        rows = tile + 2 * halo
        p = pad_ref[src0 - halo:src0 + tile + halo, :]
        y = (cb_ref[...] + p[halo:halo + tile] * cw_ref[1:2, :]
             + pltpu.roll(p, 1, 0)[halo:halo + tile] * cw_ref[0:1, :]
             + pltpu.roll(p, rows - 1, 0)[halo:halo + tile] * cw_ref[2:3, :]
             + pltpu.roll(p, rows - 2, 0)[halo:halo + tile] * cw_ref[3:4, :])
        for s in range(n_slab):
            for d0 in dst_rows:
                xl_ref[s, d0:d0 + tile, :] = y[:, s * LANES:(s + 1) * LANES]

    for t0 in range(0, n_ctx, tile):
        conv_tile(c0 + t0, (t0, n_tot + t0))
    for t0 in range(0, n_lat, tile):
        conv_tile(l0 + t0, (n_ctx + t0,))

    lam = lam_ref[...]
    log_sig = jnp.minimum(lam, 0.0) - jnp.log1p(jnp.exp(-jnp.abs(lam)))
    decay = LRU_C * log_sig
    half = RNN_W // 2

    def run_chunk(d, base, carry):
        res_ref = hf_ref if d == 0 else hb_ref
        win0 = 0 if d == 0 else n_ctx
        for v in range(SEG):
            for s in range(n_slab):
                xp_ref[d, v * SUBLANES:(v + 1) * SUBLANES, s * LANES:(s + 1) * LANES] = (
                    xl_ref[s, pl.ds(win0 + base + v, SUBLANES, stride=SEG), :])
        for j in range(2):
            ch = slice(j * half, (j + 1) * half)
            xj = xp_ref[d, :, ch]
            gates = jnp.dot(xj.astype(BF16), wg_ref[d, j], preferred_element_type=F32)
            r_gate = jax.nn.sigmoid(gates[:, :half] + ba_ref[d:d + 1, ch])
            i_gate = jax.nn.sigmoid(gates[:, half:] + bx_ref[d:d + 1, ch])
            a = jnp.exp(r_gate * decay[d:d + 1, ch])
            mult = jnp.sqrt(1.0 - a * a)
            a_ref[d, :, ch] = a
            b_ref[d, :, ch] = mult * i_gate * xj
        order = range(SEG) if d == 0 else range(SEG - 1, -1, -1)
        h = jnp.zeros((SUBLANES, RNN_W), F32)
        acum = jnp.ones((SUBLANES, RNN_W), F32)
        for v in order:
            rows = slice(v * SUBLANES, (v + 1) * SUBLANES)
            av = a_ref[d, rows, :]
            h = av * h + b_ref[d, rows, :]
            acum = av * acum
            b_ref[d, rows, :] = h
            a_ref[d, rows, :] = acum
        seg_in = [None] * SUBLANES
        state = carry
        seg_order = range(SUBLANES) if d == 0 else range(SUBLANES - 1, -1, -1)
        for i in seg_order:
            seg_in[i] = state
            state = h[i:i + 1, :] + acum[i:i + 1, :] * state
        seg_state = jnp.concatenate(seg_in, axis=0)
        for v in range(SEG):
            rows = slice(v * SUBLANES, (v + 1) * SUBLANES)
            hv = b_ref[d, rows, :] + a_ref[d, rows, :] * seg_state
            for s in range(n_slab):
                res_ref[s, pl.ds(base + v, SUBLANES, stride=SEG), :] = hv[:, s * LANES:(s + 1) * LANES]
        return state

    n_chunks = n_tot // CHUNK

    def step(c, carries):
        cf, cb = carries
        cf = run_chunk(0, c * CHUNK, cf)
        cb = run_chunk(1, (n_chunks - 1 - c) * CHUNK, cb)
        return cf, cb

    zero_state = jnp.zeros((1, RNN_W), F32)
    lax.fori_loop(0, n_chunks, step, (zero_state, zero_state))

    def emit(out_ref, ry_ref, f0, b0, length):
        for t0 in range(0, length, tile):
            hsum = jnp.concatenate(
                [hf_ref[s, f0 + t0:f0 + t0 + tile, :] + hb_ref[s, b0 + t0:b0 + t0 + tile, :]
                 for s in range(n_slab)], axis=1)
            gate = ry_ref[t0:t0 + tile, :].astype(F32)
            out_ref[t0:t0 + tile, :] = (hsum * gate).astype(BF16)

    emit(ol_ref, ryl_ref, n_ctx, 0, n_lat)
    if with_ctx_out:
        emit(oc_ref, ryc_ref, 0, n_lat, n_ctx)


def _lru(layer, rxc, rxl, ryc, ryl, conv_w, conv_b, wg, b_a, b_x, lam):
    b, n_ctx, _ = rxc.shape
    n_lat = rxl.shape[1]
    n_tot = n_ctx + n_lat
    tile = math.gcd(n_ctx, n_lat, 256)
    assert n_tot % CHUNK == 0 and tile % SUBLANES == 0
    with_ctx_out = ryc is not None
    idx = lambda bi: (bi, 0, 0)
    seq = lambda n, dt=None: pl.BlockSpec((None, n, RNN_W), idx)
    n_slab = RNN_W // LANES
    in_specs = [seq(n_ctx), seq(n_lat)] + ([seq(n_ctx)] if with_ctx_out else []) + [seq(n_lat)]
    args = [rxc, rxl] + ([ryc] if with_ctx_out else []) + [ryl]
    for wgt in (conv_w, conv_b, wg, b_a, b_x, lam):
        in_specs.append(_layer_block(wgt, layer))
        args.append(wgt)
    out_specs = [seq(n_lat)] + ([seq(n_ctx)] if with_ctx_out else [])
    out_shape = [jax.ShapeDtypeStruct((b, n_lat, RNN_W), BF16)]
    if with_ctx_out:
        out_shape.append(jax.ShapeDtypeStruct((b, n_ctx, RNN_W), BF16))
    res = pl.pallas_call(
        functools.partial(_lru_body, n_ctx=n_ctx, n_lat=n_lat, tile=tile, with_ctx_out=with_ctx_out),
        grid=(b,),
        in_specs=in_specs,
        out_specs=out_specs,
        out_shape=out_shape,
        scratch_shapes=[
            pltpu.VMEM((n_tot + 3 * SUBLANES, RNN_W), F32),
            pltpu.VMEM((n_slab, n_tot + n_ctx, LANES), F32),
            pltpu.VMEM((n_slab, n_tot, LANES), F32),
            pltpu.VMEM((n_slab, n_tot, LANES), F32),
            pltpu.VMEM((2, CHUNK, RNN_W), F32),
            pltpu.VMEM((2, CHUNK, RNN_W), F32),
            pltpu.VMEM((2, CHUNK, RNN_W), F32),
        ],
        compiler_params=_params(1),
        name="rglru",
    )(*args)
    return (res[0], res[1]) if with_ctx_out else (res[0], None)


def _mix_ffn_body(x_ref, mod_ref, at_ref, po_ref, rn_ref, gt_ref, wa_ref, wp_ref, wr_ref, wo_ref,
                  gmix_ref, gpre_ref, wgu_ref, wd_ref, gpost_ref, o_ref, m_ref, h_ref, *, d_ff, chunk):
    d = x_ref.shape[1]
    for rows in _row_halves(x_ref.shape[0]):
        merged = None
        for k, (br_ref, w_ref) in enumerate(((at_ref, wa_ref), (po_ref, wp_ref), (rn_ref, wr_ref))):
            t = gt_ref[rows, k * d:(k + 1) * d].astype(F32) * jnp.dot(
                br_ref[rows, :], w_ref[...], preferred_element_type=F32)
            merged = t if merged is None else merged + t
        m_ref[rows, :] = merged.astype(BF16)
        mix = jnp.dot(m_ref[rows, :], wo_ref[...], preferred_element_type=F32)
        o_ref[rows, :] = x_ref[rows, :] + mod_ref[2:3, :] * (_rms(mix) * gmix_ref[...])
        _norm_modulate_store(o_ref, mod_ref, gpre_ref, h_ref, 3, 4, rows)

    halves = _row_halves(x_ref.shape[0])
    accs = [None] * len(halves)
    for c0 in range(0, d_ff, chunk):
        c1 = min(c0 + chunk, d_ff)
        for i, rows in enumerate(halves):
            gate = jnp.dot(h_ref[rows, :], wgu_ref[:, c0:c1], preferred_element_type=F32)
            up = jnp.dot(h_ref[rows, :], wgu_ref[:, d_ff + c0:d_ff + c1], preferred_element_type=F32)
            act = ((gate * jax.nn.sigmoid(gate)) * up).astype(BF16)
            part = jnp.dot(act, wd_ref[c0:c1, :], preferred_element_type=F32)
            accs[i] = part if accs[i] is None else accs[i] + part
    for acc, rows in zip(accs, halves):
        o_ref[rows, :] = o_ref[rows, :] + mod_ref[5:6, :] * (_rms(acc) * gpost_ref[...])


def _mix_ffn(x2d, mod_all, layer, mod_row_of_tile, attn, pool, rnn, gates, wa, wp, wr, wo, g_mix,
             g_pre, w_gu, w_down, g_post, *, tm):
    n_tok, d = x2d.shape
    d_ff = w_down.shape[1]
    row = lambda i: (i, 0)
    tok = lambda arr: pl.BlockSpec((tm, arr.shape[1]), row)
    params = (wa, wp, wr, wo, g_mix, g_pre, w_gu, w_down, g_post)
    return pl.pallas_call(
        functools.partial(_mix_ffn_body, d_ff=d_ff, chunk=1024),
        grid=(n_tok // tm,),
        in_specs=[tok(x2d), _mod_block(mod_all, layer, mod_row_of_tile),
                  tok(attn), tok(pool), tok(rnn), tok(gates)]
                 + [_layer_block(p, layer) for p in params],
        out_specs=pl.BlockSpec((tm, d), row),
        out_shape=jax.ShapeDtypeStruct((n_tok, d), F32),
        scratch_shapes=[pltpu.VMEM((tm, d), BF16), pltpu.VMEM((tm, d), BF16)],
        compiler_params=_params(1),
        name="mix_ffn",
    )(x2d, mod_all, attn, pool, rnn, gates, *params)


def _pack_gate_weights(w_a, w_x):
    def block_diag(w):
        eye = jnp.eye(RNN_BLOCKS, dtype=w.dtype)
        return jnp.einsum('ldhij,hg->ldhigj', w, eye).reshape(w.shape[:2] + (RNN_W, RNN_W))
    bd_a, bd_x = block_diag(w_a), block_diag(w_x)
    half = RNN_W // 2
    halves = []
    for j in range(2):
        sl = slice(j * half, (j + 1) * half)
        halves.append(jnp.concatenate([bd_a[:, :, sl, sl], bd_x[:, :, sl, sl]], axis=3))
    return jnp.stack(halves, axis=2).astype(BF16)


def kernel(x, c, ctx, c_ctx, w_ada, b_ada, g_pre_mix, g_post_mix, g_pre_ffn, g_post_ffn, w_in, attn_sink, w_attn_o, pool_mix, pool_scale, w_pool_o, conv_w, conv_b, lru_w_a, lru_b_a, lru_w_x, lru_b_x, lru_lambda, w_rnn_o, w_out, w_gu, w_down):
    bsz, seq, d = x.shape
    n_ctx = ctx.shape[1]
    depth = w_ada.shape[0]
    tm = min(512, seq)
    tm_ctx = min(tm, n_ctx)

    mod_rows = -(-(bsz + 1) // SUBLANES) * SUBLANES
    cvec = jnp.zeros((mod_rows, d), F32).at[:bsz].set(c).at[bsz].set(c_ctx)
    mod_all = _adaln_mod(cvec, w_ada, b_ada).reshape(depth, mod_rows, 6, d)

    rope = _rope_tables(seq)
    lat_row = lambda i: i // (seq // tm)
    ctx_row = lambda i: bsz

    rowvec = lambda a: a.reshape(depth, 1, -1)
    w_in_b, w_gu_b, w_dn_b = w_in.astype(BF16), w_gu.astype(BF16), w_down.astype(BF16)
    wa, wp, wr, wo = (w.astype(BF16) for w in (w_attn_o, w_pool_o, w_rnn_o, w_out))
    w_mix = pool_mix.astype(BF16)
    wg = _pack_gate_weights(lru_w_a, lru_w_x)
    g_pre, g_post = rowvec(g_pre_mix), rowvec(g_post_mix)
    gf_pre, gf_post = rowvec(g_pre_ffn), rowvec(g_post_ffn)
    p_scale, cv_b = rowvec(pool_scale), rowvec(conv_b)

    x2 = x.reshape(bsz * seq, d)
    c2 = ctx.reshape(bsz * n_ctx, d)
    r3 = lambda a, n: a.reshape(bsz, n, a.shape[-1])
    flat = lambda a: a.reshape(-1, a.shape[-1])
    for l in range(depth):
        need_ctx = l < depth - 1
        q, kd, vd, rx, ry, pu, gt = _inproj(
            x2, mod_all, l, lat_row, g_pre, w_in_b, tm=tm, tiles_per_seq=seq // tm, rope_tables=rope,
            kv_only=False)
        if need_ctx:
            qc, kdc, vdc, rxc, ryc, puc, gtc = _inproj(
                c2, mod_all, l, ctx_row, g_pre, w_in_b, tm=tm_ctx, tiles_per_seq=1, rope_tables=None,
                kv_only=False)
        else:
            kdc, vdc, rxc = _inproj(
                c2, mod_all, l, ctx_row, g_pre, w_in_b, tm=tm_ctx, tiles_per_seq=1, rope_tables=None,
                kv_only=True)
            ryc = None

        kdc3, vdc3 = r3(kdc, n_ctx), r3(vdc, n_ctx)
        attn_l = _attention(attn_sink, l, r3(q, seq), r3(kd, seq), r3(vd, seq), kdc3, vdc3)
        pool_l = _pool(r3(pu, seq), l, w_mix, p_scale)
        rnn_l, rnn_c = _lru(l, r3(rxc, n_ctx), r3(rx, seq), None if ryc is None else r3(ryc, n_ctx),
                            r3(ry, seq), conv_w, cv_b, wg, lru_b_a, lru_b_x, lru_lambda)

        x2 = _mix_ffn(x2, mod_all, l, lat_row, flat(attn_l), flat(pool_l), flat(rnn_l), gt, wa, wp, wr, wo,
                      g_post, gf_pre, w_gu_b, w_dn_b, gf_post, tm=tm)

        if need_ctx:
            attn_c = _attention_ctx(attn_sink, l, r3(qc, n_ctx), kdc3, vdc3)
            pool_c = _pool(r3(puc, n_ctx), l, w_mix, p_scale)
            c2 = _mix_ffn(c2, mod_all, l, ctx_row, flat(attn_c), flat(pool_c), flat(rnn_c), gtc, wa, wp, wr,
                          wo, g_post, gf_pre, w_gu_b, w_dn_b, gf_post, tm=tm_ctx)
    return x2.reshape(bsz, seq, d)
```

```python
import functools
import math

import jax
import jax.numpy as jnp
import numpy as np
from jax import lax
from jax.experimental import pallas as pl
from jax.experimental.pallas import tpu as pltpu

F32 = jnp.float32
BF16 = jnp.bfloat16

GRID_W = 64
HEAD_DIM = 64
N_Q_HEADS = 8
N_KV_HEADS = 2
ATTN_W = N_Q_HEADS * HEAD_DIM
KV_W = N_KV_HEADS * HEAD_DIM
ATTN_BLOCK = 128
ROPE_BASE = 10000.0
POOL_WINDOWS = (2, 4, 8, 16)
POOL_W = 512
POOL_GW = POOL_W // len(POOL_WINDOWS)
RNN_W = 512
RNN_BLOCKS = 8
CONV_W = 4
LRU_C = 8.0
N_BRANCH = 3
EPS = 1e-6
NEG_INF = -1e30
LOG2E = 1.4426950408889634
TINY = 1e-30

LANES = 128
SUBLANES = 8
VMEM_LIMIT_BYTES = 56 * 1024 * 1024

SEG = 36
CHUNK = SUBLANES * SEG


def _params(n_axes):
    return pltpu.CompilerParams(
        dimension_semantics=("arbitrary",) * n_axes, vmem_limit_bytes=VMEM_LIMIT_BYTES)


def _layer_block(arr, layer, block=None):
    block = tuple(arr.shape[1:]) if block is None else block
    index = (layer,) + (0,) * len(block)
    return pl.BlockSpec((None,) + block, lambda *_: index, pipeline_mode=pl.Buffered(1))


def _mod_block(mod_all, layer, mod_row_of_tile):
    return pl.BlockSpec((None, None) + mod_all.shape[2:], lambda i: (layer, mod_row_of_tile(i), 0, 0))


def _rms(xf):
    return xf * lax.rsqrt(jnp.mean(xf * xf, axis=-1, keepdims=True) + EPS)


def _mod_body(c_ref, w_ref, b_ref, o_ref):
    c = c_ref[...]
    s = (c * jax.nn.sigmoid(c)).astype(BF16)
    o_ref[0] = jnp.dot(s, w_ref[0].astype(BF16), preferred_element_type=F32) + b_ref[0]


def _adaln_mod(cvec, w_ada, b_ada):
    n_layers, d, n6 = w_ada.shape
    rows = cvec.shape[0]
    tn = n6 // 4
    return pl.pallas_call(
        _mod_body,
        grid=(n_layers, n6 // tn),
        in_specs=[
            pl.BlockSpec((rows, d), lambda l, j: (0, 0)),
            pl.BlockSpec((1, d, tn), lambda l, j: (l, 0, j)),
            pl.BlockSpec((1, 1, tn), lambda l, j: (l, 0, j)),
        ],
        out_specs=pl.BlockSpec((1, rows, tn), lambda l, j: (l, 0, j)),
        out_shape=jax.ShapeDtypeStruct((n_layers, rows, n6), F32),
        compiler_params=_params(2),
        name="adaln_mod",
    )(cvec, w_ada, b_ada.reshape(n_layers, 1, n6))


def _rope_tables(seq_len):
    pos = np.arange(seq_len)
    row = (pos // GRID_W).astype(np.float32)
    col = (pos % GRID_W).astype(np.float32)
    half = HEAD_DIM // 2
    quarter = half // 2
    inv = (np.float32(ROPE_BASE) ** (-(np.arange(quarter, dtype=np.float32) * np.float32(2.0 / half)))
           ).astype(np.float32)
    j = np.arange(LANES) % HEAD_DIM
    is_col = j >= half
    second = (j % half) >= quarter
    freq = inv[j % quarter]
    ang = (np.where(is_col[None, :], col[:, None], row[:, None]) * freq[None, :]).astype(np.float32)
    cos, sin = np.cos(ang), np.sin(ang)
    sin_prev = np.where(second[None, :], sin, 0.0)
    sin_next = np.where(second[None, :], 0.0, -sin)
    return tuple(jnp.asarray(t, F32) for t in (cos, sin_prev, sin_next))


def _row_halves(n_rows):
    if n_rows % (2 * SUBLANES * 2) or n_rows < 256:
        return [slice(0, n_rows)]
    return [slice(0, n_rows // 2), slice(n_rows // 2, n_rows)]


def _store_dup_heads(dst_ref, rows, kv):
    low = lax.broadcasted_iota(jnp.int32, kv.shape, 1) < HEAD_DIM
    swapped = pltpu.roll(kv, HEAD_DIM, 1)
    dst_ref[rows, 0:LANES] = jnp.where(low, kv, swapped).astype(BF16)
    dst_ref[rows, LANES:2 * LANES] = jnp.where(low, swapped, kv).astype(BF16)


def _norm_modulate_store(x_ref, mod_ref, g_ref, h_ref, shift_row, scale_row, rows):
    y = _rms(x_ref[rows, :]) * g_ref[...]
    h = y * (1.0 + mod_ref[scale_row:scale_row + 1, :]) + mod_ref[shift_row:shift_row + 1, :]
    h_ref[rows, :] = h.astype(BF16)


def _inproj_full_body(*refs, rope):
    if rope:
        (x_ref, mod_ref, g_ref, w_ref, cos_ref, sp_ref, sn_ref,
         q_ref, kd_ref, vd_ref, rx_ref, ry_ref, pu_ref, gt_ref, h_ref) = refs
    else:
        (x_ref, mod_ref, g_ref, w_ref,
         q_ref, kd_ref, vd_ref, rx_ref, ry_ref, pu_ref, gt_ref, h_ref) = refs
    quarter = HEAD_DIM // 4
    d = x_ref.shape[1]
    for rows in _row_halves(x_ref.shape[0]):
        _norm_modulate_store(x_ref, mod_ref, g_ref, h_ref, 0, 1, rows)

        def proj(c0, c1):
            return jnp.dot(h_ref[rows, :], w_ref[:, c0:c1], preferred_element_type=F32)

        def rotary(pc):
            if not rope:
                return pc
            return (pc * cos_ref[rows, :]
                    + pltpu.roll(pc, quarter, 1) * sp_ref[rows, :]
                    + pltpu.roll(pc, LANES - quarter, 1) * sn_ref[rows, :])

        for c in range(ATTN_W // (2 * LANES)):
            pair = proj(c * 2 * LANES, (c + 1) * 2 * LANES)
            for half in range(2):
                col = (2 * c + half) * LANES
                pc = rotary(pair[:, half * LANES:(half + 1) * LANES])
                q_ref[rows, col:col + LANES] = (pc * (HEAD_DIM ** -0.5 * LOG2E)).astype(BF16)
        o = ATTN_W
        kv = proj(o, o + 2 * KV_W)
        _store_dup_heads(kd_ref, rows, rotary(kv[:, :KV_W]))
        _store_dup_heads(vd_ref, rows, kv[:, KV_W:])
        o += 2 * KV_W
        rx_ref[rows, :] = proj(o, o + RNN_W)
        o += RNN_W
        ry_ref[rows, :] = jax.nn.gelu(proj(o, o + RNN_W)).astype(BF16)
        o += RNN_W
        pu_ref[rows, :] = proj(o, o + POOL_W)
        o += POOL_W
        for c in range(N_BRANCH):
            gl = proj(o + c * d, o + (c + 1) * d)
            gt_ref[rows, c * d:(c + 1) * d] = jax.nn.sigmoid(gl).astype(BF16)


def _inproj_kv_body(x_ref, mod_ref, g_ref, w_ref, kd_ref, vd_ref, rx_ref, h_ref):
    rows = slice(0, x_ref.shape[0])
    _norm_modulate_store(x_ref, mod_ref, g_ref, h_ref, 0, 1, rows)
    kv = jnp.dot(h_ref[...], w_ref[:, ATTN_W:ATTN_W + 2 * KV_W], preferred_element_type=F32)
    _store_dup_heads(kd_ref, rows, kv[:, :KV_W])
    _store_dup_heads(vd_ref, rows, kv[:, KV_W:])
    rx_ref[...] = jnp.dot(h_ref[...], w_ref[:, ATTN_W + 2 * KV_W:ATTN_W + 2 * KV_W + RNN_W],
                          preferred_element_type=F32)


def _inproj(x2d, mod_all, layer, mod_row_of_tile, g, w, *, tm, tiles_per_seq, rope_tables, kv_only):
    n_tok, d = x2d.shape
    grid = (n_tok // tm,)
    row = lambda i: (i, 0)
    w_cols = ATTN_W + 2 * KV_W + RNN_W if kv_only else w.shape[2]
    in_specs = [
        pl.BlockSpec((tm, d), row),
        _mod_block(mod_all, layer, mod_row_of_tile),
        _layer_block(g, layer),
        _layer_block(w, layer, (d, w_cols)),
    ]
    args = [x2d, mod_all, g, w]
    if rope_tables is not None:
        in_specs += [pl.BlockSpec((tm, LANES), lambda i: (i % tiles_per_seq, 0))] * 3
        args += list(rope_tables)

    def out(width, dtype):
        return pl.BlockSpec((tm, width), row), jax.ShapeDtypeStruct((n_tok, width), dtype)

    if kv_only:
        outs = [out(2 * KV_W, BF16), out(2 * KV_W, BF16), out(RNN_W, F32)]
        body = _inproj_kv_body
    else:
        outs = [out(ATTN_W, BF16), out(2 * KV_W, BF16), out(2 * KV_W, BF16), out(RNN_W, F32),
                out(RNN_W, BF16), out(POOL_W, F32), out(N_BRANCH * d, BF16)]
        body = functools.partial(_inproj_full_body, rope=rope_tables is not None)
    return pl.pallas_call(
        body,
        grid=grid,
        in_specs=in_specs,
        out_specs=[o[0] for o in outs],
        out_shape=[o[1] for o in outs],
        scratch_shapes=[pltpu.VMEM((tm, d), BF16)],
        compiler_params=_params(1),
        name="inproj_kv" if kv_only else "inproj",
    )(*args)


def _attend(sink_ref, layer, items):
    heads_per_kv = N_Q_HEADS // N_KV_HEADS
    zero = jnp.zeros((), BF16)

    def scores(item, h):
        q_blk, k_rows, _, biases = item
        low = lax.broadcasted_iota(jnp.int32, (q_blk.shape[0], LANES), 1) < HEAD_DIM
        k_h = k_rows[:, h * LANES:(h + 1) * LANES]
        stacked = []
        for c2 in range(heads_per_kv // 2):
            qc = q_blk[:, (2 * h + c2) * LANES:(2 * h + c2 + 1) * LANES]
            stacked += [jnp.where(low, qc, zero), jnp.where(low, zero, qc)]
        lhs = jnp.concatenate(stacked, axis=0)
        s = lax.dot_general(lhs, k_h, (((1,), (1,)), ((), ())), preferred_element_type=F32)
        blocks = []
        for j, bias in enumerate(biases):
            blk = s[:, j * LANES:(j + 1) * LANES]
            if bias is not None:
                blk = blk + jnp.concatenate([bias] * heads_per_kv, axis=0)
            blocks.append(blk)
        return blocks

    def softmax(h, blocks):
        tq = blocks[0].shape[0] // heads_per_kv
        sink_col = jnp.concatenate(
            [jnp.full((tq, 1), sink_ref[layer, h * heads_per_kv + g] * LOG2E, F32)
             for g in range(heads_per_kv)],
            axis=0)
        m = jnp.maximum(jnp.max(functools.reduce(jnp.maximum, blocks), axis=-1, keepdims=True), sink_col)
        probs = [jnp.exp2(blk - m) for blk in blocks]
        denom = jnp.sum(functools.reduce(jnp.add, probs), axis=-1, keepdims=True) + jnp.exp2(sink_col - m)
        return jnp.concatenate(probs, axis=1).astype(BF16), denom

    def weighted(item, h, p, denom):
        tq = item[0].shape[0]
        low = lax.broadcasted_iota(jnp.int32, (tq, LANES), 1) < HEAD_DIM
        v_h = item[2][:, h * LANES:(h + 1) * LANES]
        o = jnp.dot(p, v_h, preferred_element_type=F32) * (1.0 / denom)
        return [jnp.where(low, o[(2 * c2) * tq:(2 * c2 + 1) * tq],
                          o[(2 * c2 + 1) * tq:(2 * c2 + 2) * tq]).astype(BF16)
                for c2 in range(heads_per_kv // 2)]

    units = [(item, h) for item in items for h in range(N_KV_HEADS)]
    all_scores = [scores(item, h) for item, h in units]
    all_probs = [softmax(h, blocks) for (_, h), blocks in zip(units, all_scores)]
    all_outs = [weighted(item, h, *pd) for (item, h), pd in zip(units, all_probs)]
    return [jnp.concatenate([o for outs in all_outs[i * N_KV_HEADS:(i + 1) * N_KV_HEADS] for o in outs],
                            axis=1) for i in range(len(items))]


def _attn_local_body(sink_ref, q_ref, kd_ref, vd_ref, kc_ref, vc_ref, o_ref, *, layer, n_blocks, group):
    blk = ATTN_BLOCK
    ii = lax.broadcasted_iota(jnp.int32, (blk, blk), 0)
    jj = lax.broadcasted_iota(jnp.int32, (blk, blk), 1)
    n_ctx_blocks = kc_ref.shape[0] // LANES
    rows = lambda i: pl.ds(pl.multiple_of(i * blk, blk), blk)

    def item(n):
        prv, nxt = jnp.maximum(n - 1, 0), jnp.minimum(n + 1, n_blocks - 1)
        bias_prev = jnp.where((jj >= ii) & (n > 0), 0.0, NEG_INF).astype(F32)
        bias_next = jnp.where((jj <= ii) & (n < n_blocks - 1), 0.0, NEG_INF).astype(F32)
        k_rows = jnp.concatenate([kd_ref[rows(prv), :], kd_ref[rows(n), :], kd_ref[rows(nxt), :],
                                  kc_ref[...]], axis=0)
        v_rows = jnp.concatenate([vd_ref[rows(prv), :], vd_ref[rows(n), :], vd_ref[rows(nxt), :],
                                  vc_ref[...]], axis=0)
        return (q_ref[rows(n), :], k_rows, v_rows, [bias_prev, None, bias_next] + [None] * n_ctx_blocks)

    def body(i, carry):
        blocks = [i * group + g for g in range(group)]
        for n, out in zip(blocks, _attend(sink_ref, layer, [item(n) for n in blocks])):
            o_ref[rows(n), :] = out
        return carry

    lax.fori_loop(0, n_blocks // group, body, 0)


def _attn_ctx_body(sink_ref, q_ref, kc_ref, vc_ref, o_ref, *, layer):
    item = (q_ref[...], kc_ref[...], vc_ref[...], [None] * (kc_ref.shape[0] // LANES))
    o_ref[...] = _attend(sink_ref, layer, [item])[0]


def _attention(sink, layer, q, kd, vd, kdc, vdc):
    b, s, _ = q.shape
    c = kdc.shape[1]
    kw = 2 * KV_W
    n_blocks = s // ATTN_BLOCK
    group = math.gcd(n_blocks, 4)
    idx = lambda bi: (bi, 0, 0)
    return pl.pallas_call(
        functools.partial(_attn_local_body, layer=layer, n_blocks=n_blocks, group=group),
        grid=(b,),
        in_specs=[
            pl.BlockSpec(memory_space=pltpu.SMEM),
            pl.BlockSpec((None, s, ATTN_W), idx),
            pl.BlockSpec((None, s, kw), idx), pl.BlockSpec((None, s, kw), idx),
            pl.BlockSpec((None, c, kw), idx), pl.BlockSpec((None, c, kw), idx),
        ],
        out_specs=pl.BlockSpec((None, s, ATTN_W), idx),
        out_shape=jax.ShapeDtypeStruct((b, s, ATTN_W), BF16),
        compiler_params=_params(1),
        name="attn_local",
    )(sink, q, kd, vd, kdc, vdc)


def _attention_ctx(sink, layer, qc, kdc, vdc):
    b, c, _ = qc.shape
    kw = 2 * KV_W
    idx = lambda bi: (bi, 0, 0)
    return pl.pallas_call(
        functools.partial(_attn_ctx_body, layer=layer),
        grid=(b,),
        in_specs=[
            pl.BlockSpec(memory_space=pltpu.SMEM),
            pl.BlockSpec((None, c, ATTN_W), idx),
            pl.BlockSpec((None, c, kw), idx), pl.BlockSpec((None, c, kw), idx),
        ],
        out_specs=pl.BlockSpec((None, c, ATTN_W), idx),
        out_shape=jax.ShapeDtypeStruct((b, c, ATTN_W), BF16),
        compiler_params=_params(1),
        name="attn_ctx",
    )(sink, qc, kdc, vdc)


def _pool_body(u_ref, w_ref, sc_ref, o_ref, pad_ref, *, tile):
    length = u_ref.shape[0]
    halo = SUBLANES
    pad_ref[0:halo, :] = jnp.zeros((halo, POOL_W), F32)
    pad_ref[halo + length:2 * halo + length, :] = jnp.zeros((halo, POOL_W), F32)
    pad_ref[halo:halo + length, :] = u_ref[...]
    rows = tile + 2 * halo
    for t0 in range(0, length, tile):
        t = t0 + lax.broadcasted_iota(jnp.int32, (tile, POOL_GW), 0)
        for gi, w in enumerate(POOL_WINDOWS):
            lanes = slice(gi * POOL_GW, (gi + 1) * POOL_GW)
            p = pad_ref[t0:t0 + rows, lanes]
            acc = p
            step = 1
            while step < w:
                acc = acc + pltpu.roll(acc, rows - step, 0)
                step *= 2
            back = (w - 1) // 2
            win = (pltpu.roll(acc, back, 0) if back else acc)[halo:halo + tile]
            lo = jnp.maximum(t - back, 0)
            hi = jnp.minimum(t + w // 2 + 1, length)
            mean = win / (hi - lo).astype(F32)
            dlt = (mean - p[halo:halo + tile]).astype(BF16)
            y = jnp.dot(dlt, w_ref[gi], preferred_element_type=F32) * sc_ref[:, lanes]
            o_ref[t0:t0 + tile, lanes] = y.astype(BF16)


def _pool(pu, layer, w_mix, ch_scale):
    b, length, _ = pu.shape
    tile = min(length, 256)
    idx = lambda bi: (bi, 0, 0)
    return pl.pallas_call(
        functools.partial(_pool_body, tile=tile),
        grid=(b,),
        in_specs=[pl.BlockSpec((None, length, POOL_W), idx), _layer_block(w_mix, layer),
                  _layer_block(ch_scale, layer)],
        out_specs=pl.BlockSpec((None, length, POOL_W), idx),
        out_shape=jax.ShapeDtypeStruct((b, length, POOL_W), BF16),
        scratch_shapes=[pltpu.VMEM((length + 2 * SUBLANES, POOL_W), F32)],
        compiler_params=_params(1),
        name="pool",
    )(pu, w_mix, ch_scale)


def _lru_body(*refs, n_ctx, n_lat, tile, with_ctx_out):
    if with_ctx_out:
        (rxc_ref, rxl_ref, ryc_ref, ryl_ref, cw_ref, cb_ref, wg_ref, ba_ref, bx_ref, lam_ref,
         ol_ref, oc_ref, pad_ref, xl_ref, hf_ref, hb_ref, xp_ref, a_ref, b_ref) = refs
    else:
        (rxc_ref, rxl_ref, ryl_ref, cw_ref, cb_ref, wg_ref, ba_ref, bx_ref, lam_ref,
         ol_ref, pad_ref, xl_ref, hf_ref, hb_ref, xp_ref, a_ref, b_ref) = refs
    halo = SUBLANES
    n_tot = n_ctx + n_lat
    n_slab = RNN_W // LANES

    zeros = jnp.zeros((halo, RNN_W), F32)
    c0, l0 = halo, 2 * halo + n_ctx
    pad_ref[0:halo, :] = zeros
    pad_ref[c0 + n_ctx:l0, :] = zeros
    pad_ref[l0 + n_lat:l0 + n_lat + halo, :] = zeros
    pad_ref[c0:c0 + n_ctx, :] = rxc_ref[...]
    pad_ref[l0:l0 + n_lat, :] = rxl_ref[...]

    cw = 0.5 * cw_ref[...]
    cb = 0.5 * cb_ref[...]

    def conv_tile(src0, dst_rows):
        rows = tile + 2 * halo
        p = pad_ref[src0 - halo:src0 + tile + halo, :]
        y = (cb + p[halo:halo + tile] * cw[1:2, :]
             + pltpu.roll(p, 1, 0)[halo:halo + tile] * cw[0:1, :]
             + pltpu.roll(p, rows - 1, 0)[halo:halo + tile] * cw[2:3, :]
             + pltpu.roll(p, rows - 2, 0)[halo:halo + tile] * cw[3:4, :])
        for s in range(n_slab):
            for d0 in dst_rows:
                xl_ref[s, d0:d0 + tile, :] = y[:, s * LANES:(s + 1) * LANES]

    for t0 in range(0, n_ctx, tile):
        conv_tile(c0 + t0, (t0, n_tot + t0))
    for t0 in range(0, n_lat, tile):
        conv_tile(l0 + t0, (n_ctx + t0,))

    lam = lam_ref[...]
    log_sig = jnp.minimum(lam, 0.0) - jnp.log1p(jnp.exp(-jnp.abs(lam)))
    rate = (0.5 * LOG2E * LRU_C) * log_sig
    half_ba = 0.5 * ba_ref[...]
    half_bx = 0.5 * bx_ref[...]
    half = RNN_W // 2

    def run_chunk(d, base, carry):
        res_ref = hf_ref if d == 0 else hb_ref
        win0 = 0 if d == 0 else n_ctx
        for v in range(SEG):
            for s in range(n_slab):
                xp_ref[d, v * SUBLANES:(v + 1) * SUBLANES, s * LANES:(s + 1) * LANES] = (
                    xl_ref[s, pl.ds(win0 + base + v, SUBLANES, stride=SEG), :])
        for j in range(2):
            ch = slice(j * half, (j + 1) * half)
            xj = xp_ref[d, :, ch]
            gates = jnp.dot(xj.astype(BF16), wg_ref[d, j], preferred_element_type=F32)
            t_r = jnp.tanh(gates[:, :half] + half_ba[d:d + 1, ch])
            t_i = jnp.tanh(gates[:, half:] + half_bx[d:d + 1, ch])
            a = jnp.exp2(t_r * rate[d:d + 1, ch] + rate[d:d + 1, ch])
            gap = 1.0 - a * a
            mult = gap * lax.rsqrt(jnp.maximum(gap, TINY))
            a_ref[d, :, ch] = a
            b_ref[d, :, ch] = (mult * xj) * (t_i + 1.0)
        order = range(SEG) if d == 0 else range(SEG - 1, -1, -1)
        h = jnp.zeros((SUBLANES, RNN_W), F32)
        acum = jnp.ones((SUBLANES, RNN_W), F32)
        for v in order:
            rows = slice(v * SUBLANES, (v + 1) * SUBLANES)
            av = a_ref[d, rows, :]
            h = av * h + b_ref[d, rows, :]
            acum = av * acum
            b_ref[d, rows, :] = h
            a_ref[d, rows, :] = acum
        seg_in = [None] * SUBLANES
        state = carry
        seg_order = range(SUBLANES) if d == 0 else range(SUBLANES - 1, -1, -1)
        for i in seg_order:
            seg_in[i] = state
            state = h[i:i + 1, :] + acum[i:i + 1, :] * state
        seg_state = jnp.concatenate(seg_in, axis=0)
        for v in range(SEG):
            rows = slice(v * SUBLANES, (v + 1) * SUBLANES)
            hv = b_ref[d, rows, :] + a_ref[d, rows, :] * seg_state
            for s in range(n_slab):
                res_ref[s, pl.ds(base + v, SUBLANES, stride=SEG), :] = hv[:, s * LANES:(s + 1) * LANES]
        return state

    n_chunks = n_tot // CHUNK

    def step(c, carries):
        cf, cb = carries
        cf = run_chunk(0, c * CHUNK, cf)
        cb = run_chunk(1, (n_chunks - 1 - c) * CHUNK, cb)
        return cf, cb

    zero_state = jnp.zeros((1, RNN_W), F32)
    lax.fori_loop(0, n_chunks, step, (zero_state, zero_state))

    def emit(out_ref, ry_ref, f0, b0, length):
        for t0 in range(0, length, tile):
            hsum = jnp.concatenate(
                [hf_ref[s, f0 + t0:f0 + t0 + tile, :] + hb_ref[s, b0 + t0:b0 + t0 + tile, :]
                 for s in range(n_slab)], axis=1)
            gate = ry_ref[t0:t0 + tile, :].astype(F32)
            out_ref[t0:t0 + tile, :] = (hsum * gate).astype(BF16)

    emit(ol_ref, ryl_ref, n_ctx, 0, n_lat)
    if with_ctx_out:
        emit(oc_ref, ryc_ref, 0, n_lat, n_ctx)


def _lru(layer, rxc, rxl, ryc, ryl, conv_w, conv_b, wg, b_a, b_x, lam):
    b, n_ctx, _ = rxc.shape
    n_lat = rxl.shape[1]
    n_tot = n_ctx + n_lat
    tile = math.gcd(n_ctx, n_lat, 256)
    assert n_tot % CHUNK == 0 and tile % SUBLANES == 0
    with_ctx_out = ryc is not None
    idx = lambda bi: (bi, 0, 0)
    seq = lambda n, dt=None: pl.BlockSpec((None, n, RNN_W), idx)
    n_slab = RNN_W // LANES
    in_specs = [seq(n_ctx), seq(n_lat)] + ([seq(n_ctx)] if with_ctx_out else []) + [seq(n_lat)]
    args = [rxc, rxl] + ([ryc] if with_ctx_out else []) + [ryl]
    for wgt in (conv_w, conv_b, wg, b_a, b_x, lam):
        in_specs.append(_layer_block(wgt, layer))
        args.append(wgt)
    out_specs = [seq(n_lat)] + ([seq(n_ctx)] if with_ctx_out else [])
    out_shape = [jax.ShapeDtypeStruct((b, n_lat, RNN_W), BF16)]
    if with_ctx_out:
        out_shape.append(jax.ShapeDtypeStruct((b, n_ctx, RNN_W), BF16))
    res = pl.pallas_call(
        functools.partial(_lru_body, n_ctx=n_ctx, n_lat=n_lat, tile=tile, with_ctx_out=with_ctx_out),
        grid=(b,),
        in_specs=in_specs,
        out_specs=out_specs,
        out_shape=out_shape,
        scratch_shapes=[
            pltpu.VMEM((n_tot + 3 * SUBLANES, RNN_W), F32),
            pltpu.VMEM((n_slab, n_tot + n_ctx, LANES), F32),
            pltpu.VMEM((n_slab, n_tot, LANES), F32),
            pltpu.VMEM((n_slab, n_tot, LANES), F32),
            pltpu.VMEM((2, CHUNK, RNN_W), F32),
            pltpu.VMEM((2, CHUNK, RNN_W), F32),
            pltpu.VMEM((2, CHUNK, RNN_W), F32),
        ],
        compiler_params=_params(1),
        name="rglru",
    )(*args)
    return (res[0], res[1]) if with_ctx_out else (res[0], None)


def _mix_ffn_body(x_ref, mod_ref, at_ref, po_ref, rn_ref, gt_ref, wa_ref, wp_ref, wr_ref, wo_ref,
                  gmix_ref, gpre_ref, wgu_ref, wd_ref, gpost_ref, o_ref, m_ref, h_ref, *, d_ff, chunk):
    d = x_ref.shape[1]
    for rows in _row_halves(x_ref.shape[0]):
        merged = None
        for k, (br_ref, w_ref) in enumerate(((at_ref, wa_ref), (po_ref, wp_ref), (rn_ref, wr_ref))):
            t = gt_ref[rows, k * d:(k + 1) * d].astype(F32) * jnp.dot(
                br_ref[rows, :], w_ref[...], preferred_element_type=F32)
            merged = t if merged is None else merged + t
        m_ref[rows, :] = merged.astype(BF16)
        mix = jnp.dot(m_ref[rows, :], wo_ref[...], preferred_element_type=F32)
        o_ref[rows, :] = x_ref[rows, :] + mod_ref[2:3, :] * (_rms(mix) * gmix_ref[...])
        _norm_modulate_store(o_ref, mod_ref, gpre_ref, h_ref, 3, 4, rows)

    halves = _row_halves(x_ref.shape[0])
    accs = [None] * len(halves)
    for c0 in range(0, d_ff, chunk):
        c1 = min(c0 + chunk, d_ff)
        for i, rows in enumerate(halves):
            gate = jnp.dot(h_ref[rows, :], wgu_ref[:, c0:c1], preferred_element_type=F32)
            up = jnp.dot(h_ref[rows, :], wgu_ref[:, d_ff + c0:d_ff + c1], preferred_element_type=F32)
            act = ((gate * jax.nn.sigmoid(gate)) * up).astype(BF16)
            part = jnp.dot(act, wd_ref[c0:c1, :], preferred_element_type=F32)
            accs[i] = part if accs[i] is None else accs[i] + part
    for acc, rows in zip(accs, halves):
        o_ref[rows, :] = o_ref[rows, :] + mod_ref[5:6, :] * (_rms(acc) * gpost_ref[...])


def _mix_ffn(x2d, mod_all, layer, mod_row_of_tile, attn, pool, rnn, gates, wa, wp, wr, wo, g_mix,
             g_pre, w_gu, w_down, g_post, *, tm):
    n_tok, d = x2d.shape
    d_ff = w_down.shape[1]
    row = lambda i: (i, 0)
    tok = lambda arr: pl.BlockSpec((tm, arr.shape[1]), row)
    params = (wa, wp, wr, wo, g_mix, g_pre, w_gu, w_down, g_post)
    return pl.pallas_call(
        functools.partial(_mix_ffn_body, d_ff=d_ff, chunk=1024),
        grid=(n_tok // tm,),
        in_specs=[tok(x2d), _mod_block(mod_all, layer, mod_row_of_tile),
                  tok(attn), tok(pool), tok(rnn), tok(gates)]
                 + [_layer_block(p, layer) for p in params],
        out_specs=pl.BlockSpec((tm, d), row),
        out_shape=jax.ShapeDtypeStruct((n_tok, d), F32),
        scratch_shapes=[pltpu.VMEM((tm, d), BF16), pltpu.VMEM((tm, d), BF16)],
        compiler_params=_params(1),
        name="mix_ffn",
    )(x2d, mod_all, attn, pool, rnn, gates, *params)


def _pack_gate_weights(w_a, w_x):
    def block_diag(w):
        eye = jnp.eye(RNN_BLOCKS, dtype=w.dtype)
        return jnp.einsum('ldhij,hg->ldhigj', w, eye).reshape(w.shape[:2] + (RNN_W, RNN_W))
    bd_a, bd_x = block_diag(w_a), block_diag(w_x)
    half = RNN_W // 2
    halves = []
    for j in range(2):
        sl = slice(j * half, (j + 1) * half)
        halves.append(jnp.concatenate([bd_a[:, :, sl, sl], bd_x[:, :, sl, sl]], axis=3))
    return jnp.stack(halves, axis=2).astype(BF16)


def kernel(x, c, ctx, c_ctx, w_ada, b_ada, g_pre_mix, g_post_mix, g_pre_ffn, g_post_ffn, w_in, attn_sink, w_attn_o, pool_mix, pool_scale, w_pool_o, conv_w, conv_b, lru_w_a, lru_b_a, lru_w_x, lru_b_x, lru_lambda, w_rnn_o, w_out, w_gu, w_down):
    bsz, seq, d = x.shape
    n_ctx = ctx.shape[1]
    depth = w_ada.shape[0]
    tm = min(512, seq)
    tm_ctx = min(tm, n_ctx)

    mod_rows = -(-(bsz + 1) // SUBLANES) * SUBLANES
    cvec = jnp.zeros((mod_rows, d), F32).at[:bsz].set(c).at[bsz].set(c_ctx)
    mod_all = _adaln_mod(cvec, w_ada, b_ada).reshape(depth, mod_rows, 6, d)

    rope = _rope_tables(seq)
    lat_row = lambda i: i // (seq // tm)
    ctx_row = lambda i: bsz

    rowvec = lambda a: a.reshape(depth, 1, -1)
    w_in_b, w_gu_b, w_dn_b = w_in.astype(BF16), w_gu.astype(BF16), w_down.astype(BF16)
    wa, wp, wr, wo = (w.astype(BF16) for w in (w_attn_o, w_pool_o, w_rnn_o, w_out))
    w_mix = pool_mix.astype(BF16)
    wg = _pack_gate_weights(lru_w_a, lru_w_x)
    g_pre, g_post = rowvec(g_pre_mix), rowvec(g_post_mix)
    gf_pre, gf_post = rowvec(g_pre_ffn), rowvec(g_post_ffn)
    p_scale, cv_b = rowvec(pool_scale), rowvec(conv_b)

    x2 = x.reshape(bsz * seq, d)
    c2 = ctx.reshape(bsz * n_ctx, d)
    r3 = lambda a, n: a.reshape(bsz, n, a.shape[-1])
    flat = lambda a: a.reshape(-1, a.shape[-1])
    for l in range(depth):
        need_ctx = l < depth - 1
        q, kd, vd, rx, ry, pu, gt = _inproj(
            x2, mod_all, l, lat_row, g_pre, w_in_b, tm=tm, tiles_per_seq=seq // tm, rope_tables=rope,
            kv_only=False)
        if need_ctx:
            qc, kdc, vdc, rxc, ryc, puc, gtc = _inproj(
                c2, mod_all, l, ctx_row, g_pre, w_in_b, tm=tm_ctx, tiles_per_seq=1, rope_tables=None,
                kv_only=False)
        else:
            kdc, vdc, rxc = _inproj(
                c2, mod_all, l, ctx_row, g_pre, w_in_b, tm=tm_ctx, tiles_per_seq=1, rope_tables=None,
                kv_only=True)
            ryc = None

        kdc3, vdc3 = r3(kdc, n_ctx), r3(vdc, n_ctx)
        attn_l = _attention(attn_sink, l, r3(q, seq), r3(kd, seq), r3(vd, seq), kdc3, vdc3)
        pool_l = _pool(r3(pu, seq), l, w_mix, p_scale)
        rnn_l, rnn_c = _lru(l, r3(rxc, n_ctx), r3(rx, seq), None if ryc is None else r3(ryc, n_ctx),
                            r3(ry, seq), conv_w, cv_b, wg, lru_b_a, lru_b_x, lru_lambda)

        x2 = _mix_ffn(x2, mod_all, l, lat_row, flat(attn_l), flat(pool_l), flat(rnn_l), gt, wa, wp, wr, wo,
                      g_post, gf_pre, w_gu_b, w_dn_b, gf_post, tm=tm)

        if need_ctx:
            attn_c = _attention_ctx(attn_sink, l, r3(qc, n_ctx), kdc3, vdc3)
            pool_c = _pool(r3(puc, n_ctx), l, w_mix, p_scale)
            c2 = _mix_ffn(c2, mod_all, l, ctx_row, flat(attn_c), flat(pool_c), flat(rnn_c), gtc, wa, wp, wr,
                          wo, g_post, gf_pre, w_gu_b, w_dn_b, gf_post, tm=tm_ctx)
    return x2.reshape(bsz, seq, d)
```

```python
import functools
import math

import jax
import jax.numpy as jnp
import numpy as np
from jax import lax
from jax.experimental import pallas as pl
from jax.experimental.pallas import tpu as pltpu

F32 = jnp.float32
BF16 = jnp.bfloat16

GRID_W = 64
HEAD_DIM = 64
N_Q_HEADS = 8
N_KV_HEADS = 2
ATTN_W = N_Q_HEADS * HEAD_DIM
KV_W = N_KV_HEADS * HEAD_DIM
ATTN_BLOCK = 128
ROPE_BASE = 10000.0
POOL_WINDOWS = (2, 4, 8, 16)
POOL_W = 512
POOL_GW = POOL_W // len(POOL_WINDOWS)
RNN_W = 512
RNN_BLOCKS = 8
CONV_W = 4
LRU_C = 8.0
N_BRANCH = 3
EPS = 1e-6
NEG_INF = -1e30
LOG2E = 1.4426950408889634
TINY = 1e-30

LANES = 128
SUBLANES = 8
VMEM_LIMIT_BYTES = 56 * 1024 * 1024

SEG = 36
CHUNK = SUBLANES * SEG


def _params(n_axes):
    return pltpu.CompilerParams(
        dimension_semantics=("arbitrary",) * n_axes, vmem_limit_bytes=VMEM_LIMIT_BYTES)


def _layer_block(arr, layer, block=None):
    block = tuple(arr.shape[1:]) if block is None else block
    index = (layer,) + (0,) * len(block)
    return pl.BlockSpec((None,) + block, lambda *_: index, pipeline_mode=pl.Buffered(1))


def _mod_block(mod_all, layer, mod_row_of_tile):
    return pl.BlockSpec((None, None) + mod_all.shape[2:], lambda i: (layer, mod_row_of_tile(i), 0, 0))


def _rms(xf):
    return xf * lax.rsqrt(jnp.mean(xf * xf, axis=-1, keepdims=True) + EPS)


def _mod_body(c_ref, w_ref, b_ref, o_ref):
    c = c_ref[...]
    s = (c * jax.nn.sigmoid(c)).astype(BF16)
    o_ref[0] = jnp.dot(s, w_ref[0].astype(BF16), preferred_element_type=F32) + b_ref[0]


def _adaln_mod(cvec, w_ada, b_ada):
    n_layers, d, n6 = w_ada.shape
    rows = cvec.shape[0]
    tn = n6 // 4
    return pl.pallas_call(
        _mod_body,
        grid=(n_layers, n6 // tn),
        in_specs=[
            pl.BlockSpec((rows, d), lambda l, j: (0, 0)),
            pl.BlockSpec((1, d, tn), lambda l, j: (l, 0, j)),
            pl.BlockSpec((1, 1, tn), lambda l, j: (l, 0, j)),
        ],
        out_specs=pl.BlockSpec((1, rows, tn), lambda l, j: (l, 0, j)),
        out_shape=jax.ShapeDtypeStruct((n_layers, rows, n6), F32),
        compiler_params=_params(2),
        name="adaln_mod",
    )(cvec, w_ada, b_ada.reshape(n_layers, 1, n6))


def _rope_tables(seq_len):
    pos = np.arange(seq_len)
    row = (pos // GRID_W).astype(np.float32)
    col = (pos % GRID_W).astype(np.float32)
    half = HEAD_DIM // 2
    quarter = half // 2
    inv = (np.float32(ROPE_BASE) ** (-(np.arange(quarter, dtype=np.float32) * np.float32(2.0 / half)))
           ).astype(np.float32)
    j = np.arange(LANES) % HEAD_DIM
    is_col = j >= half
    second = (j % half) >= quarter
    freq = inv[j % quarter]
    ang = (np.where(is_col[None, :], col[:, None], row[:, None]) * freq[None, :]).astype(np.float32)
    cos, sin = np.cos(ang), np.sin(ang)
    sin_prev = np.where(second[None, :], sin, 0.0)
    sin_next = np.where(second[None, :], 0.0, -sin)
    return tuple(jnp.asarray(t, F32) for t in (cos, sin_prev, sin_next))


def _row_halves(n_rows):
    if n_rows % (2 * SUBLANES * 2) or n_rows < 256:
        return [slice(0, n_rows)]
    return [slice(0, n_rows // 2), slice(n_rows // 2, n_rows)]


def _store_dup_heads(dst_ref, rows, kv):
    low = lax.broadcasted_iota(jnp.int32, kv.shape, 1) < HEAD_DIM
    swapped = pltpu.roll(kv, HEAD_DIM, 1)
    dst_ref[rows, 0:LANES] = jnp.where(low, kv, swapped).astype(BF16)
    dst_ref[rows, LANES:2 * LANES] = jnp.where(low, swapped, kv).astype(BF16)


def _norm_modulate_store(x_ref, mod_ref, g_ref, h_ref, shift_row, scale_row, rows):
    y = _rms(x_ref[rows, :]) * g_ref[...]
    h = y * (1.0 + mod_ref[scale_row:scale_row + 1, :]) + mod_ref[shift_row:shift_row + 1, :]
    h_ref[rows, :] = h.astype(BF16)


def _inproj_full_body(*refs, rope):
    if rope:
        (x_ref, mod_ref, g_ref, w_ref, cos_ref, sp_ref, sn_ref,
         q_ref, kd_ref, vd_ref, rx_ref, ry_ref, pu_ref, gt_ref, h_ref) = refs
    else:
        (x_ref, mod_ref, g_ref, w_ref,
         q_ref, kd_ref, vd_ref, rx_ref, ry_ref, pu_ref, gt_ref, h_ref) = refs
    quarter = HEAD_DIM // 4
    d = x_ref.shape[1]
    halves = _row_halves(x_ref.shape[0])
    for rows in halves:
        _norm_modulate_store(x_ref, mod_ref, g_ref, h_ref, 0, 1, rows)

    def proj(rows, c0, c1):
        return jnp.dot(h_ref[rows, :], w_ref[:, c0:c1], preferred_element_type=F32)

    def rotary(rows, pc):
        if not rope:
            return pc
        return (pc * cos_ref[rows, :]
                + pltpu.roll(pc, quarter, 1) * sp_ref[rows, :]
                + pltpu.roll(pc, LANES - quarter, 1) * sn_ref[rows, :])

    for rows in halves:
        for c in range(ATTN_W // (2 * LANES)):
            pair = proj(rows, c * 2 * LANES, (c + 1) * 2 * LANES)
            for half in range(2):
                col = (2 * c + half) * LANES
                pc = rotary(rows, pair[:, half * LANES:(half + 1) * LANES])
                q_ref[rows, col:col + LANES] = (pc * (HEAD_DIM ** -0.5 * LOG2E)).astype(BF16)
        o = ATTN_W
        kv = proj(rows, o, o + 2 * KV_W)
        _store_dup_heads(kd_ref, rows, rotary(rows, kv[:, :KV_W]))
        _store_dup_heads(vd_ref, rows, kv[:, KV_W:])
        o += 2 * KV_W
        rx_ref[rows, :] = proj(rows, o, o + RNN_W)
        o += RNN_W
        ry_ref[rows, :] = jax.nn.gelu(proj(rows, o, o + RNN_W)).astype(BF16)
        o += RNN_W
        pu_ref[rows, :] = proj(rows, o, o + POOL_W)
        o += POOL_W
        for c in range(N_BRANCH):
            gl = proj(rows, o + c * d, o + (c + 1) * d)
            gt_ref[rows, c * d:(c + 1) * d] = jax.nn.sigmoid(gl).astype(BF16)


def _inproj_kv_body(x_ref, mod_ref, g_ref, w_ref, kd_ref, vd_ref, rx_ref, h_ref):
    rows = slice(0, x_ref.shape[0])
    _norm_modulate_store(x_ref, mod_ref, g_ref, h_ref, 0, 1, rows)
    kv = jnp.dot(h_ref[...], w_ref[:, ATTN_W:ATTN_W + 2 * KV_W], preferred_element_type=F32)
    _store_dup_heads(kd_ref, rows, kv[:, :KV_W])
    _store_dup_heads(vd_ref, rows, kv[:, KV_W:])
    rx_ref[...] = jnp.dot(h_ref[...], w_ref[:, ATTN_W + 2 * KV_W:ATTN_W + 2 * KV_W + RNN_W],
                          preferred_element_type=F32)


def _inproj(x2d, mod_all, layer, mod_row_of_tile, g, w, *, tm, tiles_per_seq, rope_tables, kv_only):
    n_tok, d = x2d.shape
    grid = (n_tok // tm,)
    row = lambda i: (i, 0)
    w_cols = ATTN_W + 2 * KV_W + RNN_W if kv_only else w.shape[2]
    in_specs = [
        pl.BlockSpec((tm, d), row),
        _mod_block(mod_all, layer, mod_row_of_tile),
        _layer_block(g, layer),
        _layer_block(w, layer, (d, w_cols)),
    ]
    args = [x2d, mod_all, g, w]
    if rope_tables is not None:
        in_specs += [pl.BlockSpec((tm, LANES), lambda i: (i % tiles_per_seq, 0))] * 3
        args += list(rope_tables)

    def out(width, dtype):
        return pl.BlockSpec((tm, width), row), jax.ShapeDtypeStruct((n_tok, width), dtype)

    if kv_only:
        outs = [out(2 * KV_W, BF16), out(2 * KV_W, BF16), out(RNN_W, F32)]
        body = _inproj_kv_body
    else:
        outs = [out(ATTN_W, BF16), out(2 * KV_W, BF16), out(2 * KV_W, BF16), out(RNN_W, F32),
                out(RNN_W, BF16), out(POOL_W, F32), out(N_BRANCH * d, BF16)]
        body = functools.partial(_inproj_full_body, rope=rope_tables is not None)
    return pl.pallas_call(
        body,
        grid=grid,
        in_specs=in_specs,
        out_specs=[o[0] for o in outs],
        out_shape=[o[1] for o in outs],
        scratch_shapes=[pltpu.VMEM((tm, d), BF16)],
        compiler_params=_params(1),
        name="inproj_kv" if kv_only else "inproj",
    )(*args)


def _attend(sink_ref, layer, items):
    heads_per_kv = N_Q_HEADS // N_KV_HEADS
    zero = jnp.zeros((), BF16)

    def scores(item, h):
        q_blk, k_rows, _, biases = item
        low = lax.broadcasted_iota(jnp.int32, (q_blk.shape[0], LANES), 1) < HEAD_DIM
        k_h = k_rows[:, h * LANES:(h + 1) * LANES]
        stacked = []
        for c2 in range(heads_per_kv // 2):
            qc = q_blk[:, (2 * h + c2) * LANES:(2 * h + c2 + 1) * LANES]
            stacked += [jnp.where(low, qc, zero), jnp.where(low, zero, qc)]
        lhs = jnp.concatenate(stacked, axis=0)
        s = lax.dot_general(lhs, k_h, (((1,), (1,)), ((), ())), preferred_element_type=F32)
        blocks = []
        for j, bias in enumerate(biases):
            blk = s[:, j * LANES:(j + 1) * LANES]
            if bias is not None:
                blk = blk + jnp.concatenate([bias] * heads_per_kv, axis=0)
            blocks.append(blk)
        return blocks

    def softmax(h, blocks):
        tq = blocks[0].shape[0] // heads_per_kv
        sink_col = jnp.concatenate(
            [jnp.full((tq, 1), sink_ref[layer, h * heads_per_kv + g] * LOG2E, F32)
             for g in range(heads_per_kv)],
            axis=0)
        m = jnp.maximum(jnp.max(functools.reduce(jnp.maximum, blocks), axis=-1, keepdims=True), sink_col)
        probs = [jnp.exp2(blk - m) for blk in blocks]
        denom = jnp.sum(functools.reduce(jnp.add, probs), axis=-1, keepdims=True) + jnp.exp2(sink_col - m)
        return jnp.concatenate(probs, axis=1).astype(BF16), denom

    def weighted(item, h, p, denom):
        tq = item[0].shape[0]
        low = lax.broadcasted_iota(jnp.int32, (tq, LANES), 1) < HEAD_DIM
        v_h = item[2][:, h * LANES:(h + 1) * LANES]
        o = jnp.dot(p, v_h, preferred_element_type=F32) * (1.0 / denom)
        return [jnp.where(low, o[(2 * c2) * tq:(2 * c2 + 1) * tq],
                          o[(2 * c2 + 1) * tq:(2 * c2 + 2) * tq]).astype(BF16)
                for c2 in range(heads_per_kv // 2)]

    units = [(item, h) for item in items for h in range(N_KV_HEADS)]
    all_scores = [scores(item, h) for item, h in units]
    all_probs = [softmax(h, blocks) for (_, h), blocks in zip(units, all_scores)]
    all_outs = [weighted(item, h, *pd) for (item, h), pd in zip(units, all_probs)]
    return [jnp.concatenate([o for outs in all_outs[i * N_KV_HEADS:(i + 1) * N_KV_HEADS] for o in outs],
                            axis=1) for i in range(len(items))]


def _attn_local_body(sink_ref, q_ref, kd_ref, vd_ref, kc_ref, vc_ref, o_ref, *, layer, n_blocks, group):
    blk = ATTN_BLOCK
    ii = lax.broadcasted_iota(jnp.int32, (blk, blk), 0)
    jj = lax.broadcasted_iota(jnp.int32, (blk, blk), 1)
    n_ctx_blocks = kc_ref.shape[0] // LANES
    rows = lambda i: pl.ds(pl.multiple_of(i * blk, blk), blk)

    def item(n):
        prv, nxt = jnp.maximum(n - 1, 0), jnp.minimum(n + 1, n_blocks - 1)
        bias_prev = jnp.where((jj >= ii) & (n > 0), 0.0, NEG_INF).astype(F32)
        bias_next = jnp.where((jj <= ii) & (n < n_blocks - 1), 0.0, NEG_INF).astype(F32)
        k_rows = jnp.concatenate([kd_ref[rows(prv), :], kd_ref[rows(n), :], kd_ref[rows(nxt), :],
                                  kc_ref[...]], axis=0)
        v_rows = jnp.concatenate([vd_ref[rows(prv), :], vd_ref[rows(n), :], vd_ref[rows(nxt), :],
                                  vc_ref[...]], axis=0)
        return (q_ref[rows(n), :], k_rows, v_rows, [bias_prev, None, bias_next] + [None] * n_ctx_blocks)

    def body(i, carry):
        blocks = [i * group + g for g in range(group)]
        for n, out in zip(blocks, _attend(sink_ref, layer, [item(n) for n in blocks])):
            o_ref[rows(n), :] = out
        return carry

    lax.fori_loop(0, n_blocks // group, body, 0)


def _attn_ctx_body(sink_ref, q_ref, kc_ref, vc_ref, o_ref, *, layer):
    item = (q_ref[...], kc_ref[...], vc_ref[...], [None] * (kc_ref.shape[0] // LANES))
    o_ref[...] = _attend(sink_ref, layer, [item])[0]


def _attention(sink, layer, q, kd, vd, kdc, vdc):
    b, s, _ = q.shape
    c = kdc.shape[1]
    kw = 2 * KV_W
    n_blocks = s // ATTN_BLOCK
    group = math.gcd(n_blocks, 4)
    idx = lambda bi: (bi, 0, 0)
    return pl.pallas_call(
        functools.partial(_attn_local_body, layer=layer, n_blocks=n_blocks, group=group),
        grid=(b,),
        in_specs=[
            pl.BlockSpec(memory_space=pltpu.SMEM),
            pl.BlockSpec((None, s, ATTN_W), idx),
            pl.BlockSpec((None, s, kw), idx), pl.BlockSpec((None, s, kw), idx),
            pl.BlockSpec((None, c, kw), idx), pl.BlockSpec((None, c, kw), idx),
        ],
        out_specs=pl.BlockSpec((None, s, ATTN_W), idx),
        out_shape=jax.ShapeDtypeStruct((b, s, ATTN_W), BF16),
        compiler_params=_params(1),
        name="attn_local",
    )(sink, q, kd, vd, kdc, vdc)


def _attention_ctx(sink, layer, qc, kdc, vdc):
    b, c, _ = qc.shape
    kw = 2 * KV_W
    idx = lambda bi: (bi, 0, 0)
    return pl.pallas_call(
        functools.partial(_attn_ctx_body, layer=layer),
        grid=(b,),
        in_specs=[
            pl.BlockSpec(memory_space=pltpu.SMEM),
            pl.BlockSpec((None, c, ATTN_W), idx),
            pl.BlockSpec((None, c, kw), idx), pl.BlockSpec((None, c, kw), idx),
        ],
        out_specs=pl.BlockSpec((None, c, ATTN_W), idx),
        out_shape=jax.ShapeDtypeStruct((b, c, ATTN_W), BF16),
        compiler_params=_params(1),
        name="attn_ctx",
    )(sink, qc, kdc, vdc)


def _pool_body(u_ref, w_ref, sc_ref, o_ref, pad_ref, *, tile):
    length = u_ref.shape[0]
    halo = SUBLANES
    pad_ref[0:halo, :] = jnp.zeros((halo, POOL_W), F32)
    pad_ref[halo + length:2 * halo + length, :] = jnp.zeros((halo, POOL_W), F32)
    pad_ref[halo:halo + length, :] = u_ref[...]
    rows = tile + 2 * halo
    for t0 in range(0, length, tile):
        t = t0 + lax.broadcasted_iota(jnp.int32, (tile, POOL_GW), 0)
        for gi, w in enumerate(POOL_WINDOWS):
            lanes = slice(gi * POOL_GW, (gi + 1) * POOL_GW)
            p = pad_ref[t0:t0 + rows, lanes]
            acc = p
            step = 1
            while step < w:
                acc = acc + pltpu.roll(acc, rows - step, 0)
                step *= 2
            back = (w - 1) // 2
            win = (pltpu.roll(acc, back, 0) if back else acc)[halo:halo + tile]
            lo = jnp.maximum(t - back, 0)
            hi = jnp.minimum(t + w // 2 + 1, length)
            mean = win / (hi - lo).astype(F32)
            dlt = (mean - p[halo:halo + tile]).astype(BF16)
            y = jnp.dot(dlt, w_ref[gi], preferred_element_type=F32) * sc_ref[:, lanes]
            o_ref[t0:t0 + tile, lanes] = y.astype(BF16)


def _pool(pu, layer, w_mix, ch_scale):
    b, length, _ = pu.shape
    tile = min(length, 256)
    idx = lambda bi: (bi, 0, 0)
    return pl.pallas_call(
        functools.partial(_pool_body, tile=tile),
        grid=(b,),
        in_specs=[pl.BlockSpec((None, length, POOL_W), idx), _layer_block(w_mix, layer),
                  _layer_block(ch_scale, layer)],
        out_specs=pl.BlockSpec((None, length, POOL_W), idx),
        out_shape=jax.ShapeDtypeStruct((b, length, POOL_W), BF16),
        scratch_shapes=[pltpu.VMEM((length + 2 * SUBLANES, POOL_W), F32)],
        compiler_params=_params(1),
        name="pool",
    )(pu, w_mix, ch_scale)


def _lru_body(*refs, n_ctx, n_lat, tile, with_ctx_out):
    if with_ctx_out:
        (rxc_ref, rxl_ref, ryc_ref, ryl_ref, cw_ref, cb_ref, wg_ref, ba_ref, bx_ref, lam_ref,
         ol_ref, oc_ref, pad_ref, xl_ref, hf_ref, hb_ref, xp_ref, a_ref, b_ref) = refs
    else:
        (rxc_ref, rxl_ref, ryl_ref, cw_ref, cb_ref, wg_ref, ba_ref, bx_ref, lam_ref,
         ol_ref, pad_ref, xl_ref, hf_ref, hb_ref, xp_ref, a_ref, b_ref) = refs
    halo = SUBLANES
    n_tot = n_ctx + n_lat
    n_slab = RNN_W // LANES

    zeros = jnp.zeros((halo, RNN_W), F32)
    c0, l0 = halo, 2 * halo + n_ctx
    pad_ref[0:halo, :] = zeros
    pad_ref[c0 + n_ctx:l0, :] = zeros
    pad_ref[l0 + n_lat:l0 + n_lat + halo, :] = zeros
    pad_ref[c0:c0 + n_ctx, :] = rxc_ref[...]
    pad_ref[l0:l0 + n_lat, :] = rxl_ref[...]

    cw = 0.5 * cw_ref[...]
    cb = 0.5 * cb_ref[...]

    def conv_tile(src0, dst_rows):
        rows = tile + 2 * halo
        p = pad_ref[src0 - halo:src0 + tile + halo, :]
        y = (cb + p[halo:halo + tile] * cw[1:2, :]
             + pltpu.roll(p, 1, 0)[halo:halo + tile] * cw[0:1, :]
             + pltpu.roll(p, rows - 1, 0)[halo:halo + tile] * cw[2:3, :]
             + pltpu.roll(p, rows - 2, 0)[halo:halo + tile] * cw[3:4, :])
        for s in range(n_slab):
            for d0 in dst_rows:
                xl_ref[s, d0:d0 + tile, :] = y[:, s * LANES:(s + 1) * LANES]

    for t0 in range(0, n_ctx, tile):
        conv_tile(c0 + t0, (t0, n_tot + t0))
    for t0 in range(0, n_lat, tile):
        conv_tile(l0 + t0, (n_ctx + t0,))

    lam = lam_ref[...]
    log_sig = jnp.minimum(lam, 0.0) - jnp.log1p(jnp.exp(-jnp.abs(lam)))
    rate = (0.5 * LOG2E * LRU_C) * log_sig
    half_ba = 0.5 * ba_ref[...]
    half_bx = 0.5 * bx_ref[...]
    half = RNN_W // 2

    def run_chunk(d, base, carry):
        res_ref = hf_ref if d == 0 else hb_ref
        win0 = 0 if d == 0 else n_ctx
        for v in range(SEG):
            for s in range(n_slab):
                xp_ref[d, v * SUBLANES:(v + 1) * SUBLANES, s * LANES:(s + 1) * LANES] = (
                    xl_ref[s, pl.ds(win0 + base + v, SUBLANES, stride=SEG), :])
        for j in range(2):
            ch = slice(j * half, (j + 1) * half)
            xj = xp_ref[d, :, ch]
            gates = jnp.dot(xj.astype(BF16), wg_ref[d, j], preferred_element_type=F32)
            t_r = jnp.tanh(gates[:, :half] + half_ba[d:d + 1, ch])
            t_i = jnp.tanh(gates[:, half:] + half_bx[d:d + 1, ch])
            a = jnp.exp2(t_r * rate[d:d + 1, ch] + rate[d:d + 1, ch])
            gap = 1.0 - a * a
            mult = gap * lax.rsqrt(jnp.maximum(gap, TINY))
            a_ref[d, :, ch] = a
            b_ref[d, :, ch] = (mult * xj) * (t_i + 1.0)
        order = range(SEG) if d == 0 else range(SEG - 1, -1, -1)
        h = jnp.zeros((SUBLANES, RNN_W), F32)
        acum = jnp.ones((SUBLANES, RNN_W), F32)
        for v in order:
            rows = slice(v * SUBLANES, (v + 1) * SUBLANES)
            av = a_ref[d, rows, :]
            h = av * h + b_ref[d, rows, :]
            acum = av * acum
            b_ref[d, rows, :] = h
            a_ref[d, rows, :] = acum
        seg_in = [None] * SUBLANES
        state = carry
        seg_order = range(SUBLANES) if d == 0 else range(SUBLANES - 1, -1, -1)
        for i in seg_order:
            seg_in[i] = state
            state = h[i:i + 1, :] + acum[i:i + 1, :] * state
        seg_state = jnp.concatenate(seg_in, axis=0)
        for v in range(SEG):
            rows = slice(v * SUBLANES, (v + 1) * SUBLANES)
            hv = b_ref[d, rows, :] + a_ref[d, rows, :] * seg_state
            for s in range(n_slab):
                res_ref[s, pl.ds(base + v, SUBLANES, stride=SEG), :] = hv[:, s * LANES:(s + 1) * LANES]
        return state

    n_chunks = n_tot // CHUNK

    def step(c, carries):
        cf, cb = carries
        cf = run_chunk(0, c * CHUNK, cf)
        cb = run_chunk(1, (n_chunks - 1 - c) * CHUNK, cb)
        return cf, cb

    zero_state = jnp.zeros((1, RNN_W), F32)
    lax.fori_loop(0, n_chunks, step, (zero_state, zero_state))

    def emit(out_ref, ry_ref, f0, b0, length):
        for t0 in range(0, length, tile):
            hsum = jnp.concatenate(
                [hf_ref[s, f0 + t0:f0 + t0 + tile, :] + hb_ref[s, b0 + t0:b0 + t0 + tile, :]
                 for s in range(n_slab)], axis=1)
            gate = ry_ref[t0:t0 + tile, :].astype(F32)
            out_ref[t0:t0 + tile, :] = (hsum * gate).astype(BF16)

    emit(ol_ref, ryl_ref, n_ctx, 0, n_lat)
    if with_ctx_out:
        emit(oc_ref, ryc_ref, 0, n_lat, n_ctx)


def _lru(layer, rxc, rxl, ryc, ryl, conv_w, conv_b, wg, b_a, b_x, lam):
    b, n_ctx, _ = rxc.shape
    n_lat = rxl.shape[1]
    n_tot = n_ctx + n_lat
    tile = math.gcd(n_ctx, n_lat, 256)
    assert n_tot % CHUNK == 0 and tile % SUBLANES == 0
    with_ctx_out = ryc is not None
    idx = lambda bi: (bi, 0, 0)
    seq = lambda n, dt=None: pl.BlockSpec((None, n, RNN_W), idx)
    n_slab = RNN_W // LANES
    in_specs = [seq(n_ctx), seq(n_lat)] + ([seq(n_ctx)] if with_ctx_out else []) + [seq(n_lat)]
    args = [rxc, rxl] + ([ryc] if with_ctx_out else []) + [ryl]
    for wgt in (conv_w, conv_b, wg, b_a, b_x, lam):
        in_specs.append(_layer_block(wgt, layer))
        args.append(wgt)
    out_specs = [seq(n_lat)] + ([seq(n_ctx)] if with_ctx_out else [])
    out_shape = [jax.ShapeDtypeStruct((b, n_lat, RNN_W), BF16)]
    if with_ctx_out:
        out_shape.append(jax.ShapeDtypeStruct((b, n_ctx, RNN_W), BF16))
    res = pl.pallas_call(
        functools.partial(_lru_body, n_ctx=n_ctx, n_lat=n_lat, tile=tile, with_ctx_out=with_ctx_out),
        grid=(b,),
        in_specs=in_specs,
        out_specs=out_specs,
        out_shape=out_shape,
        scratch_shapes=[
            pltpu.VMEM((n_tot + 3 * SUBLANES, RNN_W), F32),
            pltpu.VMEM((n_slab, n_tot + n_ctx, LANES), F32),
            pltpu.VMEM((n_slab, n_tot, LANES), F32),
            pltpu.VMEM((n_slab, n_tot, LANES), F32),
            pltpu.VMEM((2, CHUNK, RNN_W), F32),
            pltpu.VMEM((2, CHUNK, RNN_W), F32),
            pltpu.VMEM((2, CHUNK, RNN_W), F32),
        ],
        compiler_params=_params(1),
        name="rglru",
    )(*args)
    return (res[0], res[1]) if with_ctx_out else (res[0], None)


def _mix_ffn_body(x_ref, mod_ref, at_ref, po_ref, rn_ref, gt_ref, wa_ref, wp_ref, wr_ref, wo_ref,
                  gmix_ref, gpre_ref, wgu_ref, wd_ref, gpost_ref, o_ref, m_ref, h_ref, *, d_ff, chunk):
    d = x_ref.shape[1]
    for rows in _row_halves(x_ref.shape[0]):
        merged = None
        for k, (br_ref, w_ref) in enumerate(((at_ref, wa_ref), (po_ref, wp_ref), (rn_ref, wr_ref))):
            t = gt_ref[rows, k * d:(k + 1) * d].astype(F32) * jnp.dot(
                br_ref[rows, :], w_ref[...], preferred_element_type=F32)
            merged = t if merged is None else merged + t
        m_ref[rows, :] = merged.astype(BF16)
        mix = jnp.dot(m_ref[rows, :], wo_ref[...], preferred_element_type=F32)
        o_ref[rows, :] = x_ref[rows, :] + mod_ref[2:3, :] * (_rms(mix) * gmix_ref[...])
        _norm_modulate_store(o_ref, mod_ref, gpre_ref, h_ref, 3, 4, rows)

    halves = _row_halves(x_ref.shape[0])
    accs = [None] * len(halves)
    for c0 in range(0, d_ff, chunk):
        c1 = min(c0 + chunk, d_ff)
        for i, rows in enumerate(halves):
            gate = jnp.dot(h_ref[rows, :], wgu_ref[:, c0:c1], preferred_element_type=F32)
            up = jnp.dot(h_ref[rows, :], wgu_ref[:, d_ff + c0:d_ff + c1], preferred_element_type=F32)
            act = ((gate * jax.nn.sigmoid(gate)) * up).astype(BF16)
            part = jnp.dot(act, wd_ref[c0:c1, :], preferred_element_type=F32)
            accs[i] = part if accs[i] is None else accs[i] + part
    for acc, rows in zip(accs, halves):
        o_ref[rows, :] = o_ref[rows, :] + mod_ref[5:6, :] * (_rms(acc) * gpost_ref[...])


def _mix_ffn(x2d, mod_all, layer, mod_row_of_tile, attn, pool, rnn, gates, wa, wp, wr, wo, g_mix,
             g_pre, w_gu, w_down, g_post, *, tm):
    n_tok, d = x2d.shape
    d_ff = w_down.shape[1]
    row = lambda i: (i, 0)
    tok = lambda arr: pl.BlockSpec((tm, arr.shape[1]), row)
    params = (wa, wp, wr, wo, g_mix, g_pre, w_gu, w_down, g_post)
    return pl.pallas_call(
        functools.partial(_mix_ffn_body, d_ff=d_ff, chunk=1024),
        grid=(n_tok // tm,),
        in_specs=[tok(x2d), _mod_block(mod_all, layer, mod_row_of_tile),
                  tok(attn), tok(pool), tok(rnn), tok(gates)]
                 + [_layer_block(p, layer) for p in params],
        out_specs=pl.BlockSpec((tm, d), row),
        out_shape=jax.ShapeDtypeStruct((n_tok, d), F32),
        scratch_shapes=[pltpu.VMEM((tm, d), BF16), pltpu.VMEM((tm, d), BF16)],
        compiler_params=_params(1),
        name="mix_ffn",
    )(x2d, mod_all, attn, pool, rnn, gates, *params)


def _pack_gate_weights(w_a, w_x):
    def block_diag(w):
        bw = RNN_W // RNN_BLOCKS
        rows = [jnp.pad(w[:, :, h], ((0, 0), (0, 0), (0, 0), (bw * h, RNN_W - bw * (h + 1))))
                for h in range(RNN_BLOCKS)]
        return jnp.concatenate(rows, axis=2)
    bd_a, bd_x = block_diag(w_a), block_diag(w_x)
    half = RNN_W // 2
    halves = []
    for j in range(2):
        sl = slice(j * half, (j + 1) * half)
        halves.append(jnp.concatenate([bd_a[:, :, sl, sl], bd_x[:, :, sl, sl]], axis=3))
    return jnp.stack(halves, axis=2).astype(BF16)


def kernel(x, c, ctx, c_ctx, w_ada, b_ada, g_pre_mix, g_post_mix, g_pre_ffn, g_post_ffn, w_in, attn_sink, w_attn_o, pool_mix, pool_scale, w_pool_o, conv_w, conv_b, lru_w_a, lru_b_a, lru_w_x, lru_b_x, lru_lambda, w_rnn_o, w_out, w_gu, w_down):
    bsz, seq, d = x.shape
    n_ctx = ctx.shape[1]
    depth = w_ada.shape[0]
    tm = min(512, seq)
    tm_ctx = min(tm, n_ctx)
    tm_in = min(1024, seq)

    mod_rows = -(-(bsz + 1) // SUBLANES) * SUBLANES
    cvec = jnp.zeros((mod_rows, d), F32).at[:bsz].set(c).at[bsz].set(c_ctx)
    mod_all = _adaln_mod(cvec, w_ada, b_ada).reshape(depth, mod_rows, 6, d)

    rope = _rope_tables(seq)
    lat_row = lambda i: i // (seq // tm)
    ctx_row = lambda i: bsz

    rowvec = lambda a: a.reshape(depth, 1, -1)
    w_in_b, w_gu_b, w_dn_b = w_in.astype(BF16), w_gu.astype(BF16), w_down.astype(BF16)
    wa, wp, wr, wo = (w.astype(BF16) for w in (w_attn_o, w_pool_o, w_rnn_o, w_out))
    w_mix = pool_mix.astype(BF16)
    wg = _pack_gate_weights(lru_w_a, lru_w_x)
    g_pre, g_post = rowvec(g_pre_mix), rowvec(g_post_mix)
    gf_pre, gf_post = rowvec(g_pre_ffn), rowvec(g_post_ffn)
    p_scale, cv_b = rowvec(pool_scale), rowvec(conv_b)

    x2 = x.reshape(bsz * seq, d)
    c2 = ctx.reshape(bsz * n_ctx, d)
    r3 = lambda a, n: a.reshape(bsz, n, a.shape[-1])
    flat = lambda a: a.reshape(-1, a.shape[-1])
    for l in range(depth):
        need_ctx = l < depth - 1
        q, kd, vd, rx, ry, pu, gt = _inproj(
            x2, mod_all, l, (lambda i: i // (seq // tm_in)), g_pre, w_in_b, tm=tm_in, tiles_per_seq=seq // tm_in, rope_tables=rope,
            kv_only=False)
        if need_ctx:
            qc, kdc, vdc, rxc, ryc, puc, gtc = _inproj(
                c2, mod_all, l, ctx_row, g_pre, w_in_b, tm=tm_ctx, tiles_per_seq=1, rope_tables=None,
                kv_only=False)
        else:
            kdc, vdc, rxc = _inproj(
                c2, mod_all, l, ctx_row, g_pre, w_in_b, tm=tm_ctx, tiles_per_seq=1, rope_tables=None,
                kv_only=True)
            ryc = None

        kdc3, vdc3 = r3(kdc, n_ctx), r3(vdc, n_ctx)
        attn_l = _attention(attn_sink, l, r3(q, seq), r3(kd, seq), r3(vd, seq), kdc3, vdc3)
        pool_l = _pool(r3(pu, seq), l, w_mix, p_scale)
        rnn_l, rnn_c = _lru(l, r3(rxc, n_ctx), r3(rx, seq), None if ryc is None else r3(ryc, n_ctx),
                            r3(ry, seq), conv_w, cv_b, wg, lru_b_a, lru_b_x, lru_lambda)

        x2 = _mix_ffn(x2, mod_all, l, lat_row, flat(attn_l), flat(pool_l), flat(rnn_l), gt, wa, wp, wr, wo,
                      g_post, gf_pre, w_gu_b, w_dn_b, gf_post, tm=tm)

        if need_ctx:
            attn_c = _attention_ctx(attn_sink, l, r3(qc, n_ctx), kdc3, vdc3)
            pool_c = _pool(r3(puc, n_ctx), l, w_mix, p_scale)
            c2 = _mix_ffn(c2, mod_all, l, ctx_row, flat(attn_c), flat(pool_c), flat(rnn_c), gtc, wa, wp, wr,
                          wo, g_post, gf_pre, w_gu_b, w_dn_b, gf_post, tm=tm_ctx)
    return x2.reshape(bsz, seq, d)
```

```python
import functools
import math

import jax
import jax.numpy as jnp
import numpy as np
from jax import lax
from jax.experimental import pallas as pl
from jax.experimental.pallas import tpu as pltpu

F32 = jnp.float32
BF16 = jnp.bfloat16

GRID_W = 64
HEAD_DIM = 64
N_Q_HEADS = 8
N_KV_HEADS = 2
ATTN_W = N_Q_HEADS * HEAD_DIM
KV_W = N_KV_HEADS * HEAD_DIM
ATTN_BLOCK = 128
ROPE_BASE = 10000.0
POOL_WINDOWS = (2, 4, 8, 16)
POOL_W = 512
POOL_GW = POOL_W // len(POOL_WINDOWS)
RNN_W = 512
RNN_BLOCKS = 8
CONV_W = 4
LRU_C = 8.0
N_BRANCH = 3
EPS = 1e-6
NEG_INF = -1e30
LOG2E = 1.4426950408889634
TINY = 1e-30

LANES = 128
SUBLANES = 8
VMEM_LIMIT_BYTES = 56 * 1024 * 1024

SEG = 36
CHUNK = SUBLANES * SEG


def _params(n_axes):
    return pltpu.CompilerParams(
        dimension_semantics=("arbitrary",) * n_axes, vmem_limit_bytes=VMEM_LIMIT_BYTES)


def _layer_block(arr, layer, block=None):
    block = tuple(arr.shape[1:]) if block is None else block
    index = (layer,) + (0,) * len(block)
    return pl.BlockSpec((None,) + block, lambda *_: index, pipeline_mode=pl.Buffered(1))


def _mod_block(mod_all, layer, mod_row_of_tile):
    return pl.BlockSpec((None, None) + mod_all.shape[2:], lambda i: (layer, mod_row_of_tile(i), 0, 0))


def _rms(xf):
    return xf * lax.rsqrt(jnp.mean(xf * xf, axis=-1, keepdims=True) + EPS)


def _mod_body(c_ref, w_ref, b_ref, o_ref):
    c = c_ref[...]
    s = (c * jax.nn.sigmoid(c)).astype(BF16)
    o_ref[0] = jnp.dot(s, w_ref[0].astype(BF16), preferred_element_type=F32) + b_ref[0]


def _adaln_mod(cvec, w_ada, b_ada):
    n_layers, d, n6 = w_ada.shape
    rows = cvec.shape[0]
    tn = n6 // 4
    return pl.pallas_call(
        _mod_body,
        grid=(n_layers, n6 // tn),
        in_specs=[
            pl.BlockSpec((rows, d), lambda l, j: (0, 0)),
            pl.BlockSpec((1, d, tn), lambda l, j: (l, 0, j)),
            pl.BlockSpec((1, 1, tn), lambda l, j: (l, 0, j)),
        ],
        out_specs=pl.BlockSpec((1, rows, tn), lambda l, j: (l, 0, j)),
        out_shape=jax.ShapeDtypeStruct((n_layers, rows, n6), F32),
        compiler_params=_params(2),
        name="adaln_mod",
    )(cvec, w_ada, b_ada.reshape(n_layers, 1, n6))


def _rope_tables(seq_len):
    pos = np.arange(seq_len)
    row = (pos // GRID_W).astype(np.float32)
    col = (pos % GRID_W).astype(np.float32)
    half = HEAD_DIM // 2
    quarter = half // 2
    inv = (np.float32(ROPE_BASE) ** (-(np.arange(quarter, dtype=np.float32) * np.float32(2.0 / half)))
           ).astype(np.float32)
    j = np.arange(LANES) % HEAD_DIM
    is_col = j >= half
    second = (j % half) >= quarter
    freq = inv[j % quarter]
    ang = (np.where(is_col[None, :], col[:, None], row[:, None]) * freq[None, :]).astype(np.float32)
    cos, sin = np.cos(ang), np.sin(ang)
    sin_prev = np.where(second[None, :], sin, 0.0)
    sin_next = np.where(second[None, :], 0.0, -sin)
    return tuple(jnp.asarray(t, F32) for t in (cos, sin_prev, sin_next))


def _row_halves(n_rows):
    if n_rows % (2 * SUBLANES * 2) or n_rows < 256:
        return [slice(0, n_rows)]
    return [slice(0, n_rows // 2), slice(n_rows // 2, n_rows)]


def _store_dup_heads(dst_ref, rows, kv):
    low = lax.broadcasted_iota(jnp.int32, kv.shape, 1) < HEAD_DIM
    swapped = pltpu.roll(kv, HEAD_DIM, 1)
    dst_ref[rows, 0:LANES] = jnp.where(low, kv, swapped).astype(BF16)
    dst_ref[rows, LANES:2 * LANES] = jnp.where(low, swapped, kv).astype(BF16)


def _norm_modulate_store(x_ref, mod_ref, g_ref, h_ref, shift_row, scale_row, rows):
    y = _rms(x_ref[rows, :]) * g_ref[...]
    h = y * (1.0 + mod_ref[scale_row:scale_row + 1, :]) + mod_ref[shift_row:shift_row + 1, :]
    h_ref[rows, :] = h.astype(BF16)


def _inproj_full_body(*refs, rope, tiles_per_seq):
    if rope:
        (x_ref, xp_ref, xn_ref, mod_ref, g_ref, w_ref, wmix_ref, psc_ref, cos_ref, sp_ref, sn_ref,
         q_ref, kd_ref, vd_ref, rx_ref, ry_ref, po_ref, gt_ref, h_ref) = refs
    else:
        (x_ref, xp_ref, xn_ref, mod_ref, g_ref, w_ref, wmix_ref, psc_ref,
         q_ref, kd_ref, vd_ref, rx_ref, ry_ref, po_ref, gt_ref, h_ref) = refs
    quarter = HEAD_DIM // 4
    tm, d = x_ref.shape
    halo = SUBLANES
    tile_in_seq = pl.program_id(0) % tiles_per_seq
    pu0 = ATTN_W + 2 * KV_W + 2 * RNN_W

    def modulated(xf):
        y = _rms(xf) * g_ref[...]
        return y * (1.0 + mod_ref[1:2, :]) + mod_ref[0:1, :]

    pack = 2 * SUBLANES
    zeros = jnp.zeros((SUBLANES, d), F32)
    before = jnp.where(tile_in_seq == 0, 0.0, modulated(xp_ref[...]))
    after = jnp.where(tile_in_seq == tiles_per_seq - 1, 0.0, modulated(xn_ref[...]))
    h_ref[0:pack, :] = jnp.concatenate([zeros, before], axis=0).astype(BF16)
    h_ref[pack + tm:2 * pack + tm, :] = jnp.concatenate([after, zeros], axis=0).astype(BF16)
    for rows in _row_halves(tm):
        h_ref[pack + rows.start:pack + rows.stop, :] = modulated(x_ref[rows, :]).astype(BF16)

    for rows in _row_halves(tm):
        r0, n_rows = rows.start, rows.stop - rows.start
        hrows = slice(pack + r0, pack + rows.stop)

        def proj(c0, c1):
            return jnp.dot(h_ref[hrows, :], w_ref[:, c0:c1], preferred_element_type=F32)

        def rotary(pc):
            if not rope:
                return pc
            return (pc * cos_ref[rows, :]
                    + pltpu.roll(pc, quarter, 1) * sp_ref[rows, :]
                    + pltpu.roll(pc, LANES - quarter, 1) * sn_ref[rows, :])

        padded = jnp.dot(h_ref[r0:r0 + n_rows + 2 * pack, :], w_ref[:, pu0:pu0 + POOL_W],
                         preferred_element_type=F32)[pack - halo:pack + n_rows + halo]
        t = tile_in_seq * tm + r0 + lax.broadcasted_iota(jnp.int32, (n_rows, POOL_GW), 0)

        def pool_group(gi):
            lanes = slice(gi * POOL_GW, (gi + 1) * POOL_GW)
            po_ref[rows, lanes] = _pool_group(padded[:, lanes], t, tiles_per_seq * tm, POOL_WINDOWS[gi],
                                              wmix_ref[gi], psc_ref[:, lanes])

        for c in range(ATTN_W // (2 * LANES)):
            pair = proj(c * 2 * LANES, (c + 1) * 2 * LANES)
            for half in range(2):
                col = (2 * c + half) * LANES
                pc = rotary(pair[:, half * LANES:(half + 1) * LANES])
                q_ref[rows, col:col + LANES] = (pc * (HEAD_DIM ** -0.5 * LOG2E)).astype(BF16)
        o = ATTN_W
        kv = proj(o, o + 2 * KV_W)
        _store_dup_heads(kd_ref, rows, rotary(kv[:, :KV_W]))
        _store_dup_heads(vd_ref, rows, kv[:, KV_W:])
        o += 2 * KV_W
        rx_ref[rows, :] = proj(o, o + RNN_W)
        o += RNN_W
        ry_ref[rows, :] = jax.nn.gelu(proj(o, o + RNN_W)).astype(BF16)
        o += RNN_W + POOL_W
        pool_group(0)
        for c in range(N_BRANCH):
            gl = proj(o + c * d, o + (c + 1) * d)
            gt_ref[rows, c * d:(c + 1) * d] = jax.nn.sigmoid(gl).astype(BF16)
            pool_group(c + 1)


def _inproj_kv_body(x_ref, mod_ref, g_ref, w_ref, kd_ref, vd_ref, rx_ref, h_ref):
    rows = slice(0, x_ref.shape[0])
    _norm_modulate_store(x_ref, mod_ref, g_ref, h_ref, 0, 1, rows)
    kv = jnp.dot(h_ref[...], w_ref[:, ATTN_W:ATTN_W + 2 * KV_W], preferred_element_type=F32)
    _store_dup_heads(kd_ref, rows, kv[:, :KV_W])
    _store_dup_heads(vd_ref, rows, kv[:, KV_W:])
    rx_ref[...] = jnp.dot(h_ref[...], w_ref[:, ATTN_W + 2 * KV_W:ATTN_W + 2 * KV_W + RNN_W],
                          preferred_element_type=F32)


def _inproj(x2d, mod_all, layer, mod_row_of_tile, g, w, w_mix, p_scale, *, tm, tiles_per_seq, rope_tables,
            kv_only):
    n_tok, d = x2d.shape
    grid = (n_tok // tm,)
    row = lambda i: (i, 0)
    in_specs = [pl.BlockSpec((tm, d), row)]
    args = [x2d]
    if not kv_only:
        per_tile = tm // SUBLANES
        in_specs += [pl.BlockSpec((SUBLANES, d), lambda i: (jnp.maximum(i * per_tile - 1, 0), 0)),
                     pl.BlockSpec((SUBLANES, d),
                                  lambda i: (jnp.minimum((i + 1) * per_tile, n_tok // SUBLANES - 1), 0))]
        args += [x2d, x2d]
    w_cols = ATTN_W + 2 * KV_W + RNN_W if kv_only else w.shape[2]
    in_specs += [_mod_block(mod_all, layer, mod_row_of_tile), _layer_block(g, layer),
                 _layer_block(w, layer, (d, w_cols))]
    args += [mod_all, g, w]
    if not kv_only:
        in_specs += [_layer_block(w_mix, layer), _layer_block(p_scale, layer)]
        args += [w_mix, p_scale]
    if rope_tables is not None:
        in_specs += [pl.BlockSpec((tm, LANES), lambda i: (i % tiles_per_seq, 0))] * 3
        args += list(rope_tables)

    def out(width, dtype):
        return pl.BlockSpec((tm, width), row), jax.ShapeDtypeStruct((n_tok, width), dtype)

    if kv_only:
        outs = [out(2 * KV_W, BF16), out(2 * KV_W, BF16), out(RNN_W, F32)]
        body = _inproj_kv_body
        h_rows = tm
    else:
        outs = [out(ATTN_W, BF16), out(2 * KV_W, BF16), out(2 * KV_W, BF16), out(RNN_W, F32),
                out(RNN_W, BF16), out(POOL_W, BF16), out(N_BRANCH * d, BF16)]
        body = functools.partial(_inproj_full_body, rope=rope_tables is not None,
                                 tiles_per_seq=tiles_per_seq)
        h_rows = tm + 4 * SUBLANES
    return pl.pallas_call(
        body,
        grid=grid,
        in_specs=in_specs,
        out_specs=[o[0] for o in outs],
        out_shape=[o[1] for o in outs],
        scratch_shapes=[pltpu.VMEM((h_rows, d), BF16)],
        compiler_params=_params(1),
        name="inproj_kv" if kv_only else "inproj",
    )(*args)


def _attend(sink_ref, layer, items):
    heads_per_kv = N_Q_HEADS // N_KV_HEADS
    zero = jnp.zeros((), BF16)

    def scores(item, h):
        q_blk, k_rows, _, biases = item
        low = lax.broadcasted_iota(jnp.int32, (q_blk.shape[0], LANES), 1) < HEAD_DIM
        k_h = k_rows[:, h * LANES:(h + 1) * LANES]
        stacked = []
        for c2 in range(heads_per_kv // 2):
            qc = q_blk[:, (2 * h + c2) * LANES:(2 * h + c2 + 1) * LANES]
            stacked += [jnp.where(low, qc, zero), jnp.where(low, zero, qc)]
        lhs = jnp.concatenate(stacked, axis=0)
        s = lax.dot_general(lhs, k_h, (((1,), (1,)), ((), ())), preferred_element_type=F32)
        blocks = []
        for j, bias in enumerate(biases):
            blk = s[:, j * LANES:(j + 1) * LANES]
            if bias is not None:
                blk = blk + jnp.concatenate([bias] * heads_per_kv, axis=0)
            blocks.append(blk)
        return blocks

    def softmax(h, blocks):
        tq = blocks[0].shape[0] // heads_per_kv
        sink_col = jnp.concatenate(
            [jnp.full((tq, 1), sink_ref[layer, h * heads_per_kv + g] * LOG2E, F32)
             for g in range(heads_per_kv)],
            axis=0)
        m = jnp.maximum(jnp.max(functools.reduce(jnp.maximum, blocks), axis=-1, keepdims=True), sink_col)
        probs = [jnp.exp2(blk - m) for blk in blocks]
        denom = jnp.sum(functools.reduce(jnp.add, probs), axis=-1, keepdims=True) + jnp.exp2(sink_col - m)
        return jnp.concatenate(probs, axis=1).astype(BF16), denom

    def weighted(item, h, p, denom):
        tq = item[0].shape[0]
        low = lax.broadcasted_iota(jnp.int32, (tq, LANES), 1) < HEAD_DIM
        v_h = item[2][:, h * LANES:(h + 1) * LANES]
        o = jnp.dot(p, v_h, preferred_element_type=F32) * (1.0 / denom)
        return [jnp.where(low, o[(2 * c2) * tq:(2 * c2 + 1) * tq],
                          o[(2 * c2 + 1) * tq:(2 * c2 + 2) * tq]).astype(BF16)
                for c2 in range(heads_per_kv // 2)]

    units = [(item, h) for item in items for h in range(N_KV_HEADS)]
    all_scores = [scores(item, h) for item, h in units]
    all_probs = [softmax(h, blocks) for (_, h), blocks in zip(units, all_scores)]
    all_outs = [weighted(item, h, *pd) for (item, h), pd in zip(units, all_probs)]
    return [jnp.concatenate([o for outs in all_outs[i * N_KV_HEADS:(i + 1) * N_KV_HEADS] for o in outs],
                            axis=1) for i in range(len(items))]


def _attn_local_body(sink_ref, q_ref, kd_ref, vd_ref, kc_ref, vc_ref, o_ref, *, layer, n_blocks, group):
    blk = ATTN_BLOCK
    ii = lax.broadcasted_iota(jnp.int32, (blk, blk), 0)
    jj = lax.broadcasted_iota(jnp.int32, (blk, blk), 1)
    n_ctx_blocks = kc_ref.shape[0] // LANES
    rows = lambda i: pl.ds(pl.multiple_of(i * blk, blk), blk)

    def item(n):
        prv, nxt = jnp.maximum(n - 1, 0), jnp.minimum(n + 1, n_blocks - 1)
        bias_prev = jnp.where((jj >= ii) & (n > 0), 0.0, NEG_INF).astype(F32)
        bias_next = jnp.where((jj <= ii) & (n < n_blocks - 1), 0.0, NEG_INF).astype(F32)
        k_rows = jnp.concatenate([kd_ref[rows(prv), :], kd_ref[rows(n), :], kd_ref[rows(nxt), :],
                                  kc_ref[...]], axis=0)
        v_rows = jnp.concatenate([vd_ref[rows(prv), :], vd_ref[rows(n), :], vd_ref[rows(nxt), :],
                                  vc_ref[...]], axis=0)
        return (q_ref[rows(n), :], k_rows, v_rows, [bias_prev, None, bias_next] + [None] * n_ctx_blocks)

    def body(i, carry):
        blocks = [i * group + g for g in range(group)]
        for n, out in zip(blocks, _attend(sink_ref, layer, [item(n) for n in blocks])):
            o_ref[rows(n), :] = out
        return carry

    lax.fori_loop(0, n_blocks // group, body, 0)


def _attn_ctx_body(sink_ref, q_ref, kc_ref, vc_ref, o_ref, *, layer):
    item = (q_ref[...], kc_ref[...], vc_ref[...], [None] * (kc_ref.shape[0] // LANES))
    o_ref[...] = _attend(sink_ref, layer, [item])[0]


def _attention(sink, layer, q, kd, vd, kdc, vdc):
    b, s, _ = q.shape
    c = kdc.shape[1]
    kw = 2 * KV_W
    n_blocks = s // ATTN_BLOCK
    group = math.gcd(n_blocks, 4)
    idx = lambda bi: (bi, 0, 0)
    return pl.pallas_call(
        functools.partial(_attn_local_body, layer=layer, n_blocks=n_blocks, group=group),
        grid=(b,),
        in_specs=[
            pl.BlockSpec(memory_space=pltpu.SMEM),
            pl.BlockSpec((None, s, ATTN_W), idx),
            pl.BlockSpec((None, s, kw), idx), pl.BlockSpec((None, s, kw), idx),
            pl.BlockSpec((None, c, kw), idx), pl.BlockSpec((None, c, kw), idx),
        ],
        out_specs=pl.BlockSpec((None, s, ATTN_W), idx),
        out_shape=jax.ShapeDtypeStruct((b, s, ATTN_W), BF16),
        compiler_params=_params(1),
        name="attn_local",
    )(sink, q, kd, vd, kdc, vdc)


def _attention_ctx(sink, layer, qc, kdc, vdc):
    b, c, _ = qc.shape
    kw = 2 * KV_W
    idx = lambda bi: (bi, 0, 0)
    return pl.pallas_call(
        functools.partial(_attn_ctx_body, layer=layer),
        grid=(b,),
        in_specs=[
            pl.BlockSpec(memory_space=pltpu.SMEM),
            pl.BlockSpec((None, c, ATTN_W), idx),
            pl.BlockSpec((None, c, kw), idx), pl.BlockSpec((None, c, kw), idx),
        ],
        out_specs=pl.BlockSpec((None, c, ATTN_W), idx),
        out_shape=jax.ShapeDtypeStruct((b, c, ATTN_W), BF16),
        compiler_params=_params(1),
        name="attn_ctx",
    )(sink, qc, kdc, vdc)


def _pool_group(padded, t, length, window, w_mix, scale):
    halo = SUBLANES
    rows = padded.shape[0]
    n_rows = rows - 2 * halo
    acc = padded
    step = 1
    while step < window:
        acc = acc + pltpu.roll(acc, rows - step, 0)
        step *= 2
    back = (window - 1) // 2
    win = (pltpu.roll(acc, back, 0) if back else acc)[halo:halo + n_rows]
    lo = jnp.maximum(t - back, 0)
    hi = jnp.minimum(t + window // 2 + 1, length)
    mean = win / (hi - lo).astype(F32)
    dlt = (mean - padded[halo:halo + n_rows]).astype(BF16)
    return (jnp.dot(dlt, w_mix, preferred_element_type=F32) * scale).astype(BF16)


def _lru_body(*refs, n_ctx, n_lat, tile, with_ctx_out):
    if with_ctx_out:
        (rxc_ref, rxl_ref, ryc_ref, ryl_ref, cw_ref, cb_ref, wg_ref, ba_ref, bx_ref, lam_ref,
         ol_ref, oc_ref, pad_ref, xl_ref, hf_ref, hb_ref, xp_ref, a_ref, b_ref) = refs
    else:
        (rxc_ref, rxl_ref, ryl_ref, cw_ref, cb_ref, wg_ref, ba_ref, bx_ref, lam_ref,
         ol_ref, pad_ref, xl_ref, hf_ref, hb_ref, xp_ref, a_ref, b_ref) = refs
    halo = SUBLANES
    n_tot = n_ctx + n_lat
    n_slab = RNN_W // LANES

    zeros = jnp.zeros((halo, RNN_W), F32)
    c0, l0 = halo, 2 * halo + n_ctx
    pad_ref[0:halo, :] = zeros
    pad_ref[c0 + n_ctx:l0, :] = zeros
    pad_ref[l0 + n_lat:l0 + n_lat + halo, :] = zeros
    pad_ref[c0:c0 + n_ctx, :] = rxc_ref[...]
    pad_ref[l0:l0 + n_lat, :] = rxl_ref[...]

    cw = 0.5 * cw_ref[...]
    cb = 0.5 * cb_ref[...]

    def conv_tile(src0, dst_rows):
        rows = tile + 2 * halo
        p = pad_ref[src0 - halo:src0 + tile + halo, :]
        y = (cb + p[halo:halo + tile] * cw[1:2, :]
             + pltpu.roll(p, 1, 0)[halo:halo + tile] * cw[0:1, :]
             + pltpu.roll(p, rows - 1, 0)[halo:halo + tile] * cw[2:3, :]
             + pltpu.roll(p, rows - 2, 0)[halo:halo + tile] * cw[3:4, :])
        for s in range(n_slab):
            for d0 in dst_rows:
                xl_ref[s, d0:d0 + tile, :] = y[:, s * LANES:(s + 1) * LANES]

    for t0 in range(0, n_ctx, tile):
        conv_tile(c0 + t0, (t0, n_tot + t0))
    for t0 in range(0, n_lat, tile):
        conv_tile(l0 + t0, (n_ctx + t0,))

    lam = lam_ref[...]
    log_sig = jnp.minimum(lam, 0.0) - jnp.log1p(jnp.exp(-jnp.abs(lam)))
    rate = (0.5 * LOG2E * LRU_C) * log_sig
    half_ba = 0.5 * ba_ref[...]
    half_bx = 0.5 * bx_ref[...]
    half = RNN_W // 2

    def run_chunk(d, base, carry):
        res_ref = hf_ref if d == 0 else hb_ref
        win0 = 0 if d == 0 else n_ctx
        for v in range(SEG):
            for s in range(n_slab):
                xp_ref[d, v * SUBLANES:(v + 1) * SUBLANES, s * LANES:(s + 1) * LANES] = (
                    xl_ref[s, pl.ds(win0 + base + v, SUBLANES, stride=SEG), :])
        for j in range(2):
            ch = slice(j * half, (j + 1) * half)
            xj = xp_ref[d, :, ch]
            gates = jnp.dot(xj.astype(BF16), wg_ref[d, j], preferred_element_type=F32)
            t_r = jnp.tanh(gates[:, :half] + half_ba[d:d + 1, ch])
            t_i = jnp.tanh(gates[:, half:] + half_bx[d:d + 1, ch])
            a = jnp.exp2(t_r * rate[d:d + 1, ch] + rate[d:d + 1, ch])
            gap = 1.0 - a * a
            mult = gap * lax.rsqrt(jnp.maximum(gap, TINY))
            a_ref[d, :, ch] = a
            b_ref[d, :, ch] = (mult * xj) * (t_i + 1.0)
        order = range(SEG) if d == 0 else range(SEG - 1, -1, -1)
        h = jnp.zeros((SUBLANES, RNN_W), F32)
        acum = jnp.ones((SUBLANES, RNN_W), F32)
        for v in order:
            rows = slice(v * SUBLANES, (v + 1) * SUBLANES)
            av = a_ref[d, rows, :]
            h = av * h + b_ref[d, rows, :]
            acum = av * acum
            b_ref[d, rows, :] = h
            a_ref[d, rows, :] = acum
        seg_in = [None] * SUBLANES
        state = carry
        seg_order = range(SUBLANES) if d == 0 else range(SUBLANES - 1, -1, -1)
        for i in seg_order:
            seg_in[i] = state
            state = h[i:i + 1, :] + acum[i:i + 1, :] * state
        seg_state = jnp.concatenate(seg_in, axis=0)
        for v in range(SEG):
            rows = slice(v * SUBLANES, (v + 1) * SUBLANES)
            hv = b_ref[d, rows, :] + a_ref[d, rows, :] * seg_state
            for s in range(n_slab):
                res_ref[s, pl.ds(base + v, SUBLANES, stride=SEG), :] = hv[:, s * LANES:(s + 1) * LANES]
        return state

    n_chunks = n_tot // CHUNK

    def step(c, carries):
        cf, cb = carries
        cf = run_chunk(0, c * CHUNK, cf)
        cb = run_chunk(1, (n_chunks - 1 - c) * CHUNK, cb)
        return cf, cb

    zero_state = jnp.zeros((1, RNN_W), F32)
    lax.fori_loop(0, n_chunks, step, (zero_state, zero_state))

    def emit(out_ref, ry_ref, f0, b0, length):
        for t0 in range(0, length, tile):
            hsum = jnp.concatenate(
                [hf_ref[s, f0 + t0:f0 + t0 + tile, :] + hb_ref[s, b0 + t0:b0 + t0 + tile, :]
                 for s in range(n_slab)], axis=1)
            gate = ry_ref[t0:t0 + tile, :].astype(F32)
            out_ref[t0:t0 + tile, :] = (hsum * gate).astype(BF16)

    emit(ol_ref, ryl_ref, n_ctx, 0, n_lat)
    if with_ctx_out:
        emit(oc_ref, ryc_ref, 0, n_lat, n_ctx)


def _lru(layer, rxc, rxl, ryc, ryl, conv_w, conv_b, wg, b_a, b_x, lam):
    b, n_ctx, _ = rxc.shape
    n_lat = rxl.shape[1]
    n_tot = n_ctx + n_lat
    tile = math.gcd(n_ctx, n_lat, 256)
    assert n_tot % CHUNK == 0 and tile % SUBLANES == 0
    with_ctx_out = ryc is not None
    idx = lambda bi: (bi, 0, 0)
    seq = lambda n, dt=None: pl.BlockSpec((None, n, RNN_W), idx)
    n_slab = RNN_W // LANES
    in_specs = [seq(n_ctx), seq(n_lat)] + ([seq(n_ctx)] if with_ctx_out else []) + [seq(n_lat)]
    args = [rxc, rxl] + ([ryc] if with_ctx_out else []) + [ryl]
    for wgt in (conv_w, conv_b, wg, b_a, b_x, lam):
        in_specs.append(_layer_block(wgt, layer))
        args.append(wgt)
    out_specs = [seq(n_lat)] + ([seq(n_ctx)] if with_ctx_out else [])
    out_shape = [jax.ShapeDtypeStruct((b, n_lat, RNN_W), BF16)]
    if with_ctx_out:
        out_shape.append(jax.ShapeDtypeStruct((b, n_ctx, RNN_W), BF16))
    res = pl.pallas_call(
        functools.partial(_lru_body, n_ctx=n_ctx, n_lat=n_lat, tile=tile, with_ctx_out=with_ctx_out),
        grid=(b,),
        in_specs=in_specs,
        out_specs=out_specs,
        out_shape=out_shape,
        scratch_shapes=[
            pltpu.VMEM((n_tot + 3 * SUBLANES, RNN_W), F32),
            pltpu.VMEM((n_slab, n_tot + n_ctx, LANES), F32),
            pltpu.VMEM((n_slab, n_tot, LANES), F32),
            pltpu.VMEM((n_slab, n_tot, LANES), F32),
            pltpu.VMEM((2, CHUNK, RNN_W), F32),
            pltpu.VMEM((2, CHUNK, RNN_W), F32),
            pltpu.VMEM((2, CHUNK, RNN_W), F32),
        ],
        compiler_params=_params(1),
        name="rglru",
    )(*args)
    return (res[0], res[1]) if with_ctx_out else (res[0], None)


def _mix_ffn_body(x_ref, mod_ref, at_ref, po_ref, rn_ref, gt_ref, wa_ref, wp_ref, wr_ref, wo_ref,
                  gmix_ref, gpre_ref, wgu_ref, wd_ref, gpost_ref, o_ref, m_ref, h_ref, *, d_ff, chunk):
    d = x_ref.shape[1]
    for rows in _row_halves(x_ref.shape[0]):
        merged = None
        for k, (br_ref, w_ref) in enumerate(((at_ref, wa_ref), (po_ref, wp_ref), (rn_ref, wr_ref))):
            t = gt_ref[rows, k * d:(k + 1) * d].astype(F32) * jnp.dot(
                br_ref[rows, :], w_ref[...], preferred_element_type=F32)
            merged = t if merged is None else merged + t
        m_ref[rows, :] = merged.astype(BF16)
        mix = jnp.dot(m_ref[rows, :], wo_ref[...], preferred_element_type=F32)
        o_ref[rows, :] = x_ref[rows, :] + mod_ref[2:3, :] * (_rms(mix) * gmix_ref[...])
        _norm_modulate_store(o_ref, mod_ref, gpre_ref, h_ref, 3, 4, rows)

    halves = _row_halves(x_ref.shape[0])
    accs = [None] * len(halves)
    for c0 in range(0, d_ff, chunk):
        c1 = min(c0 + chunk, d_ff)
        for i, rows in enumerate(halves):
            gate = jnp.dot(h_ref[rows, :], wgu_ref[:, c0:c1], preferred_element_type=F32)
            up = jnp.dot(h_ref[rows, :], wgu_ref[:, d_ff + c0:d_ff + c1], preferred_element_type=F32)
            act = ((gate * jax.nn.sigmoid(gate)) * up).astype(BF16)
            part = jnp.dot(act, wd_ref[c0:c1, :], preferred_element_type=F32)
            accs[i] = part if accs[i] is None else accs[i] + part
    for acc, rows in zip(accs, halves):
        o_ref[rows, :] = o_ref[rows, :] + mod_ref[5:6, :] * (_rms(acc) * gpost_ref[...])


def _mix_ffn(x2d, mod_all, layer, mod_row_of_tile, attn, pool, rnn, gates, wa, wp, wr, wo, g_mix,
             g_pre, w_gu, w_down, g_post, *, tm):
    n_tok, d = x2d.shape
    d_ff = w_down.shape[1]
    row = lambda i: (i, 0)
    tok = lambda arr: pl.BlockSpec((tm, arr.shape[1]), row)
    params = (wa, wp, wr, wo, g_mix, g_pre, w_gu, w_down, g_post)
    return pl.pallas_call(
        functools.partial(_mix_ffn_body, d_ff=d_ff, chunk=1024),
        grid=(n_tok // tm,),
        in_specs=[tok(x2d), _mod_block(mod_all, layer, mod_row_of_tile),
                  tok(attn), tok(pool), tok(rnn), tok(gates)]
                 + [_layer_block(p, layer) for p in params],
        out_specs=pl.BlockSpec((tm, d), row),
        out_shape=jax.ShapeDtypeStruct((n_tok, d), F32),
        scratch_shapes=[pltpu.VMEM((tm, d), BF16), pltpu.VMEM((tm, d), BF16)],
        compiler_params=_params(1),
        name="mix_ffn",
    )(x2d, mod_all, attn, pool, rnn, gates, *params)


def _pack_gate_weights(w_a, w_x):
    def block_diag(w):
        bw = RNN_W // RNN_BLOCKS
        rows = [jnp.pad(w[:, :, h], ((0, 0), (0, 0), (0, 0), (bw * h, RNN_W - bw * (h + 1))))
                for h in range(RNN_BLOCKS)]
        return jnp.concatenate(rows, axis=2)
    bd_a, bd_x = block_diag(w_a), block_diag(w_x)
    half = RNN_W // 2
    halves = []
    for j in range(2):
        sl = slice(j * half, (j + 1) * half)
        halves.append(jnp.concatenate([bd_a[:, :, sl, sl], bd_x[:, :, sl, sl]], axis=3))
    return jnp.stack(halves, axis=2).astype(BF16)


def kernel(x, c, ctx, c_ctx, w_ada, b_ada, g_pre_mix, g_post_mix, g_pre_ffn, g_post_ffn, w_in, attn_sink, w_attn_o, pool_mix, pool_scale, w_pool_o, conv_w, conv_b, lru_w_a, lru_b_a, lru_w_x, lru_b_x, lru_lambda, w_rnn_o, w_out, w_gu, w_down):
    bsz, seq, d = x.shape
    n_ctx = ctx.shape[1]
    depth = w_ada.shape[0]
    tm = min(512, seq)
    tm_ctx = min(tm, n_ctx)

    mod_rows = -(-(bsz + 1) // SUBLANES) * SUBLANES
    cvec = jnp.zeros((mod_rows, d), F32).at[:bsz].set(c).at[bsz].set(c_ctx)
    mod_all = _adaln_mod(cvec, w_ada, b_ada).reshape(depth, mod_rows, 6, d)

    rope = _rope_tables(seq)
    lat_row = lambda i: i // (seq // tm)
    ctx_row = lambda i: bsz

    rowvec = lambda a: a.reshape(depth, 1, -1)
    w_in_b, w_gu_b, w_dn_b = w_in.astype(BF16), w_gu.astype(BF16), w_down.astype(BF16)
    wa, wp, wr, wo = (w.astype(BF16) for w in (w_attn_o, w_pool_o, w_rnn_o, w_out))
    w_mix = pool_mix.astype(BF16)
    wg = _pack_gate_weights(lru_w_a, lru_w_x)
    g_pre, g_post = rowvec(g_pre_mix), rowvec(g_post_mix)
    gf_pre, gf_post = rowvec(g_pre_ffn), rowvec(g_post_ffn)
    p_scale, cv_b = rowvec(pool_scale), rowvec(conv_b)

    x2 = x.reshape(bsz * seq, d)
    c2 = ctx.reshape(bsz * n_ctx, d)
    r3 = lambda a, n: a.reshape(bsz, n, a.shape[-1])
    flat = lambda a: a.reshape(-1, a.shape[-1])
    for l in range(depth):
        need_ctx = l < depth - 1
        q, kd, vd, rx, ry, pool_l, gt = _inproj(
            x2, mod_all, l, lat_row, g_pre, w_in_b, w_mix, p_scale, tm=tm, tiles_per_seq=seq // tm,
            rope_tables=rope, kv_only=False)
        if need_ctx:
            qc, kdc, vdc, rxc, ryc, pool_c, gtc = _inproj(
                c2, mod_all, l, ctx_row, g_pre, w_in_b, w_mix, p_scale, tm=tm_ctx,
                tiles_per_seq=n_ctx // tm_ctx, rope_tables=None, kv_only=False)
        else:
            kdc, vdc, rxc = _inproj(
                c2, mod_all, l, ctx_row, g_pre, w_in_b, None, None, tm=tm_ctx, tiles_per_seq=1,
                rope_tables=None, kv_only=True)
            ryc = None

        kdc3, vdc3 = r3(kdc, n_ctx), r3(vdc, n_ctx)
        attn_l = _attention(attn_sink, l, r3(q, seq), r3(kd, seq), r3(vd, seq), kdc3, vdc3)
        rnn_l, rnn_c = _lru(l, r3(rxc, n_ctx), r3(rx, seq), None if ryc is None else r3(ryc, n_ctx),
                            r3(ry, seq), conv_w, cv_b, wg, lru_b_a, lru_b_x, lru_lambda)

        x2 = _mix_ffn(x2, mod_all, l, lat_row, flat(attn_l), pool_l, flat(rnn_l), gt, wa, wp, wr, wo,
                      g_post, gf_pre, w_gu_b, w_dn_b, gf_post, tm=tm)

        if need_ctx:
            attn_c = _attention_ctx(attn_sink, l, r3(qc, n_ctx), kdc3, vdc3)
            c2 = _mix_ffn(c2, mod_all, l, ctx_row, flat(attn_c), pool_c, flat(rnn_c), gtc, wa, wp, wr,
                          wo, g_post, gf_pre, w_gu_b, w_dn_b, gf_post, tm=tm_ctx)
    return x2.reshape(bsz, seq, d)
```

```python
import functools
import math

import jax
import jax.numpy as jnp
import numpy as np
from jax import lax
from jax.experimental import pallas as pl
from jax.experimental.pallas import tpu as pltpu

F32 = jnp.float32
BF16 = jnp.bfloat16

GRID_W = 64
HEAD_DIM = 64
N_Q_HEADS = 8
N_KV_HEADS = 2
ATTN_W = N_Q_HEADS * HEAD_DIM
KV_W = N_KV_HEADS * HEAD_DIM
ATTN_BLOCK = 128
ROPE_BASE = 10000.0
POOL_WINDOWS = (2, 4, 8, 16)
POOL_W = 512
POOL_GW = POOL_W // len(POOL_WINDOWS)
RNN_W = 512
RNN_BLOCKS = 8
CONV_W = 4
LRU_C = 8.0
N_BRANCH = 3
EPS = 1e-6
NEG_INF = -1e30
LOG2E = 1.4426950408889634
TINY = 1e-30

LANES = 128
SUBLANES = 8
VMEM_LIMIT_BYTES = 56 * 1024 * 1024

SEG = 36
CHUNK = SUBLANES * SEG


def _params(n_axes):
    return pltpu.CompilerParams(
        dimension_semantics=("arbitrary",) * n_axes, vmem_limit_bytes=VMEM_LIMIT_BYTES)


def _layer_block(arr, layer, block=None):
    block = tuple(arr.shape[1:]) if block is None else block
    index = (layer,) + (0,) * len(block)
    return pl.BlockSpec((None,) + block, lambda *_: index, pipeline_mode=pl.Buffered(1))


def _mod_block(mod_all, layer, mod_row_of_tile):
    return pl.BlockSpec((None, None) + mod_all.shape[2:], lambda i: (layer, mod_row_of_tile(i), 0, 0))


def _rms(xf):
    return xf * lax.rsqrt(jnp.mean(xf * xf, axis=-1, keepdims=True) + EPS)


def _mod_body(c_ref, w_ref, b_ref, o_ref):
    c = c_ref[...]
    s = (c * jax.nn.sigmoid(c)).astype(BF16)
    o_ref[0] = jnp.dot(s, w_ref[0].astype(BF16), preferred_element_type=F32) + b_ref[0]


def _adaln_mod(cvec, w_ada, b_ada):
    n_layers, d, n6 = w_ada.shape
    rows = cvec.shape[0]
    tn = n6 // 4
    return pl.pallas_call(
        _mod_body,
        grid=(n_layers, n6 // tn),
        in_specs=[
            pl.BlockSpec((rows, d), lambda l, j: (0, 0)),
            pl.BlockSpec((1, d, tn), lambda l, j: (l, 0, j)),
            pl.BlockSpec((1, 1, tn), lambda l, j: (l, 0, j)),
        ],
        out_specs=pl.BlockSpec((1, rows, tn), lambda l, j: (l, 0, j)),
        out_shape=jax.ShapeDtypeStruct((n_layers, rows, n6), F32),
        compiler_params=_params(2),
        name="adaln_mod",
    )(cvec, w_ada, b_ada.reshape(n_layers, 1, n6))


def _rope_tables(seq_len):
    pos = np.arange(seq_len)
    row = (pos // GRID_W).astype(np.float32)
    col = (pos % GRID_W).astype(np.float32)
    half = HEAD_DIM // 2
    quarter = half // 2
    inv = (np.float32(ROPE_BASE) ** (-(np.arange(quarter, dtype=np.float32) * np.float32(2.0 / half)))
           ).astype(np.float32)
    j = np.arange(LANES) % HEAD_DIM
    is_col = j >= half
    second = (j % half) >= quarter
    freq = inv[j % quarter]
    ang = (np.where(is_col[None, :], col[:, None], row[:, None]) * freq[None, :]).astype(np.float32)
    cos, sin = np.cos(ang), np.sin(ang)
    sin_prev = np.where(second[None, :], sin, 0.0)
    sin_next = np.where(second[None, :], 0.0, -sin)
    return tuple(jnp.asarray(t, F32) for t in (cos, sin_prev, sin_next))


def _row_halves(n_rows):
    if n_rows % (2 * SUBLANES * 2) or n_rows < 256:
        return [slice(0, n_rows)]
    return [slice(0, n_rows // 2), slice(n_rows // 2, n_rows)]


def _store_dup_heads(dst_ref, rows, kv):
    low = lax.broadcasted_iota(jnp.int32, kv.shape, 1) < HEAD_DIM
    swapped = pltpu.roll(kv, HEAD_DIM, 1)
    dst_ref[rows, 0:LANES] = jnp.where(low, kv, swapped).astype(BF16)
    dst_ref[rows, LANES:2 * LANES] = jnp.where(low, swapped, kv).astype(BF16)


def _norm_modulate_store(x_ref, mod_ref, g_ref, h_ref, shift_row, scale_row, rows):
    y = _rms(x_ref[rows, :]) * g_ref[...]
    h = y * (1.0 + mod_ref[scale_row:scale_row + 1, :]) + mod_ref[shift_row:shift_row + 1, :]
    h_ref[rows, :] = h.astype(BF16)


def _inproj_full_body(*refs, rope, tiles_per_seq):
    if rope:
        (x_ref, xp_ref, xn_ref, mod_ref, g_ref, w_ref, wmix_ref, psc_ref, cos_ref, sp_ref, sn_ref,
         q_ref, kd_ref, vd_ref, rx_ref, ry_ref, po_ref, gt_ref, h_ref) = refs
    else:
        (x_ref, xp_ref, xn_ref, mod_ref, g_ref, w_ref, wmix_ref, psc_ref,
         q_ref, kd_ref, vd_ref, rx_ref, ry_ref, po_ref, gt_ref, h_ref) = refs
    quarter = HEAD_DIM // 4
    tm, d = x_ref.shape
    halo = SUBLANES
    tile_in_seq = pl.program_id(0) % tiles_per_seq
    pu0 = ATTN_W + 2 * KV_W + 2 * RNN_W

    def modulated(xf):
        y = _rms(xf) * g_ref[...]
        return y * (1.0 + mod_ref[1:2, :]) + mod_ref[0:1, :]

    pack = 2 * SUBLANES
    zeros = jnp.zeros((SUBLANES, d), F32)
    before = jnp.where(tile_in_seq == 0, 0.0, modulated(xp_ref[...]))
    after = jnp.where(tile_in_seq == tiles_per_seq - 1, 0.0, modulated(xn_ref[...]))
    h_ref[0:pack, :] = jnp.concatenate([zeros, before], axis=0).astype(BF16)
    h_ref[pack + tm:2 * pack + tm, :] = jnp.concatenate([after, zeros], axis=0).astype(BF16)
    for rows in _row_halves(tm):
        h_ref[pack + rows.start:pack + rows.stop, :] = modulated(x_ref[rows, :]).astype(BF16)

    for rows in _row_halves(tm):
        r0, n_rows = rows.start, rows.stop - rows.start
        hrows = slice(pack + r0, pack + rows.stop)

        def proj(c0, c1):
            return jnp.dot(h_ref[hrows, :], w_ref[:, c0:c1], preferred_element_type=F32)

        def rotary(pc):
            if not rope:
                return pc
            return (pc * cos_ref[rows, :]
                    + pltpu.roll(pc, quarter, 1) * sp_ref[rows, :]
                    + pltpu.roll(pc, LANES - quarter, 1) * sn_ref[rows, :])

        padded = jnp.dot(h_ref[r0:r0 + n_rows + 2 * pack, :], w_ref[:, pu0:pu0 + POOL_W],
                         preferred_element_type=F32)[pack - halo:pack + n_rows + halo]
        t = tile_in_seq * tm + r0 + lax.broadcasted_iota(jnp.int32, (n_rows, POOL_GW), 0)

        def pool_group(gi):
            lanes = slice(gi * POOL_GW, (gi + 1) * POOL_GW)
            po_ref[rows, lanes] = _pool_group(padded[:, lanes], t, tiles_per_seq * tm, POOL_WINDOWS[gi],
                                              wmix_ref[gi], psc_ref[:, lanes])

        for c in range(ATTN_W // (2 * LANES)):
            pair = proj(c * 2 * LANES, (c + 1) * 2 * LANES)
            for half in range(2):
                col = (2 * c + half) * LANES
                pc = rotary(pair[:, half * LANES:(half + 1) * LANES])
                q_ref[rows, col:col + LANES] = (pc * (HEAD_DIM ** -0.5 * LOG2E)).astype(BF16)
        o = ATTN_W
        kv = proj(o, o + 2 * KV_W)
        _store_dup_heads(kd_ref, rows, rotary(kv[:, :KV_W]))
        _store_dup_heads(vd_ref, rows, kv[:, KV_W:])
        o += 2 * KV_W
        rx_ref[rows, :] = proj(o, o + RNN_W)
        o += RNN_W
        ry_ref[rows, :] = jax.nn.gelu(proj(o, o + RNN_W)).astype(BF16)
        o += RNN_W + POOL_W
        pool_group(0)
        for c in range(N_BRANCH):
            gl = proj(o + c * d, o + (c + 1) * d)
            gt_ref[rows, c * d:(c + 1) * d] = jax.nn.sigmoid(gl).astype(BF16)
            pool_group(c + 1)


def _inproj_kv_body(x_ref, mod_ref, g_ref, w_ref, kd_ref, vd_ref, rx_ref, h_ref):
    rows = slice(0, x_ref.shape[0])
    _norm_modulate_store(x_ref, mod_ref, g_ref, h_ref, 0, 1, rows)
    kv = jnp.dot(h_ref[...], w_ref[:, ATTN_W:ATTN_W + 2 * KV_W], preferred_element_type=F32)
    _store_dup_heads(kd_ref, rows, kv[:, :KV_W])
    _store_dup_heads(vd_ref, rows, kv[:, KV_W:])
    rx_ref[...] = jnp.dot(h_ref[...], w_ref[:, ATTN_W + 2 * KV_W:ATTN_W + 2 * KV_W + RNN_W],
                          preferred_element_type=F32)


def _inproj(x2d, mod_all, layer, mod_row_of_tile, g, w, w_mix, p_scale, *, tm, tiles_per_seq, rope_tables,
            kv_only):
    n_tok, d = x2d.shape
    grid = (n_tok // tm,)
    row = lambda i: (i, 0)
    in_specs = [pl.BlockSpec((tm, d), row)]
    args = [x2d]
    if not kv_only:
        per_tile = tm // SUBLANES
        in_specs += [pl.BlockSpec((SUBLANES, d), lambda i: (jnp.maximum(i * per_tile - 1, 0), 0)),
                     pl.BlockSpec((SUBLANES, d),
                                  lambda i: (jnp.minimum((i + 1) * per_tile, n_tok // SUBLANES - 1), 0))]
        args += [x2d, x2d]
    w_cols = ATTN_W + 2 * KV_W + RNN_W if kv_only else w.shape[2]
    in_specs += [_mod_block(mod_all, layer, mod_row_of_tile), _layer_block(g, layer),
                 _layer_block(w, layer, (d, w_cols))]
    args += [mod_all, g, w]
    if not kv_only:
        in_specs += [_layer_block(w_mix, layer), _layer_block(p_scale, layer)]
        args += [w_mix, p_scale]
    if rope_tables is not None:
        in_specs += [pl.BlockSpec((tm, LANES), lambda i: (i % tiles_per_seq, 0))] * 3
        args += list(rope_tables)

    def out(width, dtype):
        return pl.BlockSpec((tm, width), row), jax.ShapeDtypeStruct((n_tok, width), dtype)

    if kv_only:
        outs = [out(2 * KV_W, BF16), out(2 * KV_W, BF16), out(RNN_W, F32)]
        body = _inproj_kv_body
        h_rows = tm
    else:
        outs = [out(ATTN_W, BF16), out(2 * KV_W, BF16), out(2 * KV_W, BF16), out(RNN_W, F32),
                out(RNN_W, BF16), out(POOL_W, BF16), out(N_BRANCH * d, BF16)]
        body = functools.partial(_inproj_full_body, rope=rope_tables is not None,
                                 tiles_per_seq=tiles_per_seq)
        h_rows = tm + 4 * SUBLANES
    return pl.pallas_call(
        body,
        grid=grid,
        in_specs=in_specs,
        out_specs=[o[0] for o in outs],
        out_shape=[o[1] for o in outs],
        scratch_shapes=[pltpu.VMEM((h_rows, d), BF16)],
        compiler_params=_params(1),
        name="inproj_kv" if kv_only else "inproj",
    )(*args)


def _transpose_values(v_ref, vt_ref, first_block):
    for j in range(v_ref.shape[0] // LANES):
        blk = v_ref[j * LANES:(j + 1) * LANES, :].astype(F32)
        vt_ref[first_block + j] = blk.T.astype(BF16)


def _attend(sink_ref, layer, items):
    heads_per_kv = N_Q_HEADS // N_KV_HEADS
    zero = jnp.zeros((), BF16)

    def scores(item, h):
        q_blk, k_rows, _, biases = item
        low = lax.broadcasted_iota(jnp.int32, (q_blk.shape[0], LANES), 1) < HEAD_DIM
        k_h = k_rows[:, h * LANES:(h + 1) * LANES]
        stacked = []
        for c2 in range(heads_per_kv // 2):
            qc = q_blk[:, (2 * h + c2) * LANES:(2 * h + c2 + 1) * LANES]
            stacked += [jnp.where(low, qc, zero), jnp.where(low, zero, qc)]
        q_rows = jnp.concatenate(stacked, axis=0)
        s = lax.dot_general(k_h, q_rows, (((1,), (1,)), ((), ())), preferred_element_type=F32)
        blocks = []
        for j, bias in enumerate(biases):
            blk = s[j * LANES:(j + 1) * LANES, :]
            if bias is not None:
                blk = blk + jnp.concatenate([bias] * heads_per_kv, axis=1)
            blocks.append(blk)
        return blocks

    def softmax(h, blocks):
        tq = blocks[0].shape[1] // heads_per_kv
        sink_row = jnp.concatenate(
            [jnp.full((1, tq), sink_ref[layer, h * heads_per_kv + g] * LOG2E, F32)
             for g in range(heads_per_kv)],
            axis=1)
        m = jnp.maximum(jnp.max(functools.reduce(jnp.maximum, blocks), axis=0, keepdims=True), sink_row)
        probs = [jnp.exp2(blk - m) for blk in blocks]
        denom = jnp.sum(functools.reduce(jnp.add, probs), axis=0, keepdims=True) + jnp.exp2(sink_row - m)
        return jnp.concatenate(probs, axis=0).astype(BF16), denom

    def weighted(item, h, p, denom):
        tq = item[0].shape[0]
        vt_h = item[2][h * LANES:(h + 1) * LANES, :]
        o = jnp.dot(vt_h, p, preferred_element_type=F32) * (1.0 / denom)
        first_copy = lax.broadcasted_iota(jnp.int32, (LANES, tq), 0) < HEAD_DIM
        return [jnp.where(first_copy, o[:, (2 * c2) * tq:(2 * c2 + 1) * tq],
                          o[:, (2 * c2 + 1) * tq:(2 * c2 + 2) * tq]).T.astype(BF16)
                for c2 in range(heads_per_kv // 2)]

    units = [(item, h) for item in items for h in range(N_KV_HEADS)]
    all_scores = [scores(item, h) for item, h in units]
    all_probs = [softmax(h, blocks) for (_, h), blocks in zip(units, all_scores)]
    all_outs = [weighted(item, h, *pd) for (item, h), pd in zip(units, all_probs)]
    return [jnp.concatenate([o for outs in all_outs[i * N_KV_HEADS:(i + 1) * N_KV_HEADS] for o in outs],
                            axis=1) for i in range(len(items))]


def _attn_local_body(sink_ref, q_ref, kd_ref, vd_ref, kc_ref, vc_ref, o_ref, vt_ref, *, layer, n_blocks,
                     group):
    blk = ATTN_BLOCK
    n_ctx_blocks = kc_ref.shape[0] // LANES
    _transpose_values(vd_ref, vt_ref, 0)
    _transpose_values(vc_ref, vt_ref, n_blocks)
    key = lax.broadcasted_iota(jnp.int32, (blk, blk), 0)
    qry = lax.broadcasted_iota(jnp.int32, (blk, blk), 1)
    rows = lambda i: pl.ds(pl.multiple_of(i * blk, blk), blk)

    def item(n):
        prv, nxt = jnp.maximum(n - 1, 0), jnp.minimum(n + 1, n_blocks - 1)
        bias_prev = jnp.where((key >= qry) & (n > 0), 0.0, NEG_INF).astype(F32)
        bias_next = jnp.where((key <= qry) & (n < n_blocks - 1), 0.0, NEG_INF).astype(F32)
        k_rows = jnp.concatenate([kd_ref[rows(prv), :], kd_ref[rows(n), :], kd_ref[rows(nxt), :],
                                  kc_ref[...]], axis=0)
        vt = jnp.concatenate([vt_ref[prv], vt_ref[n], vt_ref[nxt]]
                             + [vt_ref[n_blocks + j] for j in range(n_ctx_blocks)], axis=1)
        return (q_ref[rows(n), :], k_rows, vt, [bias_prev, None, bias_next] + [None] * n_ctx_blocks)

    def body(i, carry):
        blocks = [i * group + g for g in range(group)]
        for n, out in zip(blocks, _attend(sink_ref, layer, [item(n) for n in blocks])):
            o_ref[rows(n), :] = out
        return carry

    lax.fori_loop(0, n_blocks // group, body, 0)


def _attn_ctx_body(sink_ref, q_ref, kc_ref, vc_ref, o_ref, vt_ref, *, layer):
    n_key_blocks = kc_ref.shape[0] // LANES
    _transpose_values(vc_ref, vt_ref, 0)
    vt = jnp.concatenate([vt_ref[j] for j in range(n_key_blocks)], axis=1)
    items = [(q_ref[i * ATTN_BLOCK:(i + 1) * ATTN_BLOCK, :], kc_ref[...], vt, [None] * n_key_blocks)
             for i in range(q_ref.shape[0] // ATTN_BLOCK)]
    for i, out in enumerate(_attend(sink_ref, layer, items)):
        o_ref[i * ATTN_BLOCK:(i + 1) * ATTN_BLOCK, :] = out


def _attention(sink, layer, q, kd, vd, kdc, vdc):
    b, s, _ = q.shape
    c = kdc.shape[1]
    kw = 2 * KV_W
    n_blocks = s // ATTN_BLOCK
    group = math.gcd(n_blocks, 4)
    idx = lambda bi: (bi, 0, 0)
    return pl.pallas_call(
        functools.partial(_attn_local_body, layer=layer, n_blocks=n_blocks, group=group),
        grid=(b,),
        in_specs=[
            pl.BlockSpec(memory_space=pltpu.SMEM),
            pl.BlockSpec((None, s, ATTN_W), idx),
            pl.BlockSpec((None, s, kw), idx), pl.BlockSpec((None, s, kw), idx),
            pl.BlockSpec((None, c, kw), idx), pl.BlockSpec((None, c, kw), idx),
        ],
        out_specs=pl.BlockSpec((None, s, ATTN_W), idx),
        out_shape=jax.ShapeDtypeStruct((b, s, ATTN_W), BF16),
        scratch_shapes=[pltpu.VMEM((n_blocks + c // LANES, kw, LANES), BF16)],
        compiler_params=_params(1),
        name="attn_local",
    )(sink, q, kd, vd, kdc, vdc)


def _attention_ctx(sink, layer, qc, kdc, vdc):
    b, c, _ = qc.shape
    kw = 2 * KV_W
    idx = lambda bi: (bi, 0, 0)
    return pl.pallas_call(
        functools.partial(_attn_ctx_body, layer=layer),
        grid=(b,),
        in_specs=[
            pl.BlockSpec(memory_space=pltpu.SMEM),
            pl.BlockSpec((None, c, ATTN_W), idx),
            pl.BlockSpec((None, c, kw), idx), pl.BlockSpec((None, c, kw), idx),
        ],
        out_specs=pl.BlockSpec((None, c, ATTN_W), idx),
        out_shape=jax.ShapeDtypeStruct((b, c, ATTN_W), BF16),
        scratch_shapes=[pltpu.VMEM((c // LANES, kw, LANES), BF16)],
        compiler_params=_params(1),
        name="attn_ctx",
    )(sink, qc, kdc, vdc)


def _pool_group(padded, t, length, window, w_mix, scale):
    halo = SUBLANES
    rows = padded.shape[0]
    n_rows = rows - 2 * halo
    acc = padded
    step = 1
    while step < window:
        acc = acc + pltpu.roll(acc, rows - step, 0)
        step *= 2
    back = (window - 1) // 2
    win = (pltpu.roll(acc, back, 0) if back else acc)[halo:halo + n_rows]
    lo = jnp.maximum(t - back, 0)
    hi = jnp.minimum(t + window // 2 + 1, length)
    mean = win / (hi - lo).astype(F32)
    dlt = (mean - padded[halo:halo + n_rows]).astype(BF16)
    return (jnp.dot(dlt, w_mix, preferred_element_type=F32) * scale).astype(BF16)


def _lru_body(*refs, n_ctx, n_lat, tile, with_ctx_out):
    if with_ctx_out:
        (rxc_ref, rxl_ref, ryc_ref, ryl_ref, cw_ref, cb_ref, wg_ref, ba_ref, bx_ref, lam_ref,
         ol_ref, oc_ref, pad_ref, xl_ref, hf_ref, hb_ref, xp_ref, a_ref, b_ref) = refs
    else:
        (rxc_ref, rxl_ref, ryl_ref, cw_ref, cb_ref, wg_ref, ba_ref, bx_ref, lam_ref,
         ol_ref, pad_ref, xl_ref, hf_ref, hb_ref, xp_ref, a_ref, b_ref) = refs
    halo = SUBLANES
    n_tot = n_ctx + n_lat
    n_slab = RNN_W // LANES

    zeros = jnp.zeros((halo, RNN_W), F32)
    c0, l0 = halo, 2 * halo + n_ctx
    pad_ref[0:halo, :] = zeros
    pad_ref[c0 + n_ctx:l0, :] = zeros
    pad_ref[l0 + n_lat:l0 + n_lat + halo, :] = zeros
    pad_ref[c0:c0 + n_ctx, :] = rxc_ref[...]
    pad_ref[l0:l0 + n_lat, :] = rxl_ref[...]

    cw = 0.5 * cw_ref[...]
    cb = 0.5 * cb_ref[...]

    def conv_tile(src0, dst_rows):
        rows = tile + 2 * halo
        p = pad_ref[src0 - halo:src0 + tile + halo, :]
        y = (cb + p[halo:halo + tile] * cw[1:2, :]
             + pltpu.roll(p, 1, 0)[halo:halo + tile] * cw[0:1, :]
             + pltpu.roll(p, rows - 1, 0)[halo:halo + tile] * cw[2:3, :]
             + pltpu.roll(p, rows - 2, 0)[halo:halo + tile] * cw[3:4, :])
        for s in range(n_slab):
            for d0 in dst_rows:
                xl_ref[s, d0:d0 + tile, :] = y[:, s * LANES:(s + 1) * LANES]

    for t0 in range(0, n_ctx, tile):
        conv_tile(c0 + t0, (t0, n_tot + t0))
    for t0 in range(0, n_lat, tile):
        conv_tile(l0 + t0, (n_ctx + t0,))

    lam = lam_ref[...]
    log_sig = jnp.minimum(lam, 0.0) - jnp.log1p(jnp.exp(-jnp.abs(lam)))
    rate = (0.5 * LOG2E * LRU_C) * log_sig
    half_ba = 0.5 * ba_ref[...]
    half_bx = 0.5 * bx_ref[...]
    half = RNN_W // 2

    def run_chunk(d, base, carry):
        res_ref = hf_ref if d == 0 else hb_ref
        win0 = 0 if d == 0 else n_ctx
        for v in range(SEG):
            for s in range(n_slab):
                xp_ref[d, v * SUBLANES:(v + 1) * SUBLANES, s * LANES:(s + 1) * LANES] = (
                    xl_ref[s, pl.ds(win0 + base + v, SUBLANES, stride=SEG), :])
        for j in range(2):
            ch = slice(j * half, (j + 1) * half)
            xj = xp_ref[d, :, ch]
            gates = jnp.dot(xj.astype(BF16), wg_ref[d, j], preferred_element_type=F32)
            t_r = jnp.tanh(gates[:, :half] + half_ba[d:d + 1, ch])
            t_i = jnp.tanh(gates[:, half:] + half_bx[d:d + 1, ch])
            a = jnp.exp2(t_r * rate[d:d + 1, ch] + rate[d:d + 1, ch])
            gap = 1.0 - a * a
            mult = gap * lax.rsqrt(jnp.maximum(gap, TINY))
            a_ref[d, :, ch] = a
            b_ref[d, :, ch] = (mult * xj) * (t_i + 1.0)
        order = range(SEG) if d == 0 else range(SEG - 1, -1, -1)
        h = jnp.zeros((SUBLANES, RNN_W), F32)
        acum = jnp.ones((SUBLANES, RNN_W), F32)
        for v in order:
            rows = slice(v * SUBLANES, (v + 1) * SUBLANES)
            av = a_ref[d, rows, :]
            h = av * h + b_ref[d, rows, :]
            acum = av * acum
            b_ref[d, rows, :] = h
            a_ref[d, rows, :] = acum
        seg_in = [None] * SUBLANES
        state = carry
        seg_order = range(SUBLANES) if d == 0 else range(SUBLANES - 1, -1, -1)
        for i in seg_order:
            seg_in[i] = state
            state = h[i:i + 1, :] + acum[i:i + 1, :] * state
        seg_state = jnp.concatenate(seg_in, axis=0)
        for v in range(SEG):
            rows = slice(v * SUBLANES, (v + 1) * SUBLANES)
            hv = b_ref[d, rows, :] + a_ref[d, rows, :] * seg_state
            for s in range(n_slab):
                res_ref[s, pl.ds(base + v, SUBLANES, stride=SEG), :] = hv[:, s * LANES:(s + 1) * LANES]
        return state

    n_chunks = n_tot // CHUNK

    def step(c, carries):
        cf, cb = carries
        cf = run_chunk(0, c * CHUNK, cf)
        cb = run_chunk(1, (n_chunks - 1 - c) * CHUNK, cb)
        return cf, cb

    zero_state = jnp.zeros((1, RNN_W), F32)
    lax.fori_loop(0, n_chunks, step, (zero_state, zero_state))

    def emit(out_ref, ry_ref, f0, b0, length):
        for t0 in range(0, length, tile):
            hsum = jnp.concatenate(
                [hf_ref[s, f0 + t0:f0 + t0 + tile, :] + hb_ref[s, b0 + t0:b0 + t0 + tile, :]
                 for s in range(n_slab)], axis=1)
            gate = ry_ref[t0:t0 + tile, :].astype(F32)
            out_ref[t0:t0 + tile, :] = (hsum * gate).astype(BF16)

    emit(ol_ref, ryl_ref, n_ctx, 0, n_lat)
    if with_ctx_out:
        emit(oc_ref, ryc_ref, 0, n_lat, n_ctx)


def _lru(layer, rxc, rxl, ryc, ryl, conv_w, conv_b, wg, b_a, b_x, lam):
    b, n_ctx, _ = rxc.shape
    n_lat = rxl.shape[1]
    n_tot = n_ctx + n_lat
    tile = math.gcd(n_ctx, n_lat, 256)
    assert n_tot % CHUNK == 0 and tile % SUBLANES == 0
    with_ctx_out = ryc is not None
    idx = lambda bi: (bi, 0, 0)
    seq = lambda n, dt=None: pl.BlockSpec((None, n, RNN_W), idx)
    n_slab = RNN_W // LANES
    in_specs = [seq(n_ctx), seq(n_lat)] + ([seq(n_ctx)] if with_ctx_out else []) + [seq(n_lat)]
    args = [rxc, rxl] + ([ryc] if with_ctx_out else []) + [ryl]
    for wgt in (conv_w, conv_b, wg, b_a, b_x, lam):
        in_specs.append(_layer_block(wgt, layer))
        args.append(wgt)
    out_specs = [seq(n_lat)] + ([seq(n_ctx)] if with_ctx_out else [])
    out_shape = [jax.ShapeDtypeStruct((b, n_lat, RNN_W), BF16)]
    if with_ctx_out:
        out_shape.append(jax.ShapeDtypeStruct((b, n_ctx, RNN_W), BF16))
    res = pl.pallas_call(
        functools.partial(_lru_body, n_ctx=n_ctx, n_lat=n_lat, tile=tile, with_ctx_out=with_ctx_out),
        grid=(b,),
        in_specs=in_specs,
        out_specs=out_specs,
        out_shape=out_shape,
        scratch_shapes=[
            pltpu.VMEM((n_tot + 3 * SUBLANES, RNN_W), F32),
            pltpu.VMEM((n_slab, n_tot + n_ctx, LANES), F32),
            pltpu.VMEM((n_slab, n_tot, LANES), F32),
            pltpu.VMEM((n_slab, n_tot, LANES), F32),
            pltpu.VMEM((2, CHUNK, RNN_W), F32),
            pltpu.VMEM((2, CHUNK, RNN_W), F32),
            pltpu.VMEM((2, CHUNK, RNN_W), F32),
        ],
        compiler_params=_params(1),
        name="rglru",
    )(*args)
    return (res[0], res[1]) if with_ctx_out else (res[0], None)


def _mix_ffn_body(x_ref, mod_ref, at_ref, po_ref, rn_ref, gt_ref, wa_ref, wp_ref, wr_ref, wo_ref,
                  gmix_ref, gpre_ref, wgu_ref, wd_ref, gpost_ref, o_ref, m_ref, h_ref, *, d_ff, chunk):
    d = x_ref.shape[1]
    for rows in _row_halves(x_ref.shape[0]):
        merged = None
        for k, (br_ref, w_ref) in enumerate(((at_ref, wa_ref), (po_ref, wp_ref), (rn_ref, wr_ref))):
            t = gt_ref[rows, k * d:(k + 1) * d].astype(F32) * jnp.dot(
                br_ref[rows, :], w_ref[...], preferred_element_type=F32)
            merged = t if merged is None else merged + t
        m_ref[rows, :] = merged.astype(BF16)
        mix = jnp.dot(m_ref[rows, :], wo_ref[...], preferred_element_type=F32)
        o_ref[rows, :] = x_ref[rows, :] + mod_ref[2:3, :] * (_rms(mix) * gmix_ref[...])
        _norm_modulate_store(o_ref, mod_ref, gpre_ref, h_ref, 3, 4, rows)

    halves = _row_halves(x_ref.shape[0])
    accs = [None] * len(halves)
    for c0 in range(0, d_ff, chunk):
        c1 = min(c0 + chunk, d_ff)
        for i, rows in enumerate(halves):
            gate = jnp.dot(h_ref[rows, :], wgu_ref[:, c0:c1], preferred_element_type=F32)
            up = jnp.dot(h_ref[rows, :], wgu_ref[:, d_ff + c0:d_ff + c1], preferred_element_type=F32)
            act = ((gate * jax.nn.sigmoid(gate)) * up).astype(BF16)
            part = jnp.dot(act, wd_ref[c0:c1, :], preferred_element_type=F32)
            accs[i] = part if accs[i] is None else accs[i] + part
    for acc, rows in zip(accs, halves):
        o_ref[rows, :] = o_ref[rows, :] + mod_ref[5:6, :] * (_rms(acc) * gpost_ref[...])


def _mix_ffn(x2d, mod_all, layer, mod_row_of_tile, attn, pool, rnn, gates, wa, wp, wr, wo, g_mix,
             g_pre, w_gu, w_down, g_post, *, tm):
    n_tok, d = x2d.shape
    d_ff = w_down.shape[1]
    row = lambda i: (i, 0)
    tok = lambda arr: pl.BlockSpec((tm, arr.shape[1]), row)
    params = (wa, wp, wr, wo, g_mix, g_pre, w_gu, w_down, g_post)
    return pl.pallas_call(
        functools.partial(_mix_ffn_body, d_ff=d_ff, chunk=1024),
        grid=(n_tok // tm,),
        in_specs=[tok(x2d), _mod_block(mod_all, layer, mod_row_of_tile),
                  tok(attn), tok(pool), tok(rnn), tok(gates)]
                 + [_layer_block(p, layer) for p in params],
        out_specs=pl.BlockSpec((tm, d), row),
        out_shape=jax.ShapeDtypeStruct((n_tok, d), F32),
        scratch_shapes=[pltpu.VMEM((tm, d), BF16), pltpu.VMEM((tm, d), BF16)],
        compiler_params=_params(1),
        name="mix_ffn",
    )(x2d, mod_all, attn, pool, rnn, gates, *params)


def _pack_gate_weights(w_a, w_x):
    def block_diag(w):
        bw = RNN_W // RNN_BLOCKS
        rows = [jnp.pad(w[:, :, h], ((0, 0), (0, 0), (0, 0), (bw * h, RNN_W - bw * (h + 1))))
                for h in range(RNN_BLOCKS)]
        return jnp.concatenate(rows, axis=2)
    bd_a, bd_x = block_diag(w_a), block_diag(w_x)
    half = RNN_W // 2
    halves = []
    for j in range(2):
        sl = slice(j * half, (j + 1) * half)
        halves.append(jnp.concatenate([bd_a[:, :, sl, sl], bd_x[:, :, sl, sl]], axis=3))
    return jnp.stack(halves, axis=2).astype(BF16)


def kernel(x, c, ctx, c_ctx, w_ada, b_ada, g_pre_mix, g_post_mix, g_pre_ffn, g_post_ffn, w_in, attn_sink, w_attn_o, pool_mix, pool_scale, w_pool_o, conv_w, conv_b, lru_w_a, lru_b_a, lru_w_x, lru_b_x, lru_lambda, w_rnn_o, w_out, w_gu, w_down):
    bsz, seq, d = x.shape
    n_ctx = ctx.shape[1]
    depth = w_ada.shape[0]
    tm = min(512, seq)
    tm_ctx = min(tm, n_ctx)

    mod_rows = -(-(bsz + 1) // SUBLANES) * SUBLANES
    cvec = jnp.zeros((mod_rows, d), F32).at[:bsz].set(c).at[bsz].set(c_ctx)
    mod_all = _adaln_mod(cvec, w_ada, b_ada).reshape(depth, mod_rows, 6, d)

    rope = _rope_tables(seq)
    lat_row = lambda i: i // (seq // tm)
    ctx_row = lambda i: bsz

    rowvec = lambda a: a.reshape(depth, 1, -1)
    w_in_b, w_gu_b, w_dn_b = w_in.astype(BF16), w_gu.astype(BF16), w_down.astype(BF16)
    wa, wp, wr, wo = (w.astype(BF16) for w in (w_attn_o, w_pool_o, w_rnn_o, w_out))
    w_mix = pool_mix.astype(BF16)
    wg = _pack_gate_weights(lru_w_a, lru_w_x)
    g_pre, g_post = rowvec(g_pre_mix), rowvec(g_post_mix)
    gf_pre, gf_post = rowvec(g_pre_ffn), rowvec(g_post_ffn)
    p_scale, cv_b = rowvec(pool_scale), rowvec(conv_b)

    x2 = x.reshape(bsz * seq, d)
    c2 = ctx.reshape(bsz * n_ctx, d)
    r3 = lambda a, n: a.reshape(bsz, n, a.shape[-1])
    flat = lambda a: a.reshape(-1, a.shape[-1])
    for l in range(depth):
        need_ctx = l < depth - 1
        q, kd, vd, rx, ry, pool_l, gt = _inproj(
            x2, mod_all, l, lat_row, g_pre, w_in_b, w_mix, p_scale, tm=tm, tiles_per_seq=seq // tm,
            rope_tables=rope, kv_only=False)
        if need_ctx:
            qc, kdc, vdc, rxc, ryc, pool_c, gtc = _inproj(
                c2, mod_all, l, ctx_row, g_pre, w_in_b, w_mix, p_scale, tm=tm_ctx,
                tiles_per_seq=n_ctx // tm_ctx, rope_tables=None, kv_only=False)
        else:
            kdc, vdc, rxc = _inproj(
                c2, mod_all, l, ctx_row, g_pre, w_in_b, None, None, tm=tm_ctx, tiles_per_seq=1,
                rope_tables=None, kv_only=True)
            ryc = None

        kdc3, vdc3 = r3(kdc, n_ctx), r3(vdc, n_ctx)
        attn_l = _attention(attn_sink, l, r3(q, seq), r3(kd, seq), r3(vd, seq), kdc3, vdc3)
        rnn_l, rnn_c = _lru(l, r3(rxc, n_ctx), r3(rx, seq), None if ryc is None else r3(ryc, n_ctx),
                            r3(ry, seq), conv_w, cv_b, wg, lru_b_a, lru_b_x, lru_lambda)

        x2 = _mix_ffn(x2, mod_all, l, lat_row, flat(attn_l), pool_l, flat(rnn_l), gt, wa, wp, wr, wo,
                      g_post, gf_pre, w_gu_b, w_dn_b, gf_post, tm=tm)

        if need_ctx:
            attn_c = _attention_ctx(attn_sink, l, r3(qc, n_ctx), kdc3, vdc3)
            c2 = _mix_ffn(c2, mod_all, l, ctx_row, flat(attn_c), pool_c, flat(rnn_c), gtc, wa, wp, wr,
                          wo, g_post, gf_pre, w_gu_b, w_dn_b, gf_post, tm=tm_ctx)
    return x2.reshape(bsz, seq, d)
```

```python
import functools
import math

import jax
import jax.numpy as jnp
import numpy as np
from jax import lax
from jax.experimental import pallas as pl
from jax.experimental.pallas import tpu as pltpu

F32 = jnp.float32
BF16 = jnp.bfloat16

GRID_W = 64
HEAD_DIM = 64
N_Q_HEADS = 8
N_KV_HEADS = 2
ATTN_W = N_Q_HEADS * HEAD_DIM
KV_W = N_KV_HEADS * HEAD_DIM
ATTN_BLOCK = 128
ROPE_BASE = 10000.0
POOL_WINDOWS = (2, 4, 8, 16)
POOL_W = 512
POOL_GW = POOL_W // len(POOL_WINDOWS)
RNN_W = 512
RNN_BLOCKS = 8
CONV_W = 4
LRU_C = 8.0
N_BRANCH = 3
EPS = 1e-6
NEG_INF = -1e30
LOG2E = 1.4426950408889634
TINY = 1e-30

LANES = 128
SUBLANES = 8
VMEM_LIMIT_BYTES = 56 * 1024 * 1024

SEG = 36
CHUNK = SUBLANES * SEG


def _params(n_axes):
    return pltpu.CompilerParams(
        dimension_semantics=("arbitrary",) * n_axes, vmem_limit_bytes=VMEM_LIMIT_BYTES)


def _layer_block(arr, layer, block=None):
    block = tuple(arr.shape[1:]) if block is None else block
    index = (layer,) + (0,) * len(block)
    return pl.BlockSpec((None,) + block, lambda *_: index, pipeline_mode=pl.Buffered(1))


def _mod_block(mod_all, layer, mod_row_of_tile):
    return pl.BlockSpec((None, None) + mod_all.shape[2:], lambda i: (layer, mod_row_of_tile(i), 0, 0))


def _rms(xf):
    return xf * lax.rsqrt(jnp.mean(xf * xf, axis=-1, keepdims=True) + EPS)


def _mod_body(c_ref, w_ref, b_ref, o_ref):
    c = c_ref[...]
    s = (c * jax.nn.sigmoid(c)).astype(BF16)
    o_ref[0] = jnp.dot(s, w_ref[0].astype(BF16), preferred_element_type=F32) + b_ref[0]


def _adaln_mod(cvec, w_ada, b_ada):
    n_layers, d, n6 = w_ada.shape
    rows = cvec.shape[0]
    tn = n6 // 4
    return pl.pallas_call(
        _mod_body,
        grid=(n_layers, n6 // tn),
        in_specs=[
            pl.BlockSpec((rows, d), lambda l, j: (0, 0)),
            pl.BlockSpec((1, d, tn), lambda l, j: (l, 0, j)),
            pl.BlockSpec((1, 1, tn), lambda l, j: (l, 0, j)),
        ],
        out_specs=pl.BlockSpec((1, rows, tn), lambda l, j: (l, 0, j)),
        out_shape=jax.ShapeDtypeStruct((n_layers, rows, n6), F32),
        compiler_params=_params(2),
        name="adaln_mod",
    )(cvec, w_ada, b_ada.reshape(n_layers, 1, n6))


def _rope_tables(seq_len):
    pos = np.arange(seq_len)
    row = (pos // GRID_W).astype(np.float32)
    col = (pos % GRID_W).astype(np.float32)
    half = HEAD_DIM // 2
    quarter = half // 2
    inv = (np.float32(ROPE_BASE) ** (-(np.arange(quarter, dtype=np.float32) * np.float32(2.0 / half)))
           ).astype(np.float32)
    j = np.arange(LANES) % HEAD_DIM
    is_col = j >= half
    second = (j % half) >= quarter
    freq = inv[j % quarter]
    ang = (np.where(is_col[None, :], col[:, None], row[:, None]) * freq[None, :]).astype(np.float32)
    cos, sin = np.cos(ang), np.sin(ang)
    sin_prev = np.where(second[None, :], sin, 0.0)
    sin_next = np.where(second[None, :], 0.0, -sin)
    return tuple(jnp.asarray(t, F32) for t in (cos, sin_prev, sin_next))


def _row_halves(n_rows):
    if n_rows % (2 * SUBLANES * 2) or n_rows < 256:
        return [slice(0, n_rows)]
    return [slice(0, n_rows // 2), slice(n_rows // 2, n_rows)]


def _store_dup_heads(dst_ref, rows, kv):
    low = lax.broadcasted_iota(jnp.int32, kv.shape, 1) < HEAD_DIM
    swapped = pltpu.roll(kv, HEAD_DIM, 1)
    dst_ref[rows, 0:LANES] = jnp.where(low, kv, swapped).astype(BF16)
    dst_ref[rows, LANES:2 * LANES] = jnp.where(low, swapped, kv).astype(BF16)


def _norm_modulate_store(x_ref, mod_ref, g_ref, h_ref, shift_row, scale_row, rows):
    y = _rms(x_ref[rows, :]) * g_ref[...]
    h = y * (1.0 + mod_ref[scale_row:scale_row + 1, :]) + mod_ref[shift_row:shift_row + 1, :]
    h_ref[rows, :] = h.astype(BF16)


def _inproj_full_body(*refs, rope, tiles_per_seq, n_cast):
    refs = list(refs)
    x_ref, xp_ref, xn_ref, mod_ref, g_ref, w_ref, wmix_ref, psc_ref = refs[:8]
    del refs[:8]
    if rope:
        cos_ref, sp_ref, sn_ref = refs[:3]
        del refs[:3]
    cast_in = refs[:n_cast]
    q_ref, kd_ref, vd_ref, rx_ref, ry_ref, po_ref, gt_ref = refs[n_cast:n_cast + 7]
    cast_out = refs[n_cast + 7:2 * n_cast + 7]
    h_ref = refs[2 * n_cast + 7]
    for src, dst in zip(cast_in, cast_out):
        dst[...] = src[...].astype(BF16)
    quarter = HEAD_DIM // 4
    tm, d = x_ref.shape
    halo = SUBLANES
    tile_in_seq = pl.program_id(0) % tiles_per_seq
    pu0 = ATTN_W + 2 * KV_W + 2 * RNN_W

    def modulated(xf):
        y = _rms(xf) * g_ref[...]
        return y * (1.0 + mod_ref[1:2, :]) + mod_ref[0:1, :]

    pack = 2 * SUBLANES
    zeros = jnp.zeros((SUBLANES, d), F32)
    before = jnp.where(tile_in_seq == 0, 0.0, modulated(xp_ref[...]))
    after = jnp.where(tile_in_seq == tiles_per_seq - 1, 0.0, modulated(xn_ref[...]))
    h_ref[0:pack, :] = jnp.concatenate([zeros, before], axis=0).astype(BF16)
    h_ref[pack + tm:2 * pack + tm, :] = jnp.concatenate([after, zeros], axis=0).astype(BF16)
    for rows in _row_halves(tm):
        h_ref[pack + rows.start:pack + rows.stop, :] = modulated(x_ref[rows, :]).astype(BF16)

    for rows in _row_halves(tm):
        r0, n_rows = rows.start, rows.stop - rows.start
        hrows = slice(pack + r0, pack + rows.stop)

        def proj(c0, c1):
            return jnp.dot(h_ref[hrows, :], w_ref[:, c0:c1], preferred_element_type=F32)

        def rotary(pc):
            if not rope:
                return pc
            return (pc * cos_ref[rows, :]
                    + pltpu.roll(pc, quarter, 1) * sp_ref[rows, :]
                    + pltpu.roll(pc, LANES - quarter, 1) * sn_ref[rows, :])

        padded = jnp.dot(h_ref[r0:r0 + n_rows + 2 * pack, :], w_ref[:, pu0:pu0 + POOL_W],
                         preferred_element_type=F32)[pack - halo:pack + n_rows + halo]
        t = tile_in_seq * tm + r0 + lax.broadcasted_iota(jnp.int32, (n_rows, POOL_GW), 0)

        def pool_group(gi):
            lanes = slice(gi * POOL_GW, (gi + 1) * POOL_GW)
            po_ref[rows, lanes] = _pool_group(padded[:, lanes], t, tiles_per_seq * tm, POOL_WINDOWS[gi],
                                              wmix_ref[gi], psc_ref[:, lanes])

        def seg_q(c):
            pair = proj(c * 2 * LANES, (c + 1) * 2 * LANES)
            for half in range(2):
                col = (2 * c + half) * LANES
                pc = rotary(pair[:, half * LANES:(half + 1) * LANES])
                q_ref[rows, col:col + LANES] = (pc * (HEAD_DIM ** -0.5 * LOG2E)).astype(BF16)

        def seg_kv():
            kv = proj(ATTN_W, ATTN_W + 2 * KV_W)
            _store_dup_heads(kd_ref, rows, rotary(kv[:, :KV_W]))
            _store_dup_heads(vd_ref, rows, kv[:, KV_W:])

        rx0 = ATTN_W + 2 * KV_W

        def seg_rx():
            rx_ref[rows, :] = proj(rx0, rx0 + RNN_W)

        def seg_ry():
            ry_ref[rows, :] = jax.nn.gelu(proj(rx0 + RNN_W, rx0 + 2 * RNN_W)).astype(BF16)

        gl0 = pu0 + POOL_W

        def seg_gate(c):
            gl = proj(gl0 + c * d, gl0 + (c + 1) * d)
            gt_ref[rows, c * d:(c + 1) * d] = jax.nn.sigmoid(gl).astype(BF16)

        seg_gate(0)
        pool_group(0)
        seg_q(0)
        seg_gate(1)
        pool_group(1)
        seg_q(1)
        seg_gate(2)
        pool_group(2)
        seg_kv()
        seg_ry()
        pool_group(3)
        seg_rx()


def _inproj_kv_body(x_ref, mod_ref, g_ref, w_ref, kd_ref, vd_ref, rx_ref, h_ref):
    rows = slice(0, x_ref.shape[0])
    _norm_modulate_store(x_ref, mod_ref, g_ref, h_ref, 0, 1, rows)
    kv = jnp.dot(h_ref[...], w_ref[:, ATTN_W:ATTN_W + 2 * KV_W], preferred_element_type=F32)
    _store_dup_heads(kd_ref, rows, kv[:, :KV_W])
    _store_dup_heads(vd_ref, rows, kv[:, KV_W:])
    rx_ref[...] = jnp.dot(h_ref[...], w_ref[:, ATTN_W + 2 * KV_W:ATTN_W + 2 * KV_W + RNN_W],
                          preferred_element_type=F32)


def _cast_slab(weight, n_steps):
    rows = weight.shape[1]
    for hold in (1, 2, 4, 8):
        if n_steps % hold == 0 and rows % (n_steps // hold) == 0 and (rows // (n_steps // hold)) % 16 == 0:
            return rows // (n_steps // hold), hold
    raise ValueError("no bf16-tileable row slab for %s over %d steps" % (weight.shape, n_steps))


def _inproj(x2d, mod_all, layer, mod_row_of_tile, g, w, w_mix, p_scale, *, tm, tiles_per_seq, rope_tables,
            kv_only, cast_weights=()):
    n_tok, d = x2d.shape
    grid = (n_tok // tm,)
    row = lambda i: (i, 0)
    in_specs = [pl.BlockSpec((tm, d), row)]
    args = [x2d]
    if not kv_only:
        per_tile = tm // SUBLANES
        in_specs += [pl.BlockSpec((SUBLANES, d), lambda i: (jnp.maximum(i * per_tile - 1, 0), 0)),
                     pl.BlockSpec((SUBLANES, d),
                                  lambda i: (jnp.minimum((i + 1) * per_tile, n_tok // SUBLANES - 1), 0))]
        args += [x2d, x2d]
    w_cols = ATTN_W + 2 * KV_W + RNN_W if kv_only else w.shape[2]
    in_specs += [_mod_block(mod_all, layer, mod_row_of_tile), _layer_block(g, layer),
                 _layer_block(w, layer, (d, w_cols))]
    args += [mod_all, g, w]
    if not kv_only:
        in_specs += [_layer_block(w_mix, layer), _layer_block(p_scale, layer)]
        args += [w_mix, p_scale]
    if rope_tables is not None:
        in_specs += [pl.BlockSpec((tm, LANES), lambda i: (i % tiles_per_seq, 0))] * 3
        args += list(rope_tables)
    cast_outs = []
    for wgt in cast_weights:
        slab, hold = _cast_slab(wgt, grid[0])
        cols = wgt.shape[2]
        in_specs.append(pl.BlockSpec((None, slab, cols), lambda i, hold=hold: (layer, i // hold, 0)))
        args.append(wgt)
        cast_outs.append((pl.BlockSpec((slab, cols), lambda i, hold=hold: (i // hold, 0)),
                          jax.ShapeDtypeStruct(wgt.shape[1:], BF16)))

    def out(width, dtype):
        return pl.BlockSpec((tm, width), row), jax.ShapeDtypeStruct((n_tok, width), dtype)

    if kv_only:
        outs = [out(2 * KV_W, BF16), out(2 * KV_W, BF16), out(RNN_W, F32)]
        body = _inproj_kv_body
        h_rows = tm
    else:
        outs = [out(ATTN_W, BF16), out(2 * KV_W, BF16), out(2 * KV_W, BF16), out(RNN_W, F32),
                out(RNN_W, BF16), out(POOL_W, BF16), out(N_BRANCH * d, BF16)] + cast_outs
        body = functools.partial(_inproj_full_body, rope=rope_tables is not None,
                                 tiles_per_seq=tiles_per_seq, n_cast=len(cast_outs))
        h_rows = tm + 4 * SUBLANES
    return pl.pallas_call(
        body,
        grid=grid,
        in_specs=in_specs,
        out_specs=[o[0] for o in outs],
        out_shape=[o[1] for o in outs],
        scratch_shapes=[pltpu.VMEM((h_rows, d), BF16)],
        compiler_params=_params(1),
        name="inproj_kv" if kv_only else "inproj",
    )(*args)


def _transpose_values(v_ref, vt_ref, first_block):
    for j in range(v_ref.shape[0] // LANES):
        blk = v_ref[j * LANES:(j + 1) * LANES, :].astype(F32)
        vt_ref[first_block + j] = blk.T.astype(BF16)


def _attend(sink_ref, layer, items):
    heads_per_kv = N_Q_HEADS // N_KV_HEADS
    zero = jnp.zeros((), BF16)

    def scores(item, h):
        q_blk, k_rows, _, biases = item
        low = lax.broadcasted_iota(jnp.int32, (q_blk.shape[0], LANES), 1) < HEAD_DIM
        k_h = k_rows[:, h * LANES:(h + 1) * LANES]
        stacked = []
        for c2 in range(heads_per_kv // 2):
            qc = q_blk[:, (2 * h + c2) * LANES:(2 * h + c2 + 1) * LANES]
            stacked += [jnp.where(low, qc, zero), jnp.where(low, zero, qc)]
        q_rows = jnp.concatenate(stacked, axis=0)
        s = lax.dot_general(k_h, q_rows, (((1,), (1,)), ((), ())), preferred_element_type=F32)
        blocks = []
        for j, bias in enumerate(biases):
            blk = s[j * LANES:(j + 1) * LANES, :]
            if bias is not None:
                blk = blk + jnp.concatenate([bias] * heads_per_kv, axis=1)
            blocks.append(blk)
        return blocks

    def softmax(h, blocks):
        tq = blocks[0].shape[1] // heads_per_kv
        sink_row = jnp.concatenate(
            [jnp.full((1, tq), sink_ref[layer, h * heads_per_kv + g] * LOG2E, F32)
             for g in range(heads_per_kv)],
            axis=1)
        m = jnp.maximum(jnp.max(functools.reduce(jnp.maximum, blocks), axis=0, keepdims=True), sink_row)
        probs = [jnp.exp2(blk - m) for blk in blocks]
        denom = jnp.sum(functools.reduce(jnp.add, probs), axis=0, keepdims=True) + jnp.exp2(sink_row - m)
        return jnp.concatenate(probs, axis=0).astype(BF16), denom

    def weighted(item, h, p, denom):
        tq = item[0].shape[0]
        vt_h = item[2][h * LANES:(h + 1) * LANES, :]
        o = jnp.dot(vt_h, p, preferred_element_type=F32) * (1.0 / denom)
        first_copy = lax.broadcasted_iota(jnp.int32, (LANES, tq), 0) < HEAD_DIM
        return [jnp.where(first_copy, o[:, (2 * c2) * tq:(2 * c2 + 1) * tq],
                          o[:, (2 * c2 + 1) * tq:(2 * c2 + 2) * tq]).T.astype(BF16)
                for c2 in range(heads_per_kv // 2)]

    units = [(item, h) for item in items for h in range(N_KV_HEADS)]
    all_scores = [scores(item, h) for item, h in units]
    all_probs = [softmax(h, blocks) for (_, h), blocks in zip(units, all_scores)]
    all_outs = [weighted(item, h, *pd) for (item, h), pd in zip(units, all_probs)]
    return [jnp.concatenate([o for outs in all_outs[i * N_KV_HEADS:(i + 1) * N_KV_HEADS] for o in outs],
                            axis=1) for i in range(len(items))]


def _attn_local_body(sink_ref, q_ref, kd_ref, vd_ref, kc_ref, vc_ref, o_ref, vt_ref, *, layer, n_blocks,
                     group):
    blk = ATTN_BLOCK
    n_ctx_blocks = kc_ref.shape[0] // LANES
    _transpose_values(vd_ref, vt_ref, 0)
    _transpose_values(vc_ref, vt_ref, n_blocks)
    key = lax.broadcasted_iota(jnp.int32, (blk, blk), 0)
    qry = lax.broadcasted_iota(jnp.int32, (blk, blk), 1)
    rows = lambda i: pl.ds(pl.multiple_of(i * blk, blk), blk)

    def item(n):
        prv, nxt = jnp.maximum(n - 1, 0), jnp.minimum(n + 1, n_blocks - 1)
        bias_prev = jnp.where((key >= qry) & (n > 0), 0.0, NEG_INF).astype(F32)
        bias_next = jnp.where((key <= qry) & (n < n_blocks - 1), 0.0, NEG_INF).astype(F32)
        k_rows = jnp.concatenate([kd_ref[rows(prv), :], kd_ref[rows(n), :], kd_ref[rows(nxt), :],
                                  kc_ref[...]], axis=0)
        vt = jnp.concatenate([vt_ref[prv], vt_ref[n], vt_ref[nxt]]
                             + [vt_ref[n_blocks + j] for j in range(n_ctx_blocks)], axis=1)
        return (q_ref[rows(n), :], k_rows, vt, [bias_prev, None, bias_next] + [None] * n_ctx_blocks)

    def body(i, carry):
        blocks = [i * group + g for g in range(group)]
        for n, out in zip(blocks, _attend(sink_ref, layer, [item(n) for n in blocks])):
            o_ref[rows(n), :] = out
        return carry

    lax.fori_loop(0, n_blocks // group, body, 0)


def _attn_ctx_body(sink_ref, q_ref, kc_ref, vc_ref, o_ref, vt_ref, *, layer):
    n_key_blocks = kc_ref.shape[0] // LANES
    _transpose_values(vc_ref, vt_ref, 0)
    vt = jnp.concatenate([vt_ref[j] for j in range(n_key_blocks)], axis=1)
    items = [(q_ref[i * ATTN_BLOCK:(i + 1) * ATTN_BLOCK, :], kc_ref[...], vt, [None] * n_key_blocks)
             for i in range(q_ref.shape[0] // ATTN_BLOCK)]
    for i, out in enumerate(_attend(sink_ref, layer, items)):
        o_ref[i * ATTN_BLOCK:(i + 1) * ATTN_BLOCK, :] = out


def _attention(sink, layer, q, kd, vd, kdc, vdc):
    b, s, _ = q.shape
    c = kdc.shape[1]
    kw = 2 * KV_W
    n_blocks = s // ATTN_BLOCK
    group = math.gcd(n_blocks, 4)
    idx = lambda bi: (bi, 0, 0)
    return pl.pallas_call(
        functools.partial(_attn_local_body, layer=layer, n_blocks=n_blocks, group=group),
        grid=(b,),
        in_specs=[
            pl.BlockSpec(memory_space=pltpu.SMEM),
            pl.BlockSpec((None, s, ATTN_W), idx),
            pl.BlockSpec((None, s, kw), idx), pl.BlockSpec((None, s, kw), idx),
            pl.BlockSpec((None, c, kw), idx), pl.BlockSpec((None, c, kw), idx),
        ],
        out_specs=pl.BlockSpec((None, s, ATTN_W), idx),
        out_shape=jax.ShapeDtypeStruct((b, s, ATTN_W), BF16),
        scratch_shapes=[pltpu.VMEM((n_blocks + c // LANES, kw, LANES), BF16)],
        compiler_params=_params(1),
        name="attn_local",
    )(sink, q, kd, vd, kdc, vdc)


def _attention_ctx(sink, layer, qc, kdc, vdc):
    b, c, _ = qc.shape
    kw = 2 * KV_W
    idx = lambda bi: (bi, 0, 0)
    return pl.pallas_call(
        functools.partial(_attn_ctx_body, layer=layer),
        grid=(b,),
        in_specs=[
            pl.BlockSpec(memory_space=pltpu.SMEM),
            pl.BlockSpec((None, c, ATTN_W), idx),
            pl.BlockSpec((None, c, kw), idx), pl.BlockSpec((None, c, kw), idx),
        ],
        out_specs=pl.BlockSpec((None, c, ATTN_W), idx),
        out_shape=jax.ShapeDtypeStruct((b, c, ATTN_W), BF16),
        scratch_shapes=[pltpu.VMEM((c // LANES, kw, LANES), BF16)],
        compiler_params=_params(1),
        name="attn_ctx",
    )(sink, qc, kdc, vdc)


def _pool_group(padded, t, length, window, w_mix, scale):
    halo = SUBLANES
    rows = padded.shape[0]
    n_rows = rows - 2 * halo
    acc = padded
    step = 1
    while step < window:
        acc = acc + pltpu.roll(acc, rows - step, 0)
        step *= 2
    back = (window - 1) // 2
    win = (pltpu.roll(acc, back, 0) if back else acc)[halo:halo + n_rows]
    lo = jnp.maximum(t - back, 0)
    hi = jnp.minimum(t + window // 2 + 1, length)
    mean = win / (hi - lo).astype(F32)
    dlt = (mean - padded[halo:halo + n_rows]).astype(BF16)
    return (jnp.dot(dlt, w_mix, preferred_element_type=F32) * scale).astype(BF16)


def _lru_body(*refs, n_ctx, n_lat, tile, with_ctx_out):
    if with_ctx_out:
        (rxc_ref, rxl_ref, ryc_ref, ryl_ref, cw_ref, cb_ref, wg_ref, ba_ref, bx_ref, lam_ref,
         ol_ref, oc_ref, pad_ref, xl_ref, hf_ref, hb_ref, xp_ref, a_ref, b_ref) = refs
    else:
        (rxc_ref, rxl_ref, ryl_ref, cw_ref, cb_ref, wg_ref, ba_ref, bx_ref, lam_ref,
         ol_ref, pad_ref, xl_ref, hf_ref, hb_ref, xp_ref, a_ref, b_ref) = refs
    halo = SUBLANES
    n_tot = n_ctx + n_lat
    n_slab = RNN_W // LANES

    zeros = jnp.zeros((halo, RNN_W), F32)
    c0, l0 = halo, 2 * halo + n_ctx
    pad_ref[0:halo, :] = zeros
    pad_ref[c0 + n_ctx:l0, :] = zeros
    pad_ref[l0 + n_lat:l0 + n_lat + halo, :] = zeros
    pad_ref[c0:c0 + n_ctx, :] = rxc_ref[...]
    pad_ref[l0:l0 + n_lat, :] = rxl_ref[...]

    cw = 0.5 * cw_ref[...]
    cb = 0.5 * cb_ref[...]

    def conv_tile(src0, dst_rows):
        rows = tile + 2 * halo
        p = pad_ref[src0 - halo:src0 + tile + halo, :]
        y = (cb + p[halo:halo + tile] * cw[1:2, :]
             + pltpu.roll(p, 1, 0)[halo:halo + tile] * cw[0:1, :]
             + pltpu.roll(p, rows - 1, 0)[halo:halo + tile] * cw[2:3, :]
             + pltpu.roll(p, rows - 2, 0)[halo:halo + tile] * cw[3:4, :])
        for s in range(n_slab):
            for d0 in dst_rows:
                xl_ref[s, d0:d0 + tile, :] = y[:, s * LANES:(s + 1) * LANES]

    for t0 in range(0, n_ctx, tile):
        conv_tile(c0 + t0, (t0, n_tot + t0))
    for t0 in range(0, n_lat, tile):
        conv_tile(l0 + t0, (n_ctx + t0,))

    lam = lam_ref[...]
    log_sig = jnp.minimum(lam, 0.0) - jnp.log1p(jnp.exp(-jnp.abs(lam)))
    rate = (0.5 * LOG2E * LRU_C) * log_sig
    half_ba = 0.5 * ba_ref[...]
    half_bx = 0.5 * bx_ref[...]
    half = RNN_W // 2

    def run_chunk(d, base, carry):
        res_ref = hf_ref if d == 0 else hb_ref
        win0 = 0 if d == 0 else n_ctx
        for v in range(SEG):
            for s in range(n_slab):
                xp_ref[d, v * SUBLANES:(v + 1) * SUBLANES, s * LANES:(s + 1) * LANES] = (
                    xl_ref[s, pl.ds(win0 + base + v, SUBLANES, stride=SEG), :])
        for j in range(2):
            ch = slice(j * half, (j + 1) * half)
            xj = xp_ref[d, :, ch]
            gates = jnp.dot(xj.astype(BF16), wg_ref[d, j], preferred_element_type=F32)
            t_r = jnp.tanh(gates[:, :half] + half_ba[d:d + 1, ch])
            t_i = jnp.tanh(gates[:, half:] + half_bx[d:d + 1, ch])
            a = jnp.exp2(t_r * rate[d:d + 1, ch] + rate[d:d + 1, ch])
            gap = 1.0 - a * a
            mult = gap * lax.rsqrt(jnp.maximum(gap, TINY))
            a_ref[d, :, ch] = a
            b_ref[d, :, ch] = (mult * xj) * (t_i + 1.0)
        order = range(SEG) if d == 0 else range(SEG - 1, -1, -1)
        h = jnp.zeros((SUBLANES, RNN_W), F32)
        acum = jnp.ones((SUBLANES, RNN_W), F32)
        for v in order:
            rows = slice(v * SUBLANES, (v + 1) * SUBLANES)
            av = a_ref[d, rows, :]
            h = av * h + b_ref[d, rows, :]
            acum = av * acum
            b_ref[d, rows, :] = h
            a_ref[d, rows, :] = acum
        seg_in = [None] * SUBLANES
        state = carry
        seg_order = range(SUBLANES) if d == 0 else range(SUBLANES - 1, -1, -1)
        for i in seg_order:
            seg_in[i] = state
            state = h[i:i + 1, :] + acum[i:i + 1, :] * state
        seg_state = jnp.concatenate(seg_in, axis=0)
        for v in range(SEG):
            rows = slice(v * SUBLANES, (v + 1) * SUBLANES)
            hv = b_ref[d, rows, :] + a_ref[d, rows, :] * seg_state
            for s in range(n_slab):
                res_ref[s, pl.ds(base + v, SUBLANES, stride=SEG), :] = hv[:, s * LANES:(s + 1) * LANES]
        return state

    n_chunks = n_tot // CHUNK

    def step(c, carries):
        cf, cb = carries
        cf = run_chunk(0, c * CHUNK, cf)
        cb = run_chunk(1, (n_chunks - 1 - c) * CHUNK, cb)
        return cf, cb

    zero_state = jnp.zeros((1, RNN_W), F32)
    lax.fori_loop(0, n_chunks, step, (zero_state, zero_state))

    def emit(out_ref, ry_ref, f0, b0, length):
        for t0 in range(0, length, tile):
            hsum = jnp.concatenate(
                [hf_ref[s, f0 + t0:f0 + t0 + tile, :] + hb_ref[s, b0 + t0:b0 + t0 + tile, :]
                 for s in range(n_slab)], axis=1)
            gate = ry_ref[t0:t0 + tile, :].astype(F32)
            out_ref[t0:t0 + tile, :] = (hsum * gate).astype(BF16)

    emit(ol_ref, ryl_ref, n_ctx, 0, n_lat)
    if with_ctx_out:
        emit(oc_ref, ryc_ref, 0, n_lat, n_ctx)


def _lru(layer, rxc, rxl, ryc, ryl, conv_w, conv_b, wg, b_a, b_x, lam):
    b, n_ctx, _ = rxc.shape
    n_lat = rxl.shape[1]
    n_tot = n_ctx + n_lat
    tile = math.gcd(n_ctx, n_lat, 256)
    assert n_tot % CHUNK == 0 and tile % SUBLANES == 0
    with_ctx_out = ryc is not None
    idx = lambda bi: (bi, 0, 0)
    seq = lambda n, dt=None: pl.BlockSpec((None, n, RNN_W), idx)
    n_slab = RNN_W // LANES
    in_specs = [seq(n_ctx), seq(n_lat)] + ([seq(n_ctx)] if with_ctx_out else []) + [seq(n_lat)]
    args = [rxc, rxl] + ([ryc] if with_ctx_out else []) + [ryl]
    for wgt in (conv_w, conv_b, wg, b_a, b_x, lam):
        in_specs.append(_layer_block(wgt, layer))
        args.append(wgt)
    out_specs = [seq(n_lat)] + ([seq(n_ctx)] if with_ctx_out else [])
    out_shape = [jax.ShapeDtypeStruct((b, n_lat, RNN_W), BF16)]
    if with_ctx_out:
        out_shape.append(jax.ShapeDtypeStruct((b, n_ctx, RNN_W), BF16))
    res = pl.pallas_call(
        functools.partial(_lru_body, n_ctx=n_ctx, n_lat=n_lat, tile=tile, with_ctx_out=with_ctx_out),
        grid=(b,),
        in_specs=in_specs,
        out_specs=out_specs,
        out_shape=out_shape,
        scratch_shapes=[
            pltpu.VMEM((n_tot + 3 * SUBLANES, RNN_W), F32),
            pltpu.VMEM((n_slab, n_tot + n_ctx, LANES), F32),
            pltpu.VMEM((n_slab, n_tot, LANES), F32),
            pltpu.VMEM((n_slab, n_tot, LANES), F32),
            pltpu.VMEM((2, CHUNK, RNN_W), F32),
            pltpu.VMEM((2, CHUNK, RNN_W), F32),
            pltpu.VMEM((2, CHUNK, RNN_W), F32),
        ],
        compiler_params=_params(1),
        name="rglru",
    )(*args)
    return (res[0], res[1]) if with_ctx_out else (res[0], None)


def _mix_ffn_body(x_ref, mod_ref, at_ref, po_ref, rn_ref, gt_ref, wa_ref, wp_ref, wr_ref, wo_ref,
                  gmix_ref, gpre_ref, wgu_ref, wd_ref, gpost_ref, o_ref, m_ref, h_ref, *, d_ff, chunk):
    d = x_ref.shape[1]
    for rows in _row_halves(x_ref.shape[0]):
        merged = None
        for k, (br_ref, w_ref) in enumerate(((at_ref, wa_ref), (po_ref, wp_ref), (rn_ref, wr_ref))):
            t = gt_ref[rows, k * d:(k + 1) * d].astype(F32) * jnp.dot(
                br_ref[rows, :], w_ref[...], preferred_element_type=F32)
            merged = t if merged is None else merged + t
        m_ref[rows, :] = merged.astype(BF16)
        mix = jnp.dot(m_ref[rows, :], wo_ref[...], preferred_element_type=F32)
        o_ref[rows, :] = x_ref[rows, :] + mod_ref[2:3, :] * (_rms(mix) * gmix_ref[...])
        _norm_modulate_store(o_ref, mod_ref, gpre_ref, h_ref, 3, 4, rows)

    halves = _row_halves(x_ref.shape[0])
    accs = [None] * len(halves)
    for c0 in range(0, d_ff, chunk):
        c1 = min(c0 + chunk, d_ff)
        for i, rows in enumerate(halves):
            gate = jnp.dot(h_ref[rows, :], wgu_ref[:, c0:c1], preferred_element_type=F32)
            up = jnp.dot(h_ref[rows, :], wgu_ref[:, d_ff + c0:d_ff + c1], preferred_element_type=F32)
            act = ((gate * jax.nn.sigmoid(gate)) * up).astype(BF16)
            part = jnp.dot(act, wd_ref[c0:c1, :], preferred_element_type=F32)
            accs[i] = part if accs[i] is None else accs[i] + part
    for acc, rows in zip(accs, halves):
        o_ref[rows, :] = o_ref[rows, :] + mod_ref[5:6, :] * (_rms(acc) * gpost_ref[...])


def _mix_ffn(x2d, mod_all, layer, mod_row_of_tile, attn, pool, rnn, gates, wa, wp, wr, wo, g_mix,
             g_pre, w_gu, w_down, g_post, *, tm):
    n_tok, d = x2d.shape
    d_ff = w_down.shape[0]
    row = lambda i: (i, 0)
    tok = lambda arr: pl.BlockSpec((tm, arr.shape[1]), row)
    own = lambda wgt: (wgt[None], 0)
    params = (own(wa), own(wp), own(wr), own(wo), (g_mix, layer), (g_pre, layer), own(w_gu), own(w_down),
              (g_post, layer))
    return pl.pallas_call(
        functools.partial(_mix_ffn_body, d_ff=d_ff, chunk=1024),
        grid=(n_tok // tm,),
        in_specs=[tok(x2d), _mod_block(mod_all, layer, mod_row_of_tile),
                  tok(attn), tok(pool), tok(rnn), tok(gates)]
                 + [_layer_block(p, lyr) for p, lyr in params],
        out_specs=pl.BlockSpec((tm, d), row),
        out_shape=jax.ShapeDtypeStruct((n_tok, d), F32),
        scratch_shapes=[pltpu.VMEM((tm, d), BF16), pltpu.VMEM((tm, d), BF16)],
        compiler_params=_params(1),
        name="mix_ffn",
    )(x2d, mod_all, attn, pool, rnn, gates, *[p for p, _ in params])


def _pack_gate_weights(w_a, w_x):
    def block_diag(w):
        bw = RNN_W // RNN_BLOCKS
        rows = [jnp.pad(w[:, :, h], ((0, 0), (0, 0), (0, 0), (bw * h, RNN_W - bw * (h + 1))))
                for h in range(RNN_BLOCKS)]
        return jnp.concatenate(rows, axis=2)
    bd_a, bd_x = block_diag(w_a), block_diag(w_x)
    half = RNN_W // 2
    halves = []
    for j in range(2):
        sl = slice(j * half, (j + 1) * half)
        halves.append(jnp.concatenate([bd_a[:, :, sl, sl], bd_x[:, :, sl, sl]], axis=3))
    return jnp.stack(halves, axis=2).astype(BF16)


def kernel(x, c, ctx, c_ctx, w_ada, b_ada, g_pre_mix, g_post_mix, g_pre_ffn, g_post_ffn, w_in, attn_sink, w_attn_o, pool_mix, pool_scale, w_pool_o, conv_w, conv_b, lru_w_a, lru_b_a, lru_w_x, lru_b_x, lru_lambda, w_rnn_o, w_out, w_gu, w_down):
    bsz, seq, d = x.shape
    n_ctx = ctx.shape[1]
    depth = w_ada.shape[0]
    tm = min(512, seq)
    tm_ctx = min(tm, n_ctx)

    mod_rows = -(-(bsz + 1) // SUBLANES) * SUBLANES
    cvec = jnp.zeros((mod_rows, d), F32).at[:bsz].set(c).at[bsz].set(c_ctx)
    mod_all = _adaln_mod(cvec, w_ada, b_ada).reshape(depth, mod_rows, 6, d)

    rope = _rope_tables(seq)
    lat_row = lambda i: i // (seq // tm)
    ctx_row = lambda i: bsz

    rowvec = lambda a: a.reshape(depth, 1, -1)
    w_in_b = w_in.astype(BF16)
    channel_weights = (w_attn_o, w_pool_o, w_rnn_o, w_out, w_gu, w_down)
    w_mix = pool_mix.astype(BF16)
    wg = _pack_gate_weights(lru_w_a, lru_w_x)
    g_pre, g_post = rowvec(g_pre_mix), rowvec(g_post_mix)
    gf_pre, gf_post = rowvec(g_pre_ffn), rowvec(g_post_ffn)
    p_scale, cv_b = rowvec(pool_scale), rowvec(conv_b)

    x2 = x.reshape(bsz * seq, d)
    c2 = ctx.reshape(bsz * n_ctx, d)
    r3 = lambda a, n: a.reshape(bsz, n, a.shape[-1])
    flat = lambda a: a.reshape(-1, a.shape[-1])
    for l in range(depth):
        need_ctx = l < depth - 1
        q, kd, vd, rx, ry, pool_l, gt, wa, wp, wr, wo, w_gu_b, w_dn_b = _inproj(
            x2, mod_all, l, lat_row, g_pre, w_in_b, w_mix, p_scale, tm=tm, tiles_per_seq=seq // tm,
            rope_tables=rope, kv_only=False, cast_weights=channel_weights)
        if need_ctx:
            qc, kdc, vdc, rxc, ryc, pool_c, gtc = _inproj(
                c2, mod_all, l, ctx_row, g_pre, w_in_b, w_mix, p_scale, tm=tm_ctx,
                tiles_per_seq=n_ctx // tm_ctx, rope_tables=None, kv_only=False)
        else:
            kdc, vdc, rxc = _inproj(
                c2, mod_all, l, ctx_row, g_pre, w_in_b, None, None, tm=tm_ctx, tiles_per_seq=1,
                rope_tables=None, kv_only=True)
            ryc = None

        kdc3, vdc3 = r3(kdc, n_ctx), r3(vdc, n_ctx)
        attn_l = _attention(attn_sink, l, r3(q, seq), r3(kd, seq), r3(vd, seq), kdc3, vdc3)
        rnn_l, rnn_c = _lru(l, r3(rxc, n_ctx), r3(rx, seq), None if ryc is None else r3(ryc, n_ctx),
                            r3(ry, seq), conv_w, cv_b, wg, lru_b_a, lru_b_x, lru_lambda)

        x2 = _mix_ffn(x2, mod_all, l, lat_row, flat(attn_l), pool_l, flat(rnn_l), gt, wa, wp, wr, wo,
                      g_post, gf_pre, w_gu_b, w_dn_b, gf_post, tm=tm)

        if need_ctx:
            attn_c = _attention_ctx(attn_sink, l, r3(qc, n_ctx), kdc3, vdc3)
            c2 = _mix_ffn(c2, mod_all, l, ctx_row, flat(attn_c), pool_c, flat(rnn_c), gtc, wa, wp, wr,
                          wo, g_post, gf_pre, w_gu_b, w_dn_b, gf_post, tm=tm_ctx)
    return x2.reshape(bsz, seq, d)
```

```python
import functools
import math

import jax
import jax.numpy as jnp
import numpy as np
from jax import lax
from jax.experimental import pallas as pl
from jax.experimental.pallas import tpu as pltpu

F32 = jnp.float32
BF16 = jnp.bfloat16

GRID_W = 64
HEAD_DIM = 64
N_Q_HEADS = 8
N_KV_HEADS = 2
ATTN_W = N_Q_HEADS * HEAD_DIM
KV_W = N_KV_HEADS * HEAD_DIM
ATTN_BLOCK = 128
ROPE_BASE = 10000.0
POOL_WINDOWS = (2, 4, 8, 16)
POOL_W = 512
POOL_GW = POOL_W // len(POOL_WINDOWS)
RNN_W = 512
RNN_BLOCKS = 8
CONV_W = 4
LRU_C = 8.0
N_BRANCH = 3
EPS = 1e-6
NEG_INF = -1e30
LOG2E = 1.4426950408889634
TINY = 1e-30

LANES = 128
SUBLANES = 8
VMEM_LIMIT_BYTES = 56 * 1024 * 1024

SEG = 36
CHUNK = SUBLANES * SEG


def _params(n_axes):
    return pltpu.CompilerParams(
        dimension_semantics=("arbitrary",) * n_axes, vmem_limit_bytes=VMEM_LIMIT_BYTES)


def _layer_block(arr, layer, block=None):
    block = tuple(arr.shape[1:]) if block is None else block
    index = (layer,) + (0,) * len(block)
    return pl.BlockSpec((None,) + block, lambda *_: index, pipeline_mode=pl.Buffered(1))


def _mod_block(mod_all, layer, mod_row_of_tile):
    return pl.BlockSpec((None, None) + mod_all.shape[2:], lambda i: (layer, mod_row_of_tile(i), 0, 0))


def _rms(xf):
    return xf * lax.rsqrt(jnp.mean(xf * xf, axis=-1, keepdims=True) + EPS)


def _mod_body(c_ref, w_ref, b_ref, o_ref):
    c = c_ref[...]
    s = (c * jax.nn.sigmoid(c)).astype(BF16)
    o_ref[0] = jnp.dot(s, w_ref[0].astype(BF16), preferred_element_type=F32) + b_ref[0]


def _adaln_mod(cvec, w_ada, b_ada):
    n_layers, d, n6 = w_ada.shape
    rows = cvec.shape[0]
    tn = n6 // 4
    return pl.pallas_call(
        _mod_body,
        grid=(n_layers, n6 // tn),
        in_specs=[
            pl.BlockSpec((rows, d), lambda l, j: (0, 0)),
            pl.BlockSpec((1, d, tn), lambda l, j: (l, 0, j)),
            pl.BlockSpec((1, 1, tn), lambda l, j: (l, 0, j)),
        ],
        out_specs=pl.BlockSpec((1, rows, tn), lambda l, j: (l, 0, j)),
        out_shape=jax.ShapeDtypeStruct((n_layers, rows, n6), F32),
        compiler_params=_params(2),
        name="adaln_mod",
    )(cvec, w_ada, b_ada.reshape(n_layers, 1, n6))


def _rope_tables(seq_len):
    pos = np.arange(seq_len)
    row = (pos // GRID_W).astype(np.float32)
    col = (pos % GRID_W).astype(np.float32)
    half = HEAD_DIM // 2
    quarter = half // 2
    inv = (np.float32(ROPE_BASE) ** (-(np.arange(quarter, dtype=np.float32) * np.float32(2.0 / half)))
           ).astype(np.float32)
    j = np.arange(LANES) % HEAD_DIM
    is_col = j >= half
    second = (j % half) >= quarter
    freq = inv[j % quarter]
    ang = (np.where(is_col[None, :], col[:, None], row[:, None]) * freq[None, :]).astype(np.float32)
    cos, sin = np.cos(ang), np.sin(ang)
    sin_prev = np.where(second[None, :], sin, 0.0)
    sin_next = np.where(second[None, :], 0.0, -sin)
    return tuple(jnp.asarray(t, F32) for t in (cos, sin_prev, sin_next))


def _row_halves(n_rows):
    if n_rows % (2 * SUBLANES * 2) or n_rows < 256:
        return [slice(0, n_rows)]
    return [slice(0, n_rows // 2), slice(n_rows // 2, n_rows)]


def _store_dup_heads(dst_ref, rows, kv):
    low = lax.broadcasted_iota(jnp.int32, kv.shape, 1) < HEAD_DIM
    swapped = pltpu.roll(kv, HEAD_DIM, 1)
    dst_ref[rows, 0:LANES] = jnp.where(low, kv, swapped).astype(BF16)
    dst_ref[rows, LANES:2 * LANES] = jnp.where(low, swapped, kv).astype(BF16)


def _norm_modulate_store(x_ref, mod_ref, g_ref, h_ref, shift_row, scale_row, rows):
    y = _rms(x_ref[rows, :]) * g_ref[...]
    h = y * (1.0 + mod_ref[scale_row:scale_row + 1, :]) + mod_ref[shift_row:shift_row + 1, :]
    h_ref[rows, :] = h.astype(BF16)


def _inproj_full_body(*refs, rope, tiles_per_seq, n_cast):
    refs = list(refs)
    x_ref, xp_ref, xn_ref, mod_ref, g_ref, w_ref, wmix_ref, psc_ref = refs[:8]
    del refs[:8]
    if rope:
        cos_ref, sp_ref, sn_ref = refs[:3]
        del refs[:3]
    cast_in = refs[:n_cast]
    q_ref, kd_ref, vd_ref, rx_ref, ry_ref, po_ref, gt_ref = refs[n_cast:n_cast + 7]
    cast_out = refs[n_cast + 7:2 * n_cast + 7]
    h_ref = refs[2 * n_cast + 7]
    for src, dst in zip(cast_in, cast_out):
        dst[...] = src[...].astype(BF16)
    quarter = HEAD_DIM // 4
    tm, d = x_ref.shape
    halo = SUBLANES
    tile_in_seq = pl.program_id(0) % tiles_per_seq
    pu0 = ATTN_W + 2 * KV_W + 2 * RNN_W

    def modulated(xf):
        y = _rms(xf) * g_ref[...]
        return y * (1.0 + mod_ref[1:2, :]) + mod_ref[0:1, :]

    pack = 2 * SUBLANES
    zeros = jnp.zeros((SUBLANES, d), F32)
    before = jnp.where(tile_in_seq == 0, 0.0, modulated(xp_ref[...]))
    after = jnp.where(tile_in_seq == tiles_per_seq - 1, 0.0, modulated(xn_ref[...]))
    h_ref[0:pack, :] = jnp.concatenate([zeros, before], axis=0).astype(BF16)
    h_ref[pack + tm:2 * pack + tm, :] = jnp.concatenate([after, zeros], axis=0).astype(BF16)
    for rows in _row_halves(tm):
        h_ref[pack + rows.start:pack + rows.stop, :] = modulated(x_ref[rows, :]).astype(BF16)

    for rows in _row_halves(tm):
        r0, n_rows = rows.start, rows.stop - rows.start
        hrows = slice(pack + r0, pack + rows.stop)

        def proj(c0, c1):
            return jnp.dot(h_ref[hrows, :], w_ref[:, c0:c1], preferred_element_type=F32)

        def rotary(pc):
            if not rope:
                return pc
            return (pc * cos_ref[rows, :]
                    + pltpu.roll(pc, quarter, 1) * sp_ref[rows, :]
                    + pltpu.roll(pc, LANES - quarter, 1) * sn_ref[rows, :])

        padded = jnp.dot(h_ref[r0:r0 + n_rows + 2 * pack, :], w_ref[:, pu0:pu0 + POOL_W],
                         preferred_element_type=F32)[pack - halo:pack + n_rows + halo]
        t = tile_in_seq * tm + r0 + lax.broadcasted_iota(jnp.int32, (n_rows, POOL_GW), 0)

        def pool_group(gi):
            lanes = slice(gi * POOL_GW, (gi + 1) * POOL_GW)
            po_ref[rows, lanes] = _pool_group(padded[:, lanes], t, tiles_per_seq * tm, POOL_WINDOWS[gi],
                                              wmix_ref[gi], psc_ref[:, lanes])

        def seg_q(c):
            pair = proj(c * 2 * LANES, (c + 1) * 2 * LANES)
            for half in range(2):
                col = (2 * c + half) * LANES
                pc = rotary(pair[:, half * LANES:(half + 1) * LANES])
                q_ref[rows, col:col + LANES] = (pc * (HEAD_DIM ** -0.5 * LOG2E)).astype(BF16)

        def seg_kv():
            kv = proj(ATTN_W, ATTN_W + 2 * KV_W)
            _store_dup_heads(kd_ref, rows, rotary(kv[:, :KV_W]))
            _store_dup_heads(vd_ref, rows, kv[:, KV_W:])

        rx0 = ATTN_W + 2 * KV_W

        def seg_rx():
            rx_ref[rows, :] = proj(rx0, rx0 + RNN_W)

        def seg_ry():
            ry_ref[rows, :] = jax.nn.gelu(proj(rx0 + RNN_W, rx0 + 2 * RNN_W)).astype(BF16)

        gl0 = pu0 + POOL_W

        def seg_gate(c):
            gl = proj(gl0 + c * d, gl0 + (c + 1) * d)
            gt_ref[rows, c * d:(c + 1) * d] = jax.nn.sigmoid(gl).astype(BF16)

        seg_gate(0)
        pool_group(0)
        seg_q(0)
        seg_gate(1)
        pool_group(1)
        seg_q(1)
        seg_gate(2)
        pool_group(2)
        seg_kv()
        seg_ry()
        pool_group(3)
        seg_rx()


def _inproj_kv_body(x_ref, mod_ref, g_ref, w_ref, kd_ref, vd_ref, rx_ref, h_ref):
    rows = slice(0, x_ref.shape[0])
    _norm_modulate_store(x_ref, mod_ref, g_ref, h_ref, 0, 1, rows)
    kv = jnp.dot(h_ref[...], w_ref[:, ATTN_W:ATTN_W + 2 * KV_W], preferred_element_type=F32)
    _store_dup_heads(kd_ref, rows, kv[:, :KV_W])
    _store_dup_heads(vd_ref, rows, kv[:, KV_W:])
    rx_ref[...] = jnp.dot(h_ref[...], w_ref[:, ATTN_W + 2 * KV_W:ATTN_W + 2 * KV_W + RNN_W],
                          preferred_element_type=F32)


def _cast_slab(weight, n_steps):
    rows = weight.shape[1]
    for hold in (1, 2, 4, 8):
        if n_steps % hold == 0 and rows % (n_steps // hold) == 0 and (rows // (n_steps // hold)) % 16 == 0:
            return rows // (n_steps // hold), hold
    raise ValueError("no bf16-tileable row slab for %s over %d steps" % (weight.shape, n_steps))


def _cast_streams(cast_weights, layer, n_steps):
    in_specs, outs = [], []
    for wgt in cast_weights:
        slab, hold = _cast_slab(wgt, n_steps)
        cols = wgt.shape[2]
        in_specs.append(pl.BlockSpec((None, slab, cols), lambda i, hold=hold: (layer, i // hold, 0)))
        outs.append((pl.BlockSpec((slab, cols), lambda i, hold=hold: (i // hold, 0)),
                     jax.ShapeDtypeStruct(wgt.shape[1:], BF16)))
    return in_specs, outs


def _inproj(x2d, mod_all, layer, mod_row_of_tile, g, w, w_mix, p_scale, *, tm, tiles_per_seq, rope_tables,
            kv_only, cast_weights=()):
    n_tok, d = x2d.shape
    grid = (n_tok // tm,)
    row = lambda i: (i, 0)
    in_specs = [pl.BlockSpec((tm, d), row)]
    args = [x2d]
    if not kv_only:
        per_tile = tm // SUBLANES
        in_specs += [pl.BlockSpec((SUBLANES, d), lambda i: (jnp.maximum(i * per_tile - 1, 0), 0)),
                     pl.BlockSpec((SUBLANES, d),
                                  lambda i: (jnp.minimum((i + 1) * per_tile, n_tok // SUBLANES - 1), 0))]
        args += [x2d, x2d]
    w_cols = ATTN_W + 2 * KV_W + RNN_W if kv_only else w.shape[2]
    in_specs += [_mod_block(mod_all, layer, mod_row_of_tile), _layer_block(g, layer),
                 _layer_block(w, 0, (d, w_cols))]
    args += [mod_all, g, w]
    if not kv_only:
        in_specs += [_layer_block(w_mix, layer), _layer_block(p_scale, layer)]
        args += [w_mix, p_scale]
    if rope_tables is not None:
        in_specs += [pl.BlockSpec((tm, LANES), lambda i: (i % tiles_per_seq, 0))] * 3
        args += list(rope_tables)
    cast_in, cast_outs = _cast_streams(cast_weights, layer, grid[0])
    in_specs += cast_in
    args += list(cast_weights)

    def out(width, dtype):
        return pl.BlockSpec((tm, width), row), jax.ShapeDtypeStruct((n_tok, width), dtype)

    if kv_only:
        outs = [out(2 * KV_W, BF16), out(2 * KV_W, BF16), out(RNN_W, F32)]
        body = _inproj_kv_body
        h_rows = tm
    else:
        outs = [out(ATTN_W, BF16), out(2 * KV_W, BF16), out(2 * KV_W, BF16), out(RNN_W, F32),
                out(RNN_W, BF16), out(POOL_W, BF16), out(N_BRANCH * d, BF16)] + cast_outs
        body = functools.partial(_inproj_full_body, rope=rope_tables is not None,
                                 tiles_per_seq=tiles_per_seq, n_cast=len(cast_outs))
        h_rows = tm + 4 * SUBLANES
    return pl.pallas_call(
        body,
        grid=grid,
        in_specs=in_specs,
        out_specs=[o[0] for o in outs],
        out_shape=[o[1] for o in outs],
        scratch_shapes=[pltpu.VMEM((h_rows, d), BF16)],
        compiler_params=_params(1),
        name="inproj_kv" if kv_only else "inproj",
    )(*args)


def _transpose_values(v_ref, vt_ref, first_block):
    for j in range(v_ref.shape[0] // LANES):
        blk = v_ref[j * LANES:(j + 1) * LANES, :].astype(F32)
        vt_ref[first_block + j] = blk.T.astype(BF16)


def _attend(sink_ref, layer, items):
    heads_per_kv = N_Q_HEADS // N_KV_HEADS
    zero = jnp.zeros((), BF16)

    def scores(item, h):
        q_blk, k_rows, _, biases = item
        low = lax.broadcasted_iota(jnp.int32, (q_blk.shape[0], LANES), 1) < HEAD_DIM
        k_h = k_rows[:, h * LANES:(h + 1) * LANES]
        stacked = []
        for c2 in range(heads_per_kv // 2):
            qc = q_blk[:, (2 * h + c2) * LANES:(2 * h + c2 + 1) * LANES]
            stacked += [jnp.where(low, qc, zero), jnp.where(low, zero, qc)]
        q_rows = jnp.concatenate(stacked, axis=0)
        s = lax.dot_general(k_h, q_rows, (((1,), (1,)), ((), ())), preferred_element_type=F32)
        blocks = []
        for j, bias in enumerate(biases):
            blk = s[j * LANES:(j + 1) * LANES, :]
            if bias is not None:
                blk = blk + jnp.concatenate([bias] * heads_per_kv, axis=1)
            blocks.append(blk)
        return blocks

    def softmax(h, blocks):
        tq = blocks[0].shape[1] // heads_per_kv
        sink_row = jnp.concatenate(
            [jnp.full((1, tq), sink_ref[layer, h * heads_per_kv + g] * LOG2E, F32)
             for g in range(heads_per_kv)],
            axis=1)
        m = jnp.maximum(jnp.max(functools.reduce(jnp.maximum, blocks), axis=0, keepdims=True), sink_row)
        probs = [jnp.exp2(blk - m) for blk in blocks]
        denom = jnp.sum(functools.reduce(jnp.add, probs), axis=0, keepdims=True) + jnp.exp2(sink_row - m)
        return jnp.concatenate(probs, axis=0).astype(BF16), denom

    def weighted(item, h, p, denom):
        tq = item[0].shape[0]
        vt_h = item[2][h * LANES:(h + 1) * LANES, :]
        o = jnp.dot(vt_h, p, preferred_element_type=F32) * (1.0 / denom)
        first_copy = lax.broadcasted_iota(jnp.int32, (LANES, tq), 0) < HEAD_DIM
        return [jnp.where(first_copy, o[:, (2 * c2) * tq:(2 * c2 + 1) * tq],
                          o[:, (2 * c2 + 1) * tq:(2 * c2 + 2) * tq]).T.astype(BF16)
                for c2 in range(heads_per_kv // 2)]

    units = [(item, h) for item in items for h in range(N_KV_HEADS)]
    all_scores = [scores(item, h) for item, h in units]
    all_probs = [softmax(h, blocks) for (_, h), blocks in zip(units, all_scores)]
    all_outs = [weighted(item, h, *pd) for (item, h), pd in zip(units, all_probs)]
    return [jnp.concatenate([o for outs in all_outs[i * N_KV_HEADS:(i + 1) * N_KV_HEADS] for o in outs],
                            axis=1) for i in range(len(items))]


def _attn_local_body(sink_ref, q_ref, kd_ref, vd_ref, kc_ref, vc_ref, o_ref, vt_ref, *, layer, n_blocks,
                     group):
    blk = ATTN_BLOCK
    n_ctx_blocks = kc_ref.shape[0] // LANES
    _transpose_values(vd_ref, vt_ref, 0)
    _transpose_values(vc_ref, vt_ref, n_blocks)
    key = lax.broadcasted_iota(jnp.int32, (blk, blk), 0)
    qry = lax.broadcasted_iota(jnp.int32, (blk, blk), 1)
    rows = lambda i: pl.ds(pl.multiple_of(i * blk, blk), blk)

    def item(n):
        prv, nxt = jnp.maximum(n - 1, 0), jnp.minimum(n + 1, n_blocks - 1)
        bias_prev = jnp.where((key >= qry) & (n > 0), 0.0, NEG_INF).astype(F32)
        bias_next = jnp.where((key <= qry) & (n < n_blocks - 1), 0.0, NEG_INF).astype(F32)
        k_rows = jnp.concatenate([kd_ref[rows(prv), :], kd_ref[rows(n), :], kd_ref[rows(nxt), :],
                                  kc_ref[...]], axis=0)
        vt = jnp.concatenate([vt_ref[prv], vt_ref[n], vt_ref[nxt]]
                             + [vt_ref[n_blocks + j] for j in range(n_ctx_blocks)], axis=1)
        return (q_ref[rows(n), :], k_rows, vt, [bias_prev, None, bias_next] + [None] * n_ctx_blocks)

    def body(i, carry):
        blocks = [i * group + g for g in range(group)]
        for n, out in zip(blocks, _attend(sink_ref, layer, [item(n) for n in blocks])):
            o_ref[rows(n), :] = out
        return carry

    lax.fori_loop(0, n_blocks // group, body, 0)


def _attn_ctx_body(sink_ref, q_ref, kc_ref, vc_ref, o_ref, vt_ref, *, layer):
    n_key_blocks = kc_ref.shape[0] // LANES
    _transpose_values(vc_ref, vt_ref, 0)
    vt = jnp.concatenate([vt_ref[j] for j in range(n_key_blocks)], axis=1)
    items = [(q_ref[i * ATTN_BLOCK:(i + 1) * ATTN_BLOCK, :], kc_ref[...], vt, [None] * n_key_blocks)
             for i in range(q_ref.shape[0] // ATTN_BLOCK)]
    for i, out in enumerate(_attend(sink_ref, layer, items)):
        o_ref[i * ATTN_BLOCK:(i + 1) * ATTN_BLOCK, :] = out


def _attention(sink, layer, q, kd, vd, kdc, vdc):
    b, s, _ = q.shape
    c = kdc.shape[1]
    kw = 2 * KV_W
    n_blocks = s // ATTN_BLOCK
    group = math.gcd(n_blocks, 4)
    idx = lambda bi: (bi, 0, 0)
    return pl.pallas_call(
        functools.partial(_attn_local_body, layer=layer, n_blocks=n_blocks, group=group),
        grid=(b,),
        in_specs=[
            pl.BlockSpec(memory_space=pltpu.SMEM),
            pl.BlockSpec((None, s, ATTN_W), idx),
            pl.BlockSpec((None, s, kw), idx), pl.BlockSpec((None, s, kw), idx),
            pl.BlockSpec((None, c, kw), idx), pl.BlockSpec((None, c, kw), idx),
        ],
        out_specs=pl.BlockSpec((None, s, ATTN_W), idx),
        out_shape=jax.ShapeDtypeStruct((b, s, ATTN_W), BF16),
        scratch_shapes=[pltpu.VMEM((n_blocks + c // LANES, kw, LANES), BF16)],
        compiler_params=_params(1),
        name="attn_local",
    )(sink, q, kd, vd, kdc, vdc)


def _attention_ctx(sink, layer, qc, kdc, vdc):
    b, c, _ = qc.shape
    kw = 2 * KV_W
    idx = lambda bi: (bi, 0, 0)
    return pl.pallas_call(
        functools.partial(_attn_ctx_body, layer=layer),
        grid=(b,),
        in_specs=[
            pl.BlockSpec(memory_space=pltpu.SMEM),
            pl.BlockSpec((None, c, ATTN_W), idx),
            pl.BlockSpec((None, c, kw), idx), pl.BlockSpec((None, c, kw), idx),
        ],
        out_specs=pl.BlockSpec((None, c, ATTN_W), idx),
        out_shape=jax.ShapeDtypeStruct((b, c, ATTN_W), BF16),
        scratch_shapes=[pltpu.VMEM((c // LANES, kw, LANES), BF16)],
        compiler_params=_params(1),
        name="attn_ctx",
    )(sink, qc, kdc, vdc)


def _pool_group(padded, t, length, window, w_mix, scale):
    halo = SUBLANES
    rows = padded.shape[0]
    n_rows = rows - 2 * halo
    acc = padded
    step = 1
    while step < window:
        acc = acc + pltpu.roll(acc, rows - step, 0)
        step *= 2
    back = (window - 1) // 2
    win = (pltpu.roll(acc, back, 0) if back else acc)[halo:halo + n_rows]
    lo = jnp.maximum(t - back, 0)
    hi = jnp.minimum(t + window // 2 + 1, length)
    mean = win / (hi - lo).astype(F32)
    dlt = (mean - padded[halo:halo + n_rows]).astype(BF16)
    return (jnp.dot(dlt, w_mix, preferred_element_type=F32) * scale).astype(BF16)


def _lru_body(*refs, n_ctx, n_lat, tile, with_ctx_out):
    if with_ctx_out:
        (rxc_ref, rxl_ref, ryc_ref, ryl_ref, cw_ref, cb_ref, wg_ref, ba_ref, bx_ref, lam_ref,
         ol_ref, oc_ref, pad_ref, xl_ref, hf_ref, hb_ref, xp_ref, a_ref, b_ref) = refs
    else:
        (rxc_ref, rxl_ref, ryl_ref, cw_ref, cb_ref, wg_ref, ba_ref, bx_ref, lam_ref,
         ol_ref, pad_ref, xl_ref, hf_ref, hb_ref, xp_ref, a_ref, b_ref) = refs
    halo = SUBLANES
    n_tot = n_ctx + n_lat
    n_slab = RNN_W // LANES

    zeros = jnp.zeros((halo, RNN_W), F32)
    c0, l0 = halo, 2 * halo + n_ctx
    pad_ref[0:halo, :] = zeros
    pad_ref[c0 + n_ctx:l0, :] = zeros
    pad_ref[l0 + n_lat:l0 + n_lat + halo, :] = zeros
    pad_ref[c0:c0 + n_ctx, :] = rxc_ref[...]
    pad_ref[l0:l0 + n_lat, :] = rxl_ref[...]

    cw = 0.5 * cw_ref[...]
    cb = 0.5 * cb_ref[...]

    def conv_tile(src0, dst_rows):
        rows = tile + 2 * halo
        p = pad_ref[src0 - halo:src0 + tile + halo, :]
        y = (cb + p[halo:halo + tile] * cw[1:2, :]
             + pltpu.roll(p, 1, 0)[halo:halo + tile] * cw[0:1, :]
             + pltpu.roll(p, rows - 1, 0)[halo:halo + tile] * cw[2:3, :]
             + pltpu.roll(p, rows - 2, 0)[halo:halo + tile] * cw[3:4, :])
        for s in range(n_slab):
            for d0 in dst_rows:
                xl_ref[s, d0:d0 + tile, :] = y[:, s * LANES:(s + 1) * LANES]

    for t0 in range(0, n_ctx, tile):
        conv_tile(c0 + t0, (t0, n_tot + t0))
    for t0 in range(0, n_lat, tile):
        conv_tile(l0 + t0, (n_ctx + t0,))

    lam = lam_ref[...]
    log_sig = jnp.minimum(lam, 0.0) - jnp.log1p(jnp.exp(-jnp.abs(lam)))
    rate = (0.5 * LOG2E * LRU_C) * log_sig
    half_ba = 0.5 * ba_ref[...]
    half_bx = 0.5 * bx_ref[...]
    half = RNN_W // 2

    def run_chunk(d, base, carry):
        res_ref = hf_ref if d == 0 else hb_ref
        win0 = 0 if d == 0 else n_ctx
        for v in range(SEG):
            for s in range(n_slab):
                xp_ref[d, v * SUBLANES:(v + 1) * SUBLANES, s * LANES:(s + 1) * LANES] = (
                    xl_ref[s, pl.ds(win0 + base + v, SUBLANES, stride=SEG), :])
        for j in range(2):
            ch = slice(j * half, (j + 1) * half)
            xj = xp_ref[d, :, ch]
            gates = jnp.dot(xj.astype(BF16), wg_ref[d, j], preferred_element_type=F32)
            t_r = jnp.tanh(gates[:, :half] + half_ba[d:d + 1, ch])
            t_i = jnp.tanh(gates[:, half:] + half_bx[d:d + 1, ch])
            a = jnp.exp2(t_r * rate[d:d + 1, ch] + rate[d:d + 1, ch])
            gap = 1.0 - a * a
            mult = gap * lax.rsqrt(jnp.maximum(gap, TINY))
            a_ref[d, :, ch] = a
            b_ref[d, :, ch] = (mult * xj) * (t_i + 1.0)
        order = range(SEG) if d == 0 else range(SEG - 1, -1, -1)
        h = jnp.zeros((SUBLANES, RNN_W), F32)
        acum = jnp.ones((SUBLANES, RNN_W), F32)
        for v in order:
            rows = slice(v * SUBLANES, (v + 1) * SUBLANES)
            av = a_ref[d, rows, :]
            h = av * h + b_ref[d, rows, :]
            acum = av * acum
            b_ref[d, rows, :] = h
            a_ref[d, rows, :] = acum
        seg_in = [None] * SUBLANES
        state = carry
        seg_order = range(SUBLANES) if d == 0 else range(SUBLANES - 1, -1, -1)
        for i in seg_order:
            seg_in[i] = state
            state = h[i:i + 1, :] + acum[i:i + 1, :] * state
        seg_state = jnp.concatenate(seg_in, axis=0)
        for v in range(SEG):
            rows = slice(v * SUBLANES, (v + 1) * SUBLANES)
            hv = b_ref[d, rows, :] + a_ref[d, rows, :] * seg_state
            for s in range(n_slab):
                res_ref[s, pl.ds(base + v, SUBLANES, stride=SEG), :] = hv[:, s * LANES:(s + 1) * LANES]
        return state

    n_chunks = n_tot // CHUNK

    def step(c, carries):
        cf, cb = carries
        cf = run_chunk(0, c * CHUNK, cf)
        cb = run_chunk(1, (n_chunks - 1 - c) * CHUNK, cb)
        return cf, cb

    zero_state = jnp.zeros((1, RNN_W), F32)
    lax.fori_loop(0, n_chunks, step, (zero_state, zero_state))

    def emit(out_ref, ry_ref, f0, b0, length):
        for t0 in range(0, length, tile):
            hsum = jnp.concatenate(
                [hf_ref[s, f0 + t0:f0 + t0 + tile, :] + hb_ref[s, b0 + t0:b0 + t0 + tile, :]
                 for s in range(n_slab)], axis=1)
            gate = ry_ref[t0:t0 + tile, :].astype(F32)
            out_ref[t0:t0 + tile, :] = (hsum * gate).astype(BF16)

    emit(ol_ref, ryl_ref, n_ctx, 0, n_lat)
    if with_ctx_out:
        emit(oc_ref, ryc_ref, 0, n_lat, n_ctx)


def _lru(layer, rxc, rxl, ryc, ryl, conv_w, conv_b, wg, b_a, b_x, lam):
    b, n_ctx, _ = rxc.shape
    n_lat = rxl.shape[1]
    n_tot = n_ctx + n_lat
    tile = math.gcd(n_ctx, n_lat, 256)
    assert n_tot % CHUNK == 0 and tile % SUBLANES == 0
    with_ctx_out = ryc is not None
    idx = lambda bi: (bi, 0, 0)
    seq = lambda n, dt=None: pl.BlockSpec((None, n, RNN_W), idx)
    n_slab = RNN_W // LANES
    in_specs = [seq(n_ctx), seq(n_lat)] + ([seq(n_ctx)] if with_ctx_out else []) + [seq(n_lat)]
    args = [rxc, rxl] + ([ryc] if with_ctx_out else []) + [ryl]
    for wgt in (conv_w, conv_b, wg, b_a, b_x, lam):
        in_specs.append(_layer_block(wgt, layer))
        args.append(wgt)
    out_specs = [seq(n_lat)] + ([seq(n_ctx)] if with_ctx_out else [])
    out_shape = [jax.ShapeDtypeStruct((b, n_lat, RNN_W), BF16)]
    if with_ctx_out:
        out_shape.append(jax.ShapeDtypeStruct((b, n_ctx, RNN_W), BF16))
    res = pl.pallas_call(
        functools.partial(_lru_body, n_ctx=n_ctx, n_lat=n_lat, tile=tile, with_ctx_out=with_ctx_out),
        grid=(b,),
        in_specs=in_specs,
        out_specs=out_specs,
        out_shape=out_shape,
        scratch_shapes=[
            pltpu.VMEM((n_tot + 3 * SUBLANES, RNN_W), F32),
            pltpu.VMEM((n_slab, n_tot + n_ctx, LANES), F32),
            pltpu.VMEM((n_slab, n_tot, LANES), F32),
            pltpu.VMEM((n_slab, n_tot, LANES), F32),
            pltpu.VMEM((2, CHUNK, RNN_W), F32),
            pltpu.VMEM((2, CHUNK, RNN_W), F32),
            pltpu.VMEM((2, CHUNK, RNN_W), F32),
        ],
        compiler_params=_params(1),
        name="rglru",
    )(*args)
    return (res[0], res[1]) if with_ctx_out else (res[0], None)


def _mix_ffn_body(*refs, d_ff, chunk, n_cast):
    (x_ref, mod_ref, at_ref, po_ref, rn_ref, gt_ref, wa_ref, wp_ref, wr_ref, wo_ref,
     gmix_ref, gpre_ref, wgu_ref, wd_ref, gpost_ref) = refs[:15]
    cast_in = refs[15:15 + n_cast]
    o_ref = refs[15 + n_cast]
    cast_out = refs[16 + n_cast:16 + 2 * n_cast]
    m_ref, h_ref = refs[16 + 2 * n_cast:]
    for src, dst in zip(cast_in, cast_out):
        dst[...] = src[...].astype(BF16)
    d = x_ref.shape[1]
    for rows in _row_halves(x_ref.shape[0]):
        merged = None
        for k, (br_ref, w_ref) in enumerate(((at_ref, wa_ref), (po_ref, wp_ref), (rn_ref, wr_ref))):
            t = gt_ref[rows, k * d:(k + 1) * d].astype(F32) * jnp.dot(
                br_ref[rows, :], w_ref[...], preferred_element_type=F32)
            merged = t if merged is None else merged + t
        m_ref[rows, :] = merged.astype(BF16)
        mix = jnp.dot(m_ref[rows, :], wo_ref[...], preferred_element_type=F32)
        o_ref[rows, :] = x_ref[rows, :] + mod_ref[2:3, :] * (_rms(mix) * gmix_ref[...])
        _norm_modulate_store(o_ref, mod_ref, gpre_ref, h_ref, 3, 4, rows)

    halves = _row_halves(x_ref.shape[0])
    accs = [None] * len(halves)
    for c0 in range(0, d_ff, chunk):
        c1 = min(c0 + chunk, d_ff)
        for i, rows in enumerate(halves):
            gate = jnp.dot(h_ref[rows, :], wgu_ref[:, c0:c1], preferred_element_type=F32)
            up = jnp.dot(h_ref[rows, :], wgu_ref[:, d_ff + c0:d_ff + c1], preferred_element_type=F32)
            act = ((gate * jax.nn.sigmoid(gate)) * up).astype(BF16)
            part = jnp.dot(act, wd_ref[c0:c1, :], preferred_element_type=F32)
            accs[i] = part if accs[i] is None else accs[i] + part
    for acc, rows in zip(accs, halves):
        o_ref[rows, :] = o_ref[rows, :] + mod_ref[5:6, :] * (_rms(acc) * gpost_ref[...])


def _mix_ffn(x2d, mod_all, layer, mod_row_of_tile, attn, pool, rnn, gates, wa, wp, wr, wo, g_mix,
             g_pre, w_gu, w_down, g_post, *, tm, cast_weights=(), cast_layer=0):
    n_tok, d = x2d.shape
    d_ff = w_down.shape[0]
    row = lambda i: (i, 0)
    tok = lambda arr: pl.BlockSpec((tm, arr.shape[1]), row)
    own = lambda wgt: (wgt[None], 0)
    params = (own(wa), own(wp), own(wr), own(wo), (g_mix, layer), (g_pre, layer), own(w_gu), own(w_down),
              (g_post, layer))
    cast_in, cast_out = _cast_streams(cast_weights, cast_layer, n_tok // tm)
    res = pl.pallas_call(
        functools.partial(_mix_ffn_body, d_ff=d_ff, chunk=1024, n_cast=len(cast_weights)),
        grid=(n_tok // tm,),
        in_specs=[tok(x2d), _mod_block(mod_all, layer, mod_row_of_tile),
                  tok(attn), tok(pool), tok(rnn), tok(gates)]
                 + [_layer_block(p, lyr) for p, lyr in params] + cast_in,
        out_specs=[pl.BlockSpec((tm, d), row)] + [o[0] for o in cast_out],
        out_shape=[jax.ShapeDtypeStruct((n_tok, d), F32)] + [o[1] for o in cast_out],
        scratch_shapes=[pltpu.VMEM((tm, d), BF16), pltpu.VMEM((tm, d), BF16)],
        compiler_params=_params(1),
        name="mix_ffn",
    )(x2d, mod_all, attn, pool, rnn, gates, *[p for p, _ in params], *cast_weights)
    return res[0] if not cast_weights else res


def _pack_gate_weights(w_a, w_x):
    def block_diag(w):
        bw = RNN_W // RNN_BLOCKS
        rows = [jnp.pad(w[:, :, h], ((0, 0), (0, 0), (0, 0), (bw * h, RNN_W - bw * (h + 1))))
                for h in range(RNN_BLOCKS)]
        return jnp.concatenate(rows, axis=2)
    bd_a, bd_x = block_diag(w_a), block_diag(w_x)
    half = RNN_W // 2
    halves = []
    for j in range(2):
        sl = slice(j * half, (j + 1) * half)
        halves.append(jnp.concatenate([bd_a[:, :, sl, sl], bd_x[:, :, sl, sl]], axis=3))
    return jnp.stack(halves, axis=2).astype(BF16)


def kernel(x, c, ctx, c_ctx, w_ada, b_ada, g_pre_mix, g_post_mix, g_pre_ffn, g_post_ffn, w_in, attn_sink, w_attn_o, pool_mix, pool_scale, w_pool_o, conv_w, conv_b, lru_w_a, lru_b_a, lru_w_x, lru_b_x, lru_lambda, w_rnn_o, w_out, w_gu, w_down):
    bsz, seq, d = x.shape
    n_ctx = ctx.shape[1]
    depth = w_ada.shape[0]
    tm = min(512, seq)
    tm_ctx = min(tm, n_ctx)

    mod_rows = -(-(bsz + 1) // SUBLANES) * SUBLANES
    cvec = jnp.zeros((mod_rows, d), F32).at[:bsz].set(c).at[bsz].set(c_ctx)
    mod_all = _adaln_mod(cvec, w_ada, b_ada).reshape(depth, mod_rows, 6, d)

    rope = _rope_tables(seq)
    lat_row = lambda i: i // (seq // tm)
    ctx_row = lambda i: bsz

    rowvec = lambda a: a.reshape(depth, 1, -1)
    w_in_l = w_in[0].astype(BF16)
    channel_weights = (w_attn_o, w_pool_o, w_rnn_o, w_out, w_gu, w_down)
    w_mix = pool_mix.astype(BF16)
    wg = _pack_gate_weights(lru_w_a, lru_w_x)
    g_pre, g_post = rowvec(g_pre_mix), rowvec(g_post_mix)
    gf_pre, gf_post = rowvec(g_pre_ffn), rowvec(g_post_ffn)
    p_scale, cv_b = rowvec(pool_scale), rowvec(conv_b)

    x2 = x.reshape(bsz * seq, d)
    c2 = ctx.reshape(bsz * n_ctx, d)
    r3 = lambda a, n: a.reshape(bsz, n, a.shape[-1])
    flat = lambda a: a.reshape(-1, a.shape[-1])
    for l in range(depth):
        need_ctx = l < depth - 1
        q, kd, vd, rx, ry, pool_l, gt, wa, wp, wr, wo, w_gu_b, w_dn_b = _inproj(
            x2, mod_all, l, lat_row, g_pre, w_in_l[None], w_mix, p_scale, tm=tm, tiles_per_seq=seq // tm,
            rope_tables=rope, kv_only=False, cast_weights=channel_weights)
        if need_ctx:
            qc, kdc, vdc, rxc, ryc, pool_c, gtc = _inproj(
                c2, mod_all, l, ctx_row, g_pre, w_in_l[None], w_mix, p_scale, tm=tm_ctx,
                tiles_per_seq=n_ctx // tm_ctx, rope_tables=None, kv_only=False)
        else:
            kdc, vdc, rxc = _inproj(
                c2, mod_all, l, ctx_row, g_pre, w_in_l[None], None, None, tm=tm_ctx, tiles_per_seq=1,
                rope_tables=None, kv_only=True)
            ryc = None

        kdc3, vdc3 = r3(kdc, n_ctx), r3(vdc, n_ctx)
        attn_l = _attention(attn_sink, l, r3(q, seq), r3(kd, seq), r3(vd, seq), kdc3, vdc3)
        rnn_l, rnn_c = _lru(l, r3(rxc, n_ctx), r3(rx, seq), None if ryc is None else r3(ryc, n_ctx),
                            r3(ry, seq), conv_w, cv_b, wg, lru_b_a, lru_b_x, lru_lambda)

        if need_ctx:
            x2, w_in_next = _mix_ffn(x2, mod_all, l, lat_row, flat(attn_l), pool_l, flat(rnn_l), gt, wa, wp,
                                     wr, wo, g_post, gf_pre, w_gu_b, w_dn_b, gf_post, tm=tm,
                                     cast_weights=(w_in,), cast_layer=l + 1)
        else:
            x2 = _mix_ffn(x2, mod_all, l, lat_row, flat(attn_l), pool_l, flat(rnn_l), gt, wa, wp, wr, wo,
                          g_post, gf_pre, w_gu_b, w_dn_b, gf_post, tm=tm)

        if need_ctx:
            attn_c = _attention_ctx(attn_sink, l, r3(qc, n_ctx), kdc3, vdc3)
            c2 = _mix_ffn(c2, mod_all, l, ctx_row, flat(attn_c), pool_c, flat(rnn_c), gtc, wa, wp, wr,
                          wo, g_post, gf_pre, w_gu_b, w_dn_b, gf_post, tm=tm_ctx)
            w_in_l = w_in_next
    return x2.reshape(bsz, seq, d)
```

```python
import functools
import math

import jax
import jax.numpy as jnp
import numpy as np
from jax import lax
from jax.experimental import pallas as pl
from jax.experimental.pallas import tpu as pltpu

F32 = jnp.float32
BF16 = jnp.bfloat16

GRID_W = 64
HEAD_DIM = 64
N_Q_HEADS = 8
N_KV_HEADS = 2
ATTN_W = N_Q_HEADS * HEAD_DIM
KV_W = N_KV_HEADS * HEAD_DIM
ATTN_BLOCK = 128
ROPE_BASE = 10000.0
POOL_WINDOWS = (2, 4, 8, 16)
POOL_W = 512
POOL_GW = POOL_W // len(POOL_WINDOWS)
RNN_W = 512
RNN_BLOCKS = 8
CONV_W = 4
LRU_C = 8.0
N_BRANCH = 3
EPS = 1e-6
NEG_INF = -1e30
LOG2E = 1.4426950408889634
TINY = 1e-30

LANES = 128
SUBLANES = 8
VMEM_LIMIT_BYTES = 56 * 1024 * 1024

SEG = 36
CHUNK = SUBLANES * SEG


def _params(n_axes):
    return pltpu.CompilerParams(
        dimension_semantics=("arbitrary",) * n_axes, vmem_limit_bytes=VMEM_LIMIT_BYTES)


def _layer_block(arr, layer, block=None):
    block = tuple(arr.shape[1:]) if block is None else block
    index = (layer,) + (0,) * len(block)
    return pl.BlockSpec((None,) + block, lambda *_: index, pipeline_mode=pl.Buffered(1))


def _mod_block(mod_all, layer, mod_row_of_tile):
    return pl.BlockSpec((None, None) + mod_all.shape[2:], lambda i: (layer, mod_row_of_tile(i), 0, 0))


def _rms(xf):
    return xf * lax.rsqrt(jnp.mean(xf * xf, axis=-1, keepdims=True) + EPS)


def _mod_body(c_ref, w_ref, b_ref, o_ref):
    c = c_ref[...]
    s = (c * jax.nn.sigmoid(c)).astype(BF16)
    o_ref[0] = jnp.dot(s, w_ref[0].astype(BF16), preferred_element_type=F32) + b_ref[0]


def _adaln_mod(cvec, w_ada, b_ada):
    n_layers, d, n6 = w_ada.shape
    rows = cvec.shape[0]
    tn = n6 // 4
    return pl.pallas_call(
        _mod_body,
        grid=(n_layers, n6 // tn),
        in_specs=[
            pl.BlockSpec((rows, d), lambda l, j: (0, 0)),
            pl.BlockSpec((1, d, tn), lambda l, j: (l, 0, j)),
            pl.BlockSpec((1, 1, tn), lambda l, j: (l, 0, j)),
        ],
        out_specs=pl.BlockSpec((1, rows, tn), lambda l, j: (l, 0, j)),
        out_shape=jax.ShapeDtypeStruct((n_layers, rows, n6), F32),
        compiler_params=_params(2),
        name="adaln_mod",
    )(cvec, w_ada, b_ada.reshape(n_layers, 1, n6))


def _rope_tables(seq_len):
    pos = np.arange(seq_len)
    row = (pos // GRID_W).astype(np.float32)
    col = (pos % GRID_W).astype(np.float32)
    half = HEAD_DIM // 2
    quarter = half // 2
    inv = (np.float32(ROPE_BASE) ** (-(np.arange(quarter, dtype=np.float32) * np.float32(2.0 / half)))
           ).astype(np.float32)
    j = np.arange(LANES) % HEAD_DIM
    is_col = j >= half
    second = (j % half) >= quarter
    freq = inv[j % quarter]
    ang = (np.where(is_col[None, :], col[:, None], row[:, None]) * freq[None, :]).astype(np.float32)
    cos, sin = np.cos(ang), np.sin(ang)
    sin_prev = np.where(second[None, :], sin, 0.0)
    sin_next = np.where(second[None, :], 0.0, -sin)
    return tuple(jnp.asarray(t, F32) for t in (cos, sin_prev, sin_next))


def _row_halves(n_rows):
    if n_rows % (2 * SUBLANES * 2) or n_rows < 256:
        return [slice(0, n_rows)]
    return [slice(0, n_rows // 2), slice(n_rows // 2, n_rows)]


def _store_dup_heads(dst_ref, rows, kv):
    low = lax.broadcasted_iota(jnp.int32, kv.shape, 1) < HEAD_DIM
    swapped = pltpu.roll(kv, HEAD_DIM, 1)
    dst_ref[rows, 0:LANES] = jnp.where(low, kv, swapped).astype(BF16)
    dst_ref[rows, LANES:2 * LANES] = jnp.where(low, swapped, kv).astype(BF16)


def _norm_modulate_store(x_ref, mod_ref, g_ref, h_ref, shift_row, scale_row, rows):
    y = _rms(x_ref[rows, :]) * g_ref[...]
    h = y * (1.0 + mod_ref[scale_row:scale_row + 1, :]) + mod_ref[shift_row:shift_row + 1, :]
    h_ref[rows, :] = h.astype(BF16)


def _inproj_full_body(*refs, rope, tiles_per_seq, n_cast):
    refs = list(refs)
    x_ref, xp_ref, xn_ref, mod_ref, g_ref, w_ref, wmix_ref, psc_ref = refs[:8]
    del refs[:8]
    if rope:
        cos_ref, sp_ref, sn_ref = refs[:3]
        del refs[:3]
    cast_in = refs[:n_cast]
    q_ref, kd_ref, vd_ref, rx_ref, ry_ref, po_ref, gt_ref = refs[n_cast:n_cast + 7]
    cast_out = refs[n_cast + 7:2 * n_cast + 7]
    h_ref = refs[2 * n_cast + 7]
    for src, dst in zip(cast_in, cast_out):
        dst[...] = src[...].astype(BF16)
    quarter = HEAD_DIM // 4
    tm, d = x_ref.shape
    halo = SUBLANES
    tile_in_seq = pl.program_id(0) % tiles_per_seq
    pu0 = ATTN_W + 2 * KV_W + 2 * RNN_W

    def modulated(xf):
        y = _rms(xf) * g_ref[...]
        return y * (1.0 + mod_ref[1:2, :]) + mod_ref[0:1, :]

    pack = 2 * SUBLANES
    zeros = jnp.zeros((SUBLANES, d), F32)
    before = jnp.where(tile_in_seq == 0, 0.0, modulated(xp_ref[...]))
    after = jnp.where(tile_in_seq == tiles_per_seq - 1, 0.0, modulated(xn_ref[...]))
    h_ref[0:pack, :] = jnp.concatenate([zeros, before], axis=0).astype(BF16)
    h_ref[pack + tm:2 * pack + tm, :] = jnp.concatenate([after, zeros], axis=0).astype(BF16)
    for rows in _row_halves(tm):
        h_ref[pack + rows.start:pack + rows.stop, :] = modulated(x_ref[rows, :]).astype(BF16)

    for rows in _row_halves(tm):
        r0, n_rows = rows.start, rows.stop - rows.start
        hrows = slice(pack + r0, pack + rows.stop)

        def proj(c0, c1):
            return jnp.dot(h_ref[hrows, :], w_ref[:, c0:c1], preferred_element_type=F32)

        def rotary(pc):
            if not rope:
                return pc
            return (pc * cos_ref[rows, :]
                    + pltpu.roll(pc, quarter, 1) * sp_ref[rows, :]
                    + pltpu.roll(pc, LANES - quarter, 1) * sn_ref[rows, :])

        padded = jnp.dot(h_ref[r0:r0 + n_rows + 2 * pack, :], w_ref[:, pu0:pu0 + POOL_W],
                         preferred_element_type=F32)[pack - halo:pack + n_rows + halo]
        t = tile_in_seq * tm + r0 + lax.broadcasted_iota(jnp.int32, (n_rows, POOL_GW), 0)

        def pool_group(gi):
            lanes = slice(gi * POOL_GW, (gi + 1) * POOL_GW)
            po_ref[rows, lanes] = _pool_group(padded[:, lanes], t, tiles_per_seq * tm, POOL_WINDOWS[gi],
                                              wmix_ref[gi], psc_ref[:, lanes])

        def seg_q(c):
            pair = proj(c * 2 * LANES, (c + 1) * 2 * LANES)
            for half in range(2):
                col = (2 * c + half) * LANES
                pc = rotary(pair[:, half * LANES:(half + 1) * LANES])
                q_ref[rows, col:col + LANES] = (pc * (HEAD_DIM ** -0.5 * LOG2E)).astype(BF16)

        def seg_kv():
            kv = proj(ATTN_W, ATTN_W + 2 * KV_W)
            _store_dup_heads(kd_ref, rows, rotary(kv[:, :KV_W]))
            _store_dup_heads(vd_ref, rows, kv[:, KV_W:])

        rx0 = ATTN_W + 2 * KV_W

        def seg_rx():
            rx_ref[rows, :] = proj(rx0, rx0 + RNN_W)

        def seg_ry():
            ry_ref[rows, :] = jax.nn.gelu(proj(rx0 + RNN_W, rx0 + 2 * RNN_W)).astype(BF16)

        gl0 = pu0 + POOL_W

        def seg_gate(c):
            gl = proj(gl0 + c * d, gl0 + (c + 1) * d)
            gt_ref[rows, c * d:(c + 1) * d] = jax.nn.sigmoid(gl).astype(BF16)

        seg_gate(0)
        pool_group(0)
        seg_q(0)
        seg_gate(1)
        pool_group(1)
        seg_q(1)
        seg_gate(2)
        pool_group(2)
        seg_kv()
        seg_ry()
        pool_group(3)
        seg_rx()


def _inproj_kv_body(x_ref, mod_ref, g_ref, w_ref, kd_ref, vd_ref, rx_ref, h_ref):
    rows = slice(0, x_ref.shape[0])
    _norm_modulate_store(x_ref, mod_ref, g_ref, h_ref, 0, 1, rows)
    kv = jnp.dot(h_ref[...], w_ref[:, ATTN_W:ATTN_W + 2 * KV_W], preferred_element_type=F32)
    _store_dup_heads(kd_ref, rows, kv[:, :KV_W])
    _store_dup_heads(vd_ref, rows, kv[:, KV_W:])
    rx_ref[...] = jnp.dot(h_ref[...], w_ref[:, ATTN_W + 2 * KV_W:ATTN_W + 2 * KV_W + RNN_W],
                          preferred_element_type=F32)


def _cast_slab(weight, n_steps):
    rows = weight.shape[1]
    for hold in (1, 2, 4, 8):
        if n_steps % hold == 0 and rows % (n_steps // hold) == 0 and (rows // (n_steps // hold)) % 16 == 0:
            return rows // (n_steps // hold), hold
    raise ValueError("no bf16-tileable row slab for %s over %d steps" % (weight.shape, n_steps))


def _cast_streams(cast_weights, layer, n_steps):
    in_specs, outs = [], []
    for wgt in cast_weights:
        slab, hold = _cast_slab(wgt, n_steps)
        cols = wgt.shape[2]
        in_specs.append(pl.BlockSpec((None, slab, cols), lambda i, hold=hold: (layer, i // hold, 0)))
        outs.append((pl.BlockSpec((slab, cols), lambda i, hold=hold: (i // hold, 0)),
                     jax.ShapeDtypeStruct(wgt.shape[1:], BF16)))
    return in_specs, outs


def _inproj(x2d, mod_all, layer, mod_row_of_tile, g, w, w_mix, p_scale, *, tm, tiles_per_seq, rope_tables,
            kv_only, cast_weights=()):
    n_tok, d = x2d.shape
    grid = (n_tok // tm,)
    row = lambda i: (i, 0)
    in_specs = [pl.BlockSpec((tm, d), row)]
    args = [x2d]
    if not kv_only:
        per_tile = tm // SUBLANES
        in_specs += [pl.BlockSpec((SUBLANES, d), lambda i: (jnp.maximum(i * per_tile - 1, 0), 0)),
                     pl.BlockSpec((SUBLANES, d),
                                  lambda i: (jnp.minimum((i + 1) * per_tile, n_tok // SUBLANES - 1), 0))]
        args += [x2d, x2d]
    w_cols = ATTN_W + 2 * KV_W + RNN_W if kv_only else w.shape[2]
    in_specs += [_mod_block(mod_all, layer, mod_row_of_tile), _layer_block(g, layer),
                 _layer_block(w, 0, (d, w_cols))]
    args += [mod_all, g, w]
    if not kv_only:
        in_specs += [_layer_block(w_mix, layer), _layer_block(p_scale, layer)]
        args += [w_mix, p_scale]
    if rope_tables is not None:
        in_specs += [pl.BlockSpec((tm, LANES), lambda i: (i % tiles_per_seq, 0))] * 3
        args += list(rope_tables)
    cast_in, cast_outs = _cast_streams(cast_weights, layer, grid[0])
    in_specs += cast_in
    args += list(cast_weights)

    def out(width, dtype):
        return pl.BlockSpec((tm, width), row), jax.ShapeDtypeStruct((n_tok, width), dtype)

    if kv_only:
        outs = [out(2 * KV_W, BF16), out(2 * KV_W, BF16), out(RNN_W, F32)]
        body = _inproj_kv_body
        h_rows = tm
    else:
        outs = [out(ATTN_W, BF16), out(2 * KV_W, BF16), out(2 * KV_W, BF16), out(RNN_W, F32),
                out(RNN_W, BF16), out(POOL_W, BF16), out(N_BRANCH * d, BF16)] + cast_outs
        body = functools.partial(_inproj_full_body, rope=rope_tables is not None,
                                 tiles_per_seq=tiles_per_seq, n_cast=len(cast_outs))
        h_rows = tm + 4 * SUBLANES
    return pl.pallas_call(
        body,
        grid=grid,
        in_specs=in_specs,
        out_specs=[o[0] for o in outs],
        out_shape=[o[1] for o in outs],
        scratch_shapes=[pltpu.VMEM((h_rows, d), BF16)],
        compiler_params=_params(1),
        name="inproj_kv" if kv_only else "inproj",
    )(*args)


def _transpose_values(v_ref, vt_ref, first_block):
    for j in range(v_ref.shape[0] // LANES):
        blk = v_ref[j * LANES:(j + 1) * LANES, :].astype(F32)
        vt_ref[first_block + j] = blk.T.astype(BF16)


HEADS_PER_KV = N_Q_HEADS // N_KV_HEADS


def _attn_scores(item, h):
    q_blk, k_rows, _, biases = item
    zero = jnp.zeros((), BF16)
    low = lax.broadcasted_iota(jnp.int32, (q_blk.shape[0], LANES), 1) < HEAD_DIM
    k_h = k_rows[:, h * LANES:(h + 1) * LANES]
    stacked = []
    for c2 in range(HEADS_PER_KV // 2):
        qc = q_blk[:, (2 * h + c2) * LANES:(2 * h + c2 + 1) * LANES]
        stacked += [jnp.where(low, qc, zero), jnp.where(low, zero, qc)]
    q_rows = jnp.concatenate(stacked, axis=0)
    s = lax.dot_general(k_h, q_rows, (((1,), (1,)), ((), ())), preferred_element_type=F32)
    blocks = []
    for j, bias in enumerate(biases):
        blk = s[j * LANES:(j + 1) * LANES, :]
        if bias is not None:
            blk = blk + jnp.concatenate([bias] * HEADS_PER_KV, axis=1)
        blocks.append(blk)
    return blocks


def _attn_softmax(sink_ref, layer, h, blocks):
    tq = blocks[0].shape[1] // HEADS_PER_KV
    sink_row = jnp.concatenate(
        [jnp.full((1, tq), sink_ref[layer, h * HEADS_PER_KV + g] * LOG2E, F32) for g in range(HEADS_PER_KV)],
        axis=1)
    m = jnp.maximum(jnp.max(functools.reduce(jnp.maximum, blocks), axis=0, keepdims=True), sink_row)
    probs = [jnp.exp2(blk - m) for blk in blocks]
    denom = jnp.sum(functools.reduce(jnp.add, probs), axis=0, keepdims=True) + jnp.exp2(sink_row - m)
    return jnp.concatenate(probs, axis=0).astype(BF16), denom


def _attn_weighted(item, h, p, denom):
    tq = item[0].shape[0]
    vt_h = item[2][h * LANES:(h + 1) * LANES, :]
    o = jnp.dot(vt_h, p, preferred_element_type=F32) * (1.0 / denom)
    first_copy = lax.broadcasted_iota(jnp.int32, (LANES, tq), 0) < HEAD_DIM
    return [jnp.where(first_copy, o[:, (2 * c2) * tq:(2 * c2 + 1) * tq],
                      o[:, (2 * c2 + 1) * tq:(2 * c2 + 2) * tq]).T.astype(BF16)
            for c2 in range(HEADS_PER_KV // 2)]


def _attend(sink_ref, layer, items):
    units = [(item, h) for item in items for h in range(N_KV_HEADS)]
    all_scores = [_attn_scores(item, h) for item, h in units]
    all_probs = [_attn_softmax(sink_ref, layer, h, blocks) for (_, h), blocks in zip(units, all_scores)]
    all_outs = [_attn_weighted(item, h, *pd) for (item, h), pd in zip(units, all_probs)]
    return [jnp.concatenate([o for outs in all_outs[i * N_KV_HEADS:(i + 1) * N_KV_HEADS] for o in outs],
                            axis=1) for i in range(len(items))]


def _attn_local_body(sink_ref, q_ref, kd_ref, vd_ref, kc_ref, vc_ref, o_ref, vt_ref, s_ref, *, layer,
                     n_blocks, group):
    blk = ATTN_BLOCK
    n_ctx_blocks = kc_ref.shape[0] // LANES
    n_key_blocks = 3 + n_ctx_blocks
    n_groups = n_blocks // group
    _transpose_values(vd_ref, vt_ref, 0)
    _transpose_values(vc_ref, vt_ref, n_blocks)
    key = lax.broadcasted_iota(jnp.int32, (blk, blk), 0)
    qry = lax.broadcasted_iota(jnp.int32, (blk, blk), 1)
    rows = lambda i: pl.ds(pl.multiple_of(i * blk, blk), blk)

    def item(n):
        prv, nxt = jnp.maximum(n - 1, 0), jnp.minimum(n + 1, n_blocks - 1)
        bias_prev = jnp.where((key >= qry) & (n > 0), 0.0, NEG_INF).astype(F32)
        bias_next = jnp.where((key <= qry) & (n < n_blocks - 1), 0.0, NEG_INF).astype(F32)
        k_rows = jnp.concatenate([kd_ref[rows(prv), :], kd_ref[rows(n), :], kd_ref[rows(nxt), :],
                                  kc_ref[...]], axis=0)
        vt = jnp.concatenate([vt_ref[prv], vt_ref[n], vt_ref[nxt]]
                             + [vt_ref[n_blocks + j] for j in range(n_ctx_blocks)], axis=1)
        return (q_ref[rows(n), :], k_rows, vt, [bias_prev, None, bias_next] + [None] * n_ctx_blocks)

    def store_scores(slot, u, unit):
        for j, sblk in enumerate(_attn_scores(*unit)):
            s_ref[slot, u, j * LANES:(j + 1) * LANES, :] = sblk

    def units_of(g):
        return [(item(g * group + i), h) for i in range(group) for h in range(N_KV_HEADS)]

    for u, unit in enumerate(units_of(0)):
        store_scores(0, u, unit)

    def finish_group(g, cur):
        next_units = units_of(g + 1) if g + 1 < n_groups else None
        outs = []
        for u, (it, h) in enumerate(units_of(g)):
            if next_units is not None:
                store_scores(1 - cur, u, next_units[u])
            blocks = [s_ref[cur, u, j * LANES:(j + 1) * LANES, :] for j in range(n_key_blocks)]
            outs.append(_attn_weighted(it, h, *_attn_softmax(sink_ref, layer, h, blocks)))
        for i in range(group):
            o_ref[rows(g * group + i), :] = jnp.concatenate(
                [o for per_head in outs[i * N_KV_HEADS:(i + 1) * N_KV_HEADS] for o in per_head], axis=1)

    for g in range(n_groups):
        finish_group(g, g % 2)


def _attn_ctx_body(sink_ref, q_ref, kc_ref, vc_ref, o_ref, vt_ref, *, layer):
    n_key_blocks = kc_ref.shape[0] // LANES
    _transpose_values(vc_ref, vt_ref, 0)
    vt = jnp.concatenate([vt_ref[j] for j in range(n_key_blocks)], axis=1)
    items = [(q_ref[i * ATTN_BLOCK:(i + 1) * ATTN_BLOCK, :], kc_ref[...], vt, [None] * n_key_blocks)
             for i in range(q_ref.shape[0] // ATTN_BLOCK)]
    for i, out in enumerate(_attend(sink_ref, layer, items)):
        o_ref[i * ATTN_BLOCK:(i + 1) * ATTN_BLOCK, :] = out


def _attention(sink, layer, q, kd, vd, kdc, vdc):
    b, s, _ = q.shape
    c = kdc.shape[1]
    kw = 2 * KV_W
    n_blocks = s // ATTN_BLOCK
    group = math.gcd(n_blocks, 4)
    idx = lambda bi: (bi, 0, 0)
    return pl.pallas_call(
        functools.partial(_attn_local_body, layer=layer, n_blocks=n_blocks, group=group),
        grid=(b,),
        in_specs=[
            pl.BlockSpec(memory_space=pltpu.SMEM),
            pl.BlockSpec((None, s, ATTN_W), idx),
            pl.BlockSpec((None, s, kw), idx), pl.BlockSpec((None, s, kw), idx),
            pl.BlockSpec((None, c, kw), idx), pl.BlockSpec((None, c, kw), idx),
        ],
        out_specs=pl.BlockSpec((None, s, ATTN_W), idx),
        out_shape=jax.ShapeDtypeStruct((b, s, ATTN_W), BF16),
        scratch_shapes=[pltpu.VMEM((n_blocks + c // LANES, kw, LANES), BF16),
                        pltpu.VMEM((2, group * N_KV_HEADS, 3 * ATTN_BLOCK + c, ATTN_W), F32)],
        compiler_params=_params(1),
        name="attn_local",
    )(sink, q, kd, vd, kdc, vdc)


def _attention_ctx(sink, layer, qc, kdc, vdc):
    b, c, _ = qc.shape
    kw = 2 * KV_W
    idx = lambda bi: (bi, 0, 0)
    return pl.pallas_call(
        functools.partial(_attn_ctx_body, layer=layer),
        grid=(b,),
        in_specs=[
            pl.BlockSpec(memory_space=pltpu.SMEM),
            pl.BlockSpec((None, c, ATTN_W), idx),
            pl.BlockSpec((None, c, kw), idx), pl.BlockSpec((None, c, kw), idx),
        ],
        out_specs=pl.BlockSpec((None, c, ATTN_W), idx),
        out_shape=jax.ShapeDtypeStruct((b, c, ATTN_W), BF16),
        scratch_shapes=[pltpu.VMEM((c // LANES, kw, LANES), BF16)],
        compiler_params=_params(1),
        name="attn_ctx",
    )(sink, qc, kdc, vdc)


def _pool_group(padded, t, length, window, w_mix, scale):
    halo = SUBLANES
    rows = padded.shape[0]
    n_rows = rows - 2 * halo
    acc = padded
    step = 1
    while step < window:
        acc = acc + pltpu.roll(acc, rows - step, 0)
        step *= 2
    back = (window - 1) // 2
    win = (pltpu.roll(acc, back, 0) if back else acc)[halo:halo + n_rows]
    lo = jnp.maximum(t - back, 0)
    hi = jnp.minimum(t + window // 2 + 1, length)
    mean = win / (hi - lo).astype(F32)
    dlt = (mean - padded[halo:halo + n_rows]).astype(BF16)
    return (jnp.dot(dlt, w_mix, preferred_element_type=F32) * scale).astype(BF16)


def _lru_body(*refs, n_ctx, n_lat, tile, with_ctx_out):
    if with_ctx_out:
        (rxc_ref, rxl_ref, ryc_ref, ryl_ref, cw_ref, cb_ref, wg_ref, ba_ref, bx_ref, lam_ref,
         ol_ref, oc_ref, pad_ref, xl_ref, hf_ref, hb_ref, xp_ref, a_ref, b_ref) = refs
    else:
        (rxc_ref, rxl_ref, ryl_ref, cw_ref, cb_ref, wg_ref, ba_ref, bx_ref, lam_ref,
         ol_ref, pad_ref, xl_ref, hf_ref, hb_ref, xp_ref, a_ref, b_ref) = refs
    halo = SUBLANES
    n_tot = n_ctx + n_lat
    n_slab = RNN_W // LANES

    zeros = jnp.zeros((halo, RNN_W), F32)
    c0, l0 = halo, 2 * halo + n_ctx
    pad_ref[0:halo, :] = zeros
    pad_ref[c0 + n_ctx:l0, :] = zeros
    pad_ref[l0 + n_lat:l0 + n_lat + halo, :] = zeros
    pad_ref[c0:c0 + n_ctx, :] = rxc_ref[...]
    pad_ref[l0:l0 + n_lat, :] = rxl_ref[...]

    cw = 0.5 * cw_ref[...]
    cb = 0.5 * cb_ref[...]

    def conv_tile(src0, dst_rows):
        rows = tile + 2 * halo
        p = pad_ref[src0 - halo:src0 + tile + halo, :]
        y = (cb + p[halo:halo + tile] * cw[1:2, :]
             + pltpu.roll(p, 1, 0)[halo:halo + tile] * cw[0:1, :]
             + pltpu.roll(p, rows - 1, 0)[halo:halo + tile] * cw[2:3, :]
             + pltpu.roll(p, rows - 2, 0)[halo:halo + tile] * cw[3:4, :])
        for s in range(n_slab):
            for d0 in dst_rows:
                xl_ref[s, d0:d0 + tile, :] = y[:, s * LANES:(s + 1) * LANES]

    for t0 in range(0, n_ctx, tile):
        conv_tile(c0 + t0, (t0, n_tot + t0))
    for t0 in range(0, n_lat, tile):
        conv_tile(l0 + t0, (n_ctx + t0,))

    lam = lam_ref[...]
    log_sig = jnp.minimum(lam, 0.0) - jnp.log1p(jnp.exp(-jnp.abs(lam)))
    rate = (0.5 * LOG2E * LRU_C) * log_sig
    half_ba = 0.5 * ba_ref[...]
    half_bx = 0.5 * bx_ref[...]
    half = RNN_W // 2

    def run_chunk(d, base, carry):
        res_ref = hf_ref if d == 0 else hb_ref
        win0 = 0 if d == 0 else n_ctx
        for v in range(SEG):
            for s in range(n_slab):
                xp_ref[d, v * SUBLANES:(v + 1) * SUBLANES, s * LANES:(s + 1) * LANES] = (
                    xl_ref[s, pl.ds(win0 + base + v, SUBLANES, stride=SEG), :])
        for j in range(2):
            ch = slice(j * half, (j + 1) * half)
            xj = xp_ref[d, :, ch]
            gates = jnp.dot(xj.astype(BF16), wg_ref[d, j], preferred_element_type=F32)
            t_r = jnp.tanh(gates[:, :half] + half_ba[d:d + 1, ch])
            t_i = jnp.tanh(gates[:, half:] + half_bx[d:d + 1, ch])
            a = jnp.exp2(t_r * rate[d:d + 1, ch] + rate[d:d + 1, ch])
            gap = 1.0 - a * a
            mult = gap * lax.rsqrt(jnp.maximum(gap, TINY))
            a_ref[d, :, ch] = a
            b_ref[d, :, ch] = (mult * xj) * (t_i + 1.0)
        order = range(SEG) if d == 0 else range(SEG - 1, -1, -1)
        h = jnp.zeros((SUBLANES, RNN_W), F32)
        acum = jnp.ones((SUBLANES, RNN_W), F32)
        for v in order:
            rows = slice(v * SUBLANES, (v + 1) * SUBLANES)
            av = a_ref[d, rows, :]
            h = av * h + b_ref[d, rows, :]
            acum = av * acum
            b_ref[d, rows, :] = h
            a_ref[d, rows, :] = acum
        seg_in = [None] * SUBLANES
        state = carry
        seg_order = range(SUBLANES) if d == 0 else range(SUBLANES - 1, -1, -1)
        for i in seg_order:
            seg_in[i] = state
            state = h[i:i + 1, :] + acum[i:i + 1, :] * state
        seg_state = jnp.concatenate(seg_in, axis=0)
        for v in range(SEG):
            rows = slice(v * SUBLANES, (v + 1) * SUBLANES)
            hv = b_ref[d, rows, :] + a_ref[d, rows, :] * seg_state
            for s in range(n_slab):
                res_ref[s, pl.ds(base + v, SUBLANES, stride=SEG), :] = hv[:, s * LANES:(s + 1) * LANES]
        return state

    n_chunks = n_tot // CHUNK

    def step(c, carries):
        cf, cb = carries
        cf = run_chunk(0, c * CHUNK, cf)
        cb = run_chunk(1, (n_chunks - 1 - c) * CHUNK, cb)
        return cf, cb

    zero_state = jnp.zeros((1, RNN_W), F32)
    lax.fori_loop(0, n_chunks, step, (zero_state, zero_state))

    def emit(out_ref, ry_ref, f0, b0, length):
        for t0 in range(0, length, tile):
            hsum = jnp.concatenate(
                [hf_ref[s, f0 + t0:f0 + t0 + tile, :] + hb_ref[s, b0 + t0:b0 + t0 + tile, :]
                 for s in range(n_slab)], axis=1)
            gate = ry_ref[t0:t0 + tile, :].astype(F32)
            out_ref[t0:t0 + tile, :] = (hsum * gate).astype(BF16)

    emit(ol_ref, ryl_ref, n_ctx, 0, n_lat)
    if with_ctx_out:
        emit(oc_ref, ryc_ref, 0, n_lat, n_ctx)


def _lru(layer, rxc, rxl, ryc, ryl, conv_w, conv_b, wg, b_a, b_x, lam):
    b, n_ctx, _ = rxc.shape
    n_lat = rxl.shape[1]
    n_tot = n_ctx + n_lat
    tile = math.gcd(n_ctx, n_lat, 256)
    assert n_tot % CHUNK == 0 and tile % SUBLANES == 0
    with_ctx_out = ryc is not None
    idx = lambda bi: (bi, 0, 0)
    seq = lambda n, dt=None: pl.BlockSpec((None, n, RNN_W), idx)
    n_slab = RNN_W // LANES
    in_specs = [seq(n_ctx), seq(n_lat)] + ([seq(n_ctx)] if with_ctx_out else []) + [seq(n_lat)]
    args = [rxc, rxl] + ([ryc] if with_ctx_out else []) + [ryl]
    for wgt in (conv_w, conv_b, wg, b_a, b_x, lam):
        in_specs.append(_layer_block(wgt, layer))
        args.append(wgt)
    out_specs = [seq(n_lat)] + ([seq(n_ctx)] if with_ctx_out else [])
    out_shape = [jax.ShapeDtypeStruct((b, n_lat, RNN_W), BF16)]
    if with_ctx_out:
        out_shape.append(jax.ShapeDtypeStruct((b, n_ctx, RNN_W), BF16))
    res = pl.pallas_call(
        functools.partial(_lru_body, n_ctx=n_ctx, n_lat=n_lat, tile=tile, with_ctx_out=with_ctx_out),
        grid=(b,),
        in_specs=in_specs,
        out_specs=out_specs,
        out_shape=out_shape,
        scratch_shapes=[
            pltpu.VMEM((n_tot + 3 * SUBLANES, RNN_W), F32),
            pltpu.VMEM((n_slab, n_tot + n_ctx, LANES), F32),
            pltpu.VMEM((n_slab, n_tot, LANES), F32),
            pltpu.VMEM((n_slab, n_tot, LANES), F32),
            pltpu.VMEM((2, CHUNK, RNN_W), F32),
            pltpu.VMEM((2, CHUNK, RNN_W), F32),
            pltpu.VMEM((2, CHUNK, RNN_W), F32),
        ],
        compiler_params=_params(1),
        name="rglru",
    )(*args)
    return (res[0], res[1]) if with_ctx_out else (res[0], None)


def _mix_ffn_body(*refs, d_ff, chunk, n_cast):
    (x_ref, mod_ref, at_ref, po_ref, rn_ref, gt_ref, wa_ref, wp_ref, wr_ref, wo_ref,
     gmix_ref, gpre_ref, wgu_ref, wd_ref, gpost_ref) = refs[:15]
    cast_in = refs[15:15 + n_cast]
    o_ref = refs[15 + n_cast]
    cast_out = refs[16 + n_cast:16 + 2 * n_cast]
    m_ref, h_ref = refs[16 + 2 * n_cast:]
    for src, dst in zip(cast_in, cast_out):
        dst[...] = src[...].astype(BF16)
    d = x_ref.shape[1]
    for rows in _row_halves(x_ref.shape[0]):
        merged = None
        for k, (br_ref, w_ref) in enumerate(((at_ref, wa_ref), (po_ref, wp_ref), (rn_ref, wr_ref))):
            t = gt_ref[rows, k * d:(k + 1) * d].astype(F32) * jnp.dot(
                br_ref[rows, :], w_ref[...], preferred_element_type=F32)
            merged = t if merged is None else merged + t
        m_ref[rows, :] = merged.astype(BF16)
        mix = jnp.dot(m_ref[rows, :], wo_ref[...], preferred_element_type=F32)
        o_ref[rows, :] = x_ref[rows, :] + mod_ref[2:3, :] * (_rms(mix) * gmix_ref[...])
        _norm_modulate_store(o_ref, mod_ref, gpre_ref, h_ref, 3, 4, rows)

    halves = _row_halves(x_ref.shape[0])
    accs = [None] * len(halves)
    for c0 in range(0, d_ff, chunk):
        c1 = min(c0 + chunk, d_ff)
        for i, rows in enumerate(halves):
            gate = jnp.dot(h_ref[rows, :], wgu_ref[:, c0:c1], preferred_element_type=F32)
            up = jnp.dot(h_ref[rows, :], wgu_ref[:, d_ff + c0:d_ff + c1], preferred_element_type=F32)
            act = ((gate * jax.nn.sigmoid(gate)) * up).astype(BF16)
            part = jnp.dot(act, wd_ref[c0:c1, :], preferred_element_type=F32)
            accs[i] = part if accs[i] is None else accs[i] + part
    for acc, rows in zip(accs, halves):
        o_ref[rows, :] = o_ref[rows, :] + mod_ref[5:6, :] * (_rms(acc) * gpost_ref[...])


def _mix_ffn(x2d, mod_all, layer, mod_row_of_tile, attn, pool, rnn, gates, wa, wp, wr, wo, g_mix,
             g_pre, w_gu, w_down, g_post, *, tm, cast_weights=(), cast_layer=0):
    n_tok, d = x2d.shape
    d_ff = w_down.shape[0]
    row = lambda i: (i, 0)
    tok = lambda arr: pl.BlockSpec((tm, arr.shape[1]), row)
    own = lambda wgt: (wgt[None], 0)
    params = (own(wa), own(wp), own(wr), own(wo), (g_mix, layer), (g_pre, layer), own(w_gu), own(w_down),
              (g_post, layer))
    cast_in, cast_out = _cast_streams(cast_weights, cast_layer, n_tok // tm)
    res = pl.pallas_call(
        functools.partial(_mix_ffn_body, d_ff=d_ff, chunk=1024, n_cast=len(cast_weights)),
        grid=(n_tok // tm,),
        in_specs=[tok(x2d), _mod_block(mod_all, layer, mod_row_of_tile),
                  tok(attn), tok(pool), tok(rnn), tok(gates)]
                 + [_layer_block(p, lyr) for p, lyr in params] + cast_in,
        out_specs=[pl.BlockSpec((tm, d), row)] + [o[0] for o in cast_out],
        out_shape=[jax.ShapeDtypeStruct((n_tok, d), F32)] + [o[1] for o in cast_out],
        scratch_shapes=[pltpu.VMEM((tm, d), BF16), pltpu.VMEM((tm, d), BF16)],
        compiler_params=_params(1),
        name="mix_ffn",
    )(x2d, mod_all, attn, pool, rnn, gates, *[p for p, _ in params], *cast_weights)
    return res[0] if not cast_weights else res


def _pack_gate_weights(w_a, w_x):
    def block_diag(w):
        bw = RNN_W // RNN_BLOCKS
        rows = [jnp.pad(w[:, :, h], ((0, 0), (0, 0), (0, 0), (bw * h, RNN_W - bw * (h + 1))))
                for h in range(RNN_BLOCKS)]
        return jnp.concatenate(rows, axis=2)
    bd_a, bd_x = block_diag(w_a), block_diag(w_x)
    half = RNN_W // 2
    halves = []
    for j in range(2):
        sl = slice(j * half, (j + 1) * half)
        halves.append(jnp.concatenate([bd_a[:, :, sl, sl], bd_x[:, :, sl, sl]], axis=3))
    return jnp.stack(halves, axis=2).astype(BF16)


def kernel(x, c, ctx, c_ctx, w_ada, b_ada, g_pre_mix, g_post_mix, g_pre_ffn, g_post_ffn, w_in, attn_sink, w_attn_o, pool_mix, pool_scale, w_pool_o, conv_w, conv_b, lru_w_a, lru_b_a, lru_w_x, lru_b_x, lru_lambda, w_rnn_o, w_out, w_gu, w_down):
    bsz, seq, d = x.shape
    n_ctx = ctx.shape[1]
    depth = w_ada.shape[0]
    tm = min(512, seq)
    tm_ctx = min(tm, n_ctx)

    mod_rows = -(-(bsz + 1) // SUBLANES) * SUBLANES
    cvec = jnp.zeros((mod_rows, d), F32).at[:bsz].set(c).at[bsz].set(c_ctx)
    mod_all = _adaln_mod(cvec, w_ada, b_ada).reshape(depth, mod_rows, 6, d)

    rope = _rope_tables(seq)
    lat_row = lambda i: i // (seq // tm)
    ctx_row = lambda i: bsz

    rowvec = lambda a: a.reshape(depth, 1, -1)
    w_in_l = w_in[0].astype(BF16)
    channel_weights = (w_attn_o, w_pool_o, w_rnn_o, w_out, w_gu, w_down)
    w_mix = pool_mix.astype(BF16)
    wg = _pack_gate_weights(lru_w_a, lru_w_x)
    g_pre, g_post = rowvec(g_pre_mix), rowvec(g_post_mix)
    gf_pre, gf_post = rowvec(g_pre_ffn), rowvec(g_post_ffn)
    p_scale, cv_b = rowvec(pool_scale), rowvec(conv_b)

    x2 = x.reshape(bsz * seq, d)
    c2 = ctx.reshape(bsz * n_ctx, d)
    r3 = lambda a, n: a.reshape(bsz, n, a.shape[-1])
    flat = lambda a: a.reshape(-1, a.shape[-1])
    for l in range(depth):
        need_ctx = l < depth - 1
        q, kd, vd, rx, ry, pool_l, gt, wa, wp, wr, wo, w_gu_b, w_dn_b = _inproj(
            x2, mod_all, l, lat_row, g_pre, w_in_l[None], w_mix, p_scale, tm=tm, tiles_per_seq=seq // tm,
            rope_tables=rope, kv_only=False, cast_weights=channel_weights)
        if need_ctx:
            qc, kdc, vdc, rxc, ryc, pool_c, gtc = _inproj(
                c2, mod_all, l, ctx_row, g_pre, w_in_l[None], w_mix, p_scale, tm=tm_ctx,
                tiles_per_seq=n_ctx // tm_ctx, rope_tables=None, kv_only=False)
        else:
            kdc, vdc, rxc = _inproj(
                c2, mod_all, l, ctx_row, g_pre, w_in_l[None], None, None, tm=tm_ctx, tiles_per_seq=1,
                rope_tables=None, kv_only=True)
            ryc = None

        kdc3, vdc3 = r3(kdc, n_ctx), r3(vdc, n_ctx)
        attn_l = _attention(attn_sink, l, r3(q, seq), r3(kd, seq), r3(vd, seq), kdc3, vdc3)
        rnn_l, rnn_c = _lru(l, r3(rxc, n_ctx), r3(rx, seq), None if ryc is None else r3(ryc, n_ctx),
                            r3(ry, seq), conv_w, cv_b, wg, lru_b_a, lru_b_x, lru_lambda)

        if need_ctx:
            x2, w_in_next = _mix_ffn(x2, mod_all, l, lat_row, flat(attn_l), pool_l, flat(rnn_l), gt, wa, wp,
                                     wr, wo, g_post, gf_pre, w_gu_b, w_dn_b, gf_post, tm=tm,
                                     cast_weights=(w_in,), cast_layer=l + 1)
        else:
            x2 = _mix_ffn(x2, mod_all, l, lat_row, flat(attn_l), pool_l, flat(rnn_l), gt, wa, wp, wr, wo,
                          g_post, gf_pre, w_gu_b, w_dn_b, gf_post, tm=tm)

        if need_ctx:
            attn_c = _attention_ctx(attn_sink, l, r3(qc, n_ctx), kdc3, vdc3)
            c2 = _mix_ffn(c2, mod_all, l, ctx_row, flat(attn_c), pool_c, flat(rnn_c), gtc, wa, wp, wr,
                          wo, g_post, gf_pre, w_gu_b, w_dn_b, gf_post, tm=tm_ctx)
            w_in_l = w_in_next
    return x2.reshape(bsz, seq, d)
```

```python
import functools
import math

import jax
import jax.numpy as jnp
import numpy as np
from jax import lax
from jax.experimental import pallas as pl
from jax.experimental.pallas import tpu as pltpu

F32 = jnp.float32
BF16 = jnp.bfloat16

GRID_W = 64
HEAD_DIM = 64
N_Q_HEADS = 8
N_KV_HEADS = 2
ATTN_W = N_Q_HEADS * HEAD_DIM
KV_W = N_KV_HEADS * HEAD_DIM
ATTN_BLOCK = 128
ROPE_BASE = 10000.0
POOL_WINDOWS = (2, 4, 8, 16)
POOL_W = 512
POOL_GW = POOL_W // len(POOL_WINDOWS)
RNN_W = 512
RNN_BLOCKS = 8
CONV_W = 4
LRU_C = 8.0
N_BRANCH = 3
EPS = 1e-6
NEG_INF = -1e30
LOG2E = 1.4426950408889634
TINY = 1e-30

LANES = 128
SUBLANES = 8
VMEM_LIMIT_BYTES = 56 * 1024 * 1024

SEG = 36
CHUNK = SUBLANES * SEG


def _params(n_axes):
    return pltpu.CompilerParams(
        dimension_semantics=("arbitrary",) * n_axes, vmem_limit_bytes=VMEM_LIMIT_BYTES)


def _layer_block(arr, layer, block=None):
    block = tuple(arr.shape[1:]) if block is None else block
    index = (layer,) + (0,) * len(block)
    return pl.BlockSpec((None,) + block, lambda *_: index, pipeline_mode=pl.Buffered(1))


def _mod_block(mod_all, layer, mod_row_of_tile):
    return pl.BlockSpec((None, None) + mod_all.shape[2:], lambda i: (layer, mod_row_of_tile(i), 0, 0))


def _rms(xf):
    return xf * lax.rsqrt(jnp.mean(xf * xf, axis=-1, keepdims=True) + EPS)


def _mod_body(c_ref, w_ref, b_ref, o_ref):
    c = c_ref[...]
    s = (c * jax.nn.sigmoid(c)).astype(BF16)
    o_ref[0] = jnp.dot(s, w_ref[0].astype(BF16), preferred_element_type=F32) + b_ref[0]


def _adaln_mod(cvec, w_ada, b_ada):
    n_layers, d, n6 = w_ada.shape
    rows = cvec.shape[0]
    tn = n6 // 4
    return pl.pallas_call(
        _mod_body,
        grid=(n_layers, n6 // tn),
        in_specs=[
            pl.BlockSpec((rows, d), lambda l, j: (0, 0)),
            pl.BlockSpec((1, d, tn), lambda l, j: (l, 0, j)),
            pl.BlockSpec((1, 1, tn), lambda l, j: (l, 0, j)),
        ],
        out_specs=pl.BlockSpec((1, rows, tn), lambda l, j: (l, 0, j)),
        out_shape=jax.ShapeDtypeStruct((n_layers, rows, n6), F32),
        compiler_params=_params(2),
        name="adaln_mod",
    )(cvec, w_ada, b_ada.reshape(n_layers, 1, n6))


def _rope_tables(seq_len):
    pos = np.arange(seq_len)
    row = (pos // GRID_W).astype(np.float32)
    col = (pos % GRID_W).astype(np.float32)
    half = HEAD_DIM // 2
    quarter = half // 2
    inv = (np.float32(ROPE_BASE) ** (-(np.arange(quarter, dtype=np.float32) * np.float32(2.0 / half)))
           ).astype(np.float32)
    j = np.arange(LANES) % HEAD_DIM
    is_col = j >= half
    second = (j % half) >= quarter
    freq = inv[j % quarter]
    ang = (np.where(is_col[None, :], col[:, None], row[:, None]) * freq[None, :]).astype(np.float32)
    cos, sin = np.cos(ang), np.sin(ang)
    sin_prev = np.where(second[None, :], sin, 0.0)
    sin_next = np.where(second[None, :], 0.0, -sin)
    return tuple(jnp.asarray(t, F32) for t in (cos, sin_prev, sin_next))


def _row_halves(n_rows):
    if n_rows % (2 * SUBLANES * 2) or n_rows < 512:
        return [slice(0, n_rows)]
    return [slice(0, n_rows // 2), slice(n_rows // 2, n_rows)]


def _store_dup_heads(dst_ref, rows, kv):
    low = lax.broadcasted_iota(jnp.int32, kv.shape, 1) < HEAD_DIM
    swapped = pltpu.roll(kv, HEAD_DIM, 1)
    dst_ref[rows, 0:LANES] = jnp.where(low, kv, swapped).astype(BF16)
    dst_ref[rows, LANES:2 * LANES] = jnp.where(low, swapped, kv).astype(BF16)


def _norm_modulate_store(x_ref, mod_ref, g_ref, h_ref, shift_row, scale_row, rows):
    y = _rms(x_ref[rows, :]) * g_ref[...]
    h = y * (1.0 + mod_ref[scale_row:scale_row + 1, :]) + mod_ref[shift_row:shift_row + 1, :]
    h_ref[rows, :] = h.astype(BF16)


def _inproj_full_body(*refs, rope, tiles_per_seq, n_cast):
    refs = list(refs)
    x_ref, xp_ref, xn_ref, mod_ref, g_ref, w_ref, wmix_ref, psc_ref = refs[:8]
    del refs[:8]
    if rope:
        cos_ref, sp_ref, sn_ref = refs[:3]
        del refs[:3]
    cast_in = refs[:n_cast]
    q_ref, kd_ref, vd_ref, rx_ref, ry_ref, po_ref, gt_ref = refs[n_cast:n_cast + 7]
    cast_out = refs[n_cast + 7:2 * n_cast + 7]
    h_ref = refs[2 * n_cast + 7]
    for src, dst in zip(cast_in, cast_out):
        dst[...] = src[...].astype(BF16)
    quarter = HEAD_DIM // 4
    tm, d = x_ref.shape
    halo = SUBLANES
    tile_in_seq = pl.program_id(0) % tiles_per_seq
    pu0 = ATTN_W + 2 * KV_W + 2 * RNN_W

    def modulated(xf):
        y = _rms(xf) * g_ref[...]
        return y * (1.0 + mod_ref[1:2, :]) + mod_ref[0:1, :]

    pack = 2 * SUBLANES
    zeros = jnp.zeros((SUBLANES, d), F32)
    before = jnp.where(tile_in_seq == 0, 0.0, modulated(xp_ref[...]))
    after = jnp.where(tile_in_seq == tiles_per_seq - 1, 0.0, modulated(xn_ref[...]))
    h_ref[0:pack, :] = jnp.concatenate([zeros, before], axis=0).astype(BF16)
    h_ref[pack + tm:2 * pack + tm, :] = jnp.concatenate([after, zeros], axis=0).astype(BF16)
    for rows in _row_halves(tm):
        h_ref[pack + rows.start:pack + rows.stop, :] = modulated(x_ref[rows, :]).astype(BF16)

    for rows in _row_halves(tm):
        r0, n_rows = rows.start, rows.stop - rows.start
        hrows = slice(pack + r0, pack + rows.stop)

        def proj(c0, c1):
            return jnp.dot(h_ref[hrows, :], w_ref[:, c0:c1], preferred_element_type=F32)

        def rotary(pc):
            if not rope:
                return pc
            return (pc * cos_ref[rows, :]
                    + pltpu.roll(pc, quarter, 1) * sp_ref[rows, :]
                    + pltpu.roll(pc, LANES - quarter, 1) * sn_ref[rows, :])

        padded = jnp.dot(h_ref[r0:r0 + n_rows + 2 * pack, :], w_ref[:, pu0:pu0 + POOL_W],
                         preferred_element_type=F32)[pack - halo:pack + n_rows + halo]
        t = tile_in_seq * tm + r0 + lax.broadcasted_iota(jnp.int32, (n_rows, POOL_GW), 0)

        def pool_group(gi):
            lanes = slice(gi * POOL_GW, (gi + 1) * POOL_GW)
            po_ref[rows, lanes] = _pool_group(padded[:, lanes], t, tiles_per_seq * tm, POOL_WINDOWS[gi],
                                              wmix_ref[gi], psc_ref[:, lanes])

        def seg_q(c):
            pair = proj(c * 2 * LANES, (c + 1) * 2 * LANES)
            for half in range(2):
                col = (2 * c + half) * LANES
                pc = rotary(pair[:, half * LANES:(half + 1) * LANES])
                q_ref[rows, col:col + LANES] = (pc * (HEAD_DIM ** -0.5 * LOG2E)).astype(BF16)

        def seg_kv():
            kv = proj(ATTN_W, ATTN_W + 2 * KV_W)
            _store_dup_heads(kd_ref, rows, rotary(kv[:, :KV_W]))
            _store_dup_heads(vd_ref, rows, kv[:, KV_W:])

        rx0 = ATTN_W + 2 * KV_W

        def seg_rx():
            rx_ref[rows, :] = proj(rx0, rx0 + RNN_W)

        def seg_ry():
            ry_ref[rows, :] = jax.nn.gelu(proj(rx0 + RNN_W, rx0 + 2 * RNN_W)).astype(BF16)

        gl0 = pu0 + POOL_W

        def seg_gate(c):
            gl = proj(gl0 + c * d, gl0 + (c + 1) * d)
            gt_ref[rows, c * d:(c + 1) * d] = jax.nn.sigmoid(gl).astype(BF16)

        seg_gate(0)
        pool_group(0)
        seg_q(0)
        seg_gate(1)
        pool_group(1)
        seg_q(1)
        seg_gate(2)
        pool_group(2)
        seg_kv()
        seg_ry()
        pool_group(3)
        seg_rx()


def _inproj_kv_body(x_ref, mod_ref, g_ref, w_ref, kd_ref, vd_ref, rx_ref, h_ref):
    rows = slice(0, x_ref.shape[0])
    _norm_modulate_store(x_ref, mod_ref, g_ref, h_ref, 0, 1, rows)
    kv = jnp.dot(h_ref[...], w_ref[:, ATTN_W:ATTN_W + 2 * KV_W], preferred_element_type=F32)
    _store_dup_heads(kd_ref, rows, kv[:, :KV_W])
    _store_dup_heads(vd_ref, rows, kv[:, KV_W:])
    rx_ref[...] = jnp.dot(h_ref[...], w_ref[:, ATTN_W + 2 * KV_W:ATTN_W + 2 * KV_W + RNN_W],
                          preferred_element_type=F32)


def _cast_slab(weight, n_steps):
    rows = weight.shape[1]
    for hold in (1, 2, 4, 8):
        if n_steps % hold == 0 and rows % (n_steps // hold) == 0 and (rows // (n_steps // hold)) % 16 == 0:
            return rows // (n_steps // hold), hold
    raise ValueError("no bf16-tileable row slab for %s over %d steps" % (weight.shape, n_steps))


def _cast_streams(cast_weights, layer, n_steps):
    in_specs, outs = [], []
    for wgt in cast_weights:
        slab, hold = _cast_slab(wgt, n_steps)
        cols = wgt.shape[2]
        in_specs.append(pl.BlockSpec((None, slab, cols), lambda i, hold=hold: (layer, i // hold, 0)))
        outs.append((pl.BlockSpec((slab, cols), lambda i, hold=hold: (i // hold, 0)),
                     jax.ShapeDtypeStruct(wgt.shape[1:], BF16)))
    return in_specs, outs


def _inproj(x2d, mod_all, layer, mod_row_of_tile, g, w, w_mix, p_scale, *, tm, tiles_per_seq, rope_tables,
            kv_only, cast_weights=()):
    n_tok, d = x2d.shape
    grid = (n_tok // tm,)
    row = lambda i: (i, 0)
    in_specs = [pl.BlockSpec((tm, d), row)]
    args = [x2d]
    if not kv_only:
        per_tile = tm // SUBLANES
        in_specs += [pl.BlockSpec((SUBLANES, d), lambda i: (jnp.maximum(i * per_tile - 1, 0), 0)),
                     pl.BlockSpec((SUBLANES, d),
                                  lambda i: (jnp.minimum((i + 1) * per_tile, n_tok // SUBLANES - 1), 0))]
        args += [x2d, x2d]
    w_cols = ATTN_W + 2 * KV_W + RNN_W if kv_only else w.shape[2]
    in_specs += [_mod_block(mod_all, layer, mod_row_of_tile), _layer_block(g, layer),
                 _layer_block(w, 0, (d, w_cols))]
    args += [mod_all, g, w]
    if not kv_only:
        in_specs += [_layer_block(w_mix, layer), _layer_block(p_scale, layer)]
        args += [w_mix, p_scale]
    if rope_tables is not None:
        in_specs += [pl.BlockSpec((tm, LANES), lambda i: (i % tiles_per_seq, 0))] * 3
        args += list(rope_tables)
    cast_in, cast_outs = _cast_streams(cast_weights, layer, grid[0])
    in_specs += cast_in
    args += list(cast_weights)

    def out(width, dtype):
        return pl.BlockSpec((tm, width), row), jax.ShapeDtypeStruct((n_tok, width), dtype)

    if kv_only:
        outs = [out(2 * KV_W, BF16), out(2 * KV_W, BF16), out(RNN_W, F32)]
        body = _inproj_kv_body
        h_rows = tm
    else:
        outs = [out(ATTN_W, BF16), out(2 * KV_W, BF16), out(2 * KV_W, BF16), out(RNN_W, F32),
                out(RNN_W, BF16), out(POOL_W, BF16), out(N_BRANCH * d, BF16)] + cast_outs
        body = functools.partial(_inproj_full_body, rope=rope_tables is not None,
                                 tiles_per_seq=tiles_per_seq, n_cast=len(cast_outs))
        h_rows = tm + 4 * SUBLANES
    return pl.pallas_call(
        body,
        grid=grid,
        in_specs=in_specs,
        out_specs=[o[0] for o in outs],
        out_shape=[o[1] for o in outs],
        scratch_shapes=[pltpu.VMEM((h_rows, d), BF16)],
        compiler_params=_params(1),
        name="inproj_kv" if kv_only else "inproj",
    )(*args)


def _transpose_values(v_ref, vt_ref, first_block):
    for j in range(v_ref.shape[0] // LANES):
        blk = v_ref[j * LANES:(j + 1) * LANES, :].astype(F32)
        vt_ref[first_block + j] = blk.T.astype(BF16)


HEADS_PER_KV = N_Q_HEADS // N_KV_HEADS


def _attn_scores(item, h):
    q_blk, k_rows, _, biases = item
    zero = jnp.zeros((), BF16)
    low = lax.broadcasted_iota(jnp.int32, (q_blk.shape[0], LANES), 1) < HEAD_DIM
    k_h = k_rows[:, h * LANES:(h + 1) * LANES]
    stacked = []
    for c2 in range(HEADS_PER_KV // 2):
        qc = q_blk[:, (2 * h + c2) * LANES:(2 * h + c2 + 1) * LANES]
        stacked += [jnp.where(low, qc, zero), jnp.where(low, zero, qc)]
    q_rows = jnp.concatenate(stacked, axis=0)
    s = lax.dot_general(k_h, q_rows, (((1,), (1,)), ((), ())), preferred_element_type=F32)
    blocks = []
    for j, bias in enumerate(biases):
        blk = s[j * LANES:(j + 1) * LANES, :]
        if bias is not None:
            blk = blk + jnp.concatenate([bias] * HEADS_PER_KV, axis=1)
        blocks.append(blk)
    return blocks


def _attn_softmax(sink_ref, layer, h, blocks):
    tq = blocks[0].shape[1] // HEADS_PER_KV
    sink_row = jnp.concatenate(
        [jnp.full((1, tq), sink_ref[layer, h * HEADS_PER_KV + g] * LOG2E, F32) for g in range(HEADS_PER_KV)],
        axis=1)
    m = jnp.maximum(jnp.max(functools.reduce(jnp.maximum, blocks), axis=0, keepdims=True), sink_row)
    probs = [jnp.exp2(blk - m) for blk in blocks]
    denom = jnp.sum(functools.reduce(jnp.add, probs), axis=0, keepdims=True) + jnp.exp2(sink_row - m)
    return jnp.concatenate(probs, axis=0).astype(BF16), denom


def _attn_weighted(item, h, p, denom):
    tq = item[0].shape[0]
    vt_h = item[2][h * LANES:(h + 1) * LANES, :]
    o = jnp.dot(vt_h, p, preferred_element_type=F32) * (1.0 / denom)
    first_copy = lax.broadcasted_iota(jnp.int32, (LANES, tq), 0) < HEAD_DIM
    return [jnp.where(first_copy, o[:, (2 * c2) * tq:(2 * c2 + 1) * tq],
                      o[:, (2 * c2 + 1) * tq:(2 * c2 + 2) * tq]).T.astype(BF16)
            for c2 in range(HEADS_PER_KV // 2)]


def _attend(sink_ref, layer, items):
    units = [(item, h) for item in items for h in range(N_KV_HEADS)]
    all_scores = [_attn_scores(item, h) for item, h in units]
    all_probs = [_attn_softmax(sink_ref, layer, h, blocks) for (_, h), blocks in zip(units, all_scores)]
    all_outs = [_attn_weighted(item, h, *pd) for (item, h), pd in zip(units, all_probs)]
    return [jnp.concatenate([o for outs in all_outs[i * N_KV_HEADS:(i + 1) * N_KV_HEADS] for o in outs],
                            axis=1) for i in range(len(items))]


def _attn_local_body(sink_ref, q_ref, kd_ref, vd_ref, kc_ref, vc_ref, o_ref, vt_ref, s_ref, *, layer,
                     n_blocks, group):
    blk = ATTN_BLOCK
    n_ctx_blocks = kc_ref.shape[0] // LANES
    n_key_blocks = 3 + n_ctx_blocks
    n_groups = n_blocks // group
    _transpose_values(vd_ref, vt_ref, 0)
    _transpose_values(vc_ref, vt_ref, n_blocks)
    key = lax.broadcasted_iota(jnp.int32, (blk, blk), 0)
    qry = lax.broadcasted_iota(jnp.int32, (blk, blk), 1)
    rows = lambda i: pl.ds(pl.multiple_of(i * blk, blk), blk)

    def item(n):
        prv, nxt = jnp.maximum(n - 1, 0), jnp.minimum(n + 1, n_blocks - 1)
        bias_prev = jnp.where((key >= qry) & (n > 0), 0.0, NEG_INF).astype(F32)
        bias_next = jnp.where((key <= qry) & (n < n_blocks - 1), 0.0, NEG_INF).astype(F32)
        k_rows = jnp.concatenate([kd_ref[rows(prv), :], kd_ref[rows(n), :], kd_ref[rows(nxt), :],
                                  kc_ref[...]], axis=0)
        vt = jnp.concatenate([vt_ref[prv], vt_ref[n], vt_ref[nxt]]
                             + [vt_ref[n_blocks + j] for j in range(n_ctx_blocks)], axis=1)
        return (q_ref[rows(n), :], k_rows, vt, [bias_prev, None, bias_next] + [None] * n_ctx_blocks)

    def store_scores(slot, u, unit):
        for j, sblk in enumerate(_attn_scores(*unit)):
            s_ref[slot, u, j * LANES:(j + 1) * LANES, :] = sblk

    def units_of(g):
        return [(item(g * group + i), h) for i in range(group) for h in range(N_KV_HEADS)]

    for u, unit in enumerate(units_of(0)):
        store_scores(0, u, unit)

    def finish_group(g, cur):
        next_units = units_of(g + 1) if g + 1 < n_groups else None
        outs = []
        for u, (it, h) in enumerate(units_of(g)):
            if next_units is not None:
                store_scores(1 - cur, u, next_units[u])
            blocks = [s_ref[cur, u, j * LANES:(j + 1) * LANES, :] for j in range(n_key_blocks)]
            outs.append(_attn_weighted(it, h, *_attn_softmax(sink_ref, layer, h, blocks)))
        for i in range(group):
            o_ref[rows(g * group + i), :] = jnp.concatenate(
                [o for per_head in outs[i * N_KV_HEADS:(i + 1) * N_KV_HEADS] for o in per_head], axis=1)

    for g in range(n_groups):
        finish_group(g, g % 2)


def _attn_ctx_body(sink_ref, q_ref, kc_ref, vc_ref, o_ref, vt_ref, *, layer):
    n_key_blocks = kc_ref.shape[0] // LANES
    _transpose_values(vc_ref, vt_ref, 0)
    vt = jnp.concatenate([vt_ref[j] for j in range(n_key_blocks)], axis=1)
    items = [(q_ref[i * ATTN_BLOCK:(i + 1) * ATTN_BLOCK, :], kc_ref[...], vt, [None] * n_key_blocks)
             for i in range(q_ref.shape[0] // ATTN_BLOCK)]
    for i, out in enumerate(_attend(sink_ref, layer, items)):
        o_ref[i * ATTN_BLOCK:(i + 1) * ATTN_BLOCK, :] = out


def _attention(sink, layer, q, kd, vd, kdc, vdc):
    b, s, _ = q.shape
    c = kdc.shape[1]
    kw = 2 * KV_W
    n_blocks = s // ATTN_BLOCK
    group = math.gcd(n_blocks, 4)
    idx = lambda bi: (bi, 0, 0)
    return pl.pallas_call(
        functools.partial(_attn_local_body, layer=layer, n_blocks=n_blocks, group=group),
        grid=(b,),
        in_specs=[
            pl.BlockSpec(memory_space=pltpu.SMEM),
            pl.BlockSpec((None, s, ATTN_W), idx),
            pl.BlockSpec((None, s, kw), idx), pl.BlockSpec((None, s, kw), idx),
            pl.BlockSpec((None, c, kw), idx), pl.BlockSpec((None, c, kw), idx),
        ],
        out_specs=pl.BlockSpec((None, s, ATTN_W), idx),
        out_shape=jax.ShapeDtypeStruct((b, s, ATTN_W), BF16),
        scratch_shapes=[pltpu.VMEM((n_blocks + c // LANES, kw, LANES), BF16),
                        pltpu.VMEM((2, group * N_KV_HEADS, 3 * ATTN_BLOCK + c, ATTN_W), F32)],
        compiler_params=_params(1),
        name="attn_local",
    )(sink, q, kd, vd, kdc, vdc)


def _attention_ctx(sink, layer, qc, kdc, vdc):
    b, c, _ = qc.shape
    kw = 2 * KV_W
    idx = lambda bi: (bi, 0, 0)
    return pl.pallas_call(
        functools.partial(_attn_ctx_body, layer=layer),
        grid=(b,),
        in_specs=[
            pl.BlockSpec(memory_space=pltpu.SMEM),
            pl.BlockSpec((None, c, ATTN_W), idx),
            pl.BlockSpec((None, c, kw), idx), pl.BlockSpec((None, c, kw), idx),
        ],
        out_specs=pl.BlockSpec((None, c, ATTN_W), idx),
        out_shape=jax.ShapeDtypeStruct((b, c, ATTN_W), BF16),
        scratch_shapes=[pltpu.VMEM((c // LANES, kw, LANES), BF16)],
        compiler_params=_params(1),
        name="attn_ctx",
    )(sink, qc, kdc, vdc)


def _pool_group(padded, t, length, window, w_mix, scale):
    halo = SUBLANES
    rows = padded.shape[0]
    n_rows = rows - 2 * halo
    acc = padded
    step = 1
    while step < window:
        acc = acc + pltpu.roll(acc, rows - step, 0)
        step *= 2
    back = (window - 1) // 2
    win = (pltpu.roll(acc, back, 0) if back else acc)[halo:halo + n_rows]
    lo = jnp.maximum(t - back, 0)
    hi = jnp.minimum(t + window // 2 + 1, length)
    mean = win / (hi - lo).astype(F32)
    dlt = (mean - padded[halo:halo + n_rows]).astype(BF16)
    return (jnp.dot(dlt, w_mix, preferred_element_type=F32) * scale).astype(BF16)


def _lru_body(*refs, n_ctx, n_lat, tile, with_ctx_out):
    if with_ctx_out:
        (rxc_ref, rxl_ref, ryc_ref, ryl_ref, cw_ref, cb_ref, wg_ref, ba_ref, bx_ref, lam_ref,
         ol_ref, oc_ref, pad_ref, xl_ref, hf_ref, hb_ref, xp_ref, a_ref, b_ref) = refs
    else:
        (rxc_ref, rxl_ref, ryl_ref, cw_ref, cb_ref, wg_ref, ba_ref, bx_ref, lam_ref,
         ol_ref, pad_ref, xl_ref, hf_ref, hb_ref, xp_ref, a_ref, b_ref) = refs
    halo = SUBLANES
    n_tot = n_ctx + n_lat
    n_slab = RNN_W // LANES

    zeros = jnp.zeros((halo, RNN_W), F32)
    c0, l0 = halo, 2 * halo + n_ctx
    pad_ref[0:halo, :] = zeros
    pad_ref[c0 + n_ctx:l0, :] = zeros
    pad_ref[l0 + n_lat:l0 + n_lat + halo, :] = zeros
    pad_ref[c0:c0 + n_ctx, :] = rxc_ref[...]
    pad_ref[l0:l0 + n_lat, :] = rxl_ref[...]

    cw = 0.5 * cw_ref[...]
    cb = 0.5 * cb_ref[...]

    def conv_tile(src0, dst_rows):
        rows = tile + 2 * halo
        p = pad_ref[src0 - halo:src0 + tile + halo, :]
        y = (cb + p[halo:halo + tile] * cw[1:2, :]
             + pltpu.roll(p, 1, 0)[halo:halo + tile] * cw[0:1, :]
             + pltpu.roll(p, rows - 1, 0)[halo:halo + tile] * cw[2:3, :]
             + pltpu.roll(p, rows - 2, 0)[halo:halo + tile] * cw[3:4, :])
        for s in range(n_slab):
            for d0 in dst_rows:
                xl_ref[s, d0:d0 + tile, :] = y[:, s * LANES:(s + 1) * LANES]

    for t0 in range(0, n_ctx, tile):
        conv_tile(c0 + t0, (t0, n_tot + t0))
    for t0 in range(0, n_lat, tile):
        conv_tile(l0 + t0, (n_ctx + t0,))

    lam = lam_ref[...]
    log_sig = jnp.minimum(lam, 0.0) - jnp.log1p(jnp.exp(-jnp.abs(lam)))
    rate = (0.5 * LOG2E * LRU_C) * log_sig
    half_ba = 0.5 * ba_ref[...]
    half_bx = 0.5 * bx_ref[...]
    half = RNN_W // 2

    def run_chunk(d, base, carry):
        res_ref = hf_ref if d == 0 else hb_ref
        win0 = 0 if d == 0 else n_ctx
        for v in range(SEG):
            for s in range(n_slab):
                xp_ref[d, v * SUBLANES:(v + 1) * SUBLANES, s * LANES:(s + 1) * LANES] = (
                    xl_ref[s, pl.ds(win0 + base + v, SUBLANES, stride=SEG), :])
        for j in range(2):
            ch = slice(j * half, (j + 1) * half)
            xj = xp_ref[d, :, ch]
            gates = jnp.dot(xj.astype(BF16), wg_ref[d, j], preferred_element_type=F32)
            t_r = jnp.tanh(gates[:, :half] + half_ba[d:d + 1, ch])
            t_i = jnp.tanh(gates[:, half:] + half_bx[d:d + 1, ch])
            a = jnp.exp2(t_r * rate[d:d + 1, ch] + rate[d:d + 1, ch])
            gap = 1.0 - a * a
            mult = gap * lax.rsqrt(jnp.maximum(gap, TINY))
            a_ref[d, :, ch] = a
            b_ref[d, :, ch] = (mult * xj) * (t_i + 1.0)
        order = range(SEG) if d == 0 else range(SEG - 1, -1, -1)
        h = jnp.zeros((SUBLANES, RNN_W), F32)
        acum = jnp.ones((SUBLANES, RNN_W), F32)
        for v in order:
            rows = slice(v * SUBLANES, (v + 1) * SUBLANES)
            av = a_ref[d, rows, :]
            h = av * h + b_ref[d, rows, :]
            acum = av * acum
            b_ref[d, rows, :] = h
            a_ref[d, rows, :] = acum
        seg_in = [None] * SUBLANES
        state = carry
        seg_order = range(SUBLANES) if d == 0 else range(SUBLANES - 1, -1, -1)
        for i in seg_order:
            seg_in[i] = state
            state = h[i:i + 1, :] + acum[i:i + 1, :] * state
        seg_state = jnp.concatenate(seg_in, axis=0)
        for v in range(SEG):
            rows = slice(v * SUBLANES, (v + 1) * SUBLANES)
            hv = b_ref[d, rows, :] + a_ref[d, rows, :] * seg_state
            for s in range(n_slab):
                res_ref[s, pl.ds(base + v, SUBLANES, stride=SEG), :] = hv[:, s * LANES:(s + 1) * LANES]
        return state

    n_chunks = n_tot // CHUNK

    def step(c, carries):
        cf, cb = carries
        cf = run_chunk(0, c * CHUNK, cf)
        cb = run_chunk(1, (n_chunks - 1 - c) * CHUNK, cb)
        return cf, cb

    zero_state = jnp.zeros((1, RNN_W), F32)
    lax.fori_loop(0, n_chunks, step, (zero_state, zero_state))

    def emit(out_ref, ry_ref, f0, b0, length):
        for t0 in range(0, length, tile):
            hsum = jnp.concatenate(
                [hf_ref[s, f0 + t0:f0 + t0 + tile, :] + hb_ref[s, b0 + t0:b0 + t0 + tile, :]
                 for s in range(n_slab)], axis=1)
            gate = ry_ref[t0:t0 + tile, :].astype(F32)
            out_ref[t0:t0 + tile, :] = (hsum * gate).astype(BF16)

    emit(ol_ref, ryl_ref, n_ctx, 0, n_lat)
    if with_ctx_out:
        emit(oc_ref, ryc_ref, 0, n_lat, n_ctx)


def _lru(layer, rxc, rxl, ryc, ryl, conv_w, conv_b, wg, b_a, b_x, lam):
    b, n_ctx, _ = rxc.shape
    n_lat = rxl.shape[1]
    n_tot = n_ctx + n_lat
    tile = math.gcd(n_ctx, n_lat, 256)
    assert n_tot % CHUNK == 0 and tile % SUBLANES == 0
    with_ctx_out = ryc is not None
    idx = lambda bi: (bi, 0, 0)
    seq = lambda n, dt=None: pl.BlockSpec((None, n, RNN_W), idx)
    n_slab = RNN_W // LANES
    in_specs = [seq(n_ctx), seq(n_lat)] + ([seq(n_ctx)] if with_ctx_out else []) + [seq(n_lat)]
    args = [rxc, rxl] + ([ryc] if with_ctx_out else []) + [ryl]
    for wgt in (conv_w, conv_b, wg, b_a, b_x, lam):
        in_specs.append(_layer_block(wgt, layer))
        args.append(wgt)
    out_specs = [seq(n_lat)] + ([seq(n_ctx)] if with_ctx_out else [])
    out_shape = [jax.ShapeDtypeStruct((b, n_lat, RNN_W), BF16)]
    if with_ctx_out:
        out_shape.append(jax.ShapeDtypeStruct((b, n_ctx, RNN_W), BF16))
    res = pl.pallas_call(
        functools.partial(_lru_body, n_ctx=n_ctx, n_lat=n_lat, tile=tile, with_ctx_out=with_ctx_out),
        grid=(b,),
        in_specs=in_specs,
        out_specs=out_specs,
        out_shape=out_shape,
        scratch_shapes=[
            pltpu.VMEM((n_tot + 3 * SUBLANES, RNN_W), F32),
            pltpu.VMEM((n_slab, n_tot + n_ctx, LANES), F32),
            pltpu.VMEM((n_slab, n_tot, LANES), F32),
            pltpu.VMEM((n_slab, n_tot, LANES), F32),
            pltpu.VMEM((2, CHUNK, RNN_W), F32),
            pltpu.VMEM((2, CHUNK, RNN_W), F32),
            pltpu.VMEM((2, CHUNK, RNN_W), F32),
        ],
        compiler_params=_params(1),
        name="rglru",
    )(*args)
    return (res[0], res[1]) if with_ctx_out else (res[0], None)


def _mix_ffn_body(*refs, d_ff, chunk, n_cast):
    (x_ref, mod_ref, at_ref, po_ref, rn_ref, gt_ref, wa_ref, wp_ref, wr_ref, wo_ref,
     gmix_ref, gpre_ref, wgu_ref, wd_ref, gpost_ref) = refs[:15]
    cast_in = refs[15:15 + n_cast]
    o_ref = refs[15 + n_cast]
    cast_out = refs[16 + n_cast:16 + 2 * n_cast]
    m_ref, h_ref = refs[16 + 2 * n_cast:]
    for src, dst in zip(cast_in, cast_out):
        dst[...] = src[...].astype(BF16)
    d = x_ref.shape[1]
    for rows in _row_halves(x_ref.shape[0]):
        merged = None
        for k, (br_ref, w_ref) in enumerate(((at_ref, wa_ref), (po_ref, wp_ref), (rn_ref, wr_ref))):
            t = gt_ref[rows, k * d:(k + 1) * d].astype(F32) * jnp.dot(
                br_ref[rows, :], w_ref[...], preferred_element_type=F32)
            merged = t if merged is None else merged + t
        m_ref[rows, :] = merged.astype(BF16)
        mix = jnp.dot(m_ref[rows, :], wo_ref[...], preferred_element_type=F32)
        o_ref[rows, :] = x_ref[rows, :] + mod_ref[2:3, :] * (_rms(mix) * gmix_ref[...])
        _norm_modulate_store(o_ref, mod_ref, gpre_ref, h_ref, 3, 4, rows)

    halves = _row_halves(x_ref.shape[0])
    accs = [None] * len(halves)
    for c0 in range(0, d_ff, chunk):
        c1 = min(c0 + chunk, d_ff)
        for i, rows in enumerate(halves):
            gate = jnp.dot(h_ref[rows, :], wgu_ref[:, c0:c1], preferred_element_type=F32)
            up = jnp.dot(h_ref[rows, :], wgu_ref[:, d_ff + c0:d_ff + c1], preferred_element_type=F32)
            act = ((gate * jax.nn.sigmoid(gate)) * up).astype(BF16)
            part = jnp.dot(act, wd_ref[c0:c1, :], preferred_element_type=F32)
            accs[i] = part if accs[i] is None else accs[i] + part
    for acc, rows in zip(accs, halves):
        o_ref[rows, :] = o_ref[rows, :] + mod_ref[5:6, :] * (_rms(acc) * gpost_ref[...])


def _mix_ffn(x2d, mod_all, layer, mod_row_of_tile, attn, pool, rnn, gates, wa, wp, wr, wo, g_mix,
             g_pre, w_gu, w_down, g_post, *, tm, cast_weights=(), cast_layer=0):
    n_tok, d = x2d.shape
    d_ff = w_down.shape[0]
    row = lambda i: (i, 0)
    tok = lambda arr: pl.BlockSpec((tm, arr.shape[1]), row)
    own = lambda wgt: (wgt[None], 0)
    params = (own(wa), own(wp), own(wr), own(wo), (g_mix, layer), (g_pre, layer), own(w_gu), own(w_down),
              (g_post, layer))
    cast_in, cast_out = _cast_streams(cast_weights, cast_layer, n_tok // tm)
    res = pl.pallas_call(
        functools.partial(_mix_ffn_body, d_ff=d_ff, chunk=1024, n_cast=len(cast_weights)),
        grid=(n_tok // tm,),
        in_specs=[tok(x2d), _mod_block(mod_all, layer, mod_row_of_tile),
                  tok(attn), tok(pool), tok(rnn), tok(gates)]
                 + [_layer_block(p, lyr) for p, lyr in params] + cast_in,
        out_specs=[pl.BlockSpec((tm, d), row)] + [o[0] for o in cast_out],
        out_shape=[jax.ShapeDtypeStruct((n_tok, d), F32)] + [o[1] for o in cast_out],
        scratch_shapes=[pltpu.VMEM((tm, d), BF16), pltpu.VMEM((tm, d), BF16)],
        compiler_params=_params(1),
        name="mix_ffn",
    )(x2d, mod_all, attn, pool, rnn, gates, *[p for p, _ in params], *cast_weights)
    return res[0] if not cast_weights else res


def _pack_gate_weights(w_a, w_x):
    def block_diag(w):
        bw = RNN_W // RNN_BLOCKS
        rows = [jnp.pad(w[:, :, h], ((0, 0), (0, 0), (0, 0), (bw * h, RNN_W - bw * (h + 1))))
                for h in range(RNN_BLOCKS)]
        return jnp.concatenate(rows, axis=2)
    bd_a, bd_x = block_diag(w_a), block_diag(w_x)
    half = RNN_W // 2
    halves = []
    for j in range(2):
        sl = slice(j * half, (j + 1) * half)
        halves.append(jnp.concatenate([bd_a[:, :, sl, sl], bd_x[:, :, sl, sl]], axis=3))
    return jnp.stack(halves, axis=2).astype(BF16)


def kernel(x, c, ctx, c_ctx, w_ada, b_ada, g_pre_mix, g_post_mix, g_pre_ffn, g_post_ffn, w_in, attn_sink, w_attn_o, pool_mix, pool_scale, w_pool_o, conv_w, conv_b, lru_w_a, lru_b_a, lru_w_x, lru_b_x, lru_lambda, w_rnn_o, w_out, w_gu, w_down):
    bsz, seq, d = x.shape
    n_ctx = ctx.shape[1]
    depth = w_ada.shape[0]
    tm = min(512, seq)
    tm_ctx = min(tm, n_ctx)

    mod_rows = -(-(bsz + 1) // SUBLANES) * SUBLANES
    cvec = jnp.zeros((mod_rows, d), F32).at[:bsz].set(c).at[bsz].set(c_ctx)
    mod_all = _adaln_mod(cvec, w_ada, b_ada).reshape(depth, mod_rows, 6, d)

    rope = _rope_tables(seq)
    lat_row = lambda i: i // (seq // tm)
    ctx_row = lambda i: bsz

    rowvec = lambda a: a.reshape(depth, 1, -1)
    w_in_l = w_in[0].astype(BF16)
    channel_weights = (w_attn_o, w_pool_o, w_rnn_o, w_out, w_gu, w_down)
    w_mix = pool_mix.astype(BF16)
    wg = _pack_gate_weights(lru_w_a, lru_w_x)
    g_pre, g_post = rowvec(g_pre_mix), rowvec(g_post_mix)
    gf_pre, gf_post = rowvec(g_pre_ffn), rowvec(g_post_ffn)
    p_scale, cv_b = rowvec(pool_scale), rowvec(conv_b)

    x2 = x.reshape(bsz * seq, d)
    c2 = ctx.reshape(bsz * n_ctx, d)
    r3 = lambda a, n: a.reshape(bsz, n, a.shape[-1])
    flat = lambda a: a.reshape(-1, a.shape[-1])
    for l in range(depth):
        need_ctx = l < depth - 1
        q, kd, vd, rx, ry, pool_l, gt, wa, wp, wr, wo, w_gu_b, w_dn_b = _inproj(
            x2, mod_all, l, lat_row, g_pre, w_in_l[None], w_mix, p_scale, tm=tm, tiles_per_seq=seq // tm,
            rope_tables=rope, kv_only=False, cast_weights=channel_weights)
        if need_ctx:
            qc, kdc, vdc, rxc, ryc, pool_c, gtc = _inproj(
                c2, mod_all, l, ctx_row, g_pre, w_in_l[None], w_mix, p_scale, tm=tm_ctx,
                tiles_per_seq=n_ctx // tm_ctx, rope_tables=None, kv_only=False)
        else:
            kdc, vdc, rxc = _inproj(
                c2, mod_all, l, ctx_row, g_pre, w_in_l[None], None, None, tm=tm_ctx, tiles_per_seq=1,
                rope_tables=None, kv_only=True)
            ryc = None

        kdc3, vdc3 = r3(kdc, n_ctx), r3(vdc, n_ctx)
        attn_l = _attention(attn_sink, l, r3(q, seq), r3(kd, seq), r3(vd, seq), kdc3, vdc3)
        rnn_l, rnn_c = _lru(l, r3(rxc, n_ctx), r3(rx, seq), None if ryc is None else r3(ryc, n_ctx),
                            r3(ry, seq), conv_w, cv_b, wg, lru_b_a, lru_b_x, lru_lambda)

        if need_ctx:
            x2, w_in_next = _mix_ffn(x2, mod_all, l, lat_row, flat(attn_l), pool_l, flat(rnn_l), gt, wa, wp,
                                     wr, wo, g_post, gf_pre, w_gu_b, w_dn_b, gf_post, tm=tm,
                                     cast_weights=(w_in,), cast_layer=l + 1)
        else:
            x2 = _mix_ffn(x2, mod_all, l, lat_row, flat(attn_l), pool_l, flat(rnn_l), gt, wa, wp, wr, wo,
                          g_post, gf_pre, w_gu_b, w_dn_b, gf_post, tm=tm)

        if need_ctx:
            attn_c = _attention_ctx(attn_sink, l, r3(qc, n_ctx), kdc3, vdc3)
            c2 = _mix_ffn(c2, mod_all, l, ctx_row, flat(attn_c), pool_c, flat(rnn_c), gtc, wa, wp, wr,
                          wo, g_post, gf_pre, w_gu_b, w_dn_b, gf_post, tm=math.gcd(tm, bsz * n_ctx))
            w_in_l = w_in_next
    return x2.reshape(bsz, seq, d)
```

```python
import functools
import math

import jax
import jax.numpy as jnp
import numpy as np
from jax import lax
from jax.experimental import pallas as pl
from jax.experimental.pallas import tpu as pltpu

F32 = jnp.float32
BF16 = jnp.bfloat16

GRID_W = 64
HEAD_DIM = 64
N_Q_HEADS = 8
N_KV_HEADS = 2
ATTN_W = N_Q_HEADS * HEAD_DIM
KV_W = N_KV_HEADS * HEAD_DIM
ATTN_BLOCK = 128
ROPE_BASE = 10000.0
POOL_WINDOWS = (2, 4, 8, 16)
POOL_W = 512
POOL_GW = POOL_W // len(POOL_WINDOWS)
RNN_W = 512
RNN_BLOCKS = 8
CONV_W = 4
LRU_C = 8.0
N_BRANCH = 3
EPS = 1e-6
NEG_INF = -1e30
LOG2E = 1.4426950408889634
TINY = 1e-30

LANES = 128
SUBLANES = 8
VMEM_LIMIT_BYTES = 56 * 1024 * 1024

SEG = 36
CHUNK = SUBLANES * SEG


def _params(n_axes):
    return pltpu.CompilerParams(
        dimension_semantics=("arbitrary",) * n_axes, vmem_limit_bytes=VMEM_LIMIT_BYTES)


def _layer_block(arr, layer, block=None):
    block = tuple(arr.shape[1:]) if block is None else block
    index = (layer,) + (0,) * len(block)
    return pl.BlockSpec((None,) + block, lambda *_: index, pipeline_mode=pl.Buffered(1))


def _mod_block(mod_all, layer, mod_row_of_tile):
    return pl.BlockSpec((None, None) + mod_all.shape[2:], lambda i: (layer, mod_row_of_tile(i), 0, 0))


def _rms(xf):
    return xf * lax.rsqrt(jnp.mean(xf * xf, axis=-1, keepdims=True) + EPS)


def _mod_body(c_ref, w_ref, b_ref, o_ref):
    c = c_ref[...]
    s = (c * jax.nn.sigmoid(c)).astype(BF16)
    o_ref[0] = jnp.dot(s, w_ref[0].astype(BF16), preferred_element_type=F32) + b_ref[0]


def _adaln_mod(cvec, w_ada, b_ada):
    n_layers, d, n6 = w_ada.shape
    rows = cvec.shape[0]
    tn = n6 // 4
    return pl.pallas_call(
        _mod_body,
        grid=(n_layers, n6 // tn),
        in_specs=[
            pl.BlockSpec((rows, d), lambda l, j: (0, 0)),
            pl.BlockSpec((1, d, tn), lambda l, j: (l, 0, j)),
            pl.BlockSpec((1, 1, tn), lambda l, j: (l, 0, j)),
        ],
        out_specs=pl.BlockSpec((1, rows, tn), lambda l, j: (l, 0, j)),
        out_shape=jax.ShapeDtypeStruct((n_layers, rows, n6), F32),
        compiler_params=_params(2),
        name="adaln_mod",
    )(cvec, w_ada, b_ada.reshape(n_layers, 1, n6))


def _rope_tables(seq_len):
    pos = np.arange(seq_len)
    row = (pos // GRID_W).astype(np.float32)
    col = (pos % GRID_W).astype(np.float32)
    half = HEAD_DIM // 2
    quarter = half // 2
    inv = (np.float32(ROPE_BASE) ** (-(np.arange(quarter, dtype=np.float32) * np.float32(2.0 / half)))
           ).astype(np.float32)
    j = np.arange(LANES) % HEAD_DIM
    is_col = j >= half
    second = (j % half) >= quarter
    freq = inv[j % quarter]
    ang = (np.where(is_col[None, :], col[:, None], row[:, None]) * freq[None, :]).astype(np.float32)
    cos, sin = np.cos(ang), np.sin(ang)
    sin_prev = np.where(second[None, :], sin, 0.0)
    sin_next = np.where(second[None, :], 0.0, -sin)
    return tuple(jnp.asarray(t, F32) for t in (cos, sin_prev, sin_next))


def _row_halves(n_rows):
    if n_rows % (2 * SUBLANES * 2) or n_rows < 512:
        return [slice(0, n_rows)]
    return [slice(0, n_rows // 2), slice(n_rows // 2, n_rows)]


def _store_dup_heads(dst_ref, rows, kv):
    low = lax.broadcasted_iota(jnp.int32, kv.shape, 1) < HEAD_DIM
    swapped = pltpu.roll(kv, HEAD_DIM, 1)
    dst_ref[rows, 0:LANES] = jnp.where(low, kv, swapped).astype(BF16)
    dst_ref[rows, LANES:2 * LANES] = jnp.where(low, swapped, kv).astype(BF16)


def _norm_modulate_store(x_ref, mod_ref, g_ref, h_ref, shift_row, scale_row, rows):
    y = _rms(x_ref[rows, :]) * g_ref[...]
    h = y * (1.0 + mod_ref[scale_row:scale_row + 1, :]) + mod_ref[shift_row:shift_row + 1, :]
    h_ref[rows, :] = h.astype(BF16)


def _inproj_full_body(*refs, rope, tiles_per_seq, n_cast):
    refs = list(refs)
    x_ref, xp_ref, xn_ref, mod_ref, g_ref, w_ref, wmix_ref, psc_ref = refs[:8]
    del refs[:8]
    if rope:
        cos_ref, sp_ref, sn_ref = refs[:3]
        del refs[:3]
    cast_in = refs[:n_cast]
    q_ref, kd_ref, vd_ref, rx_ref, ry_ref, po_ref, gt_ref = refs[n_cast:n_cast + 7]
    cast_out = refs[n_cast + 7:2 * n_cast + 7]
    h_ref = refs[2 * n_cast + 7]
    for src, dst in zip(cast_in, cast_out):
        dst[...] = src[...].astype(BF16)
    quarter = HEAD_DIM // 4
    tm, d = x_ref.shape
    halo = SUBLANES
    tile_in_seq = pl.program_id(0) % tiles_per_seq
    pu0 = ATTN_W + 2 * KV_W + 2 * RNN_W

    def modulated(xf):
        y = _rms(xf) * g_ref[...]
        return y * (1.0 + mod_ref[1:2, :]) + mod_ref[0:1, :]

    pack = 2 * SUBLANES
    zeros = jnp.zeros((SUBLANES, d), F32)
    before = jnp.where(tile_in_seq == 0, 0.0, modulated(xp_ref[...]))
    after = jnp.where(tile_in_seq == tiles_per_seq - 1, 0.0, modulated(xn_ref[...]))
    h_ref[0:pack, :] = jnp.concatenate([zeros, before], axis=0).astype(BF16)
    h_ref[pack + tm:2 * pack + tm, :] = jnp.concatenate([after, zeros], axis=0).astype(BF16)
    for rows in _row_halves(tm):
        h_ref[pack + rows.start:pack + rows.stop, :] = modulated(x_ref[rows, :]).astype(BF16)

    for rows in _row_halves(tm):
        r0, n_rows = rows.start, rows.stop - rows.start
        hrows = slice(pack + r0, pack + rows.stop)

        def proj(c0, c1):
            return jnp.dot(h_ref[hrows, :], w_ref[:, c0:c1], preferred_element_type=F32)

        def rotary(pc):
            if not rope:
                return pc
            return (pc * cos_ref[rows, :]
                    + pltpu.roll(pc, quarter, 1) * sp_ref[rows, :]
                    + pltpu.roll(pc, LANES - quarter, 1) * sn_ref[rows, :])

        padded = jnp.dot(h_ref[r0:r0 + n_rows + 2 * pack, :], w_ref[:, pu0:pu0 + POOL_W],
                         preferred_element_type=F32)[pack - halo:pack + n_rows + halo]
        t = tile_in_seq * tm + r0 + lax.broadcasted_iota(jnp.int32, (n_rows, POOL_GW), 0)

        def pool_group(gi):
            lanes = slice(gi * POOL_GW, (gi + 1) * POOL_GW)
            po_ref[rows, lanes] = _pool_group(padded[:, lanes], t, tiles_per_seq * tm, POOL_WINDOWS[gi],
                                              wmix_ref[gi], psc_ref[:, lanes])

        def seg_q(c):
            pair = proj(c * 2 * LANES, (c + 1) * 2 * LANES)
            for half in range(2):
                col = (2 * c + half) * LANES
                pc = rotary(pair[:, half * LANES:(half + 1) * LANES])
                q_ref[rows, col:col + LANES] = (pc * (HEAD_DIM ** -0.5 * LOG2E)).astype(BF16)

        def seg_kv():
            kv = proj(ATTN_W, ATTN_W + 2 * KV_W)
            _store_dup_heads(kd_ref, rows, rotary(kv[:, :KV_W]))
            _store_dup_heads(vd_ref, rows, kv[:, KV_W:])

        rx0 = ATTN_W + 2 * KV_W

        def seg_rx():
            rx_ref[rows, :] = proj(rx0, rx0 + RNN_W)

        def seg_ry():
            ry_ref[rows, :] = jax.nn.gelu(proj(rx0 + RNN_W, rx0 + 2 * RNN_W)).astype(BF16)

        gl0 = pu0 + POOL_W

        def seg_gate(c):
            gl = proj(gl0 + c * d, gl0 + (c + 1) * d)
            gt_ref[rows, c * d:(c + 1) * d] = jax.nn.sigmoid(gl).astype(BF16)

        seg_gate(0)
        pool_group(0)
        seg_q(0)
        seg_gate(1)
        pool_group(1)
        seg_q(1)
        seg_gate(2)
        pool_group(2)
        seg_kv()
        seg_ry()
        pool_group(3)
        seg_rx()


def _inproj_kv_body(x_ref, mod_ref, g_ref, w_ref, kd_ref, vd_ref, rx_ref, h_ref):
    rows = slice(0, x_ref.shape[0])
    _norm_modulate_store(x_ref, mod_ref, g_ref, h_ref, 0, 1, rows)
    kv = jnp.dot(h_ref[...], w_ref[:, ATTN_W:ATTN_W + 2 * KV_W], preferred_element_type=F32)
    _store_dup_heads(kd_ref, rows, kv[:, :KV_W])
    _store_dup_heads(vd_ref, rows, kv[:, KV_W:])
    rx_ref[...] = jnp.dot(h_ref[...], w_ref[:, ATTN_W + 2 * KV_W:ATTN_W + 2 * KV_W + RNN_W],
                          preferred_element_type=F32)


def _cast_slab(weight, n_steps):
    rows = weight.shape[1]
    for hold in (1, 2, 4, 8):
        if n_steps % hold == 0 and rows % (n_steps // hold) == 0 and (rows // (n_steps // hold)) % 16 == 0:
            return rows // (n_steps // hold), hold
    raise ValueError("no bf16-tileable row slab for %s over %d steps" % (weight.shape, n_steps))


def _cast_streams(cast_weights, layer, n_steps):
    in_specs, outs = [], []
    for wgt in cast_weights:
        slab, hold = _cast_slab(wgt, n_steps)
        cols = wgt.shape[2]
        in_specs.append(pl.BlockSpec((None, slab, cols), lambda i, hold=hold: (layer, i // hold, 0)))
        outs.append((pl.BlockSpec((slab, cols), lambda i, hold=hold: (i // hold, 0)),
                     jax.ShapeDtypeStruct(wgt.shape[1:], BF16)))
    return in_specs, outs


def _inproj(x2d, mod_all, layer, mod_row_of_tile, g, w, w_mix, p_scale, *, tm, tiles_per_seq, rope_tables,
            kv_only, cast_weights=()):
    n_tok, d = x2d.shape
    grid = (n_tok // tm,)
    row = lambda i: (i, 0)
    in_specs = [pl.BlockSpec((tm, d), row)]
    args = [x2d]
    if not kv_only:
        per_tile = tm // SUBLANES
        in_specs += [pl.BlockSpec((SUBLANES, d), lambda i: (jnp.maximum(i * per_tile - 1, 0), 0)),
                     pl.BlockSpec((SUBLANES, d),
                                  lambda i: (jnp.minimum((i + 1) * per_tile, n_tok // SUBLANES - 1), 0))]
        args += [x2d, x2d]
    w_cols = ATTN_W + 2 * KV_W + RNN_W if kv_only else w.shape[2]
    in_specs += [_mod_block(mod_all, layer, mod_row_of_tile), _layer_block(g, layer),
                 _layer_block(w, 0, (d, w_cols))]
    args += [mod_all, g, w]
    if not kv_only:
        in_specs += [_layer_block(w_mix, layer), _layer_block(p_scale, layer)]
        args += [w_mix, p_scale]
    if rope_tables is not None:
        in_specs += [pl.BlockSpec((tm, LANES), lambda i: (i % tiles_per_seq, 0))] * 3
        args += list(rope_tables)
    cast_in, cast_outs = _cast_streams(cast_weights, layer, grid[0])
    in_specs += cast_in
    args += list(cast_weights)

    def out(width, dtype):
        return pl.BlockSpec((tm, width), row), jax.ShapeDtypeStruct((n_tok, width), dtype)

    if kv_only:
        outs = [out(2 * KV_W, BF16), out(2 * KV_W, BF16), out(RNN_W, F32)]
        body = _inproj_kv_body
        h_rows = tm
    else:
        outs = [out(ATTN_W, BF16), out(2 * KV_W, BF16), out(2 * KV_W, BF16), out(RNN_W, F32),
                out(RNN_W, BF16), out(POOL_W, BF16), out(N_BRANCH * d, BF16)] + cast_outs
        body = functools.partial(_inproj_full_body, rope=rope_tables is not None,
                                 tiles_per_seq=tiles_per_seq, n_cast=len(cast_outs))
        h_rows = tm + 4 * SUBLANES
    return pl.pallas_call(
        body,
        grid=grid,
        in_specs=in_specs,
        out_specs=[o[0] for o in outs],
        out_shape=[o[1] for o in outs],
        scratch_shapes=[pltpu.VMEM((h_rows, d), BF16)],
        compiler_params=_params(1),
        name="inproj_kv" if kv_only else "inproj",
    )(*args)


def _transpose_values(v_ref, vt_ref, first_block):
    for j in range(v_ref.shape[0] // LANES):
        blk = v_ref[j * LANES:(j + 1) * LANES, :].astype(F32)
        vt_ref[first_block + j] = blk.T.astype(BF16)


HEADS_PER_KV = N_Q_HEADS // N_KV_HEADS


def _attn_scores(item, h):
    q_blk, k_rows, _, biases = item
    zero = jnp.zeros((), BF16)
    low = lax.broadcasted_iota(jnp.int32, (q_blk.shape[0], LANES), 1) < HEAD_DIM
    k_h = k_rows[:, h * LANES:(h + 1) * LANES]
    stacked = []
    for c2 in range(HEADS_PER_KV // 2):
        qc = q_blk[:, (2 * h + c2) * LANES:(2 * h + c2 + 1) * LANES]
        stacked += [jnp.where(low, qc, zero), jnp.where(low, zero, qc)]
    q_rows = jnp.concatenate(stacked, axis=0)
    s = lax.dot_general(k_h, q_rows, (((1,), (1,)), ((), ())), preferred_element_type=F32)
    blocks = []
    for j, bias in enumerate(biases):
        blk = s[j * LANES:(j + 1) * LANES, :]
        if bias is not None:
            blk = blk + jnp.concatenate([bias] * HEADS_PER_KV, axis=1)
        blocks.append(blk)
    return blocks


def _attn_softmax(sink_ref, layer, h, blocks):
    tq = blocks[0].shape[1] // HEADS_PER_KV
    sink_row = jnp.concatenate(
        [jnp.full((1, tq), sink_ref[layer, h * HEADS_PER_KV + g] * LOG2E, F32) for g in range(HEADS_PER_KV)],
        axis=1)
    m = jnp.maximum(jnp.max(functools.reduce(jnp.maximum, blocks), axis=0, keepdims=True), sink_row)
    probs = [jnp.exp2(blk - m) for blk in blocks]
    denom = jnp.sum(functools.reduce(jnp.add, probs), axis=0, keepdims=True) + jnp.exp2(sink_row - m)
    return jnp.concatenate(probs, axis=0).astype(BF16), denom


def _attn_weighted(item, h, p, denom):
    tq = item[0].shape[0]
    vt_h = item[2][h * LANES:(h + 1) * LANES, :]
    o = jnp.dot(vt_h, p, preferred_element_type=F32) * (1.0 / denom)
    first_copy = lax.broadcasted_iota(jnp.int32, (LANES, tq), 0) < HEAD_DIM
    return [jnp.where(first_copy, o[:, (2 * c2) * tq:(2 * c2 + 1) * tq],
                      o[:, (2 * c2 + 1) * tq:(2 * c2 + 2) * tq]).T.astype(BF16)
            for c2 in range(HEADS_PER_KV // 2)]


def _attend(sink_ref, layer, items):
    units = [(item, h) for item in items for h in range(N_KV_HEADS)]
    all_scores = [_attn_scores(item, h) for item, h in units]
    all_probs = [_attn_softmax(sink_ref, layer, h, blocks) for (_, h), blocks in zip(units, all_scores)]
    all_outs = [_attn_weighted(item, h, *pd) for (item, h), pd in zip(units, all_probs)]
    return [jnp.concatenate([o for outs in all_outs[i * N_KV_HEADS:(i + 1) * N_KV_HEADS] for o in outs],
                            axis=1) for i in range(len(items))]


def _attn_local_body(sink_ref, q_ref, kd_ref, vd_ref, kc_ref, vc_ref, o_ref, vt_ref, s_ref, *, layer,
                     n_blocks, group):
    blk = ATTN_BLOCK
    n_ctx_blocks = kc_ref.shape[0] // LANES
    n_key_blocks = 3 + n_ctx_blocks
    n_groups = n_blocks // group
    _transpose_values(vd_ref, vt_ref, 0)
    _transpose_values(vc_ref, vt_ref, n_blocks)
    key = lax.broadcasted_iota(jnp.int32, (blk, blk), 0)
    qry = lax.broadcasted_iota(jnp.int32, (blk, blk), 1)
    rows = lambda i: pl.ds(pl.multiple_of(i * blk, blk), blk)

    def item(n):
        prv, nxt = jnp.maximum(n - 1, 0), jnp.minimum(n + 1, n_blocks - 1)
        bias_prev = jnp.where((key >= qry) & (n > 0), 0.0, NEG_INF).astype(F32)
        bias_next = jnp.where((key <= qry) & (n < n_blocks - 1), 0.0, NEG_INF).astype(F32)
        k_rows = jnp.concatenate([kd_ref[rows(prv), :], kd_ref[rows(n), :], kd_ref[rows(nxt), :],
                                  kc_ref[...]], axis=0)
        vt = jnp.concatenate([vt_ref[prv], vt_ref[n], vt_ref[nxt]]
                             + [vt_ref[n_blocks + j] for j in range(n_ctx_blocks)], axis=1)
        return (q_ref[rows(n), :], k_rows, vt, [bias_prev, None, bias_next] + [None] * n_ctx_blocks)

    def store_scores(slot, u, unit):
        for j, sblk in enumerate(_attn_scores(*unit)):
            s_ref[slot, u, j * LANES:(j + 1) * LANES, :] = sblk

    def units_of(g):
        return [(item(g * group + i), h) for i in range(group) for h in range(N_KV_HEADS)]

    for u, unit in enumerate(units_of(0)):
        store_scores(0, u, unit)

    def finish_group(g, cur):
        next_units = units_of(jnp.minimum(g + 1, n_groups - 1))
        outs = []
        for u, (it, h) in enumerate(units_of(g)):
            store_scores(1 - cur, u, next_units[u])
            blocks = [s_ref[cur, u, j * LANES:(j + 1) * LANES, :] for j in range(n_key_blocks)]
            outs.append(_attn_weighted(it, h, *_attn_softmax(sink_ref, layer, h, blocks)))
        for i in range(group):
            o_ref[rows(g * group + i), :] = jnp.concatenate(
                [o for per_head in outs[i * N_KV_HEADS:(i + 1) * N_KV_HEADS] for o in per_head], axis=1)

    def body(pair, carry):
        finish_group(2 * pair, 0)
        finish_group(2 * pair + 1, 1)
        return carry

    lax.fori_loop(0, n_groups // 2, body, 0)


def _attn_ctx_body(sink_ref, q_ref, kc_ref, vc_ref, o_ref, vt_ref, *, layer):
    n_key_blocks = kc_ref.shape[0] // LANES
    _transpose_values(vc_ref, vt_ref, 0)
    vt = jnp.concatenate([vt_ref[j] for j in range(n_key_blocks)], axis=1)
    items = [(q_ref[i * ATTN_BLOCK:(i + 1) * ATTN_BLOCK, :], kc_ref[...], vt, [None] * n_key_blocks)
             for i in range(q_ref.shape[0] // ATTN_BLOCK)]
    for i, out in enumerate(_attend(sink_ref, layer, items)):
        o_ref[i * ATTN_BLOCK:(i + 1) * ATTN_BLOCK, :] = out


def _attention(sink, layer, q, kd, vd, kdc, vdc):
    b, s, _ = q.shape
    c = kdc.shape[1]
    kw = 2 * KV_W
    n_blocks = s // ATTN_BLOCK
    assert n_blocks % 2 == 0
    group = math.gcd(n_blocks // 2, 4)
    idx = lambda bi: (bi, 0, 0)
    return pl.pallas_call(
        functools.partial(_attn_local_body, layer=layer, n_blocks=n_blocks, group=group),
        grid=(b,),
        in_specs=[
            pl.BlockSpec(memory_space=pltpu.SMEM),
            pl.BlockSpec((None, s, ATTN_W), idx),
            pl.BlockSpec((None, s, kw), idx), pl.BlockSpec((None, s, kw), idx),
            pl.BlockSpec((None, c, kw), idx), pl.BlockSpec((None, c, kw), idx),
        ],
        out_specs=pl.BlockSpec((None, s, ATTN_W), idx),
        out_shape=jax.ShapeDtypeStruct((b, s, ATTN_W), BF16),
        scratch_shapes=[pltpu.VMEM((n_blocks + c // LANES, kw, LANES), BF16),
                        pltpu.VMEM((2, group * N_KV_HEADS, 3 * ATTN_BLOCK + c, ATTN_W), F32)],
        compiler_params=_params(1),
        name="attn_local",
    )(sink, q, kd, vd, kdc, vdc)


def _attention_ctx(sink, layer, qc, kdc, vdc):
    b, c, _ = qc.shape
    kw = 2 * KV_W
    idx = lambda bi: (bi, 0, 0)
    return pl.pallas_call(
        functools.partial(_attn_ctx_body, layer=layer),
        grid=(b,),
        in_specs=[
            pl.BlockSpec(memory_space=pltpu.SMEM),
            pl.BlockSpec((None, c, ATTN_W), idx),
            pl.BlockSpec((None, c, kw), idx), pl.BlockSpec((None, c, kw), idx),
        ],
        out_specs=pl.BlockSpec((None, c, ATTN_W), idx),
        out_shape=jax.ShapeDtypeStruct((b, c, ATTN_W), BF16),
        scratch_shapes=[pltpu.VMEM((c // LANES, kw, LANES), BF16)],
        compiler_params=_params(1),
        name="attn_ctx",
    )(sink, qc, kdc, vdc)


def _pool_group(padded, t, length, window, w_mix, scale):
    halo = SUBLANES
    rows = padded.shape[0]
    n_rows = rows - 2 * halo
    acc = padded
    step = 1
    while step < window:
        acc = acc + pltpu.roll(acc, rows - step, 0)
        step *= 2
    back = (window - 1) // 2
    win = (pltpu.roll(acc, back, 0) if back else acc)[halo:halo + n_rows]
    lo = jnp.maximum(t - back, 0)
    hi = jnp.minimum(t + window // 2 + 1, length)
    mean = win / (hi - lo).astype(F32)
    dlt = (mean - padded[halo:halo + n_rows]).astype(BF16)
    return (jnp.dot(dlt, w_mix, preferred_element_type=F32) * scale).astype(BF16)


def _lru_body(*refs, n_ctx, n_lat, tile, with_ctx_out):
    if with_ctx_out:
        (rxc_ref, rxl_ref, ryc_ref, ryl_ref, cw_ref, cb_ref, wg_ref, ba_ref, bx_ref, lam_ref,
         ol_ref, oc_ref, pad_ref, xl_ref, hf_ref, hb_ref, xp_ref, a_ref, b_ref) = refs
    else:
        (rxc_ref, rxl_ref, ryl_ref, cw_ref, cb_ref, wg_ref, ba_ref, bx_ref, lam_ref,
         ol_ref, pad_ref, xl_ref, hf_ref, hb_ref, xp_ref, a_ref, b_ref) = refs
    halo = SUBLANES
    n_tot = n_ctx + n_lat
    n_slab = RNN_W // LANES

    zeros = jnp.zeros((halo, RNN_W), F32)
    c0, l0 = halo, 2 * halo + n_ctx
    pad_ref[0:halo, :] = zeros
    pad_ref[c0 + n_ctx:l0, :] = zeros
    pad_ref[l0 + n_lat:l0 + n_lat + halo, :] = zeros
    pad_ref[c0:c0 + n_ctx, :] = rxc_ref[...]
    pad_ref[l0:l0 + n_lat, :] = rxl_ref[...]

    cw = 0.5 * cw_ref[...]
    cb = 0.5 * cb_ref[...]

    def conv_tile(src0, dst_rows):
        rows = tile + 2 * halo
        p = pad_ref[src0 - halo:src0 + tile + halo, :]
        y = (cb + p[halo:halo + tile] * cw[1:2, :]
             + pltpu.roll(p, 1, 0)[halo:halo + tile] * cw[0:1, :]
             + pltpu.roll(p, rows - 1, 0)[halo:halo + tile] * cw[2:3, :]
             + pltpu.roll(p, rows - 2, 0)[halo:halo + tile] * cw[3:4, :])
        for s in range(n_slab):
            for d0 in dst_rows:
                xl_ref[s, d0:d0 + tile, :] = y[:, s * LANES:(s + 1) * LANES]

    for t0 in range(0, n_ctx, tile):
        conv_tile(c0 + t0, (t0, n_tot + t0))
    for t0 in range(0, n_lat, tile):
        conv_tile(l0 + t0, (n_ctx + t0,))

    lam = lam_ref[...]
    log_sig = jnp.minimum(lam, 0.0) - jnp.log1p(jnp.exp(-jnp.abs(lam)))
    rate = (0.5 * LOG2E * LRU_C) * log_sig
    half_ba = 0.5 * ba_ref[...]
    half_bx = 0.5 * bx_ref[...]
    half = RNN_W // 2

    def run_chunk(d, base, carry):
        res_ref = hf_ref if d == 0 else hb_ref
        win0 = 0 if d == 0 else n_ctx
        for v in range(SEG):
            for s in range(n_slab):
                xp_ref[d, v * SUBLANES:(v + 1) * SUBLANES, s * LANES:(s + 1) * LANES] = (
                    xl_ref[s, pl.ds(win0 + base + v, SUBLANES, stride=SEG), :])
        for j in range(2):
            ch = slice(j * half, (j + 1) * half)
            xj = xp_ref[d, :, ch]
            gates = jnp.dot(xj.astype(BF16), wg_ref[d, j], preferred_element_type=F32)
            t_r = jnp.tanh(gates[:, :half] + half_ba[d:d + 1, ch])
            t_i = jnp.tanh(gates[:, half:] + half_bx[d:d + 1, ch])
            a = jnp.exp2(t_r * rate[d:d + 1, ch] + rate[d:d + 1, ch])
            gap = 1.0 - a * a
            mult = gap * lax.rsqrt(jnp.maximum(gap, TINY))
            a_ref[d, :, ch] = a
            b_ref[d, :, ch] = (mult * xj) * (t_i + 1.0)
        order = range(SEG) if d == 0 else range(SEG - 1, -1, -1)
        h = jnp.zeros((SUBLANES, RNN_W), F32)
        acum = jnp.ones((SUBLANES, RNN_W), F32)
        for v in order:
            rows = slice(v * SUBLANES, (v + 1) * SUBLANES)
            av = a_ref[d, rows, :]
            h = av * h + b_ref[d, rows, :]
            acum = av * acum
            b_ref[d, rows, :] = h
            a_ref[d, rows, :] = acum
        seg_in = [None] * SUBLANES
        state = carry
        seg_order = range(SUBLANES) if d == 0 else range(SUBLANES - 1, -1, -1)
        for i in seg_order:
            seg_in[i] = state
            state = h[i:i + 1, :] + acum[i:i + 1, :] * state
        seg_state = jnp.concatenate(seg_in, axis=0)
        for v in range(SEG):
            rows = slice(v * SUBLANES, (v + 1) * SUBLANES)
            hv = b_ref[d, rows, :] + a_ref[d, rows, :] * seg_state
            for s in range(n_slab):
                res_ref[s, pl.ds(base + v, SUBLANES, stride=SEG), :] = hv[:, s * LANES:(s + 1) * LANES]
        return state

    n_chunks = n_tot // CHUNK

    def step(c, carries):
        cf, cb = carries
        cf = run_chunk(0, c * CHUNK, cf)
        cb = run_chunk(1, (n_chunks - 1 - c) * CHUNK, cb)
        return cf, cb

    zero_state = jnp.zeros((1, RNN_W), F32)
    lax.fori_loop(0, n_chunks, step, (zero_state, zero_state))

    def emit(out_ref, ry_ref, f0, b0, length):
        for t0 in range(0, length, tile):
            hsum = jnp.concatenate(
                [hf_ref[s, f0 + t0:f0 + t0 + tile, :] + hb_ref[s, b0 + t0:b0 + t0 + tile, :]
                 for s in range(n_slab)], axis=1)
            gate = ry_ref[t0:t0 + tile, :].astype(F32)
            out_ref[t0:t0 + tile, :] = (hsum * gate).astype(BF16)

    emit(ol_ref, ryl_ref, n_ctx, 0, n_lat)
    if with_ctx_out:
        emit(oc_ref, ryc_ref, 0, n_lat, n_ctx)


def _lru(layer, rxc, rxl, ryc, ryl, conv_w, conv_b, wg, b_a, b_x, lam):
    b, n_ctx, _ = rxc.shape
    n_lat = rxl.shape[1]
    n_tot = n_ctx + n_lat
    tile = math.gcd(n_ctx, n_lat, 256)
    assert n_tot % CHUNK == 0 and tile % SUBLANES == 0
    with_ctx_out = ryc is not None
    idx = lambda bi: (bi, 0, 0)
    seq = lambda n, dt=None: pl.BlockSpec((None, n, RNN_W), idx)
    n_slab = RNN_W // LANES
    in_specs = [seq(n_ctx), seq(n_lat)] + ([seq(n_ctx)] if with_ctx_out else []) + [seq(n_lat)]
    args = [rxc, rxl] + ([ryc] if with_ctx_out else []) + [ryl]
    for wgt in (conv_w, conv_b, wg, b_a, b_x, lam):
        in_specs.append(_layer_block(wgt, layer))
        args.append(wgt)
    out_specs = [seq(n_lat)] + ([seq(n_ctx)] if with_ctx_out else [])
    out_shape = [jax.ShapeDtypeStruct((b, n_lat, RNN_W), BF16)]
    if with_ctx_out:
        out_shape.append(jax.ShapeDtypeStruct((b, n_ctx, RNN_W), BF16))
    res = pl.pallas_call(
        functools.partial(_lru_body, n_ctx=n_ctx, n_lat=n_lat, tile=tile, with_ctx_out=with_ctx_out),
        grid=(b,),
        in_specs=in_specs,
        out_specs=out_specs,
        out_shape=out_shape,
        scratch_shapes=[
            pltpu.VMEM((n_tot + 3 * SUBLANES, RNN_W), F32),
            pltpu.VMEM((n_slab, n_tot + n_ctx, LANES), F32),
            pltpu.VMEM((n_slab, n_tot, LANES), F32),
            pltpu.VMEM((n_slab, n_tot, LANES), F32),
            pltpu.VMEM((2, CHUNK, RNN_W), F32),
            pltpu.VMEM((2, CHUNK, RNN_W), F32),
            pltpu.VMEM((2, CHUNK, RNN_W), F32),
        ],
        compiler_params=_params(1),
        name="rglru",
    )(*args)
    return (res[0], res[1]) if with_ctx_out else (res[0], None)


def _mix_ffn_body(*refs, d_ff, chunk, n_cast):
    (x_ref, mod_ref, at_ref, po_ref, rn_ref, gt_ref, wa_ref, wp_ref, wr_ref, wo_ref,
     gmix_ref, gpre_ref, wgu_ref, wd_ref, gpost_ref) = refs[:15]
    cast_in = refs[15:15 + n_cast]
    o_ref = refs[15 + n_cast]
    cast_out = refs[16 + n_cast:16 + 2 * n_cast]
    m_ref, h_ref = refs[16 + 2 * n_cast:]
    for src, dst in zip(cast_in, cast_out):
        dst[...] = src[...].astype(BF16)
    d = x_ref.shape[1]
    for rows in _row_halves(x_ref.shape[0]):
        merged = None
        for k, (br_ref, w_ref) in enumerate(((at_ref, wa_ref), (po_ref, wp_ref), (rn_ref, wr_ref))):
            t = gt_ref[rows, k * d:(k + 1) * d].astype(F32) * jnp.dot(
                br_ref[rows, :], w_ref[...], preferred_element_type=F32)
            merged = t if merged is None else merged + t
        m_ref[rows, :] = merged.astype(BF16)
        mix = jnp.dot(m_ref[rows, :], wo_ref[...], preferred_element_type=F32)
        o_ref[rows, :] = x_ref[rows, :] + mod_ref[2:3, :] * (_rms(mix) * gmix_ref[...])
        _norm_modulate_store(o_ref, mod_ref, gpre_ref, h_ref, 3, 4, rows)

    halves = _row_halves(x_ref.shape[0])
    accs = [None] * len(halves)
    for c0 in range(0, d_ff, chunk):
        c1 = min(c0 + chunk, d_ff)
        for i, rows in enumerate(halves):
            gate = jnp.dot(h_ref[rows, :], wgu_ref[:, c0:c1], preferred_element_type=F32)
            up = jnp.dot(h_ref[rows, :], wgu_ref[:, d_ff + c0:d_ff + c1], preferred_element_type=F32)
            act = ((gate * jax.nn.sigmoid(gate)) * up).astype(BF16)
            part = jnp.dot(act, wd_ref[c0:c1, :], preferred_element_type=F32)
            accs[i] = part if accs[i] is None else accs[i] + part
    for acc, rows in zip(accs, halves):
        o_ref[rows, :] = o_ref[rows, :] + mod_ref[5:6, :] * (_rms(acc) * gpost_ref[...])


def _mix_ffn(x2d, mod_all, layer, mod_row_of_tile, attn, pool, rnn, gates, wa, wp, wr, wo, g_mix,
             g_pre, w_gu, w_down, g_post, *, tm, cast_weights=(), cast_layer=0):
    n_tok, d = x2d.shape
    d_ff = w_down.shape[0]
    row = lambda i: (i, 0)
    tok = lambda arr: pl.BlockSpec((tm, arr.shape[1]), row)
    own = lambda wgt: (wgt[None], 0)
    params = (own(wa), own(wp), own(wr), own(wo), (g_mix, layer), (g_pre, layer), own(w_gu), own(w_down),
              (g_post, layer))
    cast_in, cast_out = _cast_streams(cast_weights, cast_layer, n_tok // tm)
    res = pl.pallas_call(
        functools.partial(_mix_ffn_body, d_ff=d_ff, chunk=1024, n_cast=len(cast_weights)),
        grid=(n_tok // tm,),
        in_specs=[tok(x2d), _mod_block(mod_all, layer, mod_row_of_tile),
                  tok(attn), tok(pool), tok(rnn), tok(gates)]
                 + [_layer_block(p, lyr) for p, lyr in params] + cast_in,
        out_specs=[pl.BlockSpec((tm, d), row)] + [o[0] for o in cast_out],
        out_shape=[jax.ShapeDtypeStruct((n_tok, d), F32)] + [o[1] for o in cast_out],
        scratch_shapes=[pltpu.VMEM((tm, d), BF16), pltpu.VMEM((tm, d), BF16)],
        compiler_params=_params(1),
        name="mix_ffn",
    )(x2d, mod_all, attn, pool, rnn, gates, *[p for p, _ in params], *cast_weights)
    return res[0] if not cast_weights else res


def _pack_gate_weights(w_a, w_x):
    def block_diag(w):
        bw = RNN_W // RNN_BLOCKS
        rows = [jnp.pad(w[:, :, h], ((0, 0), (0, 0), (0, 0), (bw * h, RNN_W - bw * (h + 1))))
                for h in range(RNN_BLOCKS)]
        return jnp.concatenate(rows, axis=2)
    bd_a, bd_x = block_diag(w_a), block_diag(w_x)
    half = RNN_W // 2
    halves = []
    for j in range(2):
        sl = slice(j * half, (j + 1) * half)
        halves.append(jnp.concatenate([bd_a[:, :, sl, sl], bd_x[:, :, sl, sl]], axis=3))
    return jnp.stack(halves, axis=2).astype(BF16)


def kernel(x, c, ctx, c_ctx, w_ada, b_ada, g_pre_mix, g_post_mix, g_pre_ffn, g_post_ffn, w_in, attn_sink, w_attn_o, pool_mix, pool_scale, w_pool_o, conv_w, conv_b, lru_w_a, lru_b_a, lru_w_x, lru_b_x, lru_lambda, w_rnn_o, w_out, w_gu, w_down):
    bsz, seq, d = x.shape
    n_ctx = ctx.shape[1]
    depth = w_ada.shape[0]
    tm = min(512, seq)
    tm_ctx = min(tm, n_ctx)

    mod_rows = -(-(bsz + 1) // SUBLANES) * SUBLANES
    cvec = jnp.zeros((mod_rows, d), F32).at[:bsz].set(c).at[bsz].set(c_ctx)
    mod_all = _adaln_mod(cvec, w_ada, b_ada).reshape(depth, mod_rows, 6, d)

    rope = _rope_tables(seq)
    lat_row = lambda i: i // (seq // tm)
    ctx_row = lambda i: bsz

    rowvec = lambda a: a.reshape(depth, 1, -1)
    w_in_l = w_in[0].astype(BF16)
    channel_weights = (w_attn_o, w_pool_o, w_rnn_o, w_out, w_gu, w_down)
    w_mix = pool_mix.astype(BF16)
    wg = _pack_gate_weights(lru_w_a, lru_w_x)
    g_pre, g_post = rowvec(g_pre_mix), rowvec(g_post_mix)
    gf_pre, gf_post = rowvec(g_pre_ffn), rowvec(g_post_ffn)
    p_scale, cv_b = rowvec(pool_scale), rowvec(conv_b)

    x2 = x.reshape(bsz * seq, d)
    c2 = ctx.reshape(bsz * n_ctx, d)
    r3 = lambda a, n: a.reshape(bsz, n, a.shape[-1])
    flat = lambda a: a.reshape(-1, a.shape[-1])
    for l in range(depth):
        need_ctx = l < depth - 1
        q, kd, vd, rx, ry, pool_l, gt, wa, wp, wr, wo, w_gu_b, w_dn_b = _inproj(
            x2, mod_all, l, lat_row, g_pre, w_in_l[None], w_mix, p_scale, tm=tm, tiles_per_seq=seq // tm,
            rope_tables=rope, kv_only=False, cast_weights=channel_weights)
        if need_ctx:
            qc, kdc, vdc, rxc, ryc, pool_c, gtc = _inproj(
                c2, mod_all, l, ctx_row, g_pre, w_in_l[None], w_mix, p_scale, tm=tm_ctx,
                tiles_per_seq=n_ctx // tm_ctx, rope_tables=None, kv_only=False)
        else:
            kdc, vdc, rxc = _inproj(
                c2, mod_all, l, ctx_row, g_pre, w_in_l[None], None, None, tm=tm_ctx, tiles_per_seq=1,
                rope_tables=None, kv_only=True)
            ryc = None

        kdc3, vdc3 = r3(kdc, n_ctx), r3(vdc, n_ctx)
        attn_l = _attention(attn_sink, l, r3(q, seq), r3(kd, seq), r3(vd, seq), kdc3, vdc3)
        rnn_l, rnn_c = _lru(l, r3(rxc, n_ctx), r3(rx, seq), None if ryc is None else r3(ryc, n_ctx),
                            r3(ry, seq), conv_w, cv_b, wg, lru_b_a, lru_b_x, lru_lambda)

        if need_ctx:
            x2, w_in_next = _mix_ffn(x2, mod_all, l, lat_row, flat(attn_l), pool_l, flat(rnn_l), gt, wa, wp,
                                     wr, wo, g_post, gf_pre, w_gu_b, w_dn_b, gf_post, tm=tm,
                                     cast_weights=(w_in,), cast_layer=l + 1)
        else:
            x2 = _mix_ffn(x2, mod_all, l, lat_row, flat(attn_l), pool_l, flat(rnn_l), gt, wa, wp, wr, wo,
                          g_post, gf_pre, w_gu_b, w_dn_b, gf_post, tm=tm)

        if need_ctx:
            attn_c = _attention_ctx(attn_sink, l, r3(qc, n_ctx), kdc3, vdc3)
            c2 = _mix_ffn(c2, mod_all, l, ctx_row, flat(attn_c), pool_c, flat(rnn_c), gtc, wa, wp, wr,
                          wo, g_post, gf_pre, w_gu_b, w_dn_b, gf_post, tm=math.gcd(tm, bsz * n_ctx))
            w_in_l = w_in_next
    return x2.reshape(bsz, seq, d)
```
